```python
import math
import jax, jax.numpy as jnp
from jax import lax
import numpy as np

D_MODEL = 2048
BATCH = 4
SEQ = 2048
DEPTH = 1
DEC_BATCH = 128
DEC_SEQ = 4
PAST_LEN = 16384
PAGE_SIZE = 128

N_META = 16
MIX_WIDTH = D_MODEL
W_A = MIX_WIDTH // 2
W_B = MIX_WIDTH - W_A
CONV_HEADS_A = 8
CONV_HEAD_DIM_A = W_A // CONV_HEADS_A
CONV_K_A = 31
LRU_HEADS = 8
LRU_HEAD_DIM = W_B // LRU_HEADS
CONV_K_B = 4
LRU_C = 8.0
N_GROUPS = 8
EXPERTS_PER_GROUP = 8
N_EXPERTS = N_GROUPS * EXPERTS_PER_GROUP
TOP_K = 2
D_EXPERT = D_MODEL // 4
MOE_BLOCK_ROWS = 128
ALPHA = (2.0 * DEPTH) ** 0.25
BETA = (8.0 * DEPTH) ** -0.25
LN_EPS = 1e-5

kernel_name = "hymba_conformer_rglru_hmoe_step"


def layer_norm(x, g, b):
    xf = x.astype(jnp.float32)
    mu = jnp.mean(xf, axis=-1, keepdims=True)
    xc = xf - mu
    var = jnp.mean(xc * xc, axis=-1, keepdims=True)
    return (xc * lax.rsqrt(var + LN_EPS) * g.astype(jnp.float32) + b.astype(jnp.float32)).astype(x.dtype)


def group_norm_per_token(x, g, b):
    Bn, T, C = x.shape
    xf = x.astype(jnp.float32).reshape(Bn, T, CONV_HEADS_A, CONV_HEAD_DIM_A)
    mu = jnp.mean(xf, axis=-1, keepdims=True)
    xc = xf - mu
    var = jnp.mean(xc * xc, axis=-1, keepdims=True)
    y = (xc * lax.rsqrt(var + LN_EPS)).reshape(Bn, T, C)
    return (y * g.astype(jnp.float32) + b.astype(jnp.float32)).astype(x.dtype)


def causal_dwconv(x, buf, w, bias):
    K, C = w.shape
    xc = jnp.concatenate([buf.astype(x.dtype), x], axis=1)
    y = lax.conv_general_dilated(xc, w[:, None, :].astype(x.dtype), window_strides=(1,),
                                 padding="VALID", dimension_numbers=("NWC", "WIO", "NWC"),
                                 feature_group_count=C)
    return y + bias.astype(x.dtype), xc[:, -(K - 1):]


def rg_lru(x, h0, w_rg, b_rg, w_ig, b_ig, lam):
    Bn, T, C = x.shape
    xh = x.reshape(Bn, T, LRU_HEADS, LRU_HEAD_DIM)
    r = jax.nn.sigmoid(jnp.einsum("bthi,hij->bthj", xh, w_rg).reshape(Bn, T, C).astype(jnp.float32)
                       + b_rg.astype(jnp.float32))
    i = jax.nn.sigmoid(jnp.einsum("bthi,hij->bthj", xh, w_ig).reshape(Bn, T, C).astype(jnp.float32)
                       + b_ig.astype(jnp.float32))
    log_a = -LRU_C * r * jax.nn.softplus(-lam.astype(jnp.float32))
    a = jnp.exp(log_a)
    u = jnp.sqrt(-jnp.expm1(2.0 * log_a)) * (i * x.astype(jnp.float32))

    def step(h, inp):
        a_t, u_t = inp
        h = a_t * h + u_t
        return h, h

    h_last, hs = lax.scan(step, h0.astype(jnp.float32), (jnp.swapaxes(a, 0, 1), jnp.swapaxes(u, 0, 1)))
    return jnp.swapaxes(hs, 0, 1).astype(x.dtype), h_last.astype(x.dtype)


def hier_moe(h, w_router_group, b_router_group, w_router_expert, b_router_expert, w_gate, w_up, w_down):
    Bn, T, D = h.shape
    N = Bn * T
    x2 = h.reshape(N, D)
    g_prob = jax.nn.softmax(jnp.dot(x2, w_router_group).astype(jnp.float32)
                            + b_router_group.astype(jnp.float32), axis=-1)
    g_top, g_idx = lax.top_k(g_prob, 1)
    e_logits = (jnp.dot(x2, w_router_expert).astype(jnp.float32)
                + b_router_expert.astype(jnp.float32)).reshape(N, N_GROUPS, EXPERTS_PER_GROUP)
    e_logits = e_logits[jnp.arange(N), g_idx[:, 0]]
    e_top, e_loc = lax.top_k(jax.nn.softmax(e_logits, axis=-1), TOP_K)
    gate = e_top / jnp.sum(e_top, axis=-1, keepdims=True) * g_top
    e_idx = g_idx * EXPERTS_PER_GROUP + e_loc

    NK = N * TOP_K
    flat_e = e_idx.reshape(NK)
    flat_tok = jnp.repeat(jnp.arange(N, dtype=jnp.int32), TOP_K)
    flat_g = gate.reshape(NK)
    order = jnp.argsort(flat_e)
    sorted_e = flat_e[order]
    sorted_tok = flat_tok[order]
    counts = jnp.bincount(flat_e, length=N_EXPERTS)
    seg_start = jnp.cumsum(counts) - counts
    padded = ((counts + MOE_BLOCK_ROWS - 1) // MOE_BLOCK_ROWS) * MOE_BLOCK_ROWS
    pad_end = jnp.cumsum(padded)
    pad_start = pad_end - padded
    dest = pad_start[sorted_e] + (jnp.arange(NK, dtype=jnp.int32) - seg_start[sorted_e])
    n_blocks = -(-(NK + N_EXPERTS * (MOE_BLOCK_ROWS - 1)) // MOE_BLOCK_ROWS)
    P = n_blocks * MOE_BLOCK_ROWS
    rows_tok = jnp.full((P,), N, jnp.int32).at[dest].set(sorted_tok)
    rows_g = jnp.zeros((P,), jnp.float32).at[dest].set(flat_g[order])
    block_expert = jnp.clip(jnp.searchsorted(pad_end, jnp.arange(n_blocks) * MOE_BLOCK_ROWS, side="right"),
                            0, N_EXPERTS - 1).astype(jnp.int32)
    xpad = jnp.concatenate([x2, jnp.zeros((1, D), x2.dtype)], axis=0)

    def expert_block(args):
        tok, e = args
        xb = xpad[tok]
        hid = jax.nn.silu(jnp.dot(xb, w_gate[e])) * jnp.dot(xb, w_up[e])
        return jnp.dot(hid, w_down[e])

    yb = lax.map(expert_block, (rows_tok.reshape(n_blocks, MOE_BLOCK_ROWS), block_expert))
    y = jax.ops.segment_sum(yb.reshape(P, D).astype(jnp.float32) * rows_g[:, None], rows_tok,
                            num_segments=N + 1)[:N]
    return y.astype(h.dtype).reshape(Bn, T, D)


def decoder_layer(h, buf_a, buf_b, lru_h, w):
    (w_in, conv_a_w, conv_a_b, norm_a_g, norm_a_b, conv_b_w, conv_b_b, w_rg, b_rg, w_ig, b_ig,
     lru_lambda, w_out, ln1_g, ln1_b, w_router_group, b_router_group, w_router_expert,
     b_router_expert, w_gate, w_up, w_down, ln2_g, ln2_b) = w
    proj = jnp.dot(h, w_in)
    a_val, a_gate, b_x, b_gate = jnp.split(proj, [W_A, 2 * W_A, 2 * W_A + W_B], axis=-1)
    glu = a_val * jax.nn.sigmoid(a_gate)
    ca, new_buf_a = causal_dwconv(glu, buf_a, conv_a_w, conv_a_b)
    a_out = jax.nn.silu(group_norm_per_token(ca, norm_a_g, norm_a_b))
    cb, new_buf_b = causal_dwconv(b_x, buf_b, conv_b_w, conv_b_b)
    lru_y, new_h = rg_lru(cb, lru_h, w_rg, b_rg, w_ig, b_ig, lru_lambda)
    b_out = lru_y * jax.nn.gelu(b_gate)
    mix = jnp.dot(jnp.concatenate([a_out, b_out], axis=-1), w_out)
    h = layer_norm(ALPHA * h + mix, ln1_g, ln1_b)
    ffn = hier_moe(h, w_router_group, b_router_group, w_router_expert, b_router_expert, w_gate, w_up, w_down)
    h = layer_norm(ALPHA * h + ffn, ln2_g, ln2_b)
    return h, new_buf_a, new_buf_b, new_h


def run_trunk(x, ln_in_g, ln_in_b, state_a, state_b, state_h, layer_w):
    h = layer_norm(x, ln_in_g, ln_in_b)
    new_a, new_b, new_h = [], [], []
    for l in range(DEPTH):
        w = tuple(p[l] for p in layer_w)
        h, sa, sb, sh = decoder_layer(h, state_a[l], state_b[l], state_h[l], w)
        new_a.append(sa)
        new_b.append(sb)
        new_h.append(sh)
    return h, jnp.stack(new_a), jnp.stack(new_b), jnp.stack(new_h)


def setup_inputs(seed: int = 0) -> dict:
    key = jax.random.key(seed)
    ks = jax.random.split(key, 40)
    f32 = jnp.float32
    nrm = lambda k, shape, s: jax.random.normal(k, shape, f32) * s
    u = jax.random.uniform(ks[20], (DEPTH, W_B), f32, minval=0.9, maxval=0.999)
    a0 = u ** (1.0 / LRU_C)
    lru_lambda = jnp.log(a0) - jnp.log1p(-a0)
    return {
        "x_prompt": nrm(ks[0], (BATCH, SEQ, D_MODEL), 1.0),
        "x_sample": nrm(ks[1], (DEC_BATCH, DEC_SEQ, D_MODEL), 1.0),
        "state_conv_a": nrm(ks[2], (DEPTH, DEC_BATCH, CONV_K_A - 1, W_A), 0.5),
        "state_conv_b": nrm(ks[3], (DEPTH, DEC_BATCH, CONV_K_B - 1, W_B), 1.0),
        "state_lru": nrm(ks[4], (DEPTH, DEC_BATCH, W_B), 0.5),
        "meta_tokens": nrm(ks[5], (N_META, D_MODEL), 1.0),
        "ln_in_g": 1.0 + nrm(ks[6], (D_MODEL,), 0.02),
        "ln_in_b": nrm(ks[7], (D_MODEL,), 0.02),
        "w_in": nrm(ks[8], (DEPTH, D_MODEL, 2 * W_A + 2 * W_B), D_MODEL ** -0.5),
        "conv_a_w": nrm(ks[9], (DEPTH, CONV_K_A, W_A), CONV_K_A ** -0.5),
        "conv_a_b": nrm(ks[10], (DEPTH, W_A), 0.02),
        "norm_a_g": 1.0 + nrm(ks[11], (DEPTH, W_A), 0.02),
        "norm_a_b": nrm(ks[12], (DEPTH, W_A), 0.02),
        "conv_b_w": nrm(ks[13], (DEPTH, CONV_K_B, W_B), CONV_K_B ** -0.5),
        "conv_b_b": nrm(ks[14], (DEPTH, W_B), 0.02),
        "w_rg": nrm(ks[15], (DEPTH, LRU_HEADS, LRU_HEAD_DIM, LRU_HEAD_DIM), LRU_HEAD_DIM ** -0.5),
        "b_rg": nrm(ks[16], (DEPTH, W_B), 0.02),
        "w_ig": nrm(ks[17], (DEPTH, LRU_HEADS, LRU_HEAD_DIM, LRU_HEAD_DIM), LRU_HEAD_DIM ** -0.5),
        "b_ig": nrm(ks[18], (DEPTH, W_B), 0.02),
        "lru_lambda": lru_lambda,
        "w_out": nrm(ks[21], (DEPTH, W_A + W_B, D_MODEL), (W_A + W_B) ** -0.5 * BETA),
        "ln1_g": 1.0 + nrm(ks[22], (DEPTH, D_MODEL), 0.02),
        "ln1_b": nrm(ks[23], (DEPTH, D_MODEL), 0.02),
        "w_router_group": nrm(ks[24], (DEPTH, D_MODEL, N_GROUPS), D_MODEL ** -0.5),
        "b_router_group": nrm(ks[25], (DEPTH, N_GROUPS), 0.01),
        "w_router_expert": nrm(ks[26], (DEPTH, D_MODEL, N_EXPERTS), D_MODEL ** -0.5),
        "b_router_expert": nrm(ks[27], (DEPTH, N_EXPERTS), 0.01),
        "w_gate": nrm(ks[28], (DEPTH, N_EXPERTS, D_MODEL, D_EXPERT), D_MODEL ** -0.5),
        "w_up": nrm(ks[29], (DEPTH, N_EXPERTS, D_MODEL, D_EXPERT), D_MODEL ** -0.5),
        "w_down": nrm(ks[30], (DEPTH, N_EXPERTS, D_EXPERT, D_MODEL), D_EXPERT ** -0.5 * BETA),
        "ln2_g": 1.0 + nrm(ks[31], (DEPTH, D_MODEL), 0.02),
        "ln2_b": nrm(ks[32], (DEPTH, D_MODEL), 0.02),
    }


def reference(x_prompt, x_sample, state_conv_a, state_conv_b, state_lru, meta_tokens, ln_in_g, ln_in_b,
              w_in, conv_a_w, conv_a_b, norm_a_g, norm_a_b, conv_b_w, conv_b_b, w_rg, b_rg, w_ig, b_ig,
              lru_lambda, w_out, ln1_g, ln1_b, w_router_group, b_router_group, w_router_expert,
              b_router_expert, w_gate, w_up, w_down, ln2_g, ln2_b):
    layer_w = (w_in, conv_a_w, conv_a_b, norm_a_g, norm_a_b, conv_b_w, conv_b_b, w_rg, b_rg, w_ig, b_ig,
               lru_lambda, w_out, ln1_g, ln1_b, w_router_group, b_router_group, w_router_expert,
               b_router_expert, w_gate, w_up, w_down, ln2_g, ln2_b)
    dt = x_prompt.dtype
    Bp = x_prompt.shape[0]
    meta = jnp.broadcast_to(meta_tokens.astype(dt)[None], (Bp, N_META, D_MODEL))
    hp = jnp.concatenate([meta, x_prompt], axis=1)
    zero_a = jnp.zeros((DEPTH, Bp, CONV_K_A - 1, W_A), dt)
    zero_b = jnp.zeros((DEPTH, Bp, CONV_K_B - 1, W_B), dt)
    zero_h = jnp.zeros((DEPTH, Bp, W_B), dt)
    yp, pa, pb, ph = run_trunk(hp, ln_in_g, ln_in_b, zero_a, zero_b, zero_h, layer_w)
    ys, sa, sb, sh = run_trunk(x_sample, ln_in_g, ln_in_b, state_conv_a, state_conv_b, state_lru, layer_w)
    return (yp[:, N_META:], ys, pa, pb, ph, sa, sb, sh)
```

```python
import functools

import jax
import jax.numpy as jnp
from jax import lax
from jax.experimental import pallas as pl
from jax.experimental.pallas import tpu as pltpu

f32 = jnp.float32
bf16 = jnp.bfloat16
i32 = jnp.int32

CONV_HEADS_A = 8
LRU_HEADS = 8
LRU_C = 8.0
N_GROUPS = 8
EXPERTS_PER_GROUP = 8
TOP_K = 2
LN_EPS = 1e-5

SUBLANES = 8
LANES = 128
BF16_ROWS = 16
VMEM_LIMIT = 56 * 1024 * 1024

TOKEN_TILE = 256
SEQ_TILE = 256
CONV_ROWS = 16
STEP_BATCH = 32
EXPERT_ROWS = 128
COMBINE_TILE = 128
ROUTE_LANES = 128
NEG_BIG = -1e30


def _round_up(x, m):
    return (x + m - 1) // m * m


def _layer_norm(x, g, b):
    mu = jnp.mean(x, axis=-1, keepdims=True)
    xc = x - mu
    var = jnp.mean(xc * xc, axis=-1, keepdims=True)
    return xc * lax.rsqrt(var + LN_EPS) * g + b


def _group_norm_silu(x, g, b, group):
    parts = []
    for c in range(x.shape[1] // group):
        xg = x[:, c * group:(c + 1) * group]
        mu = jnp.mean(xg, axis=-1, keepdims=True)
        xc = xg - mu
        var = jnp.mean(xc * xc, axis=-1, keepdims=True)
        parts.append(xc * lax.rsqrt(var + LN_EPS))
    y = jnp.concatenate(parts, axis=-1) * g + b
    return y * jax.nn.sigmoid(y)


def _lru_gates(cb, wg_ref, brg, big, lam):
    heads, hd, _ = wg_ref.shape
    cbb = cb.astype(bf16)
    rs, iz = [], []
    for h in range(heads):
        z = jnp.dot(cbb[:, h * hd:(h + 1) * hd], wg_ref[h], preferred_element_type=f32)
        rs.append(z[:, :hd])
        iz.append(z[:, hd:])
    r = jax.nn.sigmoid(jnp.concatenate(rs, axis=-1) + brg)
    i = jax.nn.sigmoid(jnp.concatenate(iz, axis=-1) + big)
    log_a = -LRU_C * r * jax.nn.softplus(-lam)
    a = jnp.exp(log_a)
    u = jnp.sqrt(-jnp.tanh(log_a) * (a * a + 1.0)) * (i * cb)
    return a, u


def _inproj_kernel(xa_ref, xb_ref, g_ref, b_ref, w_ref, glu_ref, bx_ref, gate_ref, *, n_a_tiles):
    i = pl.program_id(0)
    x = jnp.where(i < n_a_tiles, xa_ref[...], xb_ref[...])
    xn = _layer_norm(x, g_ref[...], b_ref[...]).astype(bf16)
    width = glu_ref.shape[1]
    ch = min(width, 512)
    for c in range(width // ch):
        lo = c * ch
        av = jnp.dot(xn, w_ref[:, lo:lo + ch], preferred_element_type=f32)
        ag = jnp.dot(xn, w_ref[:, width + lo:width + lo + ch], preferred_element_type=f32)
        glu_ref[:, lo:lo + ch] = av * jax.nn.sigmoid(ag)
        bx_ref[:, lo:lo + ch] = jnp.dot(xn, w_ref[:, 2 * width + lo:2 * width + lo + ch],
                                        preferred_element_type=f32)
        bg = jnp.dot(xn, w_ref[:, 3 * width + lo:3 * width + lo + ch], preferred_element_type=f32)
        gate_ref[:, lo:lo + ch] = jax.nn.gelu(bg)


def _inproj(xa, xb, n_b_rows, ln_g, ln_b, w_in_b, tm):
    na, d = xa.shape
    width = w_in_b.shape[1] // 4
    n_a_tiles = na // tm
    n_tiles = (na + n_b_rows) // tm
    n_b_tiles = max(n_tiles - n_a_tiles, 1)
    row_out = pl.BlockSpec((tm, width), lambda i: (i, 0))
    return pl.pallas_call(
        functools.partial(_inproj_kernel, n_a_tiles=n_a_tiles),
        grid=(n_tiles,),
        in_specs=[
            pl.BlockSpec((tm, d), lambda i: (jnp.minimum(i, n_a_tiles - 1), 0)),
            pl.BlockSpec((tm, d), lambda i: (jnp.clip(i - n_a_tiles, 0, n_b_tiles - 1), 0)),
            pl.BlockSpec((1, d), lambda i: (0, 0)),
            pl.BlockSpec((1, d), lambda i: (0, 0)),
            pl.BlockSpec((d, 4 * width), lambda i: (0, 0), pipeline_mode=pl.Buffered(1)),
        ],
        out_specs=[row_out, row_out, row_out],
        out_shape=[jax.ShapeDtypeStruct((na + n_b_rows, width), f32)] * 3,
        compiler_params=pltpu.CompilerParams(dimension_semantics=("arbitrary",),
                                             vmem_limit_bytes=VMEM_LIMIT),
        name="inproj",
    )(xa, xb, ln_g, ln_b, w_in_b)


def _mixer_seq_kernel(glu_ref, bx_ref, gate_ref, sa_ref, sb_ref, sh_ref,
                      wa_ref, ba_ref, ng_ref, nb_ref, wb_ref, bb_ref, wg_ref, brg_ref, big_ref, lam_ref,
                      aout_ref, bout_ref, nsa_ref, nsb_ref, nsh_ref,
                      wina, winb, xsa, xsb, cb_s, a_s, u_s, hcar, *, rows):
    tau = pl.program_id(1)
    tt, width = glu_ref.shape
    ka, kb = wa_ref.shape[0], wb_ref.shape[0]
    ha, hb = sa_ref.shape[0], sb_ref.shape[0]

    @pl.when(tau == 0)
    def _():
        wina[0:ha, :] = sa_ref[...]
        winb[0:hb, :] = sb_ref[...]
        hcar[...] = jnp.broadcast_to(sh_ref[...], hcar.shape)

    wina[ha:ha + tt, :] = glu_ref[...]
    winb[hb:hb + tt, :] = bx_ref[...]
    for s in sorted({(ha - (ka - 1) + k) % SUBLANES for k in range(ka)} - {0}):
        xsa[s - 1] = wina[pl.ds(s, tt + ha - SUBLANES), :]
    for s in sorted({(hb - (kb - 1) + k) % SUBLANES for k in range(kb)} - {0}):
        xsb[s - 1] = winb[pl.ds(s, tt + hb - SUBLANES), :]

    def tap(win, xs, off, r0):
        q, s = divmod(off, SUBLANES)
        src = win if s == 0 else xs.at[s - 1]
        return src[pl.ds(r0 + SUBLANES * q, rows), :]

    def conv_block(rb, carry):
        r0 = pl.multiple_of(rb * rows, rows)
        acc = jnp.broadcast_to(ba_ref[...], (rows, width))
        for k in range(ka):
            acc = acc + wa_ref[k:k + 1, :] * tap(wina, xsa, ha - (ka - 1) + k, r0)
        a_out = _group_norm_silu(acc, ng_ref[...], nb_ref[...], width // CONV_HEADS_A)
        aout_ref[pl.ds(r0, rows), :] = a_out.astype(aout_ref.dtype)
        accb = jnp.broadcast_to(bb_ref[...], (rows, width))
        for k in range(kb):
            accb = accb + wb_ref[k:k + 1, :] * tap(winb, xsb, hb - (kb - 1) + k, r0)
        cb_s[pl.ds(r0, rows), :] = accb
        return carry

    lax.fori_loop(0, tt // rows, conv_block, 0)

    a, u = _lru_gates(cb_s[...], wg_ref, brg_ref[...], big_ref[...], lam_ref[...])
    row8 = lax.broadcasted_iota(i32, (tt, width), 0) & (SUBLANES - 1)
    d = 1
    while d < SUBLANES:
        m = row8 >= d
        a_sh = jnp.where(m, pltpu.roll(a, d, axis=0), 1.0)
        u_sh = jnp.where(m, pltpu.roll(u, d, axis=0), 0.0)
        u = a * u_sh + u
        a = a * a_sh
        d *= 2
    a_s[...] = a
    u_s[...] = u

    def scan_block(j, h):
        r0 = pl.multiple_of(j * SUBLANES, SUBLANES)
        hblk = a_s[pl.ds(r0, SUBLANES), :] * h + u_s[pl.ds(r0, SUBLANES), :]
        u_s[pl.ds(r0, SUBLANES), :] = hblk
        return jnp.broadcast_to(hblk[SUBLANES - 1:SUBLANES, :], h.shape)

    h_last = lax.fori_loop(0, tt // SUBLANES, scan_block, hcar[...])
    hcar[...] = h_last
    bout_ref[...] = (u_s[...] * gate_ref[...]).astype(bout_ref.dtype)

    tail_a = wina[tt:tt + ha, :]
    tail_b = winb[tt:tt + hb, :]
    wina[0:ha, :] = tail_a
    winb[0:hb, :] = tail_b
    nsa_ref[...] = tail_a
    nsb_ref[...] = tail_b
    nsh_ref[...] = h_last[0:1, :]


def _mixer_seq(glu, bx, gate, sa_pad, sb_pad, sh, mix_w, n_seq, seq_len, tt):
    wa, ba, ng, nb, wb, bb, wg, brg, big, lam = mix_w
    width = glu.shape[1]
    ha, hb = sa_pad.shape[1], sb_pad.shape[1]
    n_t = seq_len // tt
    rows = min(CONV_ROWS, tt)
    row_in = pl.BlockSpec((tt, width), lambda b, t: (b * n_t + t, 0))

    def const(shape):
        return pl.BlockSpec(shape, lambda b, t: (0,) * len(shape))

    def per_seq(r):
        return pl.BlockSpec((None, r, width), lambda b, t: (b, 0, 0))

    return pl.pallas_call(
        functools.partial(_mixer_seq_kernel, rows=rows),
        grid=(n_seq, n_t),
        in_specs=[row_in, row_in, row_in, per_seq(ha), per_seq(hb), per_seq(1),
                  const(wa.shape), const(ba.shape), const(ng.shape), const(nb.shape),
                  const(wb.shape), const(bb.shape), const(wg.shape), const(brg.shape),
                  const(big.shape), const(lam.shape)],
        out_specs=[row_in, row_in, per_seq(ha), per_seq(hb), per_seq(1)],
        out_shape=[jax.ShapeDtypeStruct((n_seq * seq_len, width), bf16),
                   jax.ShapeDtypeStruct((n_seq * seq_len, width), bf16),
                   jax.ShapeDtypeStruct((n_seq, ha, width), f32),
                   jax.ShapeDtypeStruct((n_seq, hb, width), f32),
                   jax.ShapeDtypeStruct((n_seq, 1, width), f32)],
        scratch_shapes=[pltpu.VMEM((ha + tt, width), f32),
                        pltpu.VMEM((hb + tt, width), f32),
                        pltpu.VMEM((SUBLANES - 1, tt + ha - SUBLANES, width), f32),
                        pltpu.VMEM((SUBLANES - 1, tt + hb - SUBLANES, width), f32),
                        pltpu.VMEM((tt, width), f32),
                        pltpu.VMEM((tt, width), f32),
                        pltpu.VMEM((tt, width), f32),
                        pltpu.VMEM((SUBLANES, width), f32)],
        compiler_params=pltpu.CompilerParams(dimension_semantics=("arbitrary", "arbitrary"),
                                             vmem_limit_bytes=VMEM_LIMIT),
        name="mixer_seq",
    )(glu, bx, gate, sa_pad, sb_pad, sh, wa, ba, ng, nb, wb, bb, wg, brg, big, lam)


def _mixer_step_kernel(glu_ref, bx_ref, gate_ref, glun_ref, bxn_ref, sa_ref, sb_ref, sh_ref,
                       wat_ref, wa2_ref, wbt_ref, wb2_ref,
                       ba_ref, ng_ref, nb_ref, bb_ref, wg_ref, brg_ref, big_ref, lam_ref,
                       aout_ref, bout_ref, nsa_ref, nsb_ref, nsh_ref,
                       hista, histb, hcar):
    t = pl.program_id(1)
    n_t = hista.shape[0]
    width = glu_ref.shape[1]
    ka1, kb1 = sa_ref.shape[1], sb_ref.shape[1]

    @pl.when(t == 0)
    def _():
        hista[...] = jnp.zeros_like(hista)
        histb[...] = jnp.zeros_like(histb)
        hcar[...] = sh_ref[...]
        nsa_ref[:, 0:ka1 - n_t, :] = sa_ref[:, n_t:ka1, :]
        nsa_ref[:, ka1 - n_t:ka1, :] = glun_ref[...]
        nsb_ref[...] = bxn_ref[:, n_t - kb1:n_t, :]

    hista[t] = glu_ref[...]
    histb[t] = bx_ref[...]

    ca = ba_ref[...] + jnp.sum(sa_ref[...] * wat_ref[...][None], axis=1)
    cb = bb_ref[...] + jnp.sum(sb_ref[...] * wbt_ref[...][None], axis=1)
    for j in range(n_t):
        ca = ca + wa2_ref[j:j + 1, :] * hista[j]
        cb = cb + wb2_ref[j:j + 1, :] * histb[j]
    a_out = _group_norm_silu(ca, ng_ref[...], nb_ref[...], width // CONV_HEADS_A)
    aout_ref[...] = a_out.astype(aout_ref.dtype)

    a, u = _lru_gates(cb, wg_ref, brg_ref[...], big_ref[...], lam_ref[...])
    h = a * hcar[...] + u
    hcar[...] = h
    nsh_ref[...] = h
    bout_ref[...] = (h * gate_ref[...]).astype(bout_ref.dtype)


def _mixer_step(glu, bx, gate, row_off, glu_nat, bx_nat, sa, sb, sh, step_w, mix_w, n_seq, n_t, bt):
    wat, wa2, wbt, wb2 = step_w
    wa, ba, ng, nb, wb, bb, wg, brg, big, lam = mix_w
    width = glu.shape[1]
    ka1, kb1 = sa.shape[1], sb.shape[1]
    off = row_off // bt
    nb_tiles = n_seq // bt
    row_in = pl.BlockSpec((bt, width), lambda j, t: (off + t * nb_tiles + j, 0))
    row_out = pl.BlockSpec((bt, width), lambda j, t: (t * nb_tiles + j, 0))

    def const(shape):
        return pl.BlockSpec(shape, lambda j, t: (0,) * len(shape))

    def per_t(arr):
        return pl.BlockSpec((None,) + arr.shape[1:], lambda j, t: (t, 0, 0))

    def per_b(r):
        return pl.BlockSpec((bt, r, width), lambda j, t: (j, 0, 0))

    state_h = pl.BlockSpec((bt, width), lambda j, t: (j, 0))
    return pl.pallas_call(
        _mixer_step_kernel,
        grid=(nb_tiles, n_t),
        in_specs=[row_in, row_in, row_in, per_b(n_t), per_b(n_t), per_b(ka1), per_b(kb1), state_h,
                  per_t(wat), per_t(wa2), per_t(wbt), per_t(wb2),
                  const(ba.shape), const(ng.shape), const(nb.shape), const(bb.shape),
                  const(wg.shape), const(brg.shape), const(big.shape), const(lam.shape)],
        out_specs=[row_out, row_out, per_b(ka1), per_b(kb1), state_h],
        out_shape=[jax.ShapeDtypeStruct((n_seq * n_t, width), bf16),
                   jax.ShapeDtypeStruct((n_seq * n_t, width), bf16),
                   jax.ShapeDtypeStruct((n_seq, ka1, width), f32),
                   jax.ShapeDtypeStruct((n_seq, kb1, width), f32),
                   jax.ShapeDtypeStruct((n_seq, width), f32)],
        scratch_shapes=[pltpu.VMEM((n_t, bt, width), f32),
                        pltpu.VMEM((n_t, bt, width), f32),
                        pltpu.VMEM((bt, width), f32)],
        compiler_params=pltpu.CompilerParams(dimension_semantics=("arbitrary", "arbitrary"),
                                             vmem_limit_bytes=VMEM_LIMIT),
        name="mixer_step",
    )(glu, bx, gate, glu_nat, bx_nat, sa, sb, sh, wat, wa2, wbt, wb2,
      ba, ng, nb, bb, wg, brg, big, lam)


def _outproj_kernel(ap_ref, bp_ref, as_ref, bs_ref, xp_ref, xs_ref, ling_ref, linb_ref, wo_ref,
                    l1g_ref, l1b_ref, wrh_ref, wrl_ref, br_ref,
                    h1_ref, route_ref, cnt_ref, carry, *, n_p_tiles, alpha):
    i = pl.program_id(0)
    is_p = i < n_p_tiles
    a = jnp.where(is_p, ap_ref[...], as_ref[...])
    b = jnp.where(is_p, bp_ref[...], bs_ref[...])
    x = jnp.where(is_p, xp_ref[...], xs_ref[...])
    width = a.shape[1]
    mix = (jnp.dot(a, wo_ref[0:width, :], preferred_element_type=f32)
           + jnp.dot(b, wo_ref[width:2 * width, :], preferred_element_type=f32))
    h0 = _layer_norm(x, ling_ref[...], linb_ref[...])
    h1 = _layer_norm(alpha * h0 + mix, l1g_ref[...], l1b_ref[...])
    h1_ref[...] = h1

    xh = h1.astype(bf16)
    xl = (h1 - xh.astype(f32)).astype(bf16)
    z = (jnp.dot(xh, wrh_ref[...], preferred_element_type=f32)
         + jnp.dot(xl, wrh_ref[...], preferred_element_type=f32)
         + jnp.dot(xh, wrl_ref[...], preferred_element_type=f32)) + br_ref[...]
    tm = z.shape[0]
    lane = lax.broadcasted_iota(i32, z.shape, 1).astype(f32)
    n_g, epg = float(N_GROUPS), float(EXPERTS_PER_GROUP)
    far = float(2 * ROUTE_LANES)

    gm = lane < n_g
    zg = jnp.where(gm, z, NEG_BIG)
    pg = jnp.where(gm, jnp.exp(zg - jnp.max(zg, axis=-1, keepdims=True)), 0.0)
    pg = pg / jnp.sum(pg, axis=-1, keepdims=True)
    g_top = jnp.max(pg, axis=-1, keepdims=True)
    g_idx = jnp.min(jnp.where(gm & (pg == g_top), lane, far), axis=-1, keepdims=True)

    lo = n_g + g_idx * epg
    em = (lane >= lo) & (lane < lo + epg)
    ze = jnp.where(em, z, NEG_BIG)
    pe = jnp.where(em, jnp.exp(ze - jnp.max(ze, axis=-1, keepdims=True)), 0.0)
    pe = pe / jnp.sum(pe, axis=-1, keepdims=True)
    v1 = jnp.max(jnp.where(em, pe, -1.0), axis=-1, keepdims=True)
    l1 = jnp.min(jnp.where(em & (pe == v1), lane, far), axis=-1, keepdims=True)
    pe2 = jnp.where(em & (lane != l1), pe, -1.0)
    v2 = jnp.max(pe2, axis=-1, keepdims=True)
    l2 = jnp.min(jnp.where(pe2 == v2, lane, far), axis=-1, keepdims=True)
    den = v1 + v2
    gate1 = v1 / den * g_top
    gate2 = v2 / den * g_top

    @pl.when(i == 0)
    def _():
        carry[...] = jnp.zeros_like(carry)

    o1 = lane == l1
    o2 = lane == l2
    onehot = jnp.where(o1 | o2, 1.0, 0.0)
    ri = lax.broadcasted_iota(i32, (tm, tm), 0)
    ci = lax.broadcasted_iota(i32, (tm, tm), 1)
    tri = jnp.where(ci < ri, 1.0, 0.0).astype(bf16)
    prefix = jnp.dot(tri, onehot.astype(bf16), preferred_element_type=f32) + carry[0:1, :]
    rank1 = jnp.sum(jnp.where(o1, prefix, 0.0), axis=-1, keepdims=True)
    rank2 = jnp.sum(jnp.where(o2, prefix, 0.0), axis=-1, keepdims=True)
    new_carry = carry[...] + jnp.sum(onehot, axis=0, keepdims=True)
    carry[...] = new_carry
    cnt_ref[...] = new_carry

    route = jnp.where(lane == 0.0, l1 - n_g, 0.0)
    route = jnp.where(lane == 1.0, l2 - n_g, route)
    route = jnp.where(lane == 2.0, rank1, route)
    route = jnp.where(lane == 3.0, rank2, route)
    route = jnp.where(lane == 4.0, gate1, route)
    route = jnp.where(lane == 5.0, gate2, route)
    route_ref[...] = route


def _outproj(a_p, b_p, a_s, b_s, x_p, x_s, ln_in_g, ln_in_b, w_out_b, ln1_g, ln1_b,
             wr_hi, wr_lo, b_r, alpha, tm):
    n_p, width = a_p.shape
    n_s = a_s.shape[0]
    d = x_p.shape[1]
    n_p_tiles = n_p // tm
    n_s_tiles = n_s // tm
    n_tiles = n_p_tiles + n_s_tiles

    def p_map(i):
        return (jnp.minimum(i, n_p_tiles - 1), 0)

    def s_map(i):
        return (jnp.clip(i - n_p_tiles, 0, n_s_tiles - 1), 0)

    def const(shape, **kw):
        return pl.BlockSpec(shape, lambda i: (0,) * len(shape), **kw)

    return pl.pallas_call(
        functools.partial(_outproj_kernel, n_p_tiles=n_p_tiles, alpha=alpha),
        grid=(n_tiles,),
        in_specs=[pl.BlockSpec((tm, width), p_map), pl.BlockSpec((tm, width), p_map),
                  pl.BlockSpec((tm, width), s_map), pl.BlockSpec((tm, width), s_map),
                  pl.BlockSpec((tm, d), p_map), pl.BlockSpec((tm, d), s_map),
                  const((1, d)), const((1, d)),
                  const((2 * width, d), pipeline_mode=pl.Buffered(1)),
                  const((1, d)), const((1, d)),
                  const((d, ROUTE_LANES)), const((d, ROUTE_LANES)), const((1, ROUTE_LANES))],
        out_specs=[pl.BlockSpec((tm, d), lambda i: (i, 0)),
                   pl.BlockSpec((tm, ROUTE_LANES), lambda i: (i, 0)),
                   const((SUBLANES, ROUTE_LANES))],
        out_shape=[jax.ShapeDtypeStruct((n_p + n_s, d), f32),
                   jax.ShapeDtypeStruct((n_p + n_s, ROUTE_LANES), f32),
                   jax.ShapeDtypeStruct((SUBLANES, ROUTE_LANES), f32)],
        scratch_shapes=[pltpu.VMEM((SUBLANES, ROUTE_LANES), f32)],
        compiler_params=pltpu.CompilerParams(dimension_semantics=("arbitrary",),
                                             vmem_limit_bytes=VMEM_LIMIT),
        name="outproj_router",
    )(a_p, b_p, a_s, b_s, x_p, x_s, ln_in_g, ln_in_b, w_out_b, ln1_g, ln1_b, wr_hi, wr_lo, b_r)


def _expert_kernel(rows_ref, start_ref, nblk_ref, eff_ref,
                   h1_hbm, wg_ref, wu_ref, wd_ref, ys_hbm,
                   xbuf, ybuf, wgb, wub, wdb, gsem, osem, *, blk):
    del eff_ref
    e = pl.program_id(0)
    nb = nblk_ref[e]
    base = pl.multiple_of(start_ref[e], blk)

    def gather(c, slot):
        def row(r, carry):
            tok = rows_ref[base + c * blk + r]
            pltpu.make_async_copy(h1_hbm.at[pl.ds(tok, 1), :],
                                  xbuf.at[slot, pl.ds(r, 1), :], gsem.at[slot]).start()
            return carry
        lax.fori_loop(0, blk, row, 0)

    def wait_gather(slot):
        pltpu.make_async_copy(h1_hbm.at[pl.ds(0, blk), :], xbuf.at[slot], gsem.at[slot]).wait()

    def out_copy(c, slot):
        dst = ys_hbm.at[pl.ds(pl.multiple_of(base + c * blk, blk), blk), :]
        return pltpu.make_async_copy(ybuf.at[slot], dst, osem.at[slot])

    @pl.when(nb > 0)
    def _():
        gather(0, 0)
        wgb[...] = wg_ref[...].astype(bf16)
        wub[...] = wu_ref[...].astype(bf16)
        wdb[...] = wd_ref[...].astype(bf16)

        def chunk(c, carry):
            slot = c % 2

            @pl.when(c + 1 < nb)
            def _():
                gather(c + 1, 1 - slot)

            wait_gather(slot)
            x = xbuf[slot].astype(bf16)
            hg = jnp.dot(x, wgb[...], preferred_element_type=f32)
            hu = jnp.dot(x, wub[...], preferred_element_type=f32)
            hid = (hg * jax.nn.sigmoid(hg) * hu).astype(bf16)
            y = jnp.dot(hid, wdb[...], preferred_element_type=f32)

            @pl.when(c >= 2)
            def _():
                out_copy(c - 2, slot).wait()

            ybuf[slot] = y
            out_copy(c, slot).start()
            return carry

        lax.fori_loop(0, nb, chunk, 0)

        @pl.when(nb >= 2)
        def _():
            out_copy(nb - 2, nb % 2).wait()

        out_copy(nb - 1, (nb - 1) % 2).wait()

    @pl.when(e == pl.num_programs(0) - 1)
    def _():
        n_tail = (ys_hbm.shape[0] - (base + nb * blk)) // blk
        ybuf[0] = jnp.zeros(ybuf.shape[1:], ybuf.dtype)

        def fill(c, carry):
            out_copy(nb + c, 0).start()
            return carry

        def drain(c, carry):
            out_copy(nb + c, 0).wait()
            return carry

        lax.fori_loop(0, n_tail, fill, 0)
        lax.fori_loop(0, n_tail, drain, 0)


def _experts(h1, rows_tok, pad_start, nblk, eff, w_gate, w_up, w_down, p_max, blk):
    n, d = h1.shape
    n_e, _, d_e = w_gate.shape
    grid_spec = pltpu.PrefetchScalarGridSpec(
        num_scalar_prefetch=4,
        grid=(n_e,),
        in_specs=[pl.BlockSpec(memory_space=pl.ANY),
                  pl.BlockSpec((None, d, d_e), lambda e, r, s, nbk, ef: (ef[e], 0, 0)),
                  pl.BlockSpec((None, d, d_e), lambda e, r, s, nbk, ef: (ef[e], 0, 0)),
                  pl.BlockSpec((None, d_e, d), lambda e, r, s, nbk, ef: (ef[e], 0, 0))],
        out_specs=pl.BlockSpec(memory_space=pl.ANY),
        scratch_shapes=[pltpu.VMEM((2, blk, d), f32),
                        pltpu.VMEM((2, blk, d), f32),
                        pltpu.VMEM((d, d_e), bf16),
                        pltpu.VMEM((d, d_e), bf16),
                        pltpu.VMEM((d_e, d), bf16),
                        pltpu.SemaphoreType.DMA((2,)),
                        pltpu.SemaphoreType.DMA((2,))],
    )
    return pl.pallas_call(
        functools.partial(_expert_kernel, blk=blk),
        grid_spec=grid_spec,
        out_shape=jax.ShapeDtypeStruct((p_max, d), f32),
        compiler_params=pltpu.CompilerParams(dimension_semantics=("arbitrary",),
                                             vmem_limit_bytes=VMEM_LIMIT),
        name="experts",
    )(rows_tok, pad_start, nblk, eff, h1, w_gate, w_up, w_down)


def _combine_kernel(dest_ref, h1_ref, route_ref, ys_hbm, l2g_ref, l2b_ref, yp_ref, ysm_ref,
                    gbuf, sem, *, n_p_tiles, alpha, tm):
    i = pl.program_id(0)
    n = pl.num_programs(0)

    def gather(tile, slot):
        def row(r, carry):
            p = 2 * (tile * tm + r)
            for k in range(TOP_K):
                pltpu.make_async_copy(ys_hbm.at[pl.ds(dest_ref[p + k], 1), :],
                                      gbuf.at[slot, k, pl.ds(r, 1), :], sem.at[slot]).start()
            return carry
        lax.fori_loop(0, tm, row, 0)

    @pl.when(i == 0)
    def _():
        gather(0, 0)

    @pl.when(i + 1 < n)
    def _():
        gather(i + 1, (i + 1) % 2)

    slot = i % 2
    for k in range(TOP_K):
        pltpu.make_async_copy(ys_hbm.at[pl.ds(0, tm), :], gbuf.at[slot, k], sem.at[slot]).wait()
    route = route_ref[...]
    y = gbuf[slot, 0] * route[:, 4:5] + gbuf[slot, 1] * route[:, 5:6]
    out = _layer_norm(alpha * h1_ref[...] + y, l2g_ref[...], l2b_ref[...])

    @pl.when(i < n_p_tiles)
    def _():
        yp_ref[...] = out

    @pl.when(i >= n_p_tiles)
    def _():
        ysm_ref[...] = out


def _combine(dest, h1, route, ys, ln2_g, ln2_b, n_p, alpha, tm):
    n, d = h1.shape
    n_s = n - n_p
    n_p_tiles = n_p // tm
    n_s_tiles = n_s // tm
    grid_spec = pltpu.PrefetchScalarGridSpec(
        num_scalar_prefetch=1,
        grid=(n // tm,),
        in_specs=[pl.BlockSpec((tm, d), lambda i, dst: (i, 0)),
                  pl.BlockSpec((tm, ROUTE_LANES), lambda i, dst: (i, 0)),
                  pl.BlockSpec(memory_space=pl.ANY),
                  pl.BlockSpec((1, d), lambda i, dst: (0, 0)),
                  pl.BlockSpec((1, d), lambda i, dst: (0, 0))],
        out_specs=[pl.BlockSpec((tm, d), lambda i, dst: (jnp.minimum(i, n_p_tiles - 1), 0)),
                   pl.BlockSpec((tm, d), lambda i, dst: (jnp.clip(i - n_p_tiles, 0, n_s_tiles - 1), 0))],
        scratch_shapes=[pltpu.VMEM((2, TOP_K, tm, d), f32),
                        pltpu.SemaphoreType.DMA((2,))],
    )
    return pl.pallas_call(
        functools.partial(_combine_kernel, n_p_tiles=n_p_tiles, alpha=alpha, tm=tm),
        grid_spec=grid_spec,
        out_shape=[jax.ShapeDtypeStruct((n_p, d), f32), jax.ShapeDtypeStruct((n_s, d), f32)],
        compiler_params=pltpu.CompilerParams(dimension_semantics=("arbitrary",),
                                             vmem_limit_bytes=VMEM_LIMIT),
        name="combine_ln2",
    )(dest, h1, route, ys, ln2_g, ln2_b)


def _shifted_taps(w, n_t, hist):
    k = w.shape[0]
    zero = jnp.zeros_like(w[0])
    wt = jnp.stack([jnp.stack([w[r - t] if 0 <= r - t < k else zero for r in range(hist)])
                    for t in range(n_t)])
    w2 = jnp.stack([jnp.stack([w[hist - t + j] if (j <= t and 0 <= hist - t + j < k) else zero
                               for j in range(n_t)]) for t in range(n_t)])
    return wt, w2


def kernel(x_prompt, x_sample, state_conv_a, state_conv_b, state_lru, meta_tokens, ln_in_g, ln_in_b, w_in, conv_a_w, conv_a_b, norm_a_g, norm_a_b, conv_b_w, conv_b_b, w_rg, b_rg, w_ig, b_ig, lru_lambda, w_out, ln1_g, ln1_b, w_router_group, b_router_group, w_router_expert, b_router_expert, w_gate, w_up, w_down, ln2_g, ln2_b):
    depth = w_in.shape[0]
    assert depth == 1, "single-layer trunk only"
    bp, tp, d = x_prompt.shape
    bs, ts, _ = x_sample.shape
    n_meta = meta_tokens.shape[0]
    width = conv_a_w.shape[2]
    ka, kb = conv_a_w.shape[1], conv_b_w.shape[1]
    n_e = w_gate.shape[1]
    alpha = (2.0 * depth) ** 0.25
    n_p, n_s = bp * tp, bs * ts
    n = n_p + n_s
    assert ts < ka and ts >= kb - 1
    assert n_p % TOKEN_TILE == 0 and n_s % TOKEN_TILE == 0 and tp % SEQ_TILE == 0
    assert bs % STEP_BATCH == 0 and n_p % STEP_BATCH == 0
    assert n_p % COMBINE_TILE == 0 and n_s % COMBINE_TILE == 0
    assert n_meta % BF16_ROWS == 0
    assert N_GROUPS * (1 + EXPERTS_PER_GROUP) <= ROUTE_LANES and n_e == N_GROUPS * EXPERTS_PER_GROUP

    row = lambda v: v.reshape(1, -1).astype(f32)
    ln_g, ln_b = row(ln_in_g), row(ln_in_b)
    w_in_b = w_in[0].astype(bf16)
    w_out_b = w_out[0].astype(bf16)
    mix_w = (conv_a_w[0], row(conv_a_b[0]), row(norm_a_g[0]), row(norm_a_b[0]),
             conv_b_w[0], row(conv_b_b[0]),
             jnp.concatenate([w_rg[0], w_ig[0]], axis=-1).astype(bf16),
             row(b_rg[0]), row(b_ig[0]), row(lru_lambda[0]))
    ha, hb = _round_up(ka - 1, SUBLANES), _round_up(kb - 1, SUBLANES)

    def pad_hist(s, h):
        return jnp.concatenate([jnp.zeros(s.shape[:1] + (h - s.shape[1], width), f32), s], axis=1)

    glu_m, bx_m, gate_m = _inproj(meta_tokens, meta_tokens, 0, ln_g, ln_b, w_in_b, n_meta)
    _, _, sa_m, sb_m, sh_m = _mixer_seq(
        glu_m, bx_m, gate_m, jnp.zeros((1, ha, width), f32), jnp.zeros((1, hb, width), f32),
        jnp.zeros((1, 1, width), f32), mix_w, 1, n_meta, n_meta)

    x_p = x_prompt.reshape(n_p, d)
    x_s = jnp.swapaxes(x_sample, 0, 1).reshape(n_s, d)
    glu, bx, gate = _inproj(x_p, x_s, n_s, ln_g, ln_b, w_in_b, TOKEN_TILE)

    bcast = lambda s: jnp.broadcast_to(s, (bp,) + s.shape[1:])
    a_p, b_p, nsa_p, nsb_p, nsh_p = _mixer_seq(glu, bx, gate, bcast(sa_m), bcast(sb_m), bcast(sh_m),
                                               mix_w, bp, tp, SEQ_TILE)

    nat = lambda v: jnp.swapaxes(v[n_p:].reshape(ts, bs, width), 0, 1)
    wat, wa2 = _shifted_taps(conv_a_w[0], ts, ka - 1)
    wbt, wb2 = _shifted_taps(conv_b_w[0], ts, kb - 1)
    a_s, b_s, nsa_s, nsb_s, nsh_s = _mixer_step(
        glu, bx, gate, n_p, nat(glu), nat(bx), state_conv_a[0], state_conv_b[0], state_lru[0],
        (wat, wa2, wbt, wb2), mix_w, bs, ts, STEP_BATCH)

    w_r = jnp.concatenate([w_router_group[0], w_router_expert[0]], axis=1)
    w_r = jnp.pad(w_r, ((0, 0), (0, ROUTE_LANES - w_r.shape[1])))
    wr_hi = w_r.astype(bf16)
    wr_lo = (w_r - wr_hi.astype(f32)).astype(bf16)
    b_r = jnp.concatenate([b_router_group[0], b_router_expert[0]])
    b_r = jnp.pad(b_r, (0, ROUTE_LANES - b_r.shape[0])).reshape(1, ROUTE_LANES)
    h1, route, cnt = _outproj(a_p, b_p, a_s, b_s, x_p, x_s, ln_g, ln_b, w_out_b,
                              row(ln1_g[0]), row(ln1_b[0]), wr_hi, wr_lo, b_r, alpha, TOKEN_TILE)

    blk = EXPERT_ROWS
    e_idx = route[:, 0:TOP_K].astype(i32)
    rank = route[:, 2:2 + TOP_K].astype(i32)
    counts = cnt[0, N_GROUPS:N_GROUPS + n_e].astype(i32)
    nblk = (counts + blk - 1) // blk
    pad_end = jnp.cumsum(nblk * blk)
    pad_start = pad_end - nblk * blk
    dest = (pad_start[e_idx] + rank).reshape(-1)
    p_max = _round_up(TOP_K * n + n_e * (blk - 1), blk)
    rows_tok = jnp.zeros((p_max,), i32).at[dest].set(jnp.repeat(jnp.arange(n, dtype=i32), TOP_K))
    eff = lax.cummax(jnp.where(nblk > 0, jnp.arange(n_e, dtype=i32), 0))

    ys = _experts(h1, rows_tok, pad_start.astype(i32), nblk, eff, w_gate[0], w_up[0], w_down[0],
                  p_max, blk)
    y_p, y_s = _combine(dest, h1, route, ys, row(ln2_g[0]), row(ln2_b[0]), n_p, alpha, COMBINE_TILE)

    y_prompt = y_p.reshape(bp, tp, d)
    y_sample = jnp.swapaxes(y_s.reshape(ts, bs, d), 0, 1)
    return (y_prompt, y_sample,
            nsa_p[:, ha - (ka - 1):][None], nsb_p[:, hb - (kb - 1):][None], nsh_p.reshape(1, bp, width),
            nsa_s[None], nsb_s[None], nsh_s[None])
```

```python
import functools

import jax
import jax.numpy as jnp
from jax import lax
from jax.experimental import pallas as pl
from jax.experimental.pallas import tpu as pltpu

f32 = jnp.float32
bf16 = jnp.bfloat16
i32 = jnp.int32

CONV_HEADS_A = 8
LRU_HEADS = 8
LRU_C = 8.0
N_GROUPS = 8
EXPERTS_PER_GROUP = 8
TOP_K = 2
LN_EPS = 1e-5

SUBLANES = 8
LANES = 128
BF16_ROWS = 16
VMEM_LIMIT = 56 * 1024 * 1024

TOKEN_TILE = 256
SEQ_TILE = 256
CONV_ROWS = 16
STEP_BATCH = 32
EXPERT_ROWS = 128
COMBINE_TILE = 128
ROUTE_LANES = 128
NEG_BIG = -1e30


def _round_up(x, m):
    return (x + m - 1) // m * m


def _layer_norm(x, g, b):
    mu = jnp.mean(x, axis=-1, keepdims=True)
    xc = x - mu
    var = jnp.mean(xc * xc, axis=-1, keepdims=True)
    return xc * lax.rsqrt(var + LN_EPS) * g + b


def _group_norm_silu(x, g, b, group):
    parts = []
    for c in range(x.shape[1] // group):
        xg = x[:, c * group:(c + 1) * group]
        mu = jnp.mean(xg, axis=-1, keepdims=True)
        xc = xg - mu
        var = jnp.mean(xc * xc, axis=-1, keepdims=True)
        parts.append(xc * lax.rsqrt(var + LN_EPS))
    y = jnp.concatenate(parts, axis=-1) * g + b
    return y * jax.nn.sigmoid(y)


def _lru_gates(cb, wg_ref, brg, big, lam):
    heads, hd, _ = wg_ref.shape
    cbb = cb.astype(bf16)
    rs, iz = [], []
    for h in range(heads):
        z = jnp.dot(cbb[:, h * hd:(h + 1) * hd], wg_ref[h], preferred_element_type=f32)
        rs.append(z[:, :hd])
        iz.append(z[:, hd:])
    r = jax.nn.sigmoid(jnp.concatenate(rs, axis=-1) + brg)
    i = jax.nn.sigmoid(jnp.concatenate(iz, axis=-1) + big)
    log_a = -LRU_C * r * jax.nn.softplus(-lam)
    a = jnp.exp(log_a)
    u = jnp.sqrt(-jnp.tanh(log_a) * (a * a + 1.0)) * (i * cb)
    return a, u


def _inproj_kernel(xa_ref, xb_ref, g_ref, b_ref, w_ref, glu_ref, bx_ref, gate_ref, *, n_a_tiles):
    i = pl.program_id(0)
    x = jnp.where(i < n_a_tiles, xa_ref[...], xb_ref[...])
    xn = _layer_norm(x, g_ref[...], b_ref[...]).astype(bf16)
    width = glu_ref.shape[1]
    ch = min(width, 512)
    for c in range(width // ch):
        lo = c * ch
        av = jnp.dot(xn, w_ref[:, lo:lo + ch], preferred_element_type=f32)
        ag = jnp.dot(xn, w_ref[:, width + lo:width + lo + ch], preferred_element_type=f32)
        glu_ref[:, lo:lo + ch] = av * jax.nn.sigmoid(ag)
        bx_ref[:, lo:lo + ch] = jnp.dot(xn, w_ref[:, 2 * width + lo:2 * width + lo + ch],
                                        preferred_element_type=f32)
        bg = jnp.dot(xn, w_ref[:, 3 * width + lo:3 * width + lo + ch], preferred_element_type=f32)
        gate_ref[:, lo:lo + ch] = jax.nn.gelu(bg)


def _inproj(xa, xb, n_b_rows, ln_g, ln_b, w_in_b, tm):
    na, d = xa.shape
    width = w_in_b.shape[1] // 4
    n_a_tiles = na // tm
    n_tiles = (na + n_b_rows) // tm
    n_b_tiles = max(n_tiles - n_a_tiles, 1)
    row_out = pl.BlockSpec((tm, width), lambda i: (i, 0))
    return pl.pallas_call(
        functools.partial(_inproj_kernel, n_a_tiles=n_a_tiles),
        grid=(n_tiles,),
        in_specs=[
            pl.BlockSpec((tm, d), lambda i: (jnp.minimum(i, n_a_tiles - 1), 0)),
            pl.BlockSpec((tm, d), lambda i: (jnp.clip(i - n_a_tiles, 0, n_b_tiles - 1), 0)),
            pl.BlockSpec((1, d), lambda i: (0, 0)),
            pl.BlockSpec((1, d), lambda i: (0, 0)),
            pl.BlockSpec((d, 4 * width), lambda i: (0, 0), pipeline_mode=pl.Buffered(1)),
        ],
        out_specs=[row_out, row_out, row_out],
        out_shape=[jax.ShapeDtypeStruct((na + n_b_rows, width), f32)] * 3,
        compiler_params=pltpu.CompilerParams(dimension_semantics=("arbitrary",),
                                             vmem_limit_bytes=VMEM_LIMIT),
        name="inproj",
    )(xa, xb, ln_g, ln_b, w_in_b)


def _mixer_seq_kernel(glu_ref, bx_ref, gate_ref, sa_ref, sb_ref, sh_ref,
                      wa_ref, ba_ref, ng_ref, nb_ref, wb_ref, bb_ref, wg_ref, brg_ref, big_ref, lam_ref,
                      aout_ref, bout_ref, nsa_ref, nsb_ref, nsh_ref,
                      wina, winb, xsa, xsb, cb_s, a_s, u_s, hcar, *, rows):
    tau = pl.program_id(1)
    tt, width = glu_ref.shape
    ka, kb = wa_ref.shape[0], wb_ref.shape[0]
    ha, hb = sa_ref.shape[0], sb_ref.shape[0]

    @pl.when(tau == 0)
    def _():
        wina[0:ha, :] = sa_ref[...]
        winb[0:hb, :] = sb_ref[...]
        hcar[...] = jnp.broadcast_to(sh_ref[...], hcar.shape)

    wina[ha:ha + tt, :] = glu_ref[...]
    winb[hb:hb + tt, :] = bx_ref[...]
    for s in sorted({(ha - (ka - 1) + k) % SUBLANES for k in range(ka)} - {0}):
        xsa[s - 1] = wina[pl.ds(s, tt + ha - SUBLANES), :]
    for s in sorted({(hb - (kb - 1) + k) % SUBLANES for k in range(kb)} - {0}):
        xsb[s - 1] = winb[pl.ds(s, tt + hb - SUBLANES), :]

    def tap(win, xs, off, r0):
        q, s = divmod(off, SUBLANES)
        src = win if s == 0 else xs.at[s - 1]
        return src[pl.ds(r0 + SUBLANES * q, rows), :]

    def conv_block(rb, carry):
        r0 = pl.multiple_of(rb * rows, rows)
        acc = jnp.broadcast_to(ba_ref[...], (rows, width))
        for k in range(ka):
            acc = acc + wa_ref[k:k + 1, :] * tap(wina, xsa, ha - (ka - 1) + k, r0)
        a_out = _group_norm_silu(acc, ng_ref[...], nb_ref[...], width // CONV_HEADS_A)
        aout_ref[pl.ds(r0, rows), :] = a_out.astype(aout_ref.dtype)
        accb = jnp.broadcast_to(bb_ref[...], (rows, width))
        for k in range(kb):
            accb = accb + wb_ref[k:k + 1, :] * tap(winb, xsb, hb - (kb - 1) + k, r0)
        cb_s[pl.ds(r0, rows), :] = accb
        return carry

    lax.fori_loop(0, tt // rows, conv_block, 0)

    a, u = _lru_gates(cb_s[...], wg_ref, brg_ref[...], big_ref[...], lam_ref[...])
    row8 = lax.broadcasted_iota(i32, (tt, width), 0) & (SUBLANES - 1)
    d = 1
    while d < SUBLANES:
        m = row8 >= d
        a_sh = jnp.where(m, pltpu.roll(a, d, axis=0), 1.0)
        u_sh = jnp.where(m, pltpu.roll(u, d, axis=0), 0.0)
        u = a * u_sh + u
        a = a * a_sh
        d *= 2
    a_s[...] = a
    u_s[...] = u

    def scan_block(j, h):
        r0 = pl.multiple_of(j * SUBLANES, SUBLANES)
        hblk = a_s[pl.ds(r0, SUBLANES), :] * h + u_s[pl.ds(r0, SUBLANES), :]
        u_s[pl.ds(r0, SUBLANES), :] = hblk
        return jnp.broadcast_to(hblk[SUBLANES - 1:SUBLANES, :], h.shape)

    h_last = lax.fori_loop(0, tt // SUBLANES, scan_block, hcar[...])
    hcar[...] = h_last
    bout_ref[...] = (u_s[...] * gate_ref[...]).astype(bout_ref.dtype)

    tail_a = wina[tt:tt + ha, :]
    tail_b = winb[tt:tt + hb, :]
    wina[0:ha, :] = tail_a
    winb[0:hb, :] = tail_b
    nsa_ref[...] = tail_a
    nsb_ref[...] = tail_b
    nsh_ref[...] = h_last[0:1, :]


def _mixer_seq(glu, bx, gate, sa_pad, sb_pad, sh, mix_w, n_seq, seq_len, tt):
    wa, ba, ng, nb, wb, bb, wg, brg, big, lam = mix_w
    width = glu.shape[1]
    ha, hb = sa_pad.shape[1], sb_pad.shape[1]
    n_t = seq_len // tt
    rows = min(CONV_ROWS, tt)
    row_in = pl.BlockSpec((tt, width), lambda b, t: (b * n_t + t, 0))

    def const(shape):
        return pl.BlockSpec(shape, lambda b, t: (0,) * len(shape))

    def per_seq(r):
        return pl.BlockSpec((None, r, width), lambda b, t: (b, 0, 0))

    return pl.pallas_call(
        functools.partial(_mixer_seq_kernel, rows=rows),
        grid=(n_seq, n_t),
        in_specs=[row_in, row_in, row_in, per_seq(ha), per_seq(hb), per_seq(1),
                  const(wa.shape), const(ba.shape), const(ng.shape), const(nb.shape),
                  const(wb.shape), const(bb.shape), const(wg.shape), const(brg.shape),
                  const(big.shape), const(lam.shape)],
        out_specs=[row_in, row_in, per_seq(ha), per_seq(hb), per_seq(1)],
        out_shape=[jax.ShapeDtypeStruct((n_seq * seq_len, width), bf16),
                   jax.ShapeDtypeStruct((n_seq * seq_len, width), bf16),
                   jax.ShapeDtypeStruct((n_seq, ha, width), f32),
                   jax.ShapeDtypeStruct((n_seq, hb, width), f32),
                   jax.ShapeDtypeStruct((n_seq, 1, width), f32)],
        scratch_shapes=[pltpu.VMEM((ha + tt, width), f32),
                        pltpu.VMEM((hb + tt, width), f32),
                        pltpu.VMEM((SUBLANES - 1, tt + ha - SUBLANES, width), f32),
                        pltpu.VMEM((SUBLANES - 1, tt + hb - SUBLANES, width), f32),
                        pltpu.VMEM((tt, width), f32),
                        pltpu.VMEM((tt, width), f32),
                        pltpu.VMEM((tt, width), f32),
                        pltpu.VMEM((SUBLANES, width), f32)],
        compiler_params=pltpu.CompilerParams(dimension_semantics=("arbitrary", "arbitrary"),
                                             vmem_limit_bytes=VMEM_LIMIT),
        name="mixer_seq",
    )(glu, bx, gate, sa_pad, sb_pad, sh, wa, ba, ng, nb, wb, bb, wg, brg, big, lam)


def _mixer_step_kernel(glu_ref, bx_ref, gate_ref, glun_ref, bxn_ref, sa_ref, sb_ref, sh_ref,
                       wat_ref, wa2_ref, wbt_ref, wb2_ref,
                       ba_ref, ng_ref, nb_ref, bb_ref, wg_ref, brg_ref, big_ref, lam_ref,
                       aout_ref, bout_ref, nsa_ref, nsb_ref, nsh_ref,
                       hista, histb, hcar):
    t = pl.program_id(1)
    n_t = hista.shape[0]
    width = glu_ref.shape[1]
    ka1, kb1 = sa_ref.shape[1], sb_ref.shape[1]

    @pl.when(t == 0)
    def _():
        hista[...] = jnp.zeros_like(hista)
        histb[...] = jnp.zeros_like(histb)
        hcar[...] = sh_ref[...]
        nsa_ref[:, 0:ka1 - n_t, :] = sa_ref[:, n_t:ka1, :]
        nsa_ref[:, ka1 - n_t:ka1, :] = glun_ref[...]
        nsb_ref[...] = bxn_ref[:, n_t - kb1:n_t, :]

    hista[t] = glu_ref[...]
    histb[t] = bx_ref[...]

    ca = ba_ref[...] + jnp.sum(sa_ref[...] * wat_ref[...][None], axis=1)
    cb = bb_ref[...] + jnp.sum(sb_ref[...] * wbt_ref[...][None], axis=1)
    for j in range(n_t):
        ca = ca + wa2_ref[j:j + 1, :] * hista[j]
        cb = cb + wb2_ref[j:j + 1, :] * histb[j]
    a_out = _group_norm_silu(ca, ng_ref[...], nb_ref[...], width // CONV_HEADS_A)
    aout_ref[...] = a_out.astype(aout_ref.dtype)

    a, u = _lru_gates(cb, wg_ref, brg_ref[...], big_ref[...], lam_ref[...])
    h = a * hcar[...] + u
    hcar[...] = h
    nsh_ref[...] = h
    bout_ref[...] = (h * gate_ref[...]).astype(bout_ref.dtype)


def _mixer_step(glu, bx, gate, row_off, glu_nat, bx_nat, sa, sb, sh, step_w, mix_w, n_seq, n_t, bt):
    wat, wa2, wbt, wb2 = step_w
    wa, ba, ng, nb, wb, bb, wg, brg, big, lam = mix_w
    width = glu.shape[1]
    ka1, kb1 = sa.shape[1], sb.shape[1]
    off = row_off // bt
    nb_tiles = n_seq // bt
    row_in = pl.BlockSpec((bt, width), lambda j, t: (off + t * nb_tiles + j, 0))
    row_out = pl.BlockSpec((bt, width), lambda j, t: (t * nb_tiles + j, 0))

    def const(shape):
        return pl.BlockSpec(shape, lambda j, t: (0,) * len(shape))

    def per_t(arr):
        return pl.BlockSpec((None,) + arr.shape[1:], lambda j, t: (t, 0, 0))

    def per_b(r):
        return pl.BlockSpec((bt, r, width), lambda j, t: (j, 0, 0))

    state_h = pl.BlockSpec((bt, width), lambda j, t: (j, 0))
    return pl.pallas_call(
        _mixer_step_kernel,
        grid=(nb_tiles, n_t),
        in_specs=[row_in, row_in, row_in, per_b(n_t), per_b(n_t), per_b(ka1), per_b(kb1), state_h,
                  per_t(wat), per_t(wa2), per_t(wbt), per_t(wb2),
                  const(ba.shape), const(ng.shape), const(nb.shape), const(bb.shape),
                  const(wg.shape), const(brg.shape), const(big.shape), const(lam.shape)],
        out_specs=[row_out, row_out, per_b(ka1), per_b(kb1), state_h],
        out_shape=[jax.ShapeDtypeStruct((n_seq * n_t, width), bf16),
                   jax.ShapeDtypeStruct((n_seq * n_t, width), bf16),
                   jax.ShapeDtypeStruct((n_seq, ka1, width), f32),
                   jax.ShapeDtypeStruct((n_seq, kb1, width), f32),
                   jax.ShapeDtypeStruct((n_seq, width), f32)],
        scratch_shapes=[pltpu.VMEM((n_t, bt, width), f32),
                        pltpu.VMEM((n_t, bt, width), f32),
                        pltpu.VMEM((bt, width), f32)],
        compiler_params=pltpu.CompilerParams(dimension_semantics=("arbitrary", "arbitrary"),
                                             vmem_limit_bytes=VMEM_LIMIT),
        name="mixer_step",
    )(glu, bx, gate, glu_nat, bx_nat, sa, sb, sh, wat, wa2, wbt, wb2,
      ba, ng, nb, bb, wg, brg, big, lam)


def _outproj_kernel(ap_ref, bp_ref, as_ref, bs_ref, xp_ref, xs_ref, ling_ref, linb_ref, wo_ref,
                    l1g_ref, l1b_ref, wrh_ref, wrl_ref, br_ref,
                    h1_ref, route_ref, cnt_ref, carry, *, n_p_tiles, alpha):
    i = pl.program_id(0)
    is_p = i < n_p_tiles
    a = jnp.where(is_p, ap_ref[...], as_ref[...])
    b = jnp.where(is_p, bp_ref[...], bs_ref[...])
    x = jnp.where(is_p, xp_ref[...], xs_ref[...])
    width = a.shape[1]
    mix = (jnp.dot(a, wo_ref[0:width, :], preferred_element_type=f32)
           + jnp.dot(b, wo_ref[width:2 * width, :], preferred_element_type=f32))
    h0 = _layer_norm(x, ling_ref[...], linb_ref[...])
    h1 = _layer_norm(alpha * h0 + mix, l1g_ref[...], l1b_ref[...])
    h1_ref[...] = h1

    xh = h1.astype(bf16)
    xl = (h1 - xh.astype(f32)).astype(bf16)
    z = (jnp.dot(xh, wrh_ref[...], preferred_element_type=f32)
         + jnp.dot(xl, wrh_ref[...], preferred_element_type=f32)
         + jnp.dot(xh, wrl_ref[...], preferred_element_type=f32)) + br_ref[...]
    tm = z.shape[0]
    lane = lax.broadcasted_iota(i32, z.shape, 1).astype(f32)
    n_g, epg = float(N_GROUPS), float(EXPERTS_PER_GROUP)
    far = float(2 * ROUTE_LANES)

    gm = lane < n_g
    zg = jnp.where(gm, z, NEG_BIG)
    pg = jnp.where(gm, jnp.exp(zg - jnp.max(zg, axis=-1, keepdims=True)), 0.0)
    pg = pg / jnp.sum(pg, axis=-1, keepdims=True)
    g_top = jnp.max(pg, axis=-1, keepdims=True)
    g_idx = jnp.min(jnp.where(gm & (pg == g_top), lane, far), axis=-1, keepdims=True)

    lo = n_g + g_idx * epg
    em = (lane >= lo) & (lane < lo + epg)
    ze = jnp.where(em, z, NEG_BIG)
    pe = jnp.where(em, jnp.exp(ze - jnp.max(ze, axis=-1, keepdims=True)), 0.0)
    pe = pe / jnp.sum(pe, axis=-1, keepdims=True)
    v1 = jnp.max(jnp.where(em, pe, -1.0), axis=-1, keepdims=True)
    l1 = jnp.min(jnp.where(em & (pe == v1), lane, far), axis=-1, keepdims=True)
    pe2 = jnp.where(em & (lane != l1), pe, -1.0)
    v2 = jnp.max(pe2, axis=-1, keepdims=True)
    l2 = jnp.min(jnp.where(pe2 == v2, lane, far), axis=-1, keepdims=True)
    den = v1 + v2
    gate1 = v1 / den * g_top
    gate2 = v2 / den * g_top

    @pl.when(i == 0)
    def _():
        carry[...] = jnp.zeros_like(carry)

    o1 = lane == l1
    o2 = lane == l2
    onehot = jnp.where(o1 | o2, 1.0, 0.0)
    ri = lax.broadcasted_iota(i32, (tm, tm), 0)
    ci = lax.broadcasted_iota(i32, (tm, tm), 1)
    tri = jnp.where(ci < ri, 1.0, 0.0).astype(bf16)
    prefix = jnp.dot(tri, onehot.astype(bf16), preferred_element_type=f32) + carry[0:1, :]
    rank1 = jnp.sum(jnp.where(o1, prefix, 0.0), axis=-1, keepdims=True)
    rank2 = jnp.sum(jnp.where(o2, prefix, 0.0), axis=-1, keepdims=True)
    new_carry = carry[...] + jnp.sum(onehot, axis=0, keepdims=True)
    carry[...] = new_carry
    cnt_ref[...] = new_carry

    route = jnp.where(lane == 0.0, l1 - n_g, 0.0)
    route = jnp.where(lane == 1.0, l2 - n_g, route)
    route = jnp.where(lane == 2.0, rank1, route)
    route = jnp.where(lane == 3.0, rank2, route)
    route = jnp.where(lane == 4.0, gate1, route)
    route = jnp.where(lane == 5.0, gate2, route)
    route_ref[...] = route


def _outproj(a_p, b_p, a_s, b_s, x_p, x_s, ln_in_g, ln_in_b, w_out_b, ln1_g, ln1_b,
             wr_hi, wr_lo, b_r, alpha, tm):
    n_p, width = a_p.shape
    n_s = a_s.shape[0]
    d = x_p.shape[1]
    n_p_tiles = n_p // tm
    n_s_tiles = n_s // tm
    n_tiles = n_p_tiles + n_s_tiles

    def p_map(i):
        return (jnp.minimum(i, n_p_tiles - 1), 0)

    def s_map(i):
        return (jnp.clip(i - n_p_tiles, 0, n_s_tiles - 1), 0)

    def const(shape, **kw):
        return pl.BlockSpec(shape, lambda i: (0,) * len(shape), **kw)

    return pl.pallas_call(
        functools.partial(_outproj_kernel, n_p_tiles=n_p_tiles, alpha=alpha),
        grid=(n_tiles,),
        in_specs=[pl.BlockSpec((tm, width), p_map), pl.BlockSpec((tm, width), p_map),
                  pl.BlockSpec((tm, width), s_map), pl.BlockSpec((tm, width), s_map),
                  pl.BlockSpec((tm, d), p_map), pl.BlockSpec((tm, d), s_map),
                  const((1, d)), const((1, d)),
                  const((2 * width, d), pipeline_mode=pl.Buffered(1)),
                  const((1, d)), const((1, d)),
                  const((d, ROUTE_LANES)), const((d, ROUTE_LANES)), const((1, ROUTE_LANES))],
        out_specs=[pl.BlockSpec((tm, d), lambda i: (i, 0)),
                   pl.BlockSpec((tm, ROUTE_LANES), lambda i: (i, 0)),
                   const((SUBLANES, ROUTE_LANES))],
        out_shape=[jax.ShapeDtypeStruct((n_p + n_s, d), f32),
                   jax.ShapeDtypeStruct((n_p + n_s, ROUTE_LANES), f32),
                   jax.ShapeDtypeStruct((SUBLANES, ROUTE_LANES), f32)],
        scratch_shapes=[pltpu.VMEM((SUBLANES, ROUTE_LANES), f32)],
        compiler_params=pltpu.CompilerParams(dimension_semantics=("arbitrary",),
                                             vmem_limit_bytes=VMEM_LIMIT),
        name="outproj_router",
    )(a_p, b_p, a_s, b_s, x_p, x_s, ln_in_g, ln_in_b, w_out_b, ln1_g, ln1_b, wr_hi, wr_lo, b_r)


def _expert_kernel(src_ref, dst_ref, start_ref, nblk_ref, eff_ref,
                   h1_hbm, wg_ref, wu_ref, wd_ref, y2_hbm,
                   xbuf, ybuf, wgb, wub, wdb, gsem, osem, *, blk, n_rows):
    del eff_ref
    e = pl.program_id(0)
    n_e = pl.num_programs(0)
    nb = nblk_ref[e]
    g0 = start_ref[e] // blk

    def gather(g, slot):
        for r in range(blk):
            tok = src_ref[g * blk + r]
            pltpu.make_async_copy(h1_hbm.at[pl.ds(tok, 1), :],
                                  xbuf.at[slot, pl.ds(r, 1), :], gsem.at[slot]).start()

    def scatter(g, slot):
        for r in range(blk):
            row = dst_ref[(g + 1) * blk + r]
            pltpu.make_async_copy(ybuf.at[slot, pl.ds(r, 1), :],
                                  y2_hbm.at[pl.ds(row, 1), :], osem.at[slot]).start()

    def dump_copy(slot):
        dst = y2_hbm.at[pl.ds(n_rows + slot * blk, blk), :]
        return pltpu.make_async_copy(ybuf.at[slot], dst, osem.at[slot])

    def wait_gather(slot):
        pltpu.make_async_copy(h1_hbm.at[pl.ds(0, blk), :], xbuf.at[slot], gsem.at[slot]).wait()

    @pl.when(e == 0)
    def _():
        ybuf[...] = jnp.zeros(ybuf.shape, ybuf.dtype)
        dump_copy(0).start()
        gather(0, 0)

    @pl.when(nb > 0)
    def _():
        wgb[...] = wg_ref[...].astype(bf16)
        wub[...] = wu_ref[...].astype(bf16)
        wdb[...] = wd_ref[...].astype(bf16)

        def chunk(c, carry):
            g = g0 + c
            slot = g % 2
            wait_gather(slot)
            dump_copy(slot).wait()
            gather(g + 1, 1 - slot)
            scatter(g - 1, 1 - slot)
            x = xbuf[slot].astype(bf16)
            hg = jnp.dot(x, wgb[...], preferred_element_type=f32)
            hu = jnp.dot(x, wub[...], preferred_element_type=f32)
            hid = (hg * jax.nn.sigmoid(hg) * hu).astype(bf16)
            ybuf[slot] = jnp.dot(hid, wdb[...], preferred_element_type=f32)
            return carry

        lax.fori_loop(0, nb, chunk, 0)

    @pl.when(e == n_e - 1)
    def _():
        g_end = g0 + nb
        last = (g_end + 1) % 2
        scatter(g_end - 1, last)
        dump_copy(1 - last).wait()
        dump_copy(last).wait()
        wait_gather(1 - last)


def _experts(h1, src_rows, dst_rows, pad_start, nblk, eff, w_gate, w_up, w_down, blk):
    n, d = h1.shape
    n_e, _, d_e = w_gate.shape
    n_rows = TOP_K * n
    grid_spec = pltpu.PrefetchScalarGridSpec(
        num_scalar_prefetch=5,
        grid=(n_e,),
        in_specs=[pl.BlockSpec(memory_space=pl.ANY),
                  pl.BlockSpec((None, d, d_e), lambda e, sr, ds, s, nbk, ef: (ef[e], 0, 0)),
                  pl.BlockSpec((None, d, d_e), lambda e, sr, ds, s, nbk, ef: (ef[e], 0, 0)),
                  pl.BlockSpec((None, d_e, d), lambda e, sr, ds, s, nbk, ef: (ef[e], 0, 0))],
        out_specs=pl.BlockSpec(memory_space=pl.ANY),
        scratch_shapes=[pltpu.VMEM((2, blk, d), f32),
                        pltpu.VMEM((2, blk, d), f32),
                        pltpu.VMEM((d, d_e), bf16),
                        pltpu.VMEM((d, d_e), bf16),
                        pltpu.VMEM((d_e, d), bf16),
                        pltpu.SemaphoreType.DMA((2,)),
                        pltpu.SemaphoreType.DMA((2,))],
    )
    return pl.pallas_call(
        functools.partial(_expert_kernel, blk=blk, n_rows=n_rows),
        grid_spec=grid_spec,
        out_shape=jax.ShapeDtypeStruct((n_rows + 2 * blk, d), f32),
        compiler_params=pltpu.CompilerParams(dimension_semantics=("arbitrary",),
                                             vmem_limit_bytes=VMEM_LIMIT),
        name="experts",
    )(src_rows, dst_rows, pad_start, nblk, eff, h1, w_gate, w_up, w_down)


def _combine_kernel(h1_ref, route_ref, ya_ref, yb_ref, l2g_ref, l2b_ref, yp_ref, ysm_ref,
                    *, n_p_tiles, alpha):
    i = pl.program_id(0)
    route = route_ref[...]
    y = ya_ref[...] * route[:, 4:5] + yb_ref[...] * route[:, 5:6]
    out = _layer_norm(alpha * h1_ref[...] + y, l2g_ref[...], l2b_ref[...])

    @pl.when(i < n_p_tiles)
    def _():
        yp_ref[...] = out

    @pl.when(i >= n_p_tiles)
    def _():
        ysm_ref[...] = out


def _combine(h1, route, y2, ln2_g, ln2_b, n_p, alpha, tm):
    n, d = h1.shape
    n_s = n - n_p
    n_tiles = n // tm
    n_p_tiles = n_p // tm
    n_s_tiles = n_s // tm
    return pl.pallas_call(
        functools.partial(_combine_kernel, n_p_tiles=n_p_tiles, alpha=alpha),
        grid=(n_tiles,),
        in_specs=[pl.BlockSpec((tm, d), lambda i: (i, 0)),
                  pl.BlockSpec((tm, ROUTE_LANES), lambda i: (i, 0)),
                  pl.BlockSpec((tm, d), lambda i: (i, 0)),
                  pl.BlockSpec((tm, d), lambda i: (n_tiles + i, 0)),
                  pl.BlockSpec((1, d), lambda i: (0, 0)),
                  pl.BlockSpec((1, d), lambda i: (0, 0))],
        out_specs=[pl.BlockSpec((tm, d), lambda i: (jnp.minimum(i, n_p_tiles - 1), 0)),
                   pl.BlockSpec((tm, d), lambda i: (jnp.clip(i - n_p_tiles, 0, n_s_tiles - 1), 0))],
        out_shape=[jax.ShapeDtypeStruct((n_p, d), f32), jax.ShapeDtypeStruct((n_s, d), f32)],
        compiler_params=pltpu.CompilerParams(dimension_semantics=("arbitrary",),
                                             vmem_limit_bytes=VMEM_LIMIT),
        name="combine_ln2",
    )(h1, route, y2, y2, ln2_g, ln2_b)


def _shifted_taps(w, n_t, hist):
    k = w.shape[0]
    zero = jnp.zeros_like(w[0])
    wt = jnp.stack([jnp.stack([w[r - t] if 0 <= r - t < k else zero for r in range(hist)])
                    for t in range(n_t)])
    w2 = jnp.stack([jnp.stack([w[hist - t + j] if (j <= t and 0 <= hist - t + j < k) else zero
                               for j in range(n_t)]) for t in range(n_t)])
    return wt, w2


def kernel(x_prompt, x_sample, state_conv_a, state_conv_b, state_lru, meta_tokens, ln_in_g, ln_in_b, w_in, conv_a_w, conv_a_b, norm_a_g, norm_a_b, conv_b_w, conv_b_b, w_rg, b_rg, w_ig, b_ig, lru_lambda, w_out, ln1_g, ln1_b, w_router_group, b_router_group, w_router_expert, b_router_expert, w_gate, w_up, w_down, ln2_g, ln2_b):
    depth = w_in.shape[0]
    assert depth == 1, "single-layer trunk only"
    bp, tp, d = x_prompt.shape
    bs, ts, _ = x_sample.shape
    n_meta = meta_tokens.shape[0]
    width = conv_a_w.shape[2]
    ka, kb = conv_a_w.shape[1], conv_b_w.shape[1]
    n_e = w_gate.shape[1]
    alpha = (2.0 * depth) ** 0.25
    n_p, n_s = bp * tp, bs * ts
    n = n_p + n_s
    assert ts < ka and ts >= kb - 1
    assert n_p % TOKEN_TILE == 0 and n_s % TOKEN_TILE == 0 and tp % SEQ_TILE == 0
    assert bs % STEP_BATCH == 0 and n_p % STEP_BATCH == 0
    assert n_p % COMBINE_TILE == 0 and n_s % COMBINE_TILE == 0
    assert n_meta % BF16_ROWS == 0
    assert N_GROUPS * (1 + EXPERTS_PER_GROUP) <= ROUTE_LANES and n_e == N_GROUPS * EXPERTS_PER_GROUP

    row = lambda v: v.reshape(1, -1).astype(f32)
    ln_g, ln_b = row(ln_in_g), row(ln_in_b)
    w_in_b = w_in[0].astype(bf16)
    w_out_b = w_out[0].astype(bf16)
    mix_w = (conv_a_w[0], row(conv_a_b[0]), row(norm_a_g[0]), row(norm_a_b[0]),
             conv_b_w[0], row(conv_b_b[0]),
             jnp.concatenate([w_rg[0], w_ig[0]], axis=-1).astype(bf16),
             row(b_rg[0]), row(b_ig[0]), row(lru_lambda[0]))
    ha, hb = _round_up(ka - 1, SUBLANES), _round_up(kb - 1, SUBLANES)

    def pad_hist(s, h):
        return jnp.concatenate([jnp.zeros(s.shape[:1] + (h - s.shape[1], width), f32), s], axis=1)

    glu_m, bx_m, gate_m = _inproj(meta_tokens, meta_tokens, 0, ln_g, ln_b, w_in_b, n_meta)
    _, _, sa_m, sb_m, sh_m = _mixer_seq(
        glu_m, bx_m, gate_m, jnp.zeros((1, ha, width), f32), jnp.zeros((1, hb, width), f32),
        jnp.zeros((1, 1, width), f32), mix_w, 1, n_meta, n_meta)

    x_p = x_prompt.reshape(n_p, d)
    x_s = jnp.swapaxes(x_sample, 0, 1).reshape(n_s, d)
    glu, bx, gate = _inproj(x_p, x_s, n_s, ln_g, ln_b, w_in_b, TOKEN_TILE)

    bcast = lambda s: jnp.broadcast_to(s, (bp,) + s.shape[1:])
    a_p, b_p, nsa_p, nsb_p, nsh_p = _mixer_seq(glu, bx, gate, bcast(sa_m), bcast(sb_m), bcast(sh_m),
                                               mix_w, bp, tp, SEQ_TILE)

    nat = lambda v: jnp.swapaxes(v[n_p:].reshape(ts, bs, width), 0, 1)
    wat, wa2 = _shifted_taps(conv_a_w[0], ts, ka - 1)
    wbt, wb2 = _shifted_taps(conv_b_w[0], ts, kb - 1)
    a_s, b_s, nsa_s, nsb_s, nsh_s = _mixer_step(
        glu, bx, gate, n_p, nat(glu), nat(bx), state_conv_a[0], state_conv_b[0], state_lru[0],
        (wat, wa2, wbt, wb2), mix_w, bs, ts, STEP_BATCH)

    w_r = jnp.concatenate([w_router_group[0], w_router_expert[0]], axis=1)
    w_r = jnp.pad(w_r, ((0, 0), (0, ROUTE_LANES - w_r.shape[1])))
    wr_hi = w_r.astype(bf16)
    wr_lo = (w_r - wr_hi.astype(f32)).astype(bf16)
    b_r = jnp.concatenate([b_router_group[0], b_router_expert[0]])
    b_r = jnp.pad(b_r, (0, ROUTE_LANES - b_r.shape[0])).reshape(1, ROUTE_LANES)
    h1, route, cnt = _outproj(a_p, b_p, a_s, b_s, x_p, x_s, ln_g, ln_b, w_out_b,
                              row(ln1_g[0]), row(ln1_b[0]), wr_hi, wr_lo, b_r, alpha, TOKEN_TILE)

    blk = EXPERT_ROWS
    e_idx = route[:, 0:TOP_K].astype(i32)
    rank = route[:, 2:2 + TOP_K].astype(i32)
    counts = cnt[0, N_GROUPS:N_GROUPS + n_e].astype(i32)
    nblk = (counts + blk - 1) // blk
    pad_end = jnp.cumsum(nblk * blk)
    pad_start = pad_end - nblk * blk
    start_of = jnp.sum(jnp.where(e_idx[..., None] == jnp.arange(n_e, dtype=i32), pad_start, 0), axis=-1)
    dest = (start_of + rank).reshape(-1)
    p_max = _round_up(TOP_K * n + n_e * (blk - 1), blk)
    flat = jnp.full((p_max,), -1, i32).at[dest].set(jnp.arange(TOP_K * n, dtype=i32))
    pos = jnp.arange(p_max, dtype=i32)
    valid = flat >= 0
    src_rows = jnp.where(valid, flat // TOP_K, 0)
    dst_rows = jnp.where(valid, (flat % TOP_K) * n + flat // TOP_K, TOP_K * n + pos % (2 * blk))
    src_rows = jnp.concatenate([src_rows, jnp.zeros((blk,), i32)])
    dst_rows = jnp.concatenate([TOP_K * n + blk + jnp.arange(blk, dtype=i32), dst_rows])
    eff = lax.cummax(jnp.where(nblk > 0, jnp.arange(n_e, dtype=i32), 0))

    y2 = _experts(h1, src_rows, dst_rows, pad_start.astype(i32), nblk, eff,
                  w_gate[0], w_up[0], w_down[0], blk)
    y_p, y_s = _combine(h1, route, y2, row(ln2_g[0]), row(ln2_b[0]), n_p, alpha, TOKEN_TILE)

    y_prompt = y_p.reshape(bp, tp, d)
    y_sample = jnp.swapaxes(y_s.reshape(ts, bs, d), 0, 1)
    return (y_prompt, y_sample,
            nsa_p[:, ha - (ka - 1):][None], nsb_p[:, hb - (kb - 1):][None], nsh_p.reshape(1, bp, width),
            nsa_s[None], nsb_s[None], nsh_s[None])
```

```python
import functools

import jax
import jax.numpy as jnp
from jax import lax
from jax.experimental import pallas as pl
from jax.experimental.pallas import tpu as pltpu

f32 = jnp.float32
bf16 = jnp.bfloat16
i32 = jnp.int32

CONV_HEADS_A = 8
LRU_HEADS = 8
LRU_C = 8.0
N_GROUPS = 8
EXPERTS_PER_GROUP = 8
TOP_K = 2
LN_EPS = 1e-5

SUBLANES = 8
LANES = 128
BF16_ROWS = 16
VMEM_LIMIT = 56 * 1024 * 1024

TOKEN_TILE = 256
SEQ_TILE = 256
CONV_ROWS = 32
STEP_BATCH = 32
EXPERT_ROWS = 128
COMBINE_TILE = 128
ROUTE_LANES = 128
NEG_BIG = -1e30


def _round_up(x, m):
    return (x + m - 1) // m * m


def _layer_norm(x, g, b):
    mu = jnp.mean(x, axis=-1, keepdims=True)
    xc = x - mu
    var = jnp.mean(xc * xc, axis=-1, keepdims=True)
    return xc * lax.rsqrt(var + LN_EPS) * g + b


def _group_norm_silu(x, g, b, group):
    parts = []
    for c in range(x.shape[1] // group):
        xg = x[:, c * group:(c + 1) * group]
        mu = jnp.mean(xg, axis=-1, keepdims=True)
        xc = xg - mu
        var = jnp.mean(xc * xc, axis=-1, keepdims=True)
        parts.append(xc * lax.rsqrt(var + LN_EPS))
    y = jnp.concatenate(parts, axis=-1) * g + b
    return y * jax.nn.sigmoid(y)


def _lru_gates(cb, wg_ref, brg, big, lam):
    heads, hd, _ = wg_ref.shape
    cbb = cb.astype(bf16)
    rs, iz = [], []
    for h in range(heads):
        z = jnp.dot(cbb[:, h * hd:(h + 1) * hd], wg_ref[h], preferred_element_type=f32)
        rs.append(z[:, :hd])
        iz.append(z[:, hd:])
    r = jax.nn.sigmoid(jnp.concatenate(rs, axis=-1) + brg)
    i = jax.nn.sigmoid(jnp.concatenate(iz, axis=-1) + big)
    log_a = -LRU_C * r * jax.nn.softplus(-lam)
    a = jnp.exp(log_a)
    u = jnp.sqrt(-jnp.tanh(log_a) * (a * a + 1.0)) * (i * cb)
    return a, u


def _inproj_kernel(xa_ref, xb_ref, g_ref, b_ref, w_ref, glu_ref, bx_ref, gate_ref, *, n_a_tiles):
    i = pl.program_id(0)
    x = jnp.where(i < n_a_tiles, xa_ref[...], xb_ref[...])
    xn = _layer_norm(x, g_ref[...], b_ref[...]).astype(bf16)
    width = glu_ref.shape[1]
    ch = min(width, 512)
    for c in range(width // ch):
        lo = c * ch
        av = jnp.dot(xn, w_ref[:, lo:lo + ch], preferred_element_type=f32)
        ag = jnp.dot(xn, w_ref[:, width + lo:width + lo + ch], preferred_element_type=f32)
        glu_ref[:, lo:lo + ch] = av * jax.nn.sigmoid(ag)
        bx_ref[:, lo:lo + ch] = jnp.dot(xn, w_ref[:, 2 * width + lo:2 * width + lo + ch],
                                        preferred_element_type=f32)
        bg = jnp.dot(xn, w_ref[:, 3 * width + lo:3 * width + lo + ch], preferred_element_type=f32)
        gate_ref[:, lo:lo + ch] = jax.nn.gelu(bg)


def _inproj(xa, xb, n_b_rows, ln_g, ln_b, w_in_b, tm):
    na, d = xa.shape
    width = w_in_b.shape[1] // 4
    n_a_tiles = na // tm
    n_tiles = (na + n_b_rows) // tm
    n_b_tiles = max(n_tiles - n_a_tiles, 1)
    row_out = pl.BlockSpec((tm, width), lambda i: (i, 0))
    return pl.pallas_call(
        functools.partial(_inproj_kernel, n_a_tiles=n_a_tiles),
        grid=(n_tiles,),
        in_specs=[
            pl.BlockSpec((tm, d), lambda i: (jnp.minimum(i, n_a_tiles - 1), 0)),
            pl.BlockSpec((tm, d), lambda i: (jnp.clip(i - n_a_tiles, 0, n_b_tiles - 1), 0)),
            pl.BlockSpec((1, d), lambda i: (0, 0)),
            pl.BlockSpec((1, d), lambda i: (0, 0)),
            pl.BlockSpec((d, 4 * width), lambda i: (0, 0), pipeline_mode=pl.Buffered(1)),
        ],
        out_specs=[row_out, row_out, row_out],
        out_shape=[jax.ShapeDtypeStruct((na + n_b_rows, width), f32)] * 3,
        compiler_params=pltpu.CompilerParams(dimension_semantics=("arbitrary",),
                                             vmem_limit_bytes=VMEM_LIMIT),
        name="inproj",
    )(xa, xb, ln_g, ln_b, w_in_b)


def _mixer_seq_kernel(glu_ref, bx_ref, gate_ref, sa_ref, sb_ref, sh_ref,
                      wa_ref, ba_ref, ng_ref, nb_ref, wb_ref, bb_ref, wg_ref, brg_ref, big_ref, lam_ref,
                      aout_ref, bout_ref, nsa_ref, nsb_ref, nsh_ref,
                      wina, winb, xsa, xsb, ca_s, cb_s, a_s, u_s, hcar):
    tau = pl.program_id(1)
    tt, width = glu_ref.shape
    ka, kb = wa_ref.shape[0], wb_ref.shape[0]
    ha, hb = sa_ref.shape[0], sb_ref.shape[0]

    @pl.when(tau == 0)
    def _():
        wina[0:ha, :] = sa_ref[...]
        winb[0:hb, :] = sb_ref[...]
        hcar[...] = jnp.broadcast_to(sh_ref[...], hcar.shape)

    wina[ha:ha + tt, :] = glu_ref[...]
    winb[hb:hb + tt, :] = bx_ref[...]
    for s in sorted({(ha - (ka - 1) + k) % SUBLANES for k in range(ka)} - {0}):
        xsa[s - 1] = wina[pl.ds(s, tt + ha - SUBLANES), :]
    for s in sorted({(hb - (kb - 1) + k) % SUBLANES for k in range(kb)} - {0}):
        xsb[s - 1] = winb[pl.ds(s, tt + hb - SUBLANES), :]

    def tap(win, xs, off, r0, lanes):
        q, s = divmod(off, SUBLANES)
        src = win if s == 0 else xs.at[s - 1]
        return src[pl.ds(r0 + SUBLANES * q, SUBLANES), lanes]

    rows = min(CONV_ROWS, tt)
    for g in range(width // LANES):
        lanes = pl.ds(g * LANES, LANES)
        bcast = lambda ref, k: jnp.broadcast_to(ref[k:k + 1, lanes], (SUBLANES, LANES))
        wa_k = [bcast(wa_ref, k) for k in range(ka)]
        wb_k = [bcast(wb_ref, k) for k in range(kb)]
        ba_g, bb_g = bcast(ba_ref, 0), bcast(bb_ref, 0)

        def conv_block(rb, carry):
            for j in range(rows // SUBLANES):
                r0 = pl.multiple_of(rb * rows, rows) + j * SUBLANES
                acc = ba_g
                for k in range(ka):
                    acc = acc + wa_k[k] * tap(wina, xsa, ha - (ka - 1) + k, r0, lanes)
                ca_s[pl.ds(r0, SUBLANES), lanes] = acc
                accb = bb_g
                for k in range(kb):
                    accb = accb + wb_k[k] * tap(winb, xsb, hb - (kb - 1) + k, r0, lanes)
                cb_s[pl.ds(r0, SUBLANES), lanes] = accb
            return carry

        lax.fori_loop(0, tt // rows, conv_block, 0)

    a_out = _group_norm_silu(ca_s[...], ng_ref[...], nb_ref[...], width // CONV_HEADS_A)
    aout_ref[...] = a_out.astype(aout_ref.dtype)

    a, u = _lru_gates(cb_s[...], wg_ref, brg_ref[...], big_ref[...], lam_ref[...])
    row8 = lax.broadcasted_iota(i32, (tt, width), 0) & (SUBLANES - 1)
    d = 1
    while d < SUBLANES:
        m = row8 >= d
        a_sh = jnp.where(m, pltpu.roll(a, d, axis=0), 1.0)
        u_sh = jnp.where(m, pltpu.roll(u, d, axis=0), 0.0)
        u = a * u_sh + u
        a = a * a_sh
        d *= 2
    a_s[...] = a
    u_s[...] = u

    def scan_block(j, h):
        r0 = pl.multiple_of(j * SUBLANES, SUBLANES)
        hblk = a_s[pl.ds(r0, SUBLANES), :] * h + u_s[pl.ds(r0, SUBLANES), :]
        u_s[pl.ds(r0, SUBLANES), :] = hblk
        return jnp.broadcast_to(hblk[SUBLANES - 1:SUBLANES, :], h.shape)

    h_last = lax.fori_loop(0, tt // SUBLANES, scan_block, hcar[...])
    hcar[...] = h_last
    bout_ref[...] = (u_s[...] * gate_ref[...]).astype(bout_ref.dtype)

    tail_a = wina[tt:tt + ha, :]
    tail_b = winb[tt:tt + hb, :]
    wina[0:ha, :] = tail_a
    winb[0:hb, :] = tail_b
    nsa_ref[...] = tail_a
    nsb_ref[...] = tail_b
    nsh_ref[...] = h_last[0:1, :]


def _mixer_seq(glu, bx, gate, sa_pad, sb_pad, sh, mix_w, n_seq, seq_len, tt):
    wa, ba, ng, nb, wb, bb, wg, brg, big, lam = mix_w
    width = glu.shape[1]
    ha, hb = sa_pad.shape[1], sb_pad.shape[1]
    n_t = seq_len // tt
    row_in = pl.BlockSpec((tt, width), lambda b, t: (b * n_t + t, 0))

    def const(shape):
        return pl.BlockSpec(shape, lambda b, t: (0,) * len(shape))

    def per_seq(r):
        return pl.BlockSpec((None, r, width), lambda b, t: (b, 0, 0))

    return pl.pallas_call(
        _mixer_seq_kernel,
        grid=(n_seq, n_t),
        in_specs=[row_in, row_in, row_in, per_seq(ha), per_seq(hb), per_seq(1),
                  const(wa.shape), const(ba.shape), const(ng.shape), const(nb.shape),
                  const(wb.shape), const(bb.shape), const(wg.shape), const(brg.shape),
                  const(big.shape), const(lam.shape)],
        out_specs=[row_in, row_in, per_seq(ha), per_seq(hb), per_seq(1)],
        out_shape=[jax.ShapeDtypeStruct((n_seq * seq_len, width), bf16),
                   jax.ShapeDtypeStruct((n_seq * seq_len, width), bf16),
                   jax.ShapeDtypeStruct((n_seq, ha, width), f32),
                   jax.ShapeDtypeStruct((n_seq, hb, width), f32),
                   jax.ShapeDtypeStruct((n_seq, 1, width), f32)],
        scratch_shapes=[pltpu.VMEM((ha + tt, width), f32),
                        pltpu.VMEM((hb + tt, width), f32),
                        pltpu.VMEM((SUBLANES - 1, tt + ha - SUBLANES, width), f32),
                        pltpu.VMEM((SUBLANES - 1, tt + hb - SUBLANES, width), f32),
                        pltpu.VMEM((tt, width), f32),
                        pltpu.VMEM((tt, width), f32),
                        pltpu.VMEM((tt, width), f32),
                        pltpu.VMEM((tt, width), f32),
                        pltpu.VMEM((SUBLANES, width), f32)],
        compiler_params=pltpu.CompilerParams(dimension_semantics=("arbitrary", "arbitrary"),
                                             vmem_limit_bytes=VMEM_LIMIT),
        name="mixer_seq",
    )(glu, bx, gate, sa_pad, sb_pad, sh, wa, ba, ng, nb, wb, bb, wg, brg, big, lam)


def _mixer_step_kernel(glu_ref, bx_ref, gate_ref, glun_ref, bxn_ref, sa_ref, sb_ref, sh_ref,
                       wat_ref, wa2_ref, wbt_ref, wb2_ref,
                       ba_ref, ng_ref, nb_ref, bb_ref, wg_ref, brg_ref, big_ref, lam_ref,
                       aout_ref, bout_ref, nsa_ref, nsb_ref, nsh_ref,
                       hista, histb, hcar):
    t = pl.program_id(1)
    n_t = hista.shape[0]
    width = glu_ref.shape[1]
    ka1, kb1 = sa_ref.shape[1], sb_ref.shape[1]

    @pl.when(t == 0)
    def _():
        hista[...] = jnp.zeros_like(hista)
        histb[...] = jnp.zeros_like(histb)
        hcar[...] = sh_ref[...]
        nsa_ref[:, 0:ka1 - n_t, :] = sa_ref[:, n_t:ka1, :]
        nsa_ref[:, ka1 - n_t:ka1, :] = glun_ref[...]
        nsb_ref[...] = bxn_ref[:, n_t - kb1:n_t, :]

    hista[t] = glu_ref[...]
    histb[t] = bx_ref[...]

    ca = ba_ref[...] + jnp.sum(sa_ref[...] * wat_ref[...][None], axis=1)
    cb = bb_ref[...] + jnp.sum(sb_ref[...] * wbt_ref[...][None], axis=1)
    for j in range(n_t):
        ca = ca + wa2_ref[j:j + 1, :] * hista[j]
        cb = cb + wb2_ref[j:j + 1, :] * histb[j]
    a_out = _group_norm_silu(ca, ng_ref[...], nb_ref[...], width // CONV_HEADS_A)
    aout_ref[...] = a_out.astype(aout_ref.dtype)

    a, u = _lru_gates(cb, wg_ref, brg_ref[...], big_ref[...], lam_ref[...])
    h = a * hcar[...] + u
    hcar[...] = h
    nsh_ref[...] = h
    bout_ref[...] = (h * gate_ref[...]).astype(bout_ref.dtype)


def _mixer_step(glu, bx, gate, row_off, glu_nat, bx_nat, sa, sb, sh, step_w, mix_w, n_seq, n_t, bt):
    wat, wa2, wbt, wb2 = step_w
    wa, ba, ng, nb, wb, bb, wg, brg, big, lam = mix_w
    width = glu.shape[1]
    ka1, kb1 = sa.shape[1], sb.shape[1]
    off = row_off // bt
    nb_tiles = n_seq // bt
    row_in = pl.BlockSpec((bt, width), lambda j, t: (off + t * nb_tiles + j, 0))
    row_out = pl.BlockSpec((bt, width), lambda j, t: (t * nb_tiles + j, 0))

    def const(shape):
        return pl.BlockSpec(shape, lambda j, t: (0,) * len(shape))

    def per_t(arr):
        return pl.BlockSpec((None,) + arr.shape[1:], lambda j, t: (t, 0, 0))

    def per_b(r):
        return pl.BlockSpec((bt, r, width), lambda j, t: (j, 0, 0))

    state_h = pl.BlockSpec((bt, width), lambda j, t: (j, 0))
    return pl.pallas_call(
        _mixer_step_kernel,
        grid=(nb_tiles, n_t),
        in_specs=[row_in, row_in, row_in, per_b(n_t), per_b(n_t), per_b(ka1), per_b(kb1), state_h,
                  per_t(wat), per_t(wa2), per_t(wbt), per_t(wb2),
                  const(ba.shape), const(ng.shape), const(nb.shape), const(bb.shape),
                  const(wg.shape), const(brg.shape), const(big.shape), const(lam.shape)],
        out_specs=[row_out, row_out, per_b(ka1), per_b(kb1), state_h],
        out_shape=[jax.ShapeDtypeStruct((n_seq * n_t, width), bf16),
                   jax.ShapeDtypeStruct((n_seq * n_t, width), bf16),
                   jax.ShapeDtypeStruct((n_seq, ka1, width), f32),
                   jax.ShapeDtypeStruct((n_seq, kb1, width), f32),
                   jax.ShapeDtypeStruct((n_seq, width), f32)],
        scratch_shapes=[pltpu.VMEM((n_t, bt, width), f32),
                        pltpu.VMEM((n_t, bt, width), f32),
                        pltpu.VMEM((bt, width), f32)],
        compiler_params=pltpu.CompilerParams(dimension_semantics=("arbitrary", "arbitrary"),
                                             vmem_limit_bytes=VMEM_LIMIT),
        name="mixer_step",
    )(glu, bx, gate, glu_nat, bx_nat, sa, sb, sh, wat, wa2, wbt, wb2,
      ba, ng, nb, bb, wg, brg, big, lam)


def _outproj_kernel(ap_ref, bp_ref, as_ref, bs_ref, xp_ref, xs_ref, ling_ref, linb_ref, wo_ref,
                    l1g_ref, l1b_ref, wrh_ref, wrl_ref, br_ref,
                    h1_ref, h1p_ref, route_ref, cnt_ref, carry, *, n_p_tiles, alpha):
    i = pl.program_id(0)
    is_p = i < n_p_tiles
    a = jnp.where(is_p, ap_ref[...], as_ref[...])
    b = jnp.where(is_p, bp_ref[...], bs_ref[...])
    x = jnp.where(is_p, xp_ref[...], xs_ref[...])
    width = a.shape[1]
    mix = (jnp.dot(a, wo_ref[0:width, :], preferred_element_type=f32)
           + jnp.dot(b, wo_ref[width:2 * width, :], preferred_element_type=f32))
    h0 = _layer_norm(x, ling_ref[...], linb_ref[...])
    h1 = _layer_norm(alpha * h0 + mix, l1g_ref[...], l1b_ref[...])
    h1_ref[...] = h1

    xh = h1.astype(bf16)
    xh32 = xh.astype(f32)
    half = h1.shape[1] // 2
    lo = lax.shift_right_logical(lax.bitcast_convert_type(xh32[:, :half], jnp.uint32), jnp.uint32(16))
    hi = lax.bitcast_convert_type(xh32[:, half:], jnp.uint32) & jnp.uint32(0xFFFF0000)
    h1p_ref[...] = lo | hi

    xl = (h1 - xh32).astype(bf16)
    z = (jnp.dot(xh, wrh_ref[...], preferred_element_type=f32)
         + jnp.dot(xl, wrh_ref[...], preferred_element_type=f32)
         + jnp.dot(xh, wrl_ref[...], preferred_element_type=f32)) + br_ref[...]
    tm = z.shape[0]
    lane = lax.broadcasted_iota(i32, z.shape, 1).astype(f32)
    n_g, epg = float(N_GROUPS), float(EXPERTS_PER_GROUP)
    far = float(2 * ROUTE_LANES)

    gm = lane < n_g
    zg = jnp.where(gm, z, NEG_BIG)
    pg = jnp.where(gm, jnp.exp(zg - jnp.max(zg, axis=-1, keepdims=True)), 0.0)
    pg = pg / jnp.sum(pg, axis=-1, keepdims=True)
    g_top = jnp.max(pg, axis=-1, keepdims=True)
    g_idx = jnp.min(jnp.where(gm & (pg == g_top), lane, far), axis=-1, keepdims=True)

    lo = n_g + g_idx * epg
    em = (lane >= lo) & (lane < lo + epg)
    ze = jnp.where(em, z, NEG_BIG)
    pe = jnp.where(em, jnp.exp(ze - jnp.max(ze, axis=-1, keepdims=True)), 0.0)
    pe = pe / jnp.sum(pe, axis=-1, keepdims=True)
    v1 = jnp.max(jnp.where(em, pe, -1.0), axis=-1, keepdims=True)
    l1 = jnp.min(jnp.where(em & (pe == v1), lane, far), axis=-1, keepdims=True)
    pe2 = jnp.where(em & (lane != l1), pe, -1.0)
    v2 = jnp.max(pe2, axis=-1, keepdims=True)
    l2 = jnp.min(jnp.where(pe2 == v2, lane, far), axis=-1, keepdims=True)
    den = v1 + v2
    gate1 = v1 / den * g_top
    gate2 = v2 / den * g_top

    @pl.when(i == 0)
    def _():
        carry[...] = jnp.zeros_like(carry)

    o1 = lane == l1
    o2 = lane == l2
    onehot = jnp.where(o1 | o2, 1.0, 0.0)
    ri = lax.broadcasted_iota(i32, (tm, tm), 0)
    ci = lax.broadcasted_iota(i32, (tm, tm), 1)
    tri = jnp.where(ci < ri, 1.0, 0.0).astype(bf16)
    prefix = jnp.dot(tri, onehot.astype(bf16), preferred_element_type=f32) + carry[0:1, :]
    rank1 = jnp.sum(jnp.where(o1, prefix, 0.0), axis=-1, keepdims=True)
    rank2 = jnp.sum(jnp.where(o2, prefix, 0.0), axis=-1, keepdims=True)
    new_carry = carry[...] + jnp.sum(onehot, axis=0, keepdims=True)
    carry[...] = new_carry
    cnt_ref[...] = new_carry

    route = jnp.where(lane == 0.0, l1 - n_g, 0.0)
    route = jnp.where(lane == 1.0, l2 - n_g, route)
    route = jnp.where(lane == 2.0, rank1, route)
    route = jnp.where(lane == 3.0, rank2, route)
    route = jnp.where(lane == 4.0, gate1, route)
    route = jnp.where(lane == 5.0, gate2, route)
    route_ref[...] = route


def _outproj(a_p, b_p, a_s, b_s, x_p, x_s, ln_in_g, ln_in_b, w_out_b, ln1_g, ln1_b,
             wr_hi, wr_lo, b_r, alpha, tm):
    n_p, width = a_p.shape
    n_s = a_s.shape[0]
    d = x_p.shape[1]
    n_p_tiles = n_p // tm
    n_s_tiles = n_s // tm
    n_tiles = n_p_tiles + n_s_tiles

    def p_map(i):
        return (jnp.minimum(i, n_p_tiles - 1), 0)

    def s_map(i):
        return (jnp.clip(i - n_p_tiles, 0, n_s_tiles - 1), 0)

    def const(shape, **kw):
        return pl.BlockSpec(shape, lambda i: (0,) * len(shape), **kw)

    return pl.pallas_call(
        functools.partial(_outproj_kernel, n_p_tiles=n_p_tiles, alpha=alpha),
        grid=(n_tiles,),
        in_specs=[pl.BlockSpec((tm, width), p_map), pl.BlockSpec((tm, width), p_map),
                  pl.BlockSpec((tm, width), s_map), pl.BlockSpec((tm, width), s_map),
                  pl.BlockSpec((tm, d), p_map), pl.BlockSpec((tm, d), s_map),
                  const((1, d)), const((1, d)),
                  const((2 * width, d), pipeline_mode=pl.Buffered(1)),
                  const((1, d)), const((1, d)),
                  const((d, ROUTE_LANES)), const((d, ROUTE_LANES)), const((1, ROUTE_LANES))],
        out_specs=[pl.BlockSpec((tm, d), lambda i: (i, 0)),
                   pl.BlockSpec((tm, d // 2), lambda i: (i, 0)),
                   pl.BlockSpec((tm, ROUTE_LANES), lambda i: (i, 0)),
                   const((SUBLANES, ROUTE_LANES))],
        out_shape=[jax.ShapeDtypeStruct((n_p + n_s, d), f32),
                   jax.ShapeDtypeStruct((n_p + n_s, d // 2), jnp.uint32),
                   jax.ShapeDtypeStruct((n_p + n_s, ROUTE_LANES), f32),
                   jax.ShapeDtypeStruct((SUBLANES, ROUTE_LANES), f32)],
        scratch_shapes=[pltpu.VMEM((SUBLANES, ROUTE_LANES), f32)],
        compiler_params=pltpu.CompilerParams(dimension_semantics=("arbitrary",),
                                             vmem_limit_bytes=VMEM_LIMIT),
        name="outproj_router",
    )(a_p, b_p, a_s, b_s, x_p, x_s, ln_in_g, ln_in_b, w_out_b, ln1_g, ln1_b, wr_hi, wr_lo, b_r)


def _expert_kernel(src_ref, dst_ref, start_ref, nblk_ref, eff_ref,
                   h1_hbm, wg_ref, wu_ref, wd_ref, y2_hbm,
                   xbuf, ybuf, wgb, wub, wdb, gsem, osem, *, blk, n_rows):
    del eff_ref
    e = pl.program_id(0)
    n_e = pl.num_programs(0)
    nb = nblk_ref[e]
    g0 = start_ref[e] // blk

    def gather(g, slot):
        for r in range(blk):
            tok = src_ref[g * blk + r]
            pltpu.make_async_copy(h1_hbm.at[pl.ds(tok, 1), :],
                                  xbuf.at[slot, pl.ds(r, 1), :], gsem.at[slot]).start()

    def scatter(g, slot):
        for r in range(blk):
            row = dst_ref[(g + 1) * blk + r]
            pltpu.make_async_copy(ybuf.at[slot, pl.ds(r, 1), :],
                                  y2_hbm.at[pl.ds(row, 1), :], osem.at[slot]).start()

    def dump_copy(slot):
        dst = y2_hbm.at[pl.ds(n_rows + slot * blk, blk), :]
        return pltpu.make_async_copy(ybuf.at[slot], dst, osem.at[slot])

    def wait_gather(slot):
        pltpu.make_async_copy(h1_hbm.at[pl.ds(0, blk), :], xbuf.at[slot], gsem.at[slot]).wait()

    @pl.when(e == 0)
    def _():
        ybuf[...] = jnp.zeros(ybuf.shape, ybuf.dtype)
        dump_copy(0).start()
        gather(0, 0)

    @pl.when(nb > 0)
    def _():
        wgb[...] = wg_ref[...].astype(bf16)
        wub[...] = wu_ref[...].astype(bf16)
        wdb[...] = wd_ref[...].astype(bf16)

        def chunk(c, carry):
            g = g0 + c
            slot = g % 2
            wait_gather(slot)
            dump_copy(slot).wait()
            gather(g + 1, 1 - slot)
            scatter(g - 1, 1 - slot)
            xw = xbuf[slot]
            x_lo = lax.bitcast_convert_type(lax.shift_left(xw, jnp.uint32(16)), f32)
            x_hi = lax.bitcast_convert_type(xw & jnp.uint32(0xFFFF0000), f32)
            x = jnp.concatenate([x_lo, x_hi], axis=1).astype(bf16)
            hg = jnp.dot(x, wgb[...], preferred_element_type=f32)
            hu = jnp.dot(x, wub[...], preferred_element_type=f32)
            hid = (hg * jax.nn.sigmoid(hg) * hu).astype(bf16)
            ybuf[slot] = jnp.dot(hid, wdb[...], preferred_element_type=f32)
            return carry

        lax.fori_loop(0, nb, chunk, 0)

    @pl.when(e == n_e - 1)
    def _():
        g_end = g0 + nb
        last = (g_end + 1) % 2
        scatter(g_end - 1, last)
        dump_copy(1 - last).wait()
        dump_copy(last).wait()
        wait_gather(1 - last)


def _experts(h1p, src_rows, dst_rows, pad_start, nblk, eff, w_gate, w_up, w_down, blk):
    n = h1p.shape[0]
    n_e, d, d_e = w_gate.shape
    n_rows = TOP_K * n
    grid_spec = pltpu.PrefetchScalarGridSpec(
        num_scalar_prefetch=5,
        grid=(n_e,),
        in_specs=[pl.BlockSpec(memory_space=pl.ANY),
                  pl.BlockSpec((None, d, d_e), lambda e, sr, ds, s, nbk, ef: (ef[e], 0, 0)),
                  pl.BlockSpec((None, d, d_e), lambda e, sr, ds, s, nbk, ef: (ef[e], 0, 0)),
                  pl.BlockSpec((None, d_e, d), lambda e, sr, ds, s, nbk, ef: (ef[e], 0, 0))],
        out_specs=pl.BlockSpec(memory_space=pl.ANY),
        scratch_shapes=[pltpu.VMEM((2, blk, d // 2), jnp.uint32),
                        pltpu.VMEM((2, blk, d), f32),
                        pltpu.VMEM((d, d_e), bf16),
                        pltpu.VMEM((d, d_e), bf16),
                        pltpu.VMEM((d_e, d), bf16),
                        pltpu.SemaphoreType.DMA((2,)),
                        pltpu.SemaphoreType.DMA((2,))],
    )
    return pl.pallas_call(
        functools.partial(_expert_kernel, blk=blk, n_rows=n_rows),
        grid_spec=grid_spec,
        out_shape=jax.ShapeDtypeStruct((n_rows + 2 * blk, d), f32),
        compiler_params=pltpu.CompilerParams(dimension_semantics=("arbitrary",),
                                             vmem_limit_bytes=VMEM_LIMIT),
        name="experts",
    )(src_rows, dst_rows, pad_start, nblk, eff, h1p, w_gate, w_up, w_down)


def _combine_kernel(h1_ref, route_ref, ya_ref, yb_ref, l2g_ref, l2b_ref, yp_ref, ysm_ref,
                    *, n_p_tiles, alpha):
    i = pl.program_id(0)
    route = route_ref[...]
    y = ya_ref[...] * route[:, 4:5] + yb_ref[...] * route[:, 5:6]
    out = _layer_norm(alpha * h1_ref[...] + y, l2g_ref[...], l2b_ref[...])

    @pl.when(i < n_p_tiles)
    def _():
        yp_ref[...] = out

    @pl.when(i >= n_p_tiles)
    def _():
        ysm_ref[...] = out


def _combine(h1, route, y2, ln2_g, ln2_b, n_p, alpha, tm):
    n, d = h1.shape
    n_s = n - n_p
    n_tiles = n // tm
    n_p_tiles = n_p // tm
    n_s_tiles = n_s // tm
    return pl.pallas_call(
        functools.partial(_combine_kernel, n_p_tiles=n_p_tiles, alpha=alpha),
        grid=(n_tiles,),
        in_specs=[pl.BlockSpec((tm, d), lambda i: (i, 0)),
                  pl.BlockSpec((tm, ROUTE_LANES), lambda i: (i, 0)),
                  pl.BlockSpec((tm, d), lambda i: (i, 0)),
                  pl.BlockSpec((tm, d), lambda i: (n_tiles + i, 0)),
                  pl.BlockSpec((1, d), lambda i: (0, 0)),
                  pl.BlockSpec((1, d), lambda i: (0, 0))],
        out_specs=[pl.BlockSpec((tm, d), lambda i: (jnp.minimum(i, n_p_tiles - 1), 0)),
                   pl.BlockSpec((tm, d), lambda i: (jnp.clip(i - n_p_tiles, 0, n_s_tiles - 1), 0))],
        out_shape=[jax.ShapeDtypeStruct((n_p, d), f32), jax.ShapeDtypeStruct((n_s, d), f32)],
        compiler_params=pltpu.CompilerParams(dimension_semantics=("arbitrary",),
                                             vmem_limit_bytes=VMEM_LIMIT),
        name="combine_ln2",
    )(h1, route, y2, y2, ln2_g, ln2_b)


def _shifted_taps(w, n_t, hist):
    k = w.shape[0]
    zero = jnp.zeros_like(w[0])
    wt = jnp.stack([jnp.stack([w[r - t] if 0 <= r - t < k else zero for r in range(hist)])
                    for t in range(n_t)])
    w2 = jnp.stack([jnp.stack([w[hist - t + j] if (j <= t and 0 <= hist - t + j < k) else zero
                               for j in range(n_t)]) for t in range(n_t)])
    return wt, w2


def kernel(x_prompt, x_sample, state_conv_a, state_conv_b, state_lru, meta_tokens, ln_in_g, ln_in_b, w_in, conv_a_w, conv_a_b, norm_a_g, norm_a_b, conv_b_w, conv_b_b, w_rg, b_rg, w_ig, b_ig, lru_lambda, w_out, ln1_g, ln1_b, w_router_group, b_router_group, w_router_expert, b_router_expert, w_gate, w_up, w_down, ln2_g, ln2_b):
    depth = w_in.shape[0]
    assert depth == 1, "single-layer trunk only"
    bp, tp, d = x_prompt.shape
    bs, ts, _ = x_sample.shape
    n_meta = meta_tokens.shape[0]
    width = conv_a_w.shape[2]
    ka, kb = conv_a_w.shape[1], conv_b_w.shape[1]
    n_e = w_gate.shape[1]
    alpha = (2.0 * depth) ** 0.25
    n_p, n_s = bp * tp, bs * ts
    n = n_p + n_s
    assert ts < ka and ts >= kb - 1
    assert n_p % TOKEN_TILE == 0 and n_s % TOKEN_TILE == 0 and tp % SEQ_TILE == 0
    assert bs % STEP_BATCH == 0 and n_p % STEP_BATCH == 0
    assert n_p % COMBINE_TILE == 0 and n_s % COMBINE_TILE == 0
    assert n_meta % BF16_ROWS == 0
    assert N_GROUPS * (1 + EXPERTS_PER_GROUP) <= ROUTE_LANES and n_e == N_GROUPS * EXPERTS_PER_GROUP

    row = lambda v: v.reshape(1, -1).astype(f32)
    ln_g, ln_b = row(ln_in_g), row(ln_in_b)
    w_in_b = w_in[0].astype(bf16)
    w_out_b = w_out[0].astype(bf16)
    mix_w = (conv_a_w[0], row(conv_a_b[0]), row(norm_a_g[0]), row(norm_a_b[0]),
             conv_b_w[0], row(conv_b_b[0]),
             jnp.concatenate([w_rg[0], w_ig[0]], axis=-1).astype(bf16),
             row(b_rg[0]), row(b_ig[0]), row(lru_lambda[0]))
    ha, hb = _round_up(ka - 1, SUBLANES), _round_up(kb - 1, SUBLANES)

    def pad_hist(s, h):
        return jnp.concatenate([jnp.zeros(s.shape[:1] + (h - s.shape[1], width), f32), s], axis=1)

    glu_m, bx_m, gate_m = _inproj(meta_tokens, meta_tokens, 0, ln_g, ln_b, w_in_b, n_meta)
    _, _, sa_m, sb_m, sh_m = _mixer_seq(
        glu_m, bx_m, gate_m, jnp.zeros((1, ha, width), f32), jnp.zeros((1, hb, width), f32),
        jnp.zeros((1, 1, width), f32), mix_w, 1, n_meta, n_meta)

    x_p = x_prompt.reshape(n_p, d)
    x_s = jnp.swapaxes(x_sample, 0, 1).reshape(n_s, d)
    glu, bx, gate = _inproj(x_p, x_s, n_s, ln_g, ln_b, w_in_b, TOKEN_TILE)

    bcast = lambda s: jnp.broadcast_to(s, (bp,) + s.shape[1:])
    a_p, b_p, nsa_p, nsb_p, nsh_p = _mixer_seq(glu, bx, gate, bcast(sa_m), bcast(sb_m), bcast(sh_m),
                                               mix_w, bp, tp, SEQ_TILE)

    nat = lambda v: jnp.swapaxes(v[n_p:].reshape(ts, bs, width), 0, 1)
    wat, wa2 = _shifted_taps(conv_a_w[0], ts, ka - 1)
    wbt, wb2 = _shifted_taps(conv_b_w[0], ts, kb - 1)
    a_s, b_s, nsa_s, nsb_s, nsh_s = _mixer_step(
        glu, bx, gate, n_p, nat(glu), nat(bx), state_conv_a[0], state_conv_b[0], state_lru[0],
        (wat, wa2, wbt, wb2), mix_w, bs, ts, STEP_BATCH)

    w_r = jnp.concatenate([w_router_group[0], w_router_expert[0]], axis=1)
    w_r = jnp.pad(w_r, ((0, 0), (0, ROUTE_LANES - w_r.shape[1])))
    wr_hi = w_r.astype(bf16)
    wr_lo = (w_r - wr_hi.astype(f32)).astype(bf16)
    b_r = jnp.concatenate([b_router_group[0], b_router_expert[0]])
    b_r = jnp.pad(b_r, (0, ROUTE_LANES - b_r.shape[0])).reshape(1, ROUTE_LANES)
    h1, h1p, route, cnt = _outproj(a_p, b_p, a_s, b_s, x_p, x_s, ln_g, ln_b, w_out_b,
                              row(ln1_g[0]), row(ln1_b[0]), wr_hi, wr_lo, b_r, alpha, TOKEN_TILE)

    blk = EXPERT_ROWS
    e_idx = route[:, 0:TOP_K].astype(i32)
    rank = route[:, 2:2 + TOP_K].astype(i32)
    counts = cnt[0, N_GROUPS:N_GROUPS + n_e].astype(i32)
    nblk = (counts + blk - 1) // blk
    pad_end = jnp.cumsum(nblk * blk)
    pad_start = pad_end - nblk * blk
    start_of = jnp.sum(jnp.where(e_idx[..., None] == jnp.arange(n_e, dtype=i32), pad_start, 0), axis=-1)
    dest = (start_of + rank).reshape(-1)
    p_max = _round_up(TOP_K * n + n_e * (blk - 1), blk)
    flat = jnp.full((p_max,), -1, i32).at[dest].set(jnp.arange(TOP_K * n, dtype=i32))
    pos = jnp.arange(p_max, dtype=i32)
    valid = flat >= 0
    src_rows = jnp.where(valid, flat // TOP_K, 0)
    dst_rows = jnp.where(valid, (flat % TOP_K) * n + flat // TOP_K, TOP_K * n + pos % (2 * blk))
    src_rows = jnp.concatenate([src_rows, jnp.zeros((blk,), i32)])
    dst_rows = jnp.concatenate([TOP_K * n + blk + jnp.arange(blk, dtype=i32), dst_rows])
    eff = lax.cummax(jnp.where(nblk > 0, jnp.arange(n_e, dtype=i32), 0))

    y2 = _experts(h1p, src_rows, dst_rows, pad_start.astype(i32), nblk, eff,
                  w_gate[0], w_up[0], w_down[0], blk)
    y_p, y_s = _combine(h1, route, y2, row(ln2_g[0]), row(ln2_b[0]), n_p, alpha, TOKEN_TILE)

    y_prompt = y_p.reshape(bp, tp, d)
    y_sample = jnp.swapaxes(y_s.reshape(ts, bs, d), 0, 1)
    return (y_prompt, y_sample,
            nsa_p[:, ha - (ka - 1):][None], nsb_p[:, hb - (kb - 1):][None], nsh_p.reshape(1, bp, width),
            nsa_s[None], nsb_s[None], nsh_s[None])
```

```python
import functools

import jax
import jax.numpy as jnp
from jax import lax
from jax.experimental import pallas as pl
from jax.experimental.pallas import tpu as pltpu

f32 = jnp.float32
bf16 = jnp.bfloat16
i32 = jnp.int32

CONV_HEADS_A = 8
LRU_HEADS = 8
LRU_C = 8.0
N_GROUPS = 8
EXPERTS_PER_GROUP = 8
TOP_K = 2
LN_EPS = 1e-5

SUBLANES = 8
LANES = 128
BF16_ROWS = 16
VMEM_LIMIT = 56 * 1024 * 1024

TOKEN_TILE = 256
SEQ_TILE = 256
CONV_ROWS = 32
STEP_BATCH = 32
EXPERT_ROWS = 128
COMBINE_TILE = 128
ROUTE_LANES = 128
NEG_BIG = -1e30


def _round_up(x, m):
    return (x + m - 1) // m * m


def _layer_norm(x, g, b):
    mu = jnp.mean(x, axis=-1, keepdims=True)
    xc = x - mu
    var = jnp.mean(xc * xc, axis=-1, keepdims=True)
    return xc * lax.rsqrt(var + LN_EPS) * g + b


def _group_norm_silu(x, g, b, group):
    parts = []
    for c in range(x.shape[1] // group):
        xg = x[:, c * group:(c + 1) * group]
        mu = jnp.mean(xg, axis=-1, keepdims=True)
        xc = xg - mu
        var = jnp.mean(xc * xc, axis=-1, keepdims=True)
        parts.append(xc * lax.rsqrt(var + LN_EPS))
    y = jnp.concatenate(parts, axis=-1) * g + b
    return y * jax.nn.sigmoid(y)


def _lru_gates(cb, wg_ref, brg, big, lam):
    heads, hd, _ = wg_ref.shape
    cbb = cb.astype(bf16)
    rs, iz = [], []
    for h in range(heads):
        z = jnp.dot(cbb[:, h * hd:(h + 1) * hd], wg_ref[h], preferred_element_type=f32)
        rs.append(z[:, :hd])
        iz.append(z[:, hd:])
    r = jax.nn.sigmoid(jnp.concatenate(rs, axis=-1) + brg)
    i = jax.nn.sigmoid(jnp.concatenate(iz, axis=-1) + big)
    log_a = -LRU_C * r * jax.nn.softplus(-lam)
    a = jnp.exp(log_a)
    u = jnp.sqrt(-jnp.tanh(log_a) * (a * a + 1.0)) * (i * cb)
    return a, u


def _inproj_kernel(xa_ref, xb_ref, g_ref, b_ref, w_ref, glu_ref, bx_ref, gate_ref, *, n_a_tiles):
    i = pl.program_id(0)
    x = jnp.where(i < n_a_tiles, xa_ref[...], xb_ref[...])
    xn = _layer_norm(x, g_ref[...], b_ref[...]).astype(bf16)
    width = glu_ref.shape[1]
    ch = min(width, 512)
    for c in range(width // ch):
        lo = c * ch
        av = jnp.dot(xn, w_ref[:, lo:lo + ch], preferred_element_type=f32)
        ag = jnp.dot(xn, w_ref[:, width + lo:width + lo + ch], preferred_element_type=f32)
        glu_ref[:, lo:lo + ch] = av * jax.nn.sigmoid(ag)
        bx_ref[:, lo:lo + ch] = jnp.dot(xn, w_ref[:, 2 * width + lo:2 * width + lo + ch],
                                        preferred_element_type=f32)
        bg = jnp.dot(xn, w_ref[:, 3 * width + lo:3 * width + lo + ch], preferred_element_type=f32)
        gate_ref[:, lo:lo + ch] = jax.nn.gelu(bg)


def _inproj(xa, xb, n_b_rows, ln_g, ln_b, w_in_b, tm):
    na, d = xa.shape
    width = w_in_b.shape[1] // 4
    n_a_tiles = na // tm
    n_tiles = (na + n_b_rows) // tm
    n_b_tiles = max(n_tiles - n_a_tiles, 1)
    row_out = pl.BlockSpec((tm, width), lambda i: (i, 0))
    return pl.pallas_call(
        functools.partial(_inproj_kernel, n_a_tiles=n_a_tiles),
        grid=(n_tiles,),
        in_specs=[
            pl.BlockSpec((tm, d), lambda i: (jnp.minimum(i, n_a_tiles - 1), 0)),
            pl.BlockSpec((tm, d), lambda i: (jnp.clip(i - n_a_tiles, 0, n_b_tiles - 1), 0)),
            pl.BlockSpec((1, d), lambda i: (0, 0)),
            pl.BlockSpec((1, d), lambda i: (0, 0)),
            pl.BlockSpec((d, 4 * width), lambda i: (0, 0), pipeline_mode=pl.Buffered(1)),
        ],
        out_specs=[row_out, row_out, row_out],
        out_shape=[jax.ShapeDtypeStruct((na + n_b_rows, width), f32)] * 3,
        compiler_params=pltpu.CompilerParams(dimension_semantics=("arbitrary",),
                                             vmem_limit_bytes=VMEM_LIMIT),
        name="inproj",
    )(xa, xb, ln_g, ln_b, w_in_b)


def _mixer_seq_kernel(glu_ref, bx_ref, gate_ref, sa_ref, sb_ref, sh_ref,
                      wa_ref, ba_ref, ng_ref, nb_ref, wb_ref, bb_ref, wg_ref, brg_ref, big_ref, lam_ref,
                      aout_ref, bout_ref, nsa_ref, nsb_ref, nsh_ref,
                      wina, winb, xsa, xsb, ca_s, cb_s, a_s, u_s, hcar):
    tau = pl.program_id(1)
    tt, width = glu_ref.shape
    ka, kb = wa_ref.shape[0], wb_ref.shape[0]
    ha, hb = sa_ref.shape[0], sb_ref.shape[0]

    @pl.when(tau == 0)
    def _():
        wina[0:ha, :] = sa_ref[...]
        winb[0:hb, :] = sb_ref[...]
        hcar[...] = jnp.broadcast_to(sh_ref[...], hcar.shape)

    wina[ha:ha + tt, :] = glu_ref[...]
    winb[hb:hb + tt, :] = bx_ref[...]
    for s in sorted({(ha - (ka - 1) + k) % SUBLANES for k in range(ka)} - {0}):
        xsa[s - 1] = wina[pl.ds(s, tt + ha - SUBLANES), :]
    for s in sorted({(hb - (kb - 1) + k) % SUBLANES for k in range(kb)} - {0}):
        xsb[s - 1] = winb[pl.ds(s, tt + hb - SUBLANES), :]

    def tap(win, xs, off, r0, lanes):
        q, s = divmod(off, SUBLANES)
        src = win if s == 0 else xs.at[s - 1]
        return src[pl.ds(r0 + SUBLANES * q, SUBLANES), lanes]

    rows = min(CONV_ROWS, tt)
    for g in range(width // LANES):
        lanes = pl.ds(g * LANES, LANES)
        bcast = lambda ref, k: jnp.broadcast_to(ref[k:k + 1, lanes], (SUBLANES, LANES))
        wa_k = [bcast(wa_ref, k) for k in range(ka)]
        wb_k = [bcast(wb_ref, k) for k in range(kb)]
        ba_g, bb_g = bcast(ba_ref, 0), bcast(bb_ref, 0)

        def conv_block(rb, carry):
            for j in range(rows // SUBLANES):
                r0 = pl.multiple_of(rb * rows, rows) + j * SUBLANES
                acc = ba_g
                for k in range(ka):
                    acc = acc + wa_k[k] * tap(wina, xsa, ha - (ka - 1) + k, r0, lanes)
                ca_s[pl.ds(r0, SUBLANES), lanes] = acc
                accb = bb_g
                for k in range(kb):
                    accb = accb + wb_k[k] * tap(winb, xsb, hb - (kb - 1) + k, r0, lanes)
                cb_s[pl.ds(r0, SUBLANES), lanes] = accb
            return carry

        lax.fori_loop(0, tt // rows, conv_block, 0)

    a_out = _group_norm_silu(ca_s[...], ng_ref[...], nb_ref[...], width // CONV_HEADS_A)
    aout_ref[...] = a_out.astype(aout_ref.dtype)

    a, u = _lru_gates(cb_s[...], wg_ref, brg_ref[...], big_ref[...], lam_ref[...])
    row8 = lax.broadcasted_iota(i32, (tt, width), 0) & (SUBLANES - 1)
    d = 1
    while d < SUBLANES:
        m = row8 >= d
        a_sh = jnp.where(m, pltpu.roll(a, d, axis=0), 1.0)
        u_sh = jnp.where(m, pltpu.roll(u, d, axis=0), 0.0)
        u = a * u_sh + u
        a = a * a_sh
        d *= 2
    a_s[...] = a
    u_s[...] = u

    def scan_block(j, h):
        r0 = pl.multiple_of(j * SUBLANES, SUBLANES)
        hblk = a_s[pl.ds(r0, SUBLANES), :] * h + u_s[pl.ds(r0, SUBLANES), :]
        u_s[pl.ds(r0, SUBLANES), :] = hblk
        return jnp.broadcast_to(hblk[SUBLANES - 1:SUBLANES, :], h.shape)

    h_last = lax.fori_loop(0, tt // SUBLANES, scan_block, hcar[...])
    hcar[...] = h_last
    bout_ref[...] = (u_s[...] * gate_ref[...]).astype(bout_ref.dtype)

    tail_a = wina[tt:tt + ha, :]
    tail_b = winb[tt:tt + hb, :]
    wina[0:ha, :] = tail_a
    winb[0:hb, :] = tail_b
    nsa_ref[...] = tail_a
    nsb_ref[...] = tail_b
    nsh_ref[...] = h_last[0:1, :]


def _mixer_seq(glu, bx, gate, sa_pad, sb_pad, sh, mix_w, n_seq, seq_len, tt):
    wa, ba, ng, nb, wb, bb, wg, brg, big, lam = mix_w
    width = glu.shape[1]
    ha, hb = sa_pad.shape[1], sb_pad.shape[1]
    n_t = seq_len // tt
    row_in = pl.BlockSpec((tt, width), lambda b, t: (b * n_t + t, 0))

    def const(shape):
        return pl.BlockSpec(shape, lambda b, t: (0,) * len(shape))

    def per_seq(r):
        return pl.BlockSpec((None, r, width), lambda b, t: (b, 0, 0))

    return pl.pallas_call(
        _mixer_seq_kernel,
        grid=(n_seq, n_t),
        in_specs=[row_in, row_in, row_in, per_seq(ha), per_seq(hb), per_seq(1),
                  const(wa.shape), const(ba.shape), const(ng.shape), const(nb.shape),
                  const(wb.shape), const(bb.shape), const(wg.shape), const(brg.shape),
                  const(big.shape), const(lam.shape)],
        out_specs=[row_in, row_in, per_seq(ha), per_seq(hb), per_seq(1)],
        out_shape=[jax.ShapeDtypeStruct((n_seq * seq_len, width), bf16),
                   jax.ShapeDtypeStruct((n_seq * seq_len, width), bf16),
                   jax.ShapeDtypeStruct((n_seq, ha, width), f32),
                   jax.ShapeDtypeStruct((n_seq, hb, width), f32),
                   jax.ShapeDtypeStruct((n_seq, 1, width), f32)],
        scratch_shapes=[pltpu.VMEM((ha + tt, width), f32),
                        pltpu.VMEM((hb + tt, width), f32),
                        pltpu.VMEM((SUBLANES - 1, tt + ha - SUBLANES, width), f32),
                        pltpu.VMEM((SUBLANES - 1, tt + hb - SUBLANES, width), f32),
                        pltpu.VMEM((tt, width), f32),
                        pltpu.VMEM((tt, width), f32),
                        pltpu.VMEM((tt, width), f32),
                        pltpu.VMEM((tt, width), f32),
                        pltpu.VMEM((SUBLANES, width), f32)],
        compiler_params=pltpu.CompilerParams(dimension_semantics=("arbitrary", "arbitrary"),
                                             vmem_limit_bytes=VMEM_LIMIT),
        name="mixer_seq",
    )(glu, bx, gate, sa_pad, sb_pad, sh, wa, ba, ng, nb, wb, bb, wg, brg, big, lam)


def _mixer_step_kernel(glu_ref, bx_ref, gate_ref, glun_ref, bxn_ref, sa_ref, sb_ref, sh_ref,
                       wat_ref, wa2_ref, wbt_ref, wb2_ref,
                       ba_ref, ng_ref, nb_ref, bb_ref, wg_ref, brg_ref, big_ref, lam_ref,
                       aout_ref, bout_ref, nsa_ref, nsb_ref, nsh_ref,
                       hista, histb, hcar):
    t = pl.program_id(1)
    n_t = hista.shape[0]
    width = glu_ref.shape[1]
    ka1, kb1 = sa_ref.shape[1], sb_ref.shape[1]

    @pl.when(t == 0)
    def _():
        hista[...] = jnp.zeros_like(hista)
        histb[...] = jnp.zeros_like(histb)
        hcar[...] = sh_ref[...]
        nsa_ref[:, 0:ka1 - n_t, :] = sa_ref[:, n_t:ka1, :]
        nsa_ref[:, ka1 - n_t:ka1, :] = glun_ref[...]
        nsb_ref[...] = bxn_ref[:, n_t - kb1:n_t, :]

    hista[t] = glu_ref[...]
    histb[t] = bx_ref[...]

    ca = ba_ref[...] + jnp.sum(sa_ref[...] * wat_ref[...][None], axis=1)
    cb = bb_ref[...] + jnp.sum(sb_ref[...] * wbt_ref[...][None], axis=1)
    for j in range(n_t):
        ca = ca + wa2_ref[j:j + 1, :] * hista[j]
        cb = cb + wb2_ref[j:j + 1, :] * histb[j]
    a_out = _group_norm_silu(ca, ng_ref[...], nb_ref[...], width // CONV_HEADS_A)
    aout_ref[...] = a_out.astype(aout_ref.dtype)

    a, u = _lru_gates(cb, wg_ref, brg_ref[...], big_ref[...], lam_ref[...])
    h = a * hcar[...] + u
    hcar[...] = h
    nsh_ref[...] = h
    bout_ref[...] = (h * gate_ref[...]).astype(bout_ref.dtype)


def _mixer_step(glu, bx, gate, row_off, glu_nat, bx_nat, sa, sb, sh, step_w, mix_w, n_seq, n_t, bt):
    wat, wa2, wbt, wb2 = step_w
    wa, ba, ng, nb, wb, bb, wg, brg, big, lam = mix_w
    width = glu.shape[1]
    ka1, kb1 = sa.shape[1], sb.shape[1]
    off = row_off // bt
    nb_tiles = n_seq // bt
    row_in = pl.BlockSpec((bt, width), lambda j, t: (off + t * nb_tiles + j, 0))
    row_out = pl.BlockSpec((bt, width), lambda j, t: (t * nb_tiles + j, 0))

    def const(shape):
        return pl.BlockSpec(shape, lambda j, t: (0,) * len(shape))

    def per_t(arr):
        return pl.BlockSpec((None,) + arr.shape[1:], lambda j, t: (t, 0, 0))

    def per_b(r):
        return pl.BlockSpec((bt, r, width), lambda j, t: (j, 0, 0))

    state_h = pl.BlockSpec((bt, width), lambda j, t: (j, 0))
    return pl.pallas_call(
        _mixer_step_kernel,
        grid=(nb_tiles, n_t),
        in_specs=[row_in, row_in, row_in, per_b(n_t), per_b(n_t), per_b(ka1), per_b(kb1), state_h,
                  per_t(wat), per_t(wa2), per_t(wbt), per_t(wb2),
                  const(ba.shape), const(ng.shape), const(nb.shape), const(bb.shape),
                  const(wg.shape), const(brg.shape), const(big.shape), const(lam.shape)],
        out_specs=[row_out, row_out, per_b(ka1), per_b(kb1), state_h],
        out_shape=[jax.ShapeDtypeStruct((n_seq * n_t, width), bf16),
                   jax.ShapeDtypeStruct((n_seq * n_t, width), bf16),
                   jax.ShapeDtypeStruct((n_seq, ka1, width), f32),
                   jax.ShapeDtypeStruct((n_seq, kb1, width), f32),
                   jax.ShapeDtypeStruct((n_seq, width), f32)],
        scratch_shapes=[pltpu.VMEM((n_t, bt, width), f32),
                        pltpu.VMEM((n_t, bt, width), f32),
                        pltpu.VMEM((bt, width), f32)],
        compiler_params=pltpu.CompilerParams(dimension_semantics=("arbitrary", "arbitrary"),
                                             vmem_limit_bytes=VMEM_LIMIT),
        name="mixer_step",
    )(glu, bx, gate, glu_nat, bx_nat, sa, sb, sh, wat, wa2, wbt, wb2,
      ba, ng, nb, bb, wg, brg, big, lam)


def _outproj_kernel(ap_ref, bp_ref, as_ref, bs_ref, xp_ref, xs_ref, ling_ref, linb_ref, wo_ref,
                    l1g_ref, l1b_ref, wrh_ref, wrl_ref, br_ref,
                    h1_ref, h1p_ref, route_ref, cnt_ref, carry, *, n_p_tiles, alpha):
    i = pl.program_id(0)
    is_p = i < n_p_tiles
    a = jnp.where(is_p, ap_ref[...], as_ref[...])
    b = jnp.where(is_p, bp_ref[...], bs_ref[...])
    x = jnp.where(is_p, xp_ref[...], xs_ref[...])
    width = a.shape[1]
    mix = (jnp.dot(a, wo_ref[0:width, :], preferred_element_type=f32)
           + jnp.dot(b, wo_ref[width:2 * width, :], preferred_element_type=f32))
    h0 = _layer_norm(x, ling_ref[...], linb_ref[...])
    h1 = _layer_norm(alpha * h0 + mix, l1g_ref[...], l1b_ref[...])
    h1_ref[...] = h1

    xh = h1.astype(bf16)
    xh32 = xh.astype(f32)
    half = h1.shape[1] // 2
    lo = lax.shift_right_logical(lax.bitcast_convert_type(xh32[:, :half], jnp.uint32), jnp.uint32(16))
    hi = lax.bitcast_convert_type(xh32[:, half:], jnp.uint32) & jnp.uint32(0xFFFF0000)
    word = lo | hi
    n_sub = half // LANES
    for s in range(n_sub):
        h1p_ref[pl.ds(s, word.shape[0], stride=n_sub), :] = word[:, s * LANES:(s + 1) * LANES]

    xl = (h1 - xh32).astype(bf16)
    z = (jnp.dot(xh, wrh_ref[...], preferred_element_type=f32)
         + jnp.dot(xl, wrh_ref[...], preferred_element_type=f32)
         + jnp.dot(xh, wrl_ref[...], preferred_element_type=f32)) + br_ref[...]
    tm = z.shape[0]
    lane = lax.broadcasted_iota(i32, z.shape, 1).astype(f32)
    n_g, epg = float(N_GROUPS), float(EXPERTS_PER_GROUP)
    far = float(2 * ROUTE_LANES)

    gm = lane < n_g
    zg = jnp.where(gm, z, NEG_BIG)
    pg = jnp.where(gm, jnp.exp(zg - jnp.max(zg, axis=-1, keepdims=True)), 0.0)
    pg = pg / jnp.sum(pg, axis=-1, keepdims=True)
    g_top = jnp.max(pg, axis=-1, keepdims=True)
    g_idx = jnp.min(jnp.where(gm & (pg == g_top), lane, far), axis=-1, keepdims=True)

    lo = n_g + g_idx * epg
    em = (lane >= lo) & (lane < lo + epg)
    ze = jnp.where(em, z, NEG_BIG)
    pe = jnp.where(em, jnp.exp(ze - jnp.max(ze, axis=-1, keepdims=True)), 0.0)
    pe = pe / jnp.sum(pe, axis=-1, keepdims=True)
    v1 = jnp.max(jnp.where(em, pe, -1.0), axis=-1, keepdims=True)
    l1 = jnp.min(jnp.where(em & (pe == v1), lane, far), axis=-1, keepdims=True)
    pe2 = jnp.where(em & (lane != l1), pe, -1.0)
    v2 = jnp.max(pe2, axis=-1, keepdims=True)
    l2 = jnp.min(jnp.where(pe2 == v2, lane, far), axis=-1, keepdims=True)
    den = v1 + v2
    gate1 = v1 / den * g_top
    gate2 = v2 / den * g_top

    @pl.when(i == 0)
    def _():
        carry[...] = jnp.zeros_like(carry)

    o1 = lane == l1
    o2 = lane == l2
    onehot = jnp.where(o1 | o2, 1.0, 0.0)
    ri = lax.broadcasted_iota(i32, (tm, tm), 0)
    ci = lax.broadcasted_iota(i32, (tm, tm), 1)
    tri = jnp.where(ci < ri, 1.0, 0.0).astype(bf16)
    prefix = jnp.dot(tri, onehot.astype(bf16), preferred_element_type=f32) + carry[0:1, :]
    rank1 = jnp.sum(jnp.where(o1, prefix, 0.0), axis=-1, keepdims=True)
    rank2 = jnp.sum(jnp.where(o2, prefix, 0.0), axis=-1, keepdims=True)
    new_carry = carry[...] + jnp.sum(onehot, axis=0, keepdims=True)
    carry[...] = new_carry
    cnt_ref[...] = new_carry

    route = jnp.where(lane == 0.0, l1 - n_g, 0.0)
    route = jnp.where(lane == 1.0, l2 - n_g, route)
    route = jnp.where(lane == 2.0, rank1, route)
    route = jnp.where(lane == 3.0, rank2, route)
    route = jnp.where(lane == 4.0, gate1, route)
    route = jnp.where(lane == 5.0, gate2, route)
    route_ref[...] = route


def _outproj(a_p, b_p, a_s, b_s, x_p, x_s, ln_in_g, ln_in_b, w_out_b, ln1_g, ln1_b,
             wr_hi, wr_lo, b_r, alpha, tm):
    n_p, width = a_p.shape
    n_s = a_s.shape[0]
    d = x_p.shape[1]
    n_p_tiles = n_p // tm
    n_s_tiles = n_s // tm
    n_tiles = n_p_tiles + n_s_tiles

    def p_map(i):
        return (jnp.minimum(i, n_p_tiles - 1), 0)

    def s_map(i):
        return (jnp.clip(i - n_p_tiles, 0, n_s_tiles - 1), 0)

    def const(shape, **kw):
        return pl.BlockSpec(shape, lambda i: (0,) * len(shape), **kw)

    return pl.pallas_call(
        functools.partial(_outproj_kernel, n_p_tiles=n_p_tiles, alpha=alpha),
        grid=(n_tiles,),
        in_specs=[pl.BlockSpec((tm, width), p_map), pl.BlockSpec((tm, width), p_map),
                  pl.BlockSpec((tm, width), s_map), pl.BlockSpec((tm, width), s_map),
                  pl.BlockSpec((tm, d), p_map), pl.BlockSpec((tm, d), s_map),
                  const((1, d)), const((1, d)),
                  const((2 * width, d), pipeline_mode=pl.Buffered(1)),
                  const((1, d)), const((1, d)),
                  const((d, ROUTE_LANES)), const((d, ROUTE_LANES)), const((1, ROUTE_LANES))],
        out_specs=[pl.BlockSpec((tm, d), lambda i: (i, 0)),
                   pl.BlockSpec((tm * (d // 2 // LANES), LANES), lambda i: (i, 0)),
                   pl.BlockSpec((tm, ROUTE_LANES), lambda i: (i, 0)),
                   const((SUBLANES, ROUTE_LANES))],
        out_shape=[jax.ShapeDtypeStruct((n_p + n_s, d), f32),
                   jax.ShapeDtypeStruct(((n_p + n_s) * (d // 2 // LANES), LANES), jnp.uint32),
                   jax.ShapeDtypeStruct((n_p + n_s, ROUTE_LANES), f32),
                   jax.ShapeDtypeStruct((SUBLANES, ROUTE_LANES), f32)],
        scratch_shapes=[pltpu.VMEM((SUBLANES, ROUTE_LANES), f32)],
        compiler_params=pltpu.CompilerParams(dimension_semantics=("arbitrary",),
                                             vmem_limit_bytes=VMEM_LIMIT),
        name="outproj_router",
    )(a_p, b_p, a_s, b_s, x_p, x_s, ln_in_g, ln_in_b, w_out_b, ln1_g, ln1_b, wr_hi, wr_lo, b_r)


def _expert_kernel(src_ref, dst_ref, start_ref, nblk_ref, eff_ref,
                   h1_hbm, wg_ref, wu_ref, wd_ref, y2_hbm,
                   xbuf, ybuf, wgb, wub, wdb, gsem, osem, *, blk, n_rows):
    del eff_ref
    e = pl.program_id(0)
    n_e = pl.num_programs(0)
    nb = nblk_ref[e]
    g0 = start_ref[e] // blk

    xs = xbuf.shape[1] // blk
    ys = ybuf.shape[1] // blk

    def gather(g, slot):
        for r in range(blk):
            tok = src_ref[g * blk + r]
            pltpu.make_async_copy(h1_hbm.at[pl.ds(pl.multiple_of(tok * xs, xs), xs), :],
                                  xbuf.at[slot, pl.ds(r * xs, xs), :], gsem.at[slot]).start()

    def scatter(g, slot):
        for r in range(blk):
            row = dst_ref[(g + 1) * blk + r]
            pltpu.make_async_copy(ybuf.at[slot, pl.ds(r * ys, ys), :],
                                  y2_hbm.at[pl.ds(pl.multiple_of(row * ys, ys), ys), :],
                                  osem.at[slot]).start()

    def dump_copy(slot):
        dst = y2_hbm.at[pl.ds((n_rows + slot * blk) * ys, blk * ys), :]
        return pltpu.make_async_copy(ybuf.at[slot], dst, osem.at[slot])

    def wait_gather(slot):
        pltpu.make_async_copy(h1_hbm.at[pl.ds(0, blk * xs), :], xbuf.at[slot], gsem.at[slot]).wait()

    @pl.when(e == 0)
    def _():
        ybuf[...] = jnp.zeros(ybuf.shape, ybuf.dtype)
        dump_copy(0).start()
        gather(0, 0)

    @pl.when(nb > 0)
    def _():
        wgb[...] = wg_ref[...].astype(bf16)
        wub[...] = wu_ref[...].astype(bf16)
        wdb[...] = wd_ref[...].astype(bf16)

        def chunk(c, carry):
            g = g0 + c
            slot = g % 2
            wait_gather(slot)
            dump_copy(slot).wait()
            gather(g + 1, 1 - slot)
            scatter(g - 1, 1 - slot)
            xw = [xbuf[slot, pl.ds(s, blk, stride=xs), :] for s in range(xs)]
            x_lo = [lax.bitcast_convert_type(lax.shift_left(w, jnp.uint32(16)), f32) for w in xw]
            x_hi = [lax.bitcast_convert_type(w & jnp.uint32(0xFFFF0000), f32) for w in xw]
            x = jnp.concatenate(x_lo + x_hi, axis=1).astype(bf16)
            hg = jnp.dot(x, wgb[...], preferred_element_type=f32)
            hu = jnp.dot(x, wub[...], preferred_element_type=f32)
            hid = (hg * jax.nn.sigmoid(hg) * hu).astype(bf16)
            y = jnp.dot(hid, wdb[...], preferred_element_type=f32)
            for s in range(ys):
                ybuf[slot, pl.ds(s, blk, stride=ys), :] = y[:, s * LANES:(s + 1) * LANES]
            return carry

        lax.fori_loop(0, nb, chunk, 0)

    @pl.when(e == n_e - 1)
    def _():
        g_end = g0 + nb
        last = (g_end + 1) % 2
        scatter(g_end - 1, last)
        dump_copy(1 - last).wait()
        dump_copy(last).wait()
        wait_gather(1 - last)


def _experts(h1p, src_rows, dst_rows, pad_start, nblk, eff, w_gate, w_up, w_down, blk):
    n_e, d, d_e = w_gate.shape
    xs, ys = d // 2 // LANES, d // LANES
    n = h1p.shape[0] // xs
    n_rows = TOP_K * n
    grid_spec = pltpu.PrefetchScalarGridSpec(
        num_scalar_prefetch=5,
        grid=(n_e,),
        in_specs=[pl.BlockSpec(memory_space=pl.ANY),
                  pl.BlockSpec((None, d, d_e), lambda e, sr, ds, s, nbk, ef: (ef[e], 0, 0)),
                  pl.BlockSpec((None, d, d_e), lambda e, sr, ds, s, nbk, ef: (ef[e], 0, 0)),
                  pl.BlockSpec((None, d_e, d), lambda e, sr, ds, s, nbk, ef: (ef[e], 0, 0))],
        out_specs=pl.BlockSpec(memory_space=pl.ANY),
        scratch_shapes=[pltpu.VMEM((2, blk * xs, LANES), jnp.uint32),
                        pltpu.VMEM((2, blk * ys, LANES), f32),
                        pltpu.VMEM((d, d_e), bf16),
                        pltpu.VMEM((d, d_e), bf16),
                        pltpu.VMEM((d_e, d), bf16),
                        pltpu.SemaphoreType.DMA((2,)),
                        pltpu.SemaphoreType.DMA((2,))],
    )
    return pl.pallas_call(
        functools.partial(_expert_kernel, blk=blk, n_rows=n_rows),
        grid_spec=grid_spec,
        out_shape=jax.ShapeDtypeStruct(((n_rows + 2 * blk) * ys, LANES), f32),
        compiler_params=pltpu.CompilerParams(dimension_semantics=("arbitrary",),
                                             vmem_limit_bytes=VMEM_LIMIT),
        name="experts",
    )(src_rows, dst_rows, pad_start, nblk, eff, h1p, w_gate, w_up, w_down)


def _combine_kernel(h1_ref, route_ref, ya_ref, yb_ref, l2g_ref, l2b_ref, yp_ref, ysm_ref,
                    *, n_p_tiles, alpha):
    i = pl.program_id(0)
    route = route_ref[...]
    tm = route.shape[0]
    ys = ya_ref.shape[0] // tm

    def token_rows(ref):
        return jnp.concatenate([ref[pl.ds(s, tm, stride=ys), :] for s in range(ys)], axis=1)

    y = token_rows(ya_ref) * route[:, 4:5] + token_rows(yb_ref) * route[:, 5:6]
    out = _layer_norm(alpha * h1_ref[...] + y, l2g_ref[...], l2b_ref[...])

    @pl.when(i < n_p_tiles)
    def _():
        yp_ref[...] = out

    @pl.when(i >= n_p_tiles)
    def _():
        ysm_ref[...] = out


def _combine(h1, route, y2, ln2_g, ln2_b, n_p, alpha, tm):
    n, d = h1.shape
    n_s = n - n_p
    n_tiles = n // tm
    n_p_tiles = n_p // tm
    n_s_tiles = n_s // tm
    return pl.pallas_call(
        functools.partial(_combine_kernel, n_p_tiles=n_p_tiles, alpha=alpha),
        grid=(n_tiles,),
        in_specs=[pl.BlockSpec((tm, d), lambda i: (i, 0)),
                  pl.BlockSpec((tm, ROUTE_LANES), lambda i: (i, 0)),
                  pl.BlockSpec((tm * (d // LANES), LANES), lambda i: (i, 0)),
                  pl.BlockSpec((tm * (d // LANES), LANES), lambda i: (n_tiles + i, 0)),
                  pl.BlockSpec((1, d), lambda i: (0, 0)),
                  pl.BlockSpec((1, d), lambda i: (0, 0))],
        out_specs=[pl.BlockSpec((tm, d), lambda i: (jnp.minimum(i, n_p_tiles - 1), 0)),
                   pl.BlockSpec((tm, d), lambda i: (jnp.clip(i - n_p_tiles, 0, n_s_tiles - 1), 0))],
        out_shape=[jax.ShapeDtypeStruct((n_p, d), f32), jax.ShapeDtypeStruct((n_s, d), f32)],
        compiler_params=pltpu.CompilerParams(dimension_semantics=("arbitrary",),
                                             vmem_limit_bytes=VMEM_LIMIT),
        name="combine_ln2",
    )(h1, route, y2, y2, ln2_g, ln2_b)


def _shifted_taps(w, n_t, hist):
    k = w.shape[0]
    zero = jnp.zeros_like(w[0])
    wt = jnp.stack([jnp.stack([w[r - t] if 0 <= r - t < k else zero for r in range(hist)])
                    for t in range(n_t)])
    w2 = jnp.stack([jnp.stack([w[hist - t + j] if (j <= t and 0 <= hist - t + j < k) else zero
                               for j in range(n_t)]) for t in range(n_t)])
    return wt, w2


def kernel(x_prompt, x_sample, state_conv_a, state_conv_b, state_lru, meta_tokens, ln_in_g, ln_in_b, w_in, conv_a_w, conv_a_b, norm_a_g, norm_a_b, conv_b_w, conv_b_b, w_rg, b_rg, w_ig, b_ig, lru_lambda, w_out, ln1_g, ln1_b, w_router_group, b_router_group, w_router_expert, b_router_expert, w_gate, w_up, w_down, ln2_g, ln2_b):
    depth = w_in.shape[0]
    assert depth == 1, "single-layer trunk only"
    bp, tp, d = x_prompt.shape
    bs, ts, _ = x_sample.shape
    n_meta = meta_tokens.shape[0]
    width = conv_a_w.shape[2]
    ka, kb = conv_a_w.shape[1], conv_b_w.shape[1]
    n_e = w_gate.shape[1]
    alpha = (2.0 * depth) ** 0.25
    n_p, n_s = bp * tp, bs * ts
    n = n_p + n_s
    assert ts < ka and ts >= kb - 1
    assert n_p % TOKEN_TILE == 0 and n_s % TOKEN_TILE == 0 and tp % SEQ_TILE == 0
    assert bs % STEP_BATCH == 0 and n_p % STEP_BATCH == 0
    assert n_p % COMBINE_TILE == 0 and n_s % COMBINE_TILE == 0
    assert n_meta % BF16_ROWS == 0
    assert N_GROUPS * (1 + EXPERTS_PER_GROUP) <= ROUTE_LANES and n_e == N_GROUPS * EXPERTS_PER_GROUP

    row = lambda v: v.reshape(1, -1).astype(f32)
    ln_g, ln_b = row(ln_in_g), row(ln_in_b)
    w_in_b = w_in[0].astype(bf16)
    w_out_b = w_out[0].astype(bf16)
    mix_w = (conv_a_w[0], row(conv_a_b[0]), row(norm_a_g[0]), row(norm_a_b[0]),
             conv_b_w[0], row(conv_b_b[0]),
             jnp.concatenate([w_rg[0], w_ig[0]], axis=-1).astype(bf16),
             row(b_rg[0]), row(b_ig[0]), row(lru_lambda[0]))
    ha, hb = _round_up(ka - 1, SUBLANES), _round_up(kb - 1, SUBLANES)

    def pad_hist(s, h):
        return jnp.concatenate([jnp.zeros(s.shape[:1] + (h - s.shape[1], width), f32), s], axis=1)

    glu_m, bx_m, gate_m = _inproj(meta_tokens, meta_tokens, 0, ln_g, ln_b, w_in_b, n_meta)
    _, _, sa_m, sb_m, sh_m = _mixer_seq(
        glu_m, bx_m, gate_m, jnp.zeros((1, ha, width), f32), jnp.zeros((1, hb, width), f32),
        jnp.zeros((1, 1, width), f32), mix_w, 1, n_meta, n_meta)

    x_p = x_prompt.reshape(n_p, d)
    x_s = jnp.swapaxes(x_sample, 0, 1).reshape(n_s, d)
    glu, bx, gate = _inproj(x_p, x_s, n_s, ln_g, ln_b, w_in_b, TOKEN_TILE)

    bcast = lambda s: jnp.broadcast_to(s, (bp,) + s.shape[1:])
    a_p, b_p, nsa_p, nsb_p, nsh_p = _mixer_seq(glu, bx, gate, bcast(sa_m), bcast(sb_m), bcast(sh_m),
                                               mix_w, bp, tp, SEQ_TILE)

    nat = lambda v: jnp.swapaxes(v[n_p:].reshape(ts, bs, width), 0, 1)
    wat, wa2 = _shifted_taps(conv_a_w[0], ts, ka - 1)
    wbt, wb2 = _shifted_taps(conv_b_w[0], ts, kb - 1)
    a_s, b_s, nsa_s, nsb_s, nsh_s = _mixer_step(
        glu, bx, gate, n_p, nat(glu), nat(bx), state_conv_a[0], state_conv_b[0], state_lru[0],
        (wat, wa2, wbt, wb2), mix_w, bs, ts, STEP_BATCH)

    w_r = jnp.concatenate([w_router_group[0], w_router_expert[0]], axis=1)
    w_r = jnp.pad(w_r, ((0, 0), (0, ROUTE_LANES - w_r.shape[1])))
    wr_hi = w_r.astype(bf16)
    wr_lo = (w_r - wr_hi.astype(f32)).astype(bf16)
    b_r = jnp.concatenate([b_router_group[0], b_router_expert[0]])
    b_r = jnp.pad(b_r, (0, ROUTE_LANES - b_r.shape[0])).reshape(1, ROUTE_LANES)
    h1, h1p, route, cnt = _outproj(a_p, b_p, a_s, b_s, x_p, x_s, ln_g, ln_b, w_out_b,
                              row(ln1_g[0]), row(ln1_b[0]), wr_hi, wr_lo, b_r, alpha, TOKEN_TILE)

    blk = EXPERT_ROWS
    e_idx = route[:, 0:TOP_K].astype(i32)
    rank = route[:, 2:2 + TOP_K].astype(i32)
    counts = cnt[0, N_GROUPS:N_GROUPS + n_e].astype(i32)
    nblk = (counts + blk - 1) // blk
    pad_end = jnp.cumsum(nblk * blk)
    pad_start = pad_end - nblk * blk
    start_of = jnp.sum(jnp.where(e_idx[..., None] == jnp.arange(n_e, dtype=i32), pad_start, 0), axis=-1)
    dest = (start_of + rank).reshape(-1)
    p_max = _round_up(TOP_K * n + n_e * (blk - 1), blk)
    flat = jnp.full((p_max,), -1, i32).at[dest].set(jnp.arange(TOP_K * n, dtype=i32))
    pos = jnp.arange(p_max, dtype=i32)
    valid = flat >= 0
    src_rows = jnp.where(valid, flat // TOP_K, 0)
    dst_rows = jnp.where(valid, (flat % TOP_K) * n + flat // TOP_K, TOP_K * n + pos % (2 * blk))
    src_rows = jnp.concatenate([src_rows, jnp.zeros((blk,), i32)])
    dst_rows = jnp.concatenate([TOP_K * n + blk + jnp.arange(blk, dtype=i32), dst_rows])
    eff = lax.cummax(jnp.where(nblk > 0, jnp.arange(n_e, dtype=i32), 0))

    y2 = _experts(h1p, src_rows, dst_rows, pad_start.astype(i32), nblk, eff,
                  w_gate[0], w_up[0], w_down[0], blk)
    y_p, y_s = _combine(h1, route, y2, row(ln2_g[0]), row(ln2_b[0]), n_p, alpha, TOKEN_TILE)

    y_prompt = y_p.reshape(bp, tp, d)
    y_sample = jnp.swapaxes(y_s.reshape(ts, bs, d), 0, 1)
    return (y_prompt, y_sample,
            nsa_p[:, ha - (ka - 1):][None], nsb_p[:, hb - (kb - 1):][None], nsh_p.reshape(1, bp, width),
            nsa_s[None], nsb_s[None], nsh_s[None])
```

```python
import functools

import jax
import jax.numpy as jnp
from jax import lax
from jax.experimental import pallas as pl
from jax.experimental.pallas import tpu as pltpu

f32 = jnp.float32
bf16 = jnp.bfloat16
i32 = jnp.int32

CONV_HEADS_A = 8
LRU_HEADS = 8
LRU_C = 8.0
N_GROUPS = 8
EXPERTS_PER_GROUP = 8
TOP_K = 2
LN_EPS = 1e-5

SUBLANES = 8
LANES = 128
BF16_ROWS = 16
VMEM_LIMIT = 56 * 1024 * 1024

TOKEN_TILE = 256
OUTPROJ_SPLIT = 1
SEQ_TILE = 256
CONV_ROWS = 32
STEP_BATCH = 32
EXPERT_ROWS = 128
COMBINE_TILE = 128
ROUTE_LANES = 128
GATHER_PRIORITY = 1
NEG_BIG = -1e30


def _round_up(x, m):
    return (x + m - 1) // m * m


def _layer_norm(x, g, b):
    mu = jnp.mean(x, axis=-1, keepdims=True)
    xc = x - mu
    var = jnp.mean(xc * xc, axis=-1, keepdims=True)
    return xc * lax.rsqrt(var + LN_EPS) * g + b


def _group_norm_silu(x, g, b, group):
    parts = []
    for c in range(x.shape[1] // group):
        xg = x[:, c * group:(c + 1) * group]
        mu = jnp.mean(xg, axis=-1, keepdims=True)
        xc = xg - mu
        var = jnp.mean(xc * xc, axis=-1, keepdims=True)
        parts.append(xc * lax.rsqrt(var + LN_EPS))
    y = jnp.concatenate(parts, axis=-1) * g + b
    return y * jax.nn.sigmoid(y)


def _lru_gates(cb, wg_ref, brg, big, lam):
    heads, hd, _ = wg_ref.shape
    cbb = cb.astype(bf16)
    rs, iz = [], []
    for h in range(heads):
        z = jnp.dot(cbb[:, h * hd:(h + 1) * hd], wg_ref[h], preferred_element_type=f32)
        rs.append(z[:, :hd])
        iz.append(z[:, hd:])
    r = jax.nn.sigmoid(jnp.concatenate(rs, axis=-1) + brg)
    i = jax.nn.sigmoid(jnp.concatenate(iz, axis=-1) + big)
    log_a = -LRU_C * r * jax.nn.softplus(-lam)
    a = jnp.exp(log_a)
    u = jnp.sqrt(-jnp.tanh(log_a) * (a * a + 1.0)) * (i * cb)
    return a, u


def _inproj_kernel(xa_ref, xb_ref, g_ref, b_ref, w_ref, glu_ref, bx_ref, gate_ref, *, n_a_tiles):
    i = pl.program_id(0)
    x = jnp.where(i < n_a_tiles, xa_ref[...], xb_ref[...])
    xn = _layer_norm(x, g_ref[...], b_ref[...]).astype(bf16)
    width = glu_ref.shape[1]
    ch = min(width, 512)
    for c in range(width // ch):
        lo = c * ch
        av = jnp.dot(xn, w_ref[:, lo:lo + ch], preferred_element_type=f32)
        ag = jnp.dot(xn, w_ref[:, width + lo:width + lo + ch], preferred_element_type=f32)
        glu_ref[:, lo:lo + ch] = av * jax.nn.sigmoid(ag)
        bx_ref[:, lo:lo + ch] = jnp.dot(xn, w_ref[:, 2 * width + lo:2 * width + lo + ch],
                                        preferred_element_type=f32)
        bg = jnp.dot(xn, w_ref[:, 3 * width + lo:3 * width + lo + ch], preferred_element_type=f32)
        gate_ref[:, lo:lo + ch] = jax.nn.gelu(bg)


def _inproj(xa, xb, n_b_rows, ln_g, ln_b, w_in_b, tm):
    na, d = xa.shape
    width = w_in_b.shape[1] // 4
    n_a_tiles = na // tm
    n_tiles = (na + n_b_rows) // tm
    n_b_tiles = max(n_tiles - n_a_tiles, 1)
    row_out = pl.BlockSpec((tm, width), lambda i: (i, 0))
    return pl.pallas_call(
        functools.partial(_inproj_kernel, n_a_tiles=n_a_tiles),
        grid=(n_tiles,),
        in_specs=[
            pl.BlockSpec((tm, d), lambda i: (jnp.minimum(i, n_a_tiles - 1), 0)),
            pl.BlockSpec((tm, d), lambda i: (jnp.clip(i - n_a_tiles, 0, n_b_tiles - 1), 0)),
            pl.BlockSpec((1, d), lambda i: (0, 0)),
            pl.BlockSpec((1, d), lambda i: (0, 0)),
            pl.BlockSpec((d, 4 * width), lambda i: (0, 0), pipeline_mode=pl.Buffered(1)),
        ],
        out_specs=[row_out, row_out, row_out],
        out_shape=[jax.ShapeDtypeStruct((na + n_b_rows, width), f32)] * 3,
        compiler_params=pltpu.CompilerParams(dimension_semantics=("arbitrary",),
                                             vmem_limit_bytes=VMEM_LIMIT),
        name="inproj",
    )(xa, xb, ln_g, ln_b, w_in_b)


def _mixer_seq_kernel(glu_ref, bx_ref, gate_ref, sa_ref, sb_ref, sh_ref,
                      wa_ref, ba_ref, ng_ref, nb_ref, wb_ref, bb_ref, wg_ref, brg_ref, big_ref, lam_ref,
                      aout_ref, bout_ref, nsa_ref, nsb_ref, nsh_ref,
                      wina, winb, xsa, xsb, ca_s, cb_s, a_s, u_s, hcar):
    tau = pl.program_id(1)
    tt, width = glu_ref.shape
    ka, kb = wa_ref.shape[0], wb_ref.shape[0]
    ha, hb = sa_ref.shape[0], sb_ref.shape[0]

    @pl.when(tau == 0)
    def _():
        wina[0:ha, :] = sa_ref[...]
        winb[0:hb, :] = sb_ref[...]
        hcar[...] = jnp.broadcast_to(sh_ref[...], hcar.shape)

    wina[ha:ha + tt, :] = glu_ref[...]
    winb[hb:hb + tt, :] = bx_ref[...]
    for s in sorted({(ha - (ka - 1) + k) % SUBLANES for k in range(ka)} - {0}):
        xsa[s - 1] = wina[pl.ds(s, tt + ha - SUBLANES), :]
    for s in sorted({(hb - (kb - 1) + k) % SUBLANES for k in range(kb)} - {0}):
        xsb[s - 1] = winb[pl.ds(s, tt + hb - SUBLANES), :]

    def tap(win, xs, off, r0, lanes):
        q, s = divmod(off, SUBLANES)
        src = win if s == 0 else xs.at[s - 1]
        return src[pl.ds(r0 + SUBLANES * q, SUBLANES), lanes]

    rows = min(CONV_ROWS, tt)
    for g in range(width // LANES):
        lanes = pl.ds(g * LANES, LANES)
        bcast = lambda ref, k: jnp.broadcast_to(ref[k:k + 1, lanes], (SUBLANES, LANES))
        wa_k = [bcast(wa_ref, k) for k in range(ka)]
        wb_k = [bcast(wb_ref, k) for k in range(kb)]
        ba_g, bb_g = bcast(ba_ref, 0), bcast(bb_ref, 0)

        def conv_block(rb, carry):
            for j in range(rows // SUBLANES):
                r0 = pl.multiple_of(rb * rows, rows) + j * SUBLANES
                acc = ba_g
                for k in range(ka):
                    acc = acc + wa_k[k] * tap(wina, xsa, ha - (ka - 1) + k, r0, lanes)
                ca_s[pl.ds(r0, SUBLANES), lanes] = acc
                accb = bb_g
                for k in range(kb):
                    accb = accb + wb_k[k] * tap(winb, xsb, hb - (kb - 1) + k, r0, lanes)
                cb_s[pl.ds(r0, SUBLANES), lanes] = accb
            return carry

        lax.fori_loop(0, tt // rows, conv_block, 0)

    a_out = _group_norm_silu(ca_s[...], ng_ref[...], nb_ref[...], width // CONV_HEADS_A)
    aout_ref[...] = a_out.astype(aout_ref.dtype)

    a, u = _lru_gates(cb_s[...], wg_ref, brg_ref[...], big_ref[...], lam_ref[...])
    row8 = lax.broadcasted_iota(i32, (tt, width), 0) & (SUBLANES - 1)
    d = 1
    while d < SUBLANES:
        m = row8 >= d
        a_sh = jnp.where(m, pltpu.roll(a, d, axis=0), 1.0)
        u_sh = jnp.where(m, pltpu.roll(u, d, axis=0), 0.0)
        u = a * u_sh + u
        a = a * a_sh
        d *= 2
    a_s[...] = a
    u_s[...] = u

    def scan_block(j, h):
        r0 = pl.multiple_of(j * SUBLANES, SUBLANES)
        hblk = a_s[pl.ds(r0, SUBLANES), :] * h + u_s[pl.ds(r0, SUBLANES), :]
        u_s[pl.ds(r0, SUBLANES), :] = hblk
        return jnp.broadcast_to(hblk[SUBLANES - 1:SUBLANES, :], h.shape)

    h_last = lax.fori_loop(0, tt // SUBLANES, scan_block, hcar[...])
    hcar[...] = h_last
    bout_ref[...] = (u_s[...] * gate_ref[...]).astype(bout_ref.dtype)

    tail_a = wina[tt:tt + ha, :]
    tail_b = winb[tt:tt + hb, :]
    wina[0:ha, :] = tail_a
    winb[0:hb, :] = tail_b
    nsa_ref[...] = tail_a
    nsb_ref[...] = tail_b
    nsh_ref[...] = h_last[0:1, :]


def _mixer_seq(glu, bx, gate, sa_pad, sb_pad, sh, mix_w, n_seq, seq_len, tt):
    wa, ba, ng, nb, wb, bb, wg, brg, big, lam = mix_w
    width = glu.shape[1]
    ha, hb = sa_pad.shape[1], sb_pad.shape[1]
    n_t = seq_len // tt
    row_in = pl.BlockSpec((tt, width), lambda b, t: (b * n_t + t, 0))

    def const(shape):
        return pl.BlockSpec(shape, lambda b, t: (0,) * len(shape))

    def per_seq(r):
        return pl.BlockSpec((None, r, width), lambda b, t: (b, 0, 0))

    return pl.pallas_call(
        _mixer_seq_kernel,
        grid=(n_seq, n_t),
        in_specs=[row_in, row_in, row_in, per_seq(ha), per_seq(hb), per_seq(1),
                  const(wa.shape), const(ba.shape), const(ng.shape), const(nb.shape),
                  const(wb.shape), const(bb.shape), const(wg.shape), const(brg.shape),
                  const(big.shape), const(lam.shape)],
        out_specs=[row_in, row_in, per_seq(ha), per_seq(hb), per_seq(1)],
        out_shape=[jax.ShapeDtypeStruct((n_seq * seq_len, width), bf16),
                   jax.ShapeDtypeStruct((n_seq * seq_len, width), bf16),
                   jax.ShapeDtypeStruct((n_seq, ha, width), f32),
                   jax.ShapeDtypeStruct((n_seq, hb, width), f32),
                   jax.ShapeDtypeStruct((n_seq, 1, width), f32)],
        scratch_shapes=[pltpu.VMEM((ha + tt, width), f32),
                        pltpu.VMEM((hb + tt, width), f32),
                        pltpu.VMEM((SUBLANES - 1, tt + ha - SUBLANES, width), f32),
                        pltpu.VMEM((SUBLANES - 1, tt + hb - SUBLANES, width), f32),
                        pltpu.VMEM((tt, width), f32),
                        pltpu.VMEM((tt, width), f32),
                        pltpu.VMEM((tt, width), f32),
                        pltpu.VMEM((tt, width), f32),
                        pltpu.VMEM((SUBLANES, width), f32)],
        compiler_params=pltpu.CompilerParams(dimension_semantics=("arbitrary", "arbitrary"),
                                             vmem_limit_bytes=VMEM_LIMIT),
        name="mixer_seq",
    )(glu, bx, gate, sa_pad, sb_pad, sh, wa, ba, ng, nb, wb, bb, wg, brg, big, lam)


def _mixer_step_kernel(glu_ref, bx_ref, gate_ref, glun_ref, bxn_ref, sa_ref, sb_ref, sh_ref,
                       wat_ref, wa2_ref, wbt_ref, wb2_ref,
                       ba_ref, ng_ref, nb_ref, bb_ref, wg_ref, brg_ref, big_ref, lam_ref,
                       aout_ref, bout_ref, nsa_ref, nsb_ref, nsh_ref,
                       hista, histb, hcar):
    t = pl.program_id(1)
    n_t = hista.shape[0]
    width = glu_ref.shape[1]
    ka1, kb1 = sa_ref.shape[1], sb_ref.shape[1]

    @pl.when(t == 0)
    def _():
        hista[...] = jnp.zeros_like(hista)
        histb[...] = jnp.zeros_like(histb)
        hcar[...] = sh_ref[...]
        nsa_ref[:, 0:ka1 - n_t, :] = sa_ref[:, n_t:ka1, :]
        nsa_ref[:, ka1 - n_t:ka1, :] = glun_ref[...]
        nsb_ref[...] = bxn_ref[:, n_t - kb1:n_t, :]

    hista[t] = glu_ref[...]
    histb[t] = bx_ref[...]

    ca = ba_ref[...] + jnp.sum(sa_ref[...] * wat_ref[...][None], axis=1)
    cb = bb_ref[...] + jnp.sum(sb_ref[...] * wbt_ref[...][None], axis=1)
    for j in range(n_t):
        ca = ca + wa2_ref[j:j + 1, :] * hista[j]
        cb = cb + wb2_ref[j:j + 1, :] * histb[j]
    a_out = _group_norm_silu(ca, ng_ref[...], nb_ref[...], width // CONV_HEADS_A)
    aout_ref[...] = a_out.astype(aout_ref.dtype)

    a, u = _lru_gates(cb, wg_ref, brg_ref[...], big_ref[...], lam_ref[...])
    h = a * hcar[...] + u
    hcar[...] = h
    nsh_ref[...] = h
    bout_ref[...] = (h * gate_ref[...]).astype(bout_ref.dtype)


def _mixer_step(glu, bx, gate, row_off, glu_nat, bx_nat, sa, sb, sh, step_w, mix_w, n_seq, n_t, bt):
    wat, wa2, wbt, wb2 = step_w
    wa, ba, ng, nb, wb, bb, wg, brg, big, lam = mix_w
    width = glu.shape[1]
    ka1, kb1 = sa.shape[1], sb.shape[1]
    off = row_off // bt
    nb_tiles = n_seq // bt
    row_in = pl.BlockSpec((bt, width), lambda j, t: (off + t * nb_tiles + j, 0))
    row_out = pl.BlockSpec((bt, width), lambda j, t: (t * nb_tiles + j, 0))

    def const(shape):
        return pl.BlockSpec(shape, lambda j, t: (0,) * len(shape))

    def per_t(arr):
        return pl.BlockSpec((None,) + arr.shape[1:], lambda j, t: (t, 0, 0))

    def per_b(r):
        return pl.BlockSpec((bt, r, width), lambda j, t: (j, 0, 0))

    state_h = pl.BlockSpec((bt, width), lambda j, t: (j, 0))
    return pl.pallas_call(
        _mixer_step_kernel,
        grid=(nb_tiles, n_t),
        in_specs=[row_in, row_in, row_in, per_b(n_t), per_b(n_t), per_b(ka1), per_b(kb1), state_h,
                  per_t(wat), per_t(wa2), per_t(wbt), per_t(wb2),
                  const(ba.shape), const(ng.shape), const(nb.shape), const(bb.shape),
                  const(wg.shape), const(brg.shape), const(big.shape), const(lam.shape)],
        out_specs=[row_out, row_out, per_b(ka1), per_b(kb1), state_h],
        out_shape=[jax.ShapeDtypeStruct((n_seq * n_t, width), bf16),
                   jax.ShapeDtypeStruct((n_seq * n_t, width), bf16),
                   jax.ShapeDtypeStruct((n_seq, ka1, width), f32),
                   jax.ShapeDtypeStruct((n_seq, kb1, width), f32),
                   jax.ShapeDtypeStruct((n_seq, width), f32)],
        scratch_shapes=[pltpu.VMEM((n_t, bt, width), f32),
                        pltpu.VMEM((n_t, bt, width), f32),
                        pltpu.VMEM((bt, width), f32)],
        compiler_params=pltpu.CompilerParams(dimension_semantics=("arbitrary", "arbitrary"),
                                             vmem_limit_bytes=VMEM_LIMIT),
        name="mixer_step",
    )(glu, bx, gate, glu_nat, bx_nat, sa, sb, sh, wat, wa2, wbt, wb2,
      ba, ng, nb, bb, wg, brg, big, lam)


def _outproj_kernel(*refs, split, **kw):
    for h in range(split):
        _outproj_rows(h, split, *refs, **kw)


def _outproj_rows(h, split, ap_ref, bp_ref, as_ref, bs_ref, xp_ref, xs_ref, ling_ref, linb_ref, wo_ref,
                  l1g_ref, l1b_ref, wrh_ref, wrl_ref, br_ref,
                  h1_ref, h1p_ref, route_ref, cnt_ref, carry, *, n_p_tiles, alpha):
    i = pl.program_id(0)
    is_p = i < n_p_tiles
    sub = ap_ref.shape[0] // split
    per_tok = h1p_ref.shape[0] // ap_ref.shape[0]
    rows = pl.ds(h * sub, sub)
    ap_ref, bp_ref, as_ref, bs_ref = (r.at[rows] for r in (ap_ref, bp_ref, as_ref, bs_ref))
    xp_ref, xs_ref, h1_ref, route_ref = (r.at[rows] for r in (xp_ref, xs_ref, h1_ref, route_ref))
    h1p_ref = h1p_ref.at[pl.ds(h * sub * per_tok, sub * per_tok)]
    a = jnp.where(is_p, ap_ref[...], as_ref[...])
    b = jnp.where(is_p, bp_ref[...], bs_ref[...])
    x = jnp.where(is_p, xp_ref[...], xs_ref[...])
    width = a.shape[1]
    mix = (jnp.dot(a, wo_ref[0:width, :], preferred_element_type=f32)
           + jnp.dot(b, wo_ref[width:2 * width, :], preferred_element_type=f32))
    h0 = _layer_norm(x, ling_ref[...], linb_ref[...])
    h1 = _layer_norm(alpha * h0 + mix, l1g_ref[...], l1b_ref[...])
    h1_ref[...] = h1

    xh = h1.astype(bf16)
    xh32 = xh.astype(f32)
    half = h1.shape[1] // 2
    lo = lax.shift_right_logical(lax.bitcast_convert_type(xh32[:, :half], jnp.uint32), jnp.uint32(16))
    hi = lax.bitcast_convert_type(xh32[:, half:], jnp.uint32) & jnp.uint32(0xFFFF0000)
    word = lo | hi
    n_sub = half // LANES
    for s in range(n_sub):
        h1p_ref[pl.ds(s, word.shape[0], stride=n_sub), :] = word[:, s * LANES:(s + 1) * LANES]

    xl = (h1 - xh32).astype(bf16)
    z = (jnp.dot(xh, wrh_ref[...], preferred_element_type=f32)
         + jnp.dot(xl, wrh_ref[...], preferred_element_type=f32)
         + jnp.dot(xh, wrl_ref[...], preferred_element_type=f32)) + br_ref[...]
    tm = z.shape[0]
    lane = lax.broadcasted_iota(i32, z.shape, 1).astype(f32)
    n_g, epg = float(N_GROUPS), float(EXPERTS_PER_GROUP)
    far = float(2 * ROUTE_LANES)

    gm = lane < n_g
    zg = jnp.where(gm, z, NEG_BIG)
    pg = jnp.where(gm, jnp.exp(zg - jnp.max(zg, axis=-1, keepdims=True)), 0.0)
    pg = pg / jnp.sum(pg, axis=-1, keepdims=True)
    g_top = jnp.max(pg, axis=-1, keepdims=True)
    g_idx = jnp.min(jnp.where(gm & (pg == g_top), lane, far), axis=-1, keepdims=True)

    lo = n_g + g_idx * epg
    em = (lane >= lo) & (lane < lo + epg)
    ze = jnp.where(em, z, NEG_BIG)
    pe = jnp.where(em, jnp.exp(ze - jnp.max(ze, axis=-1, keepdims=True)), 0.0)
    pe = pe / jnp.sum(pe, axis=-1, keepdims=True)
    v1 = jnp.max(jnp.where(em, pe, -1.0), axis=-1, keepdims=True)
    l1 = jnp.min(jnp.where(em & (pe == v1), lane, far), axis=-1, keepdims=True)
    pe2 = jnp.where(em & (lane != l1), pe, -1.0)
    v2 = jnp.max(pe2, axis=-1, keepdims=True)
    l2 = jnp.min(jnp.where(pe2 == v2, lane, far), axis=-1, keepdims=True)
    den = v1 + v2
    gate1 = v1 / den * g_top
    gate2 = v2 / den * g_top

    @pl.when((i == 0) & (h == 0))
    def _():
        carry[...] = jnp.zeros_like(carry)

    o1 = lane == l1
    o2 = lane == l2
    onehot = jnp.where(o1 | o2, 1.0, 0.0)
    ri = lax.broadcasted_iota(i32, (tm, tm), 0)
    ci = lax.broadcasted_iota(i32, (tm, tm), 1)
    tri = jnp.where(ci < ri, 1.0, 0.0).astype(bf16)
    prefix = jnp.dot(tri, onehot.astype(bf16), preferred_element_type=f32) + carry[0:1, :]
    rank1 = jnp.sum(jnp.where(o1, prefix, 0.0), axis=-1, keepdims=True)
    rank2 = jnp.sum(jnp.where(o2, prefix, 0.0), axis=-1, keepdims=True)
    new_carry = carry[...] + jnp.sum(onehot, axis=0, keepdims=True)
    carry[...] = new_carry
    cnt_ref[...] = new_carry

    route = jnp.where(lane == 0.0, l1 - n_g, 0.0)
    route = jnp.where(lane == 1.0, l2 - n_g, route)
    route = jnp.where(lane == 2.0, rank1, route)
    route = jnp.where(lane == 3.0, rank2, route)
    route = jnp.where(lane == 4.0, gate1, route)
    route = jnp.where(lane == 5.0, gate2, route)
    route_ref[...] = route


def _outproj(a_p, b_p, a_s, b_s, x_p, x_s, ln_in_g, ln_in_b, w_out_b, ln1_g, ln1_b,
             wr_hi, wr_lo, b_r, alpha, tm):
    n_p, width = a_p.shape
    n_s = a_s.shape[0]
    d = x_p.shape[1]
    n_p_tiles = n_p // tm
    n_s_tiles = n_s // tm
    n_tiles = n_p_tiles + n_s_tiles

    def p_map(i):
        return (jnp.minimum(i, n_p_tiles - 1), 0)

    def s_map(i):
        return (jnp.clip(i - n_p_tiles, 0, n_s_tiles - 1), 0)

    def const(shape, **kw):
        return pl.BlockSpec(shape, lambda i: (0,) * len(shape), **kw)

    return pl.pallas_call(
        functools.partial(_outproj_kernel, split=OUTPROJ_SPLIT, n_p_tiles=n_p_tiles, alpha=alpha),
        grid=(n_tiles,),
        in_specs=[pl.BlockSpec((tm, width), p_map), pl.BlockSpec((tm, width), p_map),
                  pl.BlockSpec((tm, width), s_map), pl.BlockSpec((tm, width), s_map),
                  pl.BlockSpec((tm, d), p_map), pl.BlockSpec((tm, d), s_map),
                  const((1, d)), const((1, d)),
                  const((2 * width, d), pipeline_mode=pl.Buffered(1)),
                  const((1, d)), const((1, d)),
                  const((d, ROUTE_LANES)), const((d, ROUTE_LANES)), const((1, ROUTE_LANES))],
        out_specs=[pl.BlockSpec((tm, d), lambda i: (i, 0)),
                   pl.BlockSpec((tm * (d // 2 // LANES), LANES), lambda i: (i, 0)),
                   pl.BlockSpec((tm, ROUTE_LANES), lambda i: (i, 0)),
                   const((SUBLANES, ROUTE_LANES))],
        out_shape=[jax.ShapeDtypeStruct((n_p + n_s, d), f32),
                   jax.ShapeDtypeStruct(((n_p + n_s) * (d // 2 // LANES), LANES), jnp.uint32),
                   jax.ShapeDtypeStruct((n_p + n_s, ROUTE_LANES), f32),
                   jax.ShapeDtypeStruct((SUBLANES, ROUTE_LANES), f32)],
        scratch_shapes=[pltpu.VMEM((SUBLANES, ROUTE_LANES), f32)],
        compiler_params=pltpu.CompilerParams(dimension_semantics=("arbitrary",),
                                             vmem_limit_bytes=VMEM_LIMIT),
        name="outproj_router",
    )(a_p, b_p, a_s, b_s, x_p, x_s, ln_in_g, ln_in_b, w_out_b, ln1_g, ln1_b, wr_hi, wr_lo, b_r)


def _expert_kernel(src_ref, dst_ref, start_ref, nblk_ref, eff_ref,
                   h1_hbm, wg_ref, wu_ref, wd_ref, y2_hbm,
                   xbuf, ybuf, wgb, wub, wdb, gsem, osem, *, blk, n_rows):
    del eff_ref
    e = pl.program_id(0)
    n_e = pl.num_programs(0)
    nb = nblk_ref[e]
    g0 = start_ref[e] // blk

    xs = xbuf.shape[1] // blk
    ys = ybuf.shape[1] // blk

    def gather(g, slot):
        for r in range(blk):
            tok = src_ref[g * blk + r]
            pltpu.make_async_copy(h1_hbm.at[pl.ds(pl.multiple_of(tok * xs, xs), xs), :],
                                  xbuf.at[slot, pl.ds(r * xs, xs), :], gsem.at[slot]
                                  ).start(priority=GATHER_PRIORITY)

    def scatter(g, slot):
        for r in range(blk):
            row = dst_ref[(g + 1) * blk + r]
            pltpu.make_async_copy(ybuf.at[slot, pl.ds(r * ys, ys), :],
                                  y2_hbm.at[pl.ds(pl.multiple_of(row * ys, ys), ys), :],
                                  osem.at[slot]).start(priority=r % 2)

    def dump_copy(slot):
        dst = y2_hbm.at[pl.ds((n_rows + slot * blk) * ys, blk * ys), :]
        return pltpu.make_async_copy(ybuf.at[slot], dst, osem.at[slot])

    def wait_gather(slot):
        pltpu.make_async_copy(h1_hbm.at[pl.ds(0, blk * xs), :], xbuf.at[slot], gsem.at[slot]).wait()

    @pl.when(e == 0)
    def _():
        ybuf[...] = jnp.zeros(ybuf.shape, ybuf.dtype)
        dump_copy(0).start()
        gather(0, 0)

    @pl.when(nb > 0)
    def _():
        wgb[...] = wg_ref[...].astype(bf16)
        wub[...] = wu_ref[...].astype(bf16)
        wdb[...] = wd_ref[...].astype(bf16)

        def chunk(c, carry):
            g = g0 + c
            slot = g % 2
            wait_gather(slot)
            dump_copy(slot).wait()
            gather(g + 1, 1 - slot)
            scatter(g - 1, 1 - slot)
            xw = [xbuf[slot, pl.ds(s, blk, stride=xs), :] for s in range(xs)]
            x_lo = [lax.bitcast_convert_type(lax.shift_left(w, jnp.uint32(16)), f32) for w in xw]
            x_hi = [lax.bitcast_convert_type(w & jnp.uint32(0xFFFF0000), f32) for w in xw]
            x = jnp.concatenate(x_lo + x_hi, axis=1).astype(bf16)
            hg = jnp.dot(x, wgb[...], preferred_element_type=f32)
            hu = jnp.dot(x, wub[...], preferred_element_type=f32)
            hid = (hg * jax.nn.sigmoid(hg) * hu).astype(bf16)
            y = jnp.dot(hid, wdb[...], preferred_element_type=f32)
            for s in range(ys):
                ybuf[slot, pl.ds(s, blk, stride=ys), :] = y[:, s * LANES:(s + 1) * LANES]
            return carry

        lax.fori_loop(0, nb, chunk, 0)

    @pl.when(e == n_e - 1)
    def _():
        g_end = g0 + nb
        last = (g_end + 1) % 2
        scatter(g_end - 1, last)
        dump_copy(1 - last).wait()
        dump_copy(last).wait()
        wait_gather(1 - last)


def _experts(h1p, src_rows, dst_rows, pad_start, nblk, eff, w_gate, w_up, w_down, blk):
    n_e, d, d_e = w_gate.shape
    xs, ys = d // 2 // LANES, d // LANES
    n = h1p.shape[0] // xs
    n_rows = TOP_K * n
    grid_spec = pltpu.PrefetchScalarGridSpec(
        num_scalar_prefetch=5,
        grid=(n_e,),
        in_specs=[pl.BlockSpec(memory_space=pl.ANY),
                  pl.BlockSpec((None, d, d_e), lambda e, sr, ds, s, nbk, ef: (ef[e], 0, 0)),
                  pl.BlockSpec((None, d, d_e), lambda e, sr, ds, s, nbk, ef: (ef[e], 0, 0)),
                  pl.BlockSpec((None, d_e, d), lambda e, sr, ds, s, nbk, ef: (ef[e], 0, 0))],
        out_specs=pl.BlockSpec(memory_space=pl.ANY),
        scratch_shapes=[pltpu.VMEM((2, blk * xs, LANES), jnp.uint32),
                        pltpu.VMEM((2, blk * ys, LANES), f32),
                        pltpu.VMEM((d, d_e), bf16),
                        pltpu.VMEM((d, d_e), bf16),
                        pltpu.VMEM((d_e, d), bf16),
                        pltpu.SemaphoreType.DMA((2,)),
                        pltpu.SemaphoreType.DMA((2,))],
    )
    return pl.pallas_call(
        functools.partial(_expert_kernel, blk=blk, n_rows=n_rows),
        grid_spec=grid_spec,
        out_shape=jax.ShapeDtypeStruct(((n_rows + 2 * blk) * ys, LANES), f32),
        compiler_params=pltpu.CompilerParams(dimension_semantics=("arbitrary",),
                                             vmem_limit_bytes=VMEM_LIMIT),
        name="experts",
    )(src_rows, dst_rows, pad_start, nblk, eff, h1p, w_gate, w_up, w_down)


def _combine_kernel(h1_ref, route_ref, ya_ref, yb_ref, l2g_ref, l2b_ref, yp_ref, ysm_ref,
                    *, n_p_tiles, alpha):
    i = pl.program_id(0)
    route = route_ref[...]
    tm = route.shape[0]
    ys = ya_ref.shape[0] // tm

    def token_rows(ref):
        return jnp.concatenate([ref[pl.ds(s, tm, stride=ys), :] for s in range(ys)], axis=1)

    y = token_rows(ya_ref) * route[:, 4:5] + token_rows(yb_ref) * route[:, 5:6]
    out = _layer_norm(alpha * h1_ref[...] + y, l2g_ref[...], l2b_ref[...])

    @pl.when(i < n_p_tiles)
    def _():
        yp_ref[...] = out

    @pl.when(i >= n_p_tiles)
    def _():
        ysm_ref[...] = out


def _combine(h1, route, y2, ln2_g, ln2_b, n_p, alpha, tm):
    n, d = h1.shape
    n_s = n - n_p
    n_tiles = n // tm
    n_p_tiles = n_p // tm
    n_s_tiles = n_s // tm
    return pl.pallas_call(
        functools.partial(_combine_kernel, n_p_tiles=n_p_tiles, alpha=alpha),
        grid=(n_tiles,),
        in_specs=[pl.BlockSpec((tm, d), lambda i: (i, 0)),
                  pl.BlockSpec((tm, ROUTE_LANES), lambda i: (i, 0)),
                  pl.BlockSpec((tm * (d // LANES), LANES), lambda i: (i, 0)),
                  pl.BlockSpec((tm * (d // LANES), LANES), lambda i: (n_tiles + i, 0)),
                  pl.BlockSpec((1, d), lambda i: (0, 0)),
                  pl.BlockSpec((1, d), lambda i: (0, 0))],
        out_specs=[pl.BlockSpec((tm, d), lambda i: (jnp.minimum(i, n_p_tiles - 1), 0)),
                   pl.BlockSpec((tm, d), lambda i: (jnp.clip(i - n_p_tiles, 0, n_s_tiles - 1), 0))],
        out_shape=[jax.ShapeDtypeStruct((n_p, d), f32), jax.ShapeDtypeStruct((n_s, d), f32)],
        compiler_params=pltpu.CompilerParams(dimension_semantics=("arbitrary",),
                                             vmem_limit_bytes=VMEM_LIMIT),
        name="combine_ln2",
    )(h1, route, y2, y2, ln2_g, ln2_b)


def _shifted_taps(w, n_t, hist):
    k = w.shape[0]
    zero = jnp.zeros_like(w[0])
    wt = jnp.stack([jnp.stack([w[r - t] if 0 <= r - t < k else zero for r in range(hist)])
                    for t in range(n_t)])
    w2 = jnp.stack([jnp.stack([w[hist - t + j] if (j <= t and 0 <= hist - t + j < k) else zero
                               for j in range(n_t)]) for t in range(n_t)])
    return wt, w2


def kernel(x_prompt, x_sample, state_conv_a, state_conv_b, state_lru, meta_tokens, ln_in_g, ln_in_b, w_in, conv_a_w, conv_a_b, norm_a_g, norm_a_b, conv_b_w, conv_b_b, w_rg, b_rg, w_ig, b_ig, lru_lambda, w_out, ln1_g, ln1_b, w_router_group, b_router_group, w_router_expert, b_router_expert, w_gate, w_up, w_down, ln2_g, ln2_b):
    depth = w_in.shape[0]
    assert depth == 1, "single-layer trunk only"
    bp, tp, d = x_prompt.shape
    bs, ts, _ = x_sample.shape
    n_meta = meta_tokens.shape[0]
    width = conv_a_w.shape[2]
    ka, kb = conv_a_w.shape[1], conv_b_w.shape[1]
    n_e = w_gate.shape[1]
    alpha = (2.0 * depth) ** 0.25
    n_p, n_s = bp * tp, bs * ts
    n = n_p + n_s
    assert ts < ka and ts >= kb - 1
    assert n_p % TOKEN_TILE == 0 and n_s % TOKEN_TILE == 0 and tp % SEQ_TILE == 0
    assert bs % STEP_BATCH == 0 and n_p % STEP_BATCH == 0
    assert n_p % COMBINE_TILE == 0 and n_s % COMBINE_TILE == 0
    assert n_meta % BF16_ROWS == 0
    assert N_GROUPS * (1 + EXPERTS_PER_GROUP) <= ROUTE_LANES and n_e == N_GROUPS * EXPERTS_PER_GROUP

    row = lambda v: v.reshape(1, -1).astype(f32)
    ln_g, ln_b = row(ln_in_g), row(ln_in_b)
    w_in_b = w_in[0].astype(bf16)
    w_out_b = w_out[0].astype(bf16)
    mix_w = (conv_a_w[0], row(conv_a_b[0]), row(norm_a_g[0]), row(norm_a_b[0]),
             conv_b_w[0], row(conv_b_b[0]),
             jnp.concatenate([w_rg[0], w_ig[0]], axis=-1).astype(bf16),
             row(b_rg[0]), row(b_ig[0]), row(lru_lambda[0]))
    ha, hb = _round_up(ka - 1, SUBLANES), _round_up(kb - 1, SUBLANES)

    def pad_hist(s, h):
        return jnp.concatenate([jnp.zeros(s.shape[:1] + (h - s.shape[1], width), f32), s], axis=1)

    glu_m, bx_m, gate_m = _inproj(meta_tokens, meta_tokens, 0, ln_g, ln_b, w_in_b, n_meta)
    _, _, sa_m, sb_m, sh_m = _mixer_seq(
        glu_m, bx_m, gate_m, jnp.zeros((1, ha, width), f32), jnp.zeros((1, hb, width), f32),
        jnp.zeros((1, 1, width), f32), mix_w, 1, n_meta, n_meta)

    x_p = x_prompt.reshape(n_p, d)
    x_s = jnp.swapaxes(x_sample, 0, 1).reshape(n_s, d)
    glu, bx, gate = _inproj(x_p, x_s, n_s, ln_g, ln_b, w_in_b, TOKEN_TILE)

    bcast = lambda s: jnp.broadcast_to(s, (bp,) + s.shape[1:])
    a_p, b_p, nsa_p, nsb_p, nsh_p = _mixer_seq(glu, bx, gate, bcast(sa_m), bcast(sb_m), bcast(sh_m),
                                               mix_w, bp, tp, SEQ_TILE)

    nat = lambda v: jnp.swapaxes(v[n_p:].reshape(ts, bs, width), 0, 1)
    wat, wa2 = _shifted_taps(conv_a_w[0], ts, ka - 1)
    wbt, wb2 = _shifted_taps(conv_b_w[0], ts, kb - 1)
    a_s, b_s, nsa_s, nsb_s, nsh_s = _mixer_step(
        glu, bx, gate, n_p, nat(glu), nat(bx), state_conv_a[0], state_conv_b[0], state_lru[0],
        (wat, wa2, wbt, wb2), mix_w, bs, ts, STEP_BATCH)

    w_r = jnp.concatenate([w_router_group[0], w_router_expert[0]], axis=1)
    w_r = jnp.pad(w_r, ((0, 0), (0, ROUTE_LANES - w_r.shape[1])))
    wr_hi = w_r.astype(bf16)
    wr_lo = (w_r - wr_hi.astype(f32)).astype(bf16)
    b_r = jnp.concatenate([b_router_group[0], b_router_expert[0]])
    b_r = jnp.pad(b_r, (0, ROUTE_LANES - b_r.shape[0])).reshape(1, ROUTE_LANES)
    h1, h1p, route, cnt = _outproj(a_p, b_p, a_s, b_s, x_p, x_s, ln_g, ln_b, w_out_b,
                              row(ln1_g[0]), row(ln1_b[0]), wr_hi, wr_lo, b_r, alpha, TOKEN_TILE)

    blk = EXPERT_ROWS
    e_idx = route[:, 0:TOP_K].astype(i32)
    rank = route[:, 2:2 + TOP_K].astype(i32)
    counts = cnt[0, N_GROUPS:N_GROUPS + n_e].astype(i32)
    nblk = (counts + blk - 1) // blk
    pad_end = jnp.cumsum(nblk * blk)
    pad_start = pad_end - nblk * blk
    start_of = jnp.sum(jnp.where(e_idx[..., None] == jnp.arange(n_e, dtype=i32), pad_start, 0), axis=-1)
    dest = (start_of + rank).reshape(-1)
    p_max = _round_up(TOP_K * n + n_e * (blk - 1), blk)
    flat = jnp.full((p_max,), -1, i32).at[dest].set(jnp.arange(TOP_K * n, dtype=i32))
    pos = jnp.arange(p_max, dtype=i32)
    valid = flat >= 0
    src_rows = jnp.where(valid, flat // TOP_K, 0)
    dst_rows = jnp.where(valid, (flat % TOP_K) * n + flat // TOP_K, TOP_K * n + pos % (2 * blk))
    src_rows = jnp.concatenate([src_rows, jnp.zeros((blk,), i32)])
    dst_rows = jnp.concatenate([TOP_K * n + blk + jnp.arange(blk, dtype=i32), dst_rows])
    eff = lax.cummax(jnp.where(nblk > 0, jnp.arange(n_e, dtype=i32), 0))

    y2 = _experts(h1p, src_rows, dst_rows, pad_start.astype(i32), nblk, eff,
                  w_gate[0], w_up[0], w_down[0], blk)
    y_p, y_s = _combine(h1, route, y2, row(ln2_g[0]), row(ln2_b[0]), n_p, alpha, TOKEN_TILE)

    y_prompt = y_p.reshape(bp, tp, d)
    y_sample = jnp.swapaxes(y_s.reshape(ts, bs, d), 0, 1)
    return (y_prompt, y_sample,
            nsa_p[:, ha - (ka - 1):][None], nsb_p[:, hb - (kb - 1):][None], nsh_p.reshape(1, bp, width),
            nsa_s[None], nsb_s[None], nsh_s[None])
```

```python
import functools

import jax
import jax.numpy as jnp
from jax import lax
from jax.experimental import pallas as pl
from jax.experimental.pallas import tpu as pltpu

f32 = jnp.float32
bf16 = jnp.bfloat16
i32 = jnp.int32

CONV_HEADS_A = 8
LRU_HEADS = 8
LRU_C = 8.0
N_GROUPS = 8
EXPERTS_PER_GROUP = 8
TOP_K = 2
LN_EPS = 1e-5

SUBLANES = 8
LANES = 128
BF16_ROWS = 16
VMEM_LIMIT = 56 * 1024 * 1024

TOKEN_TILE = 256
OUTPROJ_SPLIT = 1
SEQ_TILE = 256
CONV_ROWS = 32
STEP_BATCH = 32
EXPERT_ROWS = 128
COMBINE_TILE = 128
ROUTE_LANES = 128
GATHER_SLOTS = 4
SCATTER_SLOTS = 4
GATHER_PRIORITY = 1
NEG_BIG = -1e30


def _round_up(x, m):
    return (x + m - 1) // m * m


def _layer_norm(x, g, b):
    mu = jnp.mean(x, axis=-1, keepdims=True)
    xc = x - mu
    var = jnp.mean(xc * xc, axis=-1, keepdims=True)
    return xc * lax.rsqrt(var + LN_EPS) * g + b


def _group_norm_silu(x, g, b, group):
    parts = []
    for c in range(x.shape[1] // group):
        xg = x[:, c * group:(c + 1) * group]
        mu = jnp.mean(xg, axis=-1, keepdims=True)
        xc = xg - mu
        var = jnp.mean(xc * xc, axis=-1, keepdims=True)
        parts.append(xc * lax.rsqrt(var + LN_EPS))
    y = jnp.concatenate(parts, axis=-1) * g + b
    return y * jax.nn.sigmoid(y)


def _lru_gates(cb, wg_ref, brg, big, lam):
    heads, hd, _ = wg_ref.shape
    cbb = cb.astype(bf16)
    rs, iz = [], []
    for h in range(heads):
        z = jnp.dot(cbb[:, h * hd:(h + 1) * hd], wg_ref[h], preferred_element_type=f32)
        rs.append(z[:, :hd])
        iz.append(z[:, hd:])
    r = jax.nn.sigmoid(jnp.concatenate(rs, axis=-1) + brg)
    i = jax.nn.sigmoid(jnp.concatenate(iz, axis=-1) + big)
    log_a = -LRU_C * r * jax.nn.softplus(-lam)
    a = jnp.exp(log_a)
    u = jnp.sqrt(-jnp.tanh(log_a) * (a * a + 1.0)) * (i * cb)
    return a, u


def _inproj_kernel(xa_ref, xb_ref, g_ref, b_ref, w_ref, glu_ref, bx_ref, gate_ref, *, n_a_tiles):
    i = pl.program_id(0)
    x = jnp.where(i < n_a_tiles, xa_ref[...], xb_ref[...])
    xn = _layer_norm(x, g_ref[...], b_ref[...]).astype(bf16)
    width = glu_ref.shape[1]
    ch = min(width, 512)
    for c in range(width // ch):
        lo = c * ch
        av = jnp.dot(xn, w_ref[:, lo:lo + ch], preferred_element_type=f32)
        ag = jnp.dot(xn, w_ref[:, width + lo:width + lo + ch], preferred_element_type=f32)
        glu_ref[:, lo:lo + ch] = av * jax.nn.sigmoid(ag)
        bx_ref[:, lo:lo + ch] = jnp.dot(xn, w_ref[:, 2 * width + lo:2 * width + lo + ch],
                                        preferred_element_type=f32)
        bg = jnp.dot(xn, w_ref[:, 3 * width + lo:3 * width + lo + ch], preferred_element_type=f32)
        gate_ref[:, lo:lo + ch] = jax.nn.gelu(bg)


def _inproj(xa, xb, n_b_rows, ln_g, ln_b, w_in_b, tm):
    na, d = xa.shape
    width = w_in_b.shape[1] // 4
    n_a_tiles = na // tm
    n_tiles = (na + n_b_rows) // tm
    n_b_tiles = max(n_tiles - n_a_tiles, 1)
    row_out = pl.BlockSpec((tm, width), lambda i: (i, 0))
    return pl.pallas_call(
        functools.partial(_inproj_kernel, n_a_tiles=n_a_tiles),
        grid=(n_tiles,),
        in_specs=[
            pl.BlockSpec((tm, d), lambda i: (jnp.minimum(i, n_a_tiles - 1), 0)),
            pl.BlockSpec((tm, d), lambda i: (jnp.clip(i - n_a_tiles, 0, n_b_tiles - 1), 0)),
            pl.BlockSpec((1, d), lambda i: (0, 0)),
            pl.BlockSpec((1, d), lambda i: (0, 0)),
            pl.BlockSpec((d, 4 * width), lambda i: (0, 0), pipeline_mode=pl.Buffered(1)),
        ],
        out_specs=[row_out, row_out, row_out],
        out_shape=[jax.ShapeDtypeStruct((na + n_b_rows, width), f32)] * 3,
        compiler_params=pltpu.CompilerParams(dimension_semantics=("arbitrary",),
                                             vmem_limit_bytes=VMEM_LIMIT),
        name="inproj",
    )(xa, xb, ln_g, ln_b, w_in_b)


def _mixer_seq_kernel(glu_ref, bx_ref, gate_ref, sa_ref, sb_ref, sh_ref,
                      wa_ref, ba_ref, ng_ref, nb_ref, wb_ref, bb_ref, wg_ref, brg_ref, big_ref, lam_ref,
                      aout_ref, bout_ref, nsa_ref, nsb_ref, nsh_ref,
                      wina, winb, xsa, xsb, ca_s, cb_s, a_s, u_s, hcar):
    tau = pl.program_id(1)
    tt, width = glu_ref.shape
    ka, kb = wa_ref.shape[0], wb_ref.shape[0]
    ha, hb = sa_ref.shape[0], sb_ref.shape[0]

    @pl.when(tau == 0)
    def _():
        wina[0:ha, :] = sa_ref[...]
        winb[0:hb, :] = sb_ref[...]
        hcar[...] = jnp.broadcast_to(sh_ref[...], hcar.shape)

    wina[ha:ha + tt, :] = glu_ref[...]
    winb[hb:hb + tt, :] = bx_ref[...]
    for s in sorted({(ha - (ka - 1) + k) % SUBLANES for k in range(ka)} - {0}):
        xsa[s - 1] = wina[pl.ds(s, tt + ha - SUBLANES), :]
    for s in sorted({(hb - (kb - 1) + k) % SUBLANES for k in range(kb)} - {0}):
        xsb[s - 1] = winb[pl.ds(s, tt + hb - SUBLANES), :]

    def tap(win, xs, off, r0, lanes):
        q, s = divmod(off, SUBLANES)
        src = win if s == 0 else xs.at[s - 1]
        return src[pl.ds(r0 + SUBLANES * q, SUBLANES), lanes]

    rows = min(CONV_ROWS, tt)
    for g in range(width // LANES):
        lanes = pl.ds(g * LANES, LANES)
        bcast = lambda ref, k: jnp.broadcast_to(ref[k:k + 1, lanes], (SUBLANES, LANES))
        wa_k = [bcast(wa_ref, k) for k in range(ka)]
        wb_k = [bcast(wb_ref, k) for k in range(kb)]
        ba_g, bb_g = bcast(ba_ref, 0), bcast(bb_ref, 0)

        def conv_block(rb, carry):
            for j in range(rows // SUBLANES):
                r0 = pl.multiple_of(rb * rows, rows) + j * SUBLANES
                acc = ba_g
                for k in range(ka):
                    acc = acc + wa_k[k] * tap(wina, xsa, ha - (ka - 1) + k, r0, lanes)
                ca_s[pl.ds(r0, SUBLANES), lanes] = acc
                accb = bb_g
                for k in range(kb):
                    accb = accb + wb_k[k] * tap(winb, xsb, hb - (kb - 1) + k, r0, lanes)
                cb_s[pl.ds(r0, SUBLANES), lanes] = accb
            return carry

        lax.fori_loop(0, tt // rows, conv_block, 0)

    a_out = _group_norm_silu(ca_s[...], ng_ref[...], nb_ref[...], width // CONV_HEADS_A)
    aout_ref[...] = a_out.astype(aout_ref.dtype)

    a, u = _lru_gates(cb_s[...], wg_ref, brg_ref[...], big_ref[...], lam_ref[...])
    row8 = lax.broadcasted_iota(i32, (tt, width), 0) & (SUBLANES - 1)
    d = 1
    while d < SUBLANES:
        m = row8 >= d
        a_sh = jnp.where(m, pltpu.roll(a, d, axis=0), 1.0)
        u_sh = jnp.where(m, pltpu.roll(u, d, axis=0), 0.0)
        u = a * u_sh + u
        a = a * a_sh
        d *= 2
    a_s[...] = a
    u_s[...] = u

    def scan_block(j, h):
        r0 = pl.multiple_of(j * SUBLANES, SUBLANES)
        hblk = a_s[pl.ds(r0, SUBLANES), :] * h + u_s[pl.ds(r0, SUBLANES), :]
        u_s[pl.ds(r0, SUBLANES), :] = hblk
        return jnp.broadcast_to(hblk[SUBLANES - 1:SUBLANES, :], h.shape)

    h_last = lax.fori_loop(0, tt // SUBLANES, scan_block, hcar[...])
    hcar[...] = h_last
    bout_ref[...] = (u_s[...] * gate_ref[...]).astype(bout_ref.dtype)

    tail_a = wina[tt:tt + ha, :]
    tail_b = winb[tt:tt + hb, :]
    wina[0:ha, :] = tail_a
    winb[0:hb, :] = tail_b
    nsa_ref[...] = tail_a
    nsb_ref[...] = tail_b
    nsh_ref[...] = h_last[0:1, :]


def _mixer_seq(glu, bx, gate, sa_pad, sb_pad, sh, mix_w, n_seq, seq_len, tt):
    wa, ba, ng, nb, wb, bb, wg, brg, big, lam = mix_w
    width = glu.shape[1]
    ha, hb = sa_pad.shape[1], sb_pad.shape[1]
    n_t = seq_len // tt
    row_in = pl.BlockSpec((tt, width), lambda b, t: (b * n_t + t, 0))

    def const(shape):
        return pl.BlockSpec(shape, lambda b, t: (0,) * len(shape))

    def per_seq(r):
        return pl.BlockSpec((None, r, width), lambda b, t: (b, 0, 0))

    return pl.pallas_call(
        _mixer_seq_kernel,
        grid=(n_seq, n_t),
        in_specs=[row_in, row_in, row_in, per_seq(ha), per_seq(hb), per_seq(1),
                  const(wa.shape), const(ba.shape), const(ng.shape), const(nb.shape),
                  const(wb.shape), const(bb.shape), const(wg.shape), const(brg.shape),
                  const(big.shape), const(lam.shape)],
        out_specs=[row_in, row_in, per_seq(ha), per_seq(hb), per_seq(1)],
        out_shape=[jax.ShapeDtypeStruct((n_seq * seq_len, width), bf16),
                   jax.ShapeDtypeStruct((n_seq * seq_len, width), bf16),
                   jax.ShapeDtypeStruct((n_seq, ha, width), f32),
                   jax.ShapeDtypeStruct((n_seq, hb, width), f32),
                   jax.ShapeDtypeStruct((n_seq, 1, width), f32)],
        scratch_shapes=[pltpu.VMEM((ha + tt, width), f32),
                        pltpu.VMEM((hb + tt, width), f32),
                        pltpu.VMEM((SUBLANES - 1, tt + ha - SUBLANES, width), f32),
                        pltpu.VMEM((SUBLANES - 1, tt + hb - SUBLANES, width), f32),
                        pltpu.VMEM((tt, width), f32),
                        pltpu.VMEM((tt, width), f32),
                        pltpu.VMEM((tt, width), f32),
                        pltpu.VMEM((tt, width), f32),
                        pltpu.VMEM((SUBLANES, width), f32)],
        compiler_params=pltpu.CompilerParams(dimension_semantics=("arbitrary", "arbitrary"),
                                             vmem_limit_bytes=VMEM_LIMIT),
        name="mixer_seq",
    )(glu, bx, gate, sa_pad, sb_pad, sh, wa, ba, ng, nb, wb, bb, wg, brg, big, lam)


def _mixer_step_kernel(glu_ref, bx_ref, gate_ref, glun_ref, bxn_ref, sa_ref, sb_ref, sh_ref,
                       wat_ref, wa2_ref, wbt_ref, wb2_ref,
                       ba_ref, ng_ref, nb_ref, bb_ref, wg_ref, brg_ref, big_ref, lam_ref,
                       aout_ref, bout_ref, nsa_ref, nsb_ref, nsh_ref,
                       hista, histb, hcar):
    t = pl.program_id(1)
    n_t = hista.shape[0]
    width = glu_ref.shape[1]
    ka1, kb1 = sa_ref.shape[1], sb_ref.shape[1]

    @pl.when(t == 0)
    def _():
        hista[...] = jnp.zeros_like(hista)
        histb[...] = jnp.zeros_like(histb)
        hcar[...] = sh_ref[...]
        nsa_ref[:, 0:ka1 - n_t, :] = sa_ref[:, n_t:ka1, :]
        nsa_ref[:, ka1 - n_t:ka1, :] = glun_ref[...]
        nsb_ref[...] = bxn_ref[:, n_t - kb1:n_t, :]

    hista[t] = glu_ref[...]
    histb[t] = bx_ref[...]

    ca = ba_ref[...] + jnp.sum(sa_ref[...] * wat_ref[...][None], axis=1)
    cb = bb_ref[...] + jnp.sum(sb_ref[...] * wbt_ref[...][None], axis=1)
    for j in range(n_t):
        ca = ca + wa2_ref[j:j + 1, :] * hista[j]
        cb = cb + wb2_ref[j:j + 1, :] * histb[j]
    a_out = _group_norm_silu(ca, ng_ref[...], nb_ref[...], width // CONV_HEADS_A)
    aout_ref[...] = a_out.astype(aout_ref.dtype)

    a, u = _lru_gates(cb, wg_ref, brg_ref[...], big_ref[...], lam_ref[...])
    h = a * hcar[...] + u
    hcar[...] = h
    nsh_ref[...] = h
    bout_ref[...] = (h * gate_ref[...]).astype(bout_ref.dtype)


def _mixer_step(glu, bx, gate, row_off, glu_nat, bx_nat, sa, sb, sh, step_w, mix_w, n_seq, n_t, bt):
    wat, wa2, wbt, wb2 = step_w
    wa, ba, ng, nb, wb, bb, wg, brg, big, lam = mix_w
    width = glu.shape[1]
    ka1, kb1 = sa.shape[1], sb.shape[1]
    off = row_off // bt
    nb_tiles = n_seq // bt
    row_in = pl.BlockSpec((bt, width), lambda j, t: (off + t * nb_tiles + j, 0))
    row_out = pl.BlockSpec((bt, width), lambda j, t: (t * nb_tiles + j, 0))

    def const(shape):
        return pl.BlockSpec(shape, lambda j, t: (0,) * len(shape))

    def per_t(arr):
        return pl.BlockSpec((None,) + arr.shape[1:], lambda j, t: (t, 0, 0))

    def per_b(r):
        return pl.BlockSpec((bt, r, width), lambda j, t: (j, 0, 0))

    state_h = pl.BlockSpec((bt, width), lambda j, t: (j, 0))
    return pl.pallas_call(
        _mixer_step_kernel,
        grid=(nb_tiles, n_t),
        in_specs=[row_in, row_in, row_in, per_b(n_t), per_b(n_t), per_b(ka1), per_b(kb1), state_h,
                  per_t(wat), per_t(wa2), per_t(wbt), per_t(wb2),
                  const(ba.shape), const(ng.shape), const(nb.shape), const(bb.shape),
                  const(wg.shape), const(brg.shape), const(big.shape), const(lam.shape)],
        out_specs=[row_out, row_out, per_b(ka1), per_b(kb1), state_h],
        out_shape=[jax.ShapeDtypeStruct((n_seq * n_t, width), bf16),
                   jax.ShapeDtypeStruct((n_seq * n_t, width), bf16),
                   jax.ShapeDtypeStruct((n_seq, ka1, width), f32),
                   jax.ShapeDtypeStruct((n_seq, kb1, width), f32),
                   jax.ShapeDtypeStruct((n_seq, width), f32)],
        scratch_shapes=[pltpu.VMEM((n_t, bt, width), f32),
                        pltpu.VMEM((n_t, bt, width), f32),
                        pltpu.VMEM((bt, width), f32)],
        compiler_params=pltpu.CompilerParams(dimension_semantics=("arbitrary", "arbitrary"),
                                             vmem_limit_bytes=VMEM_LIMIT),
        name="mixer_step",
    )(glu, bx, gate, glu_nat, bx_nat, sa, sb, sh, wat, wa2, wbt, wb2,
      ba, ng, nb, bb, wg, brg, big, lam)


def _outproj_kernel(*refs, split, **kw):
    for h in range(split):
        _outproj_rows(h, split, *refs, **kw)


def _outproj_rows(h, split, ap_ref, bp_ref, as_ref, bs_ref, xp_ref, xs_ref, ling_ref, linb_ref, wo_ref,
                  l1g_ref, l1b_ref, wrh_ref, wrl_ref, br_ref,
                  h1_ref, h1p_ref, route_ref, cnt_ref, carry, *, n_p_tiles, alpha):
    i = pl.program_id(0)
    is_p = i < n_p_tiles
    sub = ap_ref.shape[0] // split
    per_tok = h1p_ref.shape[0] // ap_ref.shape[0]
    rows = pl.ds(h * sub, sub)
    ap_ref, bp_ref, as_ref, bs_ref = (r.at[rows] for r in (ap_ref, bp_ref, as_ref, bs_ref))
    xp_ref, xs_ref, h1_ref, route_ref = (r.at[rows] for r in (xp_ref, xs_ref, h1_ref, route_ref))
    h1p_ref = h1p_ref.at[pl.ds(h * sub * per_tok, sub * per_tok)]
    a = jnp.where(is_p, ap_ref[...], as_ref[...])
    b = jnp.where(is_p, bp_ref[...], bs_ref[...])
    x = jnp.where(is_p, xp_ref[...], xs_ref[...])
    width = a.shape[1]
    mix = (jnp.dot(a, wo_ref[0:width, :], preferred_element_type=f32)
           + jnp.dot(b, wo_ref[width:2 * width, :], preferred_element_type=f32))
    h0 = _layer_norm(x, ling_ref[...], linb_ref[...])
    h1 = _layer_norm(alpha * h0 + mix, l1g_ref[...], l1b_ref[...])
    h1_ref[...] = h1

    xh = h1.astype(bf16)
    xh32 = xh.astype(f32)
    half = h1.shape[1] // 2
    lo = lax.shift_right_logical(lax.bitcast_convert_type(xh32[:, :half], jnp.uint32), jnp.uint32(16))
    hi = lax.bitcast_convert_type(xh32[:, half:], jnp.uint32) & jnp.uint32(0xFFFF0000)
    word = lo | hi
    n_sub = half // LANES
    for s in range(n_sub):
        h1p_ref[pl.ds(s, word.shape[0], stride=n_sub), :] = word[:, s * LANES:(s + 1) * LANES]

    xl = (h1 - xh32).astype(bf16)
    z = (jnp.dot(xh, wrh_ref[...], preferred_element_type=f32)
         + jnp.dot(xl, wrh_ref[...], preferred_element_type=f32)
         + jnp.dot(xh, wrl_ref[...], preferred_element_type=f32)) + br_ref[...]
    tm = z.shape[0]
    lane = lax.broadcasted_iota(i32, z.shape, 1).astype(f32)
    n_g, epg = float(N_GROUPS), float(EXPERTS_PER_GROUP)
    far = float(2 * ROUTE_LANES)

    gm = lane < n_g
    zg = jnp.where(gm, z, NEG_BIG)
    pg = jnp.where(gm, jnp.exp(zg - jnp.max(zg, axis=-1, keepdims=True)), 0.0)
    pg = pg / jnp.sum(pg, axis=-1, keepdims=True)
    g_top = jnp.max(pg, axis=-1, keepdims=True)
    g_idx = jnp.min(jnp.where(gm & (pg == g_top), lane, far), axis=-1, keepdims=True)

    lo = n_g + g_idx * epg
    em = (lane >= lo) & (lane < lo + epg)
    ze = jnp.where(em, z, NEG_BIG)
    pe = jnp.where(em, jnp.exp(ze - jnp.max(ze, axis=-1, keepdims=True)), 0.0)
    pe = pe / jnp.sum(pe, axis=-1, keepdims=True)
    v1 = jnp.max(jnp.where(em, pe, -1.0), axis=-1, keepdims=True)
    l1 = jnp.min(jnp.where(em & (pe == v1), lane, far), axis=-1, keepdims=True)
    pe2 = jnp.where(em & (lane != l1), pe, -1.0)
    v2 = jnp.max(pe2, axis=-1, keepdims=True)
    l2 = jnp.min(jnp.where(pe2 == v2, lane, far), axis=-1, keepdims=True)
    den = v1 + v2
    gate1 = v1 / den * g_top
    gate2 = v2 / den * g_top

    @pl.when((i == 0) & (h == 0))
    def _():
        carry[...] = jnp.zeros_like(carry)

    o1 = lane == l1
    o2 = lane == l2
    onehot = jnp.where(o1 | o2, 1.0, 0.0)
    ri = lax.broadcasted_iota(i32, (tm, tm), 0)
    ci = lax.broadcasted_iota(i32, (tm, tm), 1)
    tri = jnp.where(ci < ri, 1.0, 0.0).astype(bf16)
    prefix = jnp.dot(tri, onehot.astype(bf16), preferred_element_type=f32) + carry[0:1, :]
    rank1 = jnp.sum(jnp.where(o1, prefix, 0.0), axis=-1, keepdims=True)
    rank2 = jnp.sum(jnp.where(o2, prefix, 0.0), axis=-1, keepdims=True)
    new_carry = carry[...] + jnp.sum(onehot, axis=0, keepdims=True)
    carry[...] = new_carry
    cnt_ref[...] = new_carry

    route = jnp.where(lane == 0.0, l1 - n_g, 0.0)
    route = jnp.where(lane == 1.0, l2 - n_g, route)
    route = jnp.where(lane == 2.0, rank1, route)
    route = jnp.where(lane == 3.0, rank2, route)
    route = jnp.where(lane == 4.0, gate1, route)
    route = jnp.where(lane == 5.0, gate2, route)
    route_ref[...] = route


def _outproj(a_p, b_p, a_s, b_s, x_p, x_s, ln_in_g, ln_in_b, w_out_b, ln1_g, ln1_b,
             wr_hi, wr_lo, b_r, alpha, tm):
    n_p, width = a_p.shape
    n_s = a_s.shape[0]
    d = x_p.shape[1]
    n_p_tiles = n_p // tm
    n_s_tiles = n_s // tm
    n_tiles = n_p_tiles + n_s_tiles

    def p_map(i):
        return (jnp.minimum(i, n_p_tiles - 1), 0)

    def s_map(i):
        return (jnp.clip(i - n_p_tiles, 0, n_s_tiles - 1), 0)

    def const(shape, **kw):
        return pl.BlockSpec(shape, lambda i: (0,) * len(shape), **kw)

    return pl.pallas_call(
        functools.partial(_outproj_kernel, split=OUTPROJ_SPLIT, n_p_tiles=n_p_tiles, alpha=alpha),
        grid=(n_tiles,),
        in_specs=[pl.BlockSpec((tm, width), p_map), pl.BlockSpec((tm, width), p_map),
                  pl.BlockSpec((tm, width), s_map), pl.BlockSpec((tm, width), s_map),
                  pl.BlockSpec((tm, d), p_map), pl.BlockSpec((tm, d), s_map),
                  const((1, d)), const((1, d)),
                  const((2 * width, d), pipeline_mode=pl.Buffered(1)),
                  const((1, d)), const((1, d)),
                  const((d, ROUTE_LANES)), const((d, ROUTE_LANES)), const((1, ROUTE_LANES))],
        out_specs=[pl.BlockSpec((tm, d), lambda i: (i, 0)),
                   pl.BlockSpec((tm * (d // 2 // LANES), LANES), lambda i: (i, 0)),
                   pl.BlockSpec((tm, ROUTE_LANES), lambda i: (i, 0)),
                   const((SUBLANES, ROUTE_LANES))],
        out_shape=[jax.ShapeDtypeStruct((n_p + n_s, d), f32),
                   jax.ShapeDtypeStruct(((n_p + n_s) * (d // 2 // LANES), LANES), jnp.uint32),
                   jax.ShapeDtypeStruct((n_p + n_s, ROUTE_LANES), f32),
                   jax.ShapeDtypeStruct((SUBLANES, ROUTE_LANES), f32)],
        scratch_shapes=[pltpu.VMEM((SUBLANES, ROUTE_LANES), f32)],
        compiler_params=pltpu.CompilerParams(dimension_semantics=("arbitrary",),
                                             vmem_limit_bytes=VMEM_LIMIT),
        name="outproj_router",
    )(a_p, b_p, a_s, b_s, x_p, x_s, ln_in_g, ln_in_b, w_out_b, ln1_g, ln1_b, wr_hi, wr_lo, b_r)


def _expert_kernel(src_ref, dst_ref, start_ref, nblk_ref,
                   h1_hbm, wg_ref, wu_ref, wd_ref, y2_hbm,
                   xbuf, ybuf, wgb, wub, wdb, gsem, osem, *, blk, n_rows):
    e = pl.program_id(0)
    n_e = pl.num_programs(0)
    nb = nblk_ref[e]
    g0 = start_ref[e] // blk

    xs = xbuf.shape[1] // blk
    ys = ybuf.shape[1] // blk

    def gather(g, slot):
        for r in range(blk):
            tok = src_ref[g * blk + r]
            pltpu.make_async_copy(h1_hbm.at[pl.ds(pl.multiple_of(tok * xs, xs), xs), :],
                                  xbuf.at[slot, pl.ds(r * xs, xs), :], gsem.at[slot]
                                  ).start(priority=GATHER_PRIORITY)

    def scatter(g, slot):
        for r in range(blk):
            row = dst_ref[(g + 1) * blk + r]
            pltpu.make_async_copy(ybuf.at[slot, pl.ds(r * ys, ys), :],
                                  y2_hbm.at[pl.ds(pl.multiple_of(row * ys, ys), ys), :],
                                  osem.at[slot]).start(priority=r % 2)

    def dump_copy(slot):
        dst = y2_hbm.at[pl.ds((n_rows + slot * blk) * ys, blk * ys), :]
        return pltpu.make_async_copy(ybuf.at[slot], dst, osem.at[slot])

    def wait_gather(slot):
        pltpu.make_async_copy(h1_hbm.at[pl.ds(0, blk * xs), :], xbuf.at[slot], gsem.at[slot]).wait()

    nx, ny = xbuf.shape[0], ybuf.shape[0]
    ahead = nx - 1

    @pl.when(e == 0)
    def _():
        ybuf[...] = jnp.zeros(ybuf.shape, ybuf.dtype)
        for s in range(ny - 1):
            dump_copy(s).start()
        for g in range(ahead):
            gather(g, g)

    @pl.when(nb > 0)
    def _():
        wgb[...] = wg_ref[...].astype(bf16)
        wub[...] = wu_ref[...].astype(bf16)
        wdb[...] = wd_ref[...].astype(bf16)

        def chunk(c, carry):
            g = g0 + c
            slot = g % nx
            yslot = g % ny
            wait_gather(slot)
            dump_copy(yslot).wait()
            gather(g + ahead, (g + ahead) % nx)
            scatter(g - 1, (g + ny - 1) % ny)
            xw = [xbuf[slot, pl.ds(s, blk, stride=xs), :] for s in range(xs)]
            x_lo = [lax.bitcast_convert_type(lax.shift_left(w, jnp.uint32(16)), f32) for w in xw]
            x_hi = [lax.bitcast_convert_type(w & jnp.uint32(0xFFFF0000), f32) for w in xw]
            x = jnp.concatenate(x_lo + x_hi, axis=1).astype(bf16)
            hg = jnp.dot(x, wgb[...], preferred_element_type=f32)
            hu = jnp.dot(x, wub[...], preferred_element_type=f32)
            hid = (hg * jax.nn.sigmoid(hg) * hu).astype(bf16)
            y = jnp.dot(hid, wdb[...], preferred_element_type=f32)
            for s in range(ys):
                ybuf[yslot, pl.ds(s, blk, stride=ys), :] = y[:, s * LANES:(s + 1) * LANES]
            return carry

        lax.fori_loop(0, nb, chunk, 0)

    @pl.when(e == n_e - 1)
    def _():
        g_end = g0 + nb
        scatter(g_end - 1, (g_end + ny - 1) % ny)
        for s in range(ny):
            dump_copy(s).wait()
        for j in range(ahead):
            wait_gather((g_end + j) % nx)


def _experts(h1p, src_rows, dst_rows, pad_start, nblk, w_gate, w_up, w_down, blk):
    n_e, d, d_e = w_gate.shape
    xs, ys = d // 2 // LANES, d // LANES
    n = h1p.shape[0] // xs
    n_rows = TOP_K * n
    grid_spec = pltpu.PrefetchScalarGridSpec(
        num_scalar_prefetch=4,
        grid=(n_e,),
        in_specs=[pl.BlockSpec(memory_space=pl.ANY),
                  pl.BlockSpec((None, d, d_e), lambda e, *_: (e, 0, 0)),
                  pl.BlockSpec((None, d, d_e), lambda e, *_: (e, 0, 0)),
                  pl.BlockSpec((None, d_e, d), lambda e, *_: (e, 0, 0))],
        out_specs=pl.BlockSpec(memory_space=pl.ANY),
        scratch_shapes=[pltpu.VMEM((GATHER_SLOTS, blk * xs, LANES), jnp.uint32),
                        pltpu.VMEM((SCATTER_SLOTS, blk * ys, LANES), f32),
                        pltpu.VMEM((d, d_e), bf16),
                        pltpu.VMEM((d, d_e), bf16),
                        pltpu.VMEM((d_e, d), bf16),
                        pltpu.SemaphoreType.DMA((GATHER_SLOTS,)),
                        pltpu.SemaphoreType.DMA((SCATTER_SLOTS,))],
    )
    return pl.pallas_call(
        functools.partial(_expert_kernel, blk=blk, n_rows=n_rows),
        grid_spec=grid_spec,
        out_shape=jax.ShapeDtypeStruct(((n_rows + SCATTER_SLOTS * blk) * ys, LANES), f32),
        compiler_params=pltpu.CompilerParams(dimension_semantics=("arbitrary",),
                                             vmem_limit_bytes=VMEM_LIMIT),
        name="experts",
    )(src_rows, dst_rows, pad_start, nblk, h1p, w_gate, w_up, w_down)


def _combine_kernel(h1_ref, route_ref, ya_ref, yb_ref, l2g_ref, l2b_ref, yp_ref, ysm_ref,
                    *, n_p_tiles, alpha):
    i = pl.program_id(0)
    route = route_ref[...]
    tm = route.shape[0]
    ys = ya_ref.shape[0] // tm

    def token_rows(ref):
        return jnp.concatenate([ref[pl.ds(s, tm, stride=ys), :] for s in range(ys)], axis=1)

    y = token_rows(ya_ref) * route[:, 4:5] + token_rows(yb_ref) * route[:, 5:6]
    out = _layer_norm(alpha * h1_ref[...] + y, l2g_ref[...], l2b_ref[...])

    @pl.when(i < n_p_tiles)
    def _():
        yp_ref[...] = out

    @pl.when(i >= n_p_tiles)
    def _():
        ysm_ref[...] = out


def _combine(h1, route, y2, ln2_g, ln2_b, n_p, alpha, tm):
    n, d = h1.shape
    n_s = n - n_p
    n_tiles = n // tm
    n_p_tiles = n_p // tm
    n_s_tiles = n_s // tm
    return pl.pallas_call(
        functools.partial(_combine_kernel, n_p_tiles=n_p_tiles, alpha=alpha),
        grid=(n_tiles,),
        in_specs=[pl.BlockSpec((tm, d), lambda i: (i, 0)),
                  pl.BlockSpec((tm, ROUTE_LANES), lambda i: (i, 0)),
                  pl.BlockSpec((tm * (d // LANES), LANES), lambda i: (i, 0)),
                  pl.BlockSpec((tm * (d // LANES), LANES), lambda i: (n_tiles + i, 0)),
                  pl.BlockSpec((1, d), lambda i: (0, 0)),
                  pl.BlockSpec((1, d), lambda i: (0, 0))],
        out_specs=[pl.BlockSpec((tm, d), lambda i: (jnp.minimum(i, n_p_tiles - 1), 0)),
                   pl.BlockSpec((tm, d), lambda i: (jnp.clip(i - n_p_tiles, 0, n_s_tiles - 1), 0))],
        out_shape=[jax.ShapeDtypeStruct((n_p, d), f32), jax.ShapeDtypeStruct((n_s, d), f32)],
        compiler_params=pltpu.CompilerParams(dimension_semantics=("arbitrary",),
                                             vmem_limit_bytes=VMEM_LIMIT),
        name="combine_ln2",
    )(h1, route, y2, y2, ln2_g, ln2_b)


def _shifted_taps(w, n_t, hist):
    k = w.shape[0]
    zero = jnp.zeros_like(w[0])
    wt = jnp.stack([jnp.stack([w[r - t] if 0 <= r - t < k else zero for r in range(hist)])
                    for t in range(n_t)])
    w2 = jnp.stack([jnp.stack([w[hist - t + j] if (j <= t and 0 <= hist - t + j < k) else zero
                               for j in range(n_t)]) for t in range(n_t)])
    return wt, w2


def kernel(x_prompt, x_sample, state_conv_a, state_conv_b, state_lru, meta_tokens, ln_in_g, ln_in_b, w_in, conv_a_w, conv_a_b, norm_a_g, norm_a_b, conv_b_w, conv_b_b, w_rg, b_rg, w_ig, b_ig, lru_lambda, w_out, ln1_g, ln1_b, w_router_group, b_router_group, w_router_expert, b_router_expert, w_gate, w_up, w_down, ln2_g, ln2_b):
    depth = w_in.shape[0]
    assert depth == 1, "single-layer trunk only"
    bp, tp, d = x_prompt.shape
    bs, ts, _ = x_sample.shape
    n_meta = meta_tokens.shape[0]
    width = conv_a_w.shape[2]
    ka, kb = conv_a_w.shape[1], conv_b_w.shape[1]
    n_e = w_gate.shape[1]
    alpha = (2.0 * depth) ** 0.25
    n_p, n_s = bp * tp, bs * ts
    n = n_p + n_s
    assert ts < ka and ts >= kb - 1
    assert n_p % TOKEN_TILE == 0 and n_s % TOKEN_TILE == 0 and tp % SEQ_TILE == 0
    assert bs % STEP_BATCH == 0 and n_p % STEP_BATCH == 0
    assert n_p % COMBINE_TILE == 0 and n_s % COMBINE_TILE == 0
    assert n_meta % BF16_ROWS == 0
    assert N_GROUPS * (1 + EXPERTS_PER_GROUP) <= ROUTE_LANES and n_e == N_GROUPS * EXPERTS_PER_GROUP

    row = lambda v: v.reshape(1, -1).astype(f32)
    ln_g, ln_b = row(ln_in_g), row(ln_in_b)
    w_in_b = w_in[0].astype(bf16)
    w_out_b = w_out[0].astype(bf16)
    mix_w = (conv_a_w[0], row(conv_a_b[0]), row(norm_a_g[0]), row(norm_a_b[0]),
             conv_b_w[0], row(conv_b_b[0]),
             jnp.concatenate([w_rg[0], w_ig[0]], axis=-1).astype(bf16),
             row(b_rg[0]), row(b_ig[0]), row(lru_lambda[0]))
    ha, hb = _round_up(ka - 1, SUBLANES), _round_up(kb - 1, SUBLANES)

    def pad_hist(s, h):
        return jnp.concatenate([jnp.zeros(s.shape[:1] + (h - s.shape[1], width), f32), s], axis=1)

    glu_m, bx_m, gate_m = _inproj(meta_tokens, meta_tokens, 0, ln_g, ln_b, w_in_b, n_meta)
    _, _, sa_m, sb_m, sh_m = _mixer_seq(
        glu_m, bx_m, gate_m, jnp.zeros((1, ha, width), f32), jnp.zeros((1, hb, width), f32),
        jnp.zeros((1, 1, width), f32), mix_w, 1, n_meta, n_meta)

    x_p = x_prompt.reshape(n_p, d)
    x_s = jnp.swapaxes(x_sample, 0, 1).reshape(n_s, d)
    glu, bx, gate = _inproj(x_p, x_s, n_s, ln_g, ln_b, w_in_b, TOKEN_TILE)

    bcast = lambda s: jnp.broadcast_to(s, (bp,) + s.shape[1:])
    a_p, b_p, nsa_p, nsb_p, nsh_p = _mixer_seq(glu, bx, gate, bcast(sa_m), bcast(sb_m), bcast(sh_m),
                                               mix_w, bp, tp, SEQ_TILE)

    nat = lambda v: jnp.swapaxes(v[n_p:].reshape(ts, bs, width), 0, 1)
    wat, wa2 = _shifted_taps(conv_a_w[0], ts, ka - 1)
    wbt, wb2 = _shifted_taps(conv_b_w[0], ts, kb - 1)
    a_s, b_s, nsa_s, nsb_s, nsh_s = _mixer_step(
        glu, bx, gate, n_p, nat(glu), nat(bx), state_conv_a[0], state_conv_b[0], state_lru[0],
        (wat, wa2, wbt, wb2), mix_w, bs, ts, STEP_BATCH)

    w_r = jnp.concatenate([w_router_group[0], w_router_expert[0]], axis=1)
    w_r = jnp.pad(w_r, ((0, 0), (0, ROUTE_LANES - w_r.shape[1])))
    wr_hi = w_r.astype(bf16)
    wr_lo = (w_r - wr_hi.astype(f32)).astype(bf16)
    b_r = jnp.concatenate([b_router_group[0], b_router_expert[0]])
    b_r = jnp.pad(b_r, (0, ROUTE_LANES - b_r.shape[0])).reshape(1, ROUTE_LANES)
    h1, h1p, route, cnt = _outproj(a_p, b_p, a_s, b_s, x_p, x_s, ln_g, ln_b, w_out_b,
                              row(ln1_g[0]), row(ln1_b[0]), wr_hi, wr_lo, b_r, alpha, TOKEN_TILE)

    blk = EXPERT_ROWS
    e_idx = route[:, 0:TOP_K].astype(i32)
    rank = route[:, 2:2 + TOP_K].astype(i32)
    counts = cnt[0, N_GROUPS:N_GROUPS + n_e].astype(i32)
    nblk = (counts + blk - 1) // blk
    pad_end = jnp.cumsum(nblk * blk)
    pad_start = pad_end - nblk * blk
    start_of = jnp.sum(jnp.where(e_idx[..., None] == jnp.arange(n_e, dtype=i32), pad_start, 0), axis=-1)
    dest = (start_of + rank).reshape(-1)
    p_max = _round_up(TOP_K * n + n_e * (blk - 1), blk)
    flat = jnp.full((p_max,), -1, i32).at[dest].set(jnp.arange(TOP_K * n, dtype=i32))
    pos = jnp.arange(p_max, dtype=i32)
    valid = flat >= 0
    src_rows = jnp.where(valid, flat // TOP_K, 0)
    dst_rows = jnp.where(valid, (flat % TOP_K) * n + flat // TOP_K, TOP_K * n + pos % (SCATTER_SLOTS * blk))
    src_rows = jnp.concatenate([src_rows, jnp.zeros(((GATHER_SLOTS - 1) * blk,), i32)])
    dst_rows = jnp.concatenate([TOP_K * n + (SCATTER_SLOTS - 1) * blk + jnp.arange(blk, dtype=i32), dst_rows])
    y2 = _experts(h1p, src_rows, dst_rows, pad_start.astype(i32), nblk,
                  w_gate[0], w_up[0], w_down[0], blk)
    y_p, y_s = _combine(h1, route, y2, row(ln2_g[0]), row(ln2_b[0]), n_p, alpha, TOKEN_TILE)

    y_prompt = y_p.reshape(bp, tp, d)
    y_sample = jnp.swapaxes(y_s.reshape(ts, bs, d), 0, 1)
    return (y_prompt, y_sample,
            nsa_p[:, ha - (ka - 1):][None], nsb_p[:, hb - (kb - 1):][None], nsh_p.reshape(1, bp, width),
            nsa_s[None], nsb_s[None], nsh_s[None])
```

```python
import functools

import jax
import jax.numpy as jnp
from jax import lax
from jax.experimental import pallas as pl
from jax.experimental.pallas import tpu as pltpu

f32 = jnp.float32
bf16 = jnp.bfloat16
i32 = jnp.int32

CONV_HEADS_A = 8
LRU_HEADS = 8
LRU_C = 8.0
N_GROUPS = 8
EXPERTS_PER_GROUP = 8
TOP_K = 2
LN_EPS = 1e-5

SUBLANES = 8
LANES = 128
BF16_ROWS = 16
VMEM_LIMIT = 56 * 1024 * 1024

TOKEN_TILE = 256
OUTPROJ_SPLIT = 1
SEQ_TILE = 256
CONV_ROWS = 32
STEP_BATCH = 32
EXPERT_ROWS = 160
COMBINE_TILE = 128
ROUTE_LANES = 128
GATHER_SLOTS = 4
SCATTER_SLOTS = 4
GATHER_PRIORITY = 1
NEG_BIG = -1e30


def _round_up(x, m):
    return (x + m - 1) // m * m


def _layer_norm(x, g, b):
    mu = jnp.mean(x, axis=-1, keepdims=True)
    xc = x - mu
    var = jnp.mean(xc * xc, axis=-1, keepdims=True)
    return xc * lax.rsqrt(var + LN_EPS) * g + b


def _group_norm_silu(x, g, b, group):
    parts = []
    for c in range(x.shape[1] // group):
        xg = x[:, c * group:(c + 1) * group]
        mu = jnp.mean(xg, axis=-1, keepdims=True)
        xc = xg - mu
        var = jnp.mean(xc * xc, axis=-1, keepdims=True)
        parts.append(xc * lax.rsqrt(var + LN_EPS))
    y = jnp.concatenate(parts, axis=-1) * g + b
    return y * jax.nn.sigmoid(y)


def _lru_gates(cb, wg_ref, brg, big, lam):
    heads, hd, _ = wg_ref.shape
    cbb = cb.astype(bf16)
    rs, iz = [], []
    for h in range(heads):
        z = jnp.dot(cbb[:, h * hd:(h + 1) * hd], wg_ref[h], preferred_element_type=f32)
        rs.append(z[:, :hd])
        iz.append(z[:, hd:])
    r = jax.nn.sigmoid(jnp.concatenate(rs, axis=-1) + brg)
    i = jax.nn.sigmoid(jnp.concatenate(iz, axis=-1) + big)
    log_a = -LRU_C * r * jax.nn.softplus(-lam)
    a = jnp.exp(log_a)
    u = jnp.sqrt(-jnp.tanh(log_a) * (a * a + 1.0)) * (i * cb)
    return a, u


def _inproj_kernel(xa_ref, xb_ref, g_ref, b_ref, w_ref, glu_ref, bx_ref, gate_ref, *, n_a_tiles):
    i = pl.program_id(0)
    x = jnp.where(i < n_a_tiles, xa_ref[...], xb_ref[...])
    xn = _layer_norm(x, g_ref[...], b_ref[...]).astype(bf16)
    width = glu_ref.shape[1]
    ch = min(width, 512)
    for c in range(width // ch):
        lo = c * ch
        av = jnp.dot(xn, w_ref[:, lo:lo + ch], preferred_element_type=f32)
        ag = jnp.dot(xn, w_ref[:, width + lo:width + lo + ch], preferred_element_type=f32)
        glu_ref[:, lo:lo + ch] = av * jax.nn.sigmoid(ag)
        bx_ref[:, lo:lo + ch] = jnp.dot(xn, w_ref[:, 2 * width + lo:2 * width + lo + ch],
                                        preferred_element_type=f32)
        bg = jnp.dot(xn, w_ref[:, 3 * width + lo:3 * width + lo + ch], preferred_element_type=f32)
        gate_ref[:, lo:lo + ch] = jax.nn.gelu(bg)


def _inproj(xa, xb, n_b_rows, ln_g, ln_b, w_in_b, tm):
    na, d = xa.shape
    width = w_in_b.shape[1] // 4
    n_a_tiles = na // tm
    n_tiles = (na + n_b_rows) // tm
    n_b_tiles = max(n_tiles - n_a_tiles, 1)
    row_out = pl.BlockSpec((tm, width), lambda i: (i, 0))
    return pl.pallas_call(
        functools.partial(_inproj_kernel, n_a_tiles=n_a_tiles),
        grid=(n_tiles,),
        in_specs=[
            pl.BlockSpec((tm, d), lambda i: (jnp.minimum(i, n_a_tiles - 1), 0)),
            pl.BlockSpec((tm, d), lambda i: (jnp.clip(i - n_a_tiles, 0, n_b_tiles - 1), 0)),
            pl.BlockSpec((1, d), lambda i: (0, 0)),
            pl.BlockSpec((1, d), lambda i: (0, 0)),
            pl.BlockSpec((d, 4 * width), lambda i: (0, 0), pipeline_mode=pl.Buffered(1)),
        ],
        out_specs=[row_out, row_out, row_out],
        out_shape=[jax.ShapeDtypeStruct((na + n_b_rows, width), f32)] * 3,
        compiler_params=pltpu.CompilerParams(dimension_semantics=("arbitrary",),
                                             vmem_limit_bytes=VMEM_LIMIT),
        name="inproj",
    )(xa, xb, ln_g, ln_b, w_in_b)


def _mixer_seq_kernel(glu_ref, bx_ref, gate_ref, sa_ref, sb_ref, sh_ref,
                      wa_ref, ba_ref, ng_ref, nb_ref, wb_ref, bb_ref, wg_ref, brg_ref, big_ref, lam_ref,
                      aout_ref, bout_ref, nsa_ref, nsb_ref, nsh_ref,
                      wina, winb, xsa, xsb, ca_s, cb_s, a_s, u_s, hcar):
    tau = pl.program_id(1)
    tt, width = glu_ref.shape
    ka, kb = wa_ref.shape[0], wb_ref.shape[0]
    ha, hb = sa_ref.shape[0], sb_ref.shape[0]

    @pl.when(tau == 0)
    def _():
        wina[0:ha, :] = sa_ref[...]
        winb[0:hb, :] = sb_ref[...]
        hcar[...] = jnp.broadcast_to(sh_ref[...], hcar.shape)

    wina[ha:ha + tt, :] = glu_ref[...]
    winb[hb:hb + tt, :] = bx_ref[...]
    for s in sorted({(ha - (ka - 1) + k) % SUBLANES for k in range(ka)} - {0}):
        xsa[s - 1] = wina[pl.ds(s, tt + ha - SUBLANES), :]
    for s in sorted({(hb - (kb - 1) + k) % SUBLANES for k in range(kb)} - {0}):
        xsb[s - 1] = winb[pl.ds(s, tt + hb - SUBLANES), :]

    def tap(win, xs, off, r0, lanes):
        q, s = divmod(off, SUBLANES)
        src = win if s == 0 else xs.at[s - 1]
        return src[pl.ds(r0 + SUBLANES * q, SUBLANES), lanes]

    rows = min(CONV_ROWS, tt)
    for g in range(width // LANES):
        lanes = pl.ds(g * LANES, LANES)
        bcast = lambda ref, k: jnp.broadcast_to(ref[k:k + 1, lanes], (SUBLANES, LANES))
        wa_k = [bcast(wa_ref, k) for k in range(ka)]
        wb_k = [bcast(wb_ref, k) for k in range(kb)]
        ba_g, bb_g = bcast(ba_ref, 0), bcast(bb_ref, 0)

        def conv_block(rb, carry):
            for j in range(rows // SUBLANES):
                r0 = pl.multiple_of(rb * rows, rows) + j * SUBLANES
                acc = ba_g
                for k in range(ka):
                    acc = acc + wa_k[k] * tap(wina, xsa, ha - (ka - 1) + k, r0, lanes)
                ca_s[pl.ds(r0, SUBLANES), lanes] = acc
                accb = bb_g
                for k in range(kb):
                    accb = accb + wb_k[k] * tap(winb, xsb, hb - (kb - 1) + k, r0, lanes)
                cb_s[pl.ds(r0, SUBLANES), lanes] = accb
            return carry

        lax.fori_loop(0, tt // rows, conv_block, 0)

    a_out = _group_norm_silu(ca_s[...], ng_ref[...], nb_ref[...], width // CONV_HEADS_A)
    aout_ref[...] = a_out.astype(aout_ref.dtype)

    a, u = _lru_gates(cb_s[...], wg_ref, brg_ref[...], big_ref[...], lam_ref[...])
    row8 = lax.broadcasted_iota(i32, (tt, width), 0) & (SUBLANES - 1)
    d = 1
    while d < SUBLANES:
        m = row8 >= d
        a_sh = jnp.where(m, pltpu.roll(a, d, axis=0), 1.0)
        u_sh = jnp.where(m, pltpu.roll(u, d, axis=0), 0.0)
        u = a * u_sh + u
        a = a * a_sh
        d *= 2
    a_s[...] = a
    u_s[...] = u

    def scan_block(j, h):
        r0 = pl.multiple_of(j * SUBLANES, SUBLANES)
        hblk = a_s[pl.ds(r0, SUBLANES), :] * h + u_s[pl.ds(r0, SUBLANES), :]
        u_s[pl.ds(r0, SUBLANES), :] = hblk
        return jnp.broadcast_to(hblk[SUBLANES - 1:SUBLANES, :], h.shape)

    h_last = lax.fori_loop(0, tt // SUBLANES, scan_block, hcar[...])
    hcar[...] = h_last
    bout_ref[...] = (u_s[...] * gate_ref[...]).astype(bout_ref.dtype)

    tail_a = wina[tt:tt + ha, :]
    tail_b = winb[tt:tt + hb, :]
    wina[0:ha, :] = tail_a
    winb[0:hb, :] = tail_b
    nsa_ref[...] = tail_a
    nsb_ref[...] = tail_b
    nsh_ref[...] = h_last[0:1, :]


def _mixer_seq(glu, bx, gate, sa_pad, sb_pad, sh, mix_w, n_seq, seq_len, tt):
    wa, ba, ng, nb, wb, bb, wg, brg, big, lam = mix_w
    width = glu.shape[1]
    ha, hb = sa_pad.shape[1], sb_pad.shape[1]
    n_t = seq_len // tt
    row_in = pl.BlockSpec((tt, width), lambda b, t: (b * n_t + t, 0))

    def const(shape):
        return pl.BlockSpec(shape, lambda b, t: (0,) * len(shape))

    def per_seq(r):
        return pl.BlockSpec((None, r, width), lambda b, t: (b, 0, 0))

    return pl.pallas_call(
        _mixer_seq_kernel,
        grid=(n_seq, n_t),
        in_specs=[row_in, row_in, row_in, per_seq(ha), per_seq(hb), per_seq(1),
                  const(wa.shape), const(ba.shape), const(ng.shape), const(nb.shape),
                  const(wb.shape), const(bb.shape), const(wg.shape), const(brg.shape),
                  const(big.shape), const(lam.shape)],
        out_specs=[row_in, row_in, per_seq(ha), per_seq(hb), per_seq(1)],
        out_shape=[jax.ShapeDtypeStruct((n_seq * seq_len, width), bf16),
                   jax.ShapeDtypeStruct((n_seq * seq_len, width), bf16),
                   jax.ShapeDtypeStruct((n_seq, ha, width), f32),
                   jax.ShapeDtypeStruct((n_seq, hb, width), f32),
                   jax.ShapeDtypeStruct((n_seq, 1, width), f32)],
        scratch_shapes=[pltpu.VMEM((ha + tt, width), f32),
                        pltpu.VMEM((hb + tt, width), f32),
                        pltpu.VMEM((SUBLANES - 1, tt + ha - SUBLANES, width), f32),
                        pltpu.VMEM((SUBLANES - 1, tt + hb - SUBLANES, width), f32),
                        pltpu.VMEM((tt, width), f32),
                        pltpu.VMEM((tt, width), f32),
                        pltpu.VMEM((tt, width), f32),
                        pltpu.VMEM((tt, width), f32),
                        pltpu.VMEM((SUBLANES, width), f32)],
        compiler_params=pltpu.CompilerParams(dimension_semantics=("arbitrary", "arbitrary"),
                                             vmem_limit_bytes=VMEM_LIMIT),
        name="mixer_seq",
    )(glu, bx, gate, sa_pad, sb_pad, sh, wa, ba, ng, nb, wb, bb, wg, brg, big, lam)


def _mixer_step_kernel(glu_ref, bx_ref, gate_ref, glun_ref, bxn_ref, sa_ref, sb_ref, sh_ref,
                       wat_ref, wa2_ref, wbt_ref, wb2_ref,
                       ba_ref, ng_ref, nb_ref, bb_ref, wg_ref, brg_ref, big_ref, lam_ref,
                       aout_ref, bout_ref, nsa_ref, nsb_ref, nsh_ref,
                       hista, histb, hcar):
    t = pl.program_id(1)
    n_t = hista.shape[0]
    width = glu_ref.shape[1]
    ka1, kb1 = sa_ref.shape[1], sb_ref.shape[1]

    @pl.when(t == 0)
    def _():
        hista[...] = jnp.zeros_like(hista)
        histb[...] = jnp.zeros_like(histb)
        hcar[...] = sh_ref[...]
        nsa_ref[:, 0:ka1 - n_t, :] = sa_ref[:, n_t:ka1, :]
        nsa_ref[:, ka1 - n_t:ka1, :] = glun_ref[...]
        nsb_ref[...] = bxn_ref[:, n_t - kb1:n_t, :]

    hista[t] = glu_ref[...]
    histb[t] = bx_ref[...]

    ca = ba_ref[...] + jnp.sum(sa_ref[...] * wat_ref[...][None], axis=1)
    cb = bb_ref[...] + jnp.sum(sb_ref[...] * wbt_ref[...][None], axis=1)
    for j in range(n_t):
        ca = ca + wa2_ref[j:j + 1, :] * hista[j]
        cb = cb + wb2_ref[j:j + 1, :] * histb[j]
    a_out = _group_norm_silu(ca, ng_ref[...], nb_ref[...], width // CONV_HEADS_A)
    aout_ref[...] = a_out.astype(aout_ref.dtype)

    a, u = _lru_gates(cb, wg_ref, brg_ref[...], big_ref[...], lam_ref[...])
    h = a * hcar[...] + u
    hcar[...] = h
    nsh_ref[...] = h
    bout_ref[...] = (h * gate_ref[...]).astype(bout_ref.dtype)


def _mixer_step(glu, bx, gate, row_off, glu_nat, bx_nat, sa, sb, sh, step_w, mix_w, n_seq, n_t, bt):
    wat, wa2, wbt, wb2 = step_w
    wa, ba, ng, nb, wb, bb, wg, brg, big, lam = mix_w
    width = glu.shape[1]
    ka1, kb1 = sa.shape[1], sb.shape[1]
    off = row_off // bt
    nb_tiles = n_seq // bt
    row_in = pl.BlockSpec((bt, width), lambda j, t: (off + t * nb_tiles + j, 0))
    row_out = pl.BlockSpec((bt, width), lambda j, t: (t * nb_tiles + j, 0))

    def const(shape):
        return pl.BlockSpec(shape, lambda j, t: (0,) * len(shape))

    def per_t(arr):
        return pl.BlockSpec((None,) + arr.shape[1:], lambda j, t: (t, 0, 0))

    def per_b(r):
        return pl.BlockSpec((bt, r, width), lambda j, t: (j, 0, 0))

    state_h = pl.BlockSpec((bt, width), lambda j, t: (j, 0))
    return pl.pallas_call(
        _mixer_step_kernel,
        grid=(nb_tiles, n_t),
        in_specs=[row_in, row_in, row_in, per_b(n_t), per_b(n_t), per_b(ka1), per_b(kb1), state_h,
                  per_t(wat), per_t(wa2), per_t(wbt), per_t(wb2),
                  const(ba.shape), const(ng.shape), const(nb.shape), const(bb.shape),
                  const(wg.shape), const(brg.shape), const(big.shape), const(lam.shape)],
        out_specs=[row_out, row_out, per_b(ka1), per_b(kb1), state_h],
        out_shape=[jax.ShapeDtypeStruct((n_seq * n_t, width), bf16),
                   jax.ShapeDtypeStruct((n_seq * n_t, width), bf16),
                   jax.ShapeDtypeStruct((n_seq, ka1, width), f32),
                   jax.ShapeDtypeStruct((n_seq, kb1, width), f32),
                   jax.ShapeDtypeStruct((n_seq, width), f32)],
        scratch_shapes=[pltpu.VMEM((n_t, bt, width), f32),
                        pltpu.VMEM((n_t, bt, width), f32),
                        pltpu.VMEM((bt, width), f32)],
        compiler_params=pltpu.CompilerParams(dimension_semantics=("arbitrary", "arbitrary"),
                                             vmem_limit_bytes=VMEM_LIMIT),
        name="mixer_step",
    )(glu, bx, gate, glu_nat, bx_nat, sa, sb, sh, wat, wa2, wbt, wb2,
      ba, ng, nb, bb, wg, brg, big, lam)


def _outproj_kernel(*refs, split, **kw):
    for h in range(split):
        _outproj_rows(h, split, *refs, **kw)


def _outproj_rows(h, split, ap_ref, bp_ref, as_ref, bs_ref, xp_ref, xs_ref, ling_ref, linb_ref, wo_ref,
                  l1g_ref, l1b_ref, wrh_ref, wrl_ref, br_ref,
                  h1_ref, h1p_ref, route_ref, cnt_ref, carry, *, n_p_tiles, alpha):
    i = pl.program_id(0)
    is_p = i < n_p_tiles
    sub = ap_ref.shape[0] // split
    per_tok = h1p_ref.shape[0] // ap_ref.shape[0]
    rows = pl.ds(h * sub, sub)
    ap_ref, bp_ref, as_ref, bs_ref = (r.at[rows] for r in (ap_ref, bp_ref, as_ref, bs_ref))
    xp_ref, xs_ref, h1_ref, route_ref = (r.at[rows] for r in (xp_ref, xs_ref, h1_ref, route_ref))
    h1p_ref = h1p_ref.at[pl.ds(h * sub * per_tok, sub * per_tok)]
    a = jnp.where(is_p, ap_ref[...], as_ref[...])
    b = jnp.where(is_p, bp_ref[...], bs_ref[...])
    x = jnp.where(is_p, xp_ref[...], xs_ref[...])
    width = a.shape[1]
    mix = (jnp.dot(a, wo_ref[0:width, :], preferred_element_type=f32)
           + jnp.dot(b, wo_ref[width:2 * width, :], preferred_element_type=f32))
    h0 = _layer_norm(x, ling_ref[...], linb_ref[...])
    h1 = _layer_norm(alpha * h0 + mix, l1g_ref[...], l1b_ref[...])
    h1_ref[...] = h1

    xh = h1.astype(bf16)
    xh32 = xh.astype(f32)
    half = h1.shape[1] // 2
    lo = lax.shift_right_logical(lax.bitcast_convert_type(xh32[:, :half], jnp.uint32), jnp.uint32(16))
    hi = lax.bitcast_convert_type(xh32[:, half:], jnp.uint32) & jnp.uint32(0xFFFF0000)
    word = lo | hi
    n_sub = half // LANES
    for s in range(n_sub):
        h1p_ref[pl.ds(s, word.shape[0], stride=n_sub), :] = word[:, s * LANES:(s + 1) * LANES]

    xl = (h1 - xh32).astype(bf16)
    z = (jnp.dot(xh, wrh_ref[...], preferred_element_type=f32)
         + jnp.dot(xl, wrh_ref[...], preferred_element_type=f32)
         + jnp.dot(xh, wrl_ref[...], preferred_element_type=f32)) + br_ref[...]
    tm = z.shape[0]
    lane = lax.broadcasted_iota(i32, z.shape, 1).astype(f32)
    n_g, epg = float(N_GROUPS), float(EXPERTS_PER_GROUP)
    far = float(2 * ROUTE_LANES)

    gm = lane < n_g
    zg = jnp.where(gm, z, NEG_BIG)
    pg = jnp.where(gm, jnp.exp(zg - jnp.max(zg, axis=-1, keepdims=True)), 0.0)
    pg = pg / jnp.sum(pg, axis=-1, keepdims=True)
    g_top = jnp.max(pg, axis=-1, keepdims=True)
    g_idx = jnp.min(jnp.where(gm & (pg == g_top), lane, far), axis=-1, keepdims=True)

    lo = n_g + g_idx * epg
    em = (lane >= lo) & (lane < lo + epg)
    ze = jnp.where(em, z, NEG_BIG)
    pe = jnp.where(em, jnp.exp(ze - jnp.max(ze, axis=-1, keepdims=True)), 0.0)
    pe = pe / jnp.sum(pe, axis=-1, keepdims=True)
    v1 = jnp.max(jnp.where(em, pe, -1.0), axis=-1, keepdims=True)
    l1 = jnp.min(jnp.where(em & (pe == v1), lane, far), axis=-1, keepdims=True)
    pe2 = jnp.where(em & (lane != l1), pe, -1.0)
    v2 = jnp.max(pe2, axis=-1, keepdims=True)
    l2 = jnp.min(jnp.where(pe2 == v2, lane, far), axis=-1, keepdims=True)
    den = v1 + v2
    gate1 = v1 / den * g_top
    gate2 = v2 / den * g_top

    @pl.when((i == 0) & (h == 0))
    def _():
        carry[...] = jnp.zeros_like(carry)

    o1 = lane == l1
    o2 = lane == l2
    onehot = jnp.where(o1 | o2, 1.0, 0.0)
    ri = lax.broadcasted_iota(i32, (tm, tm), 0)
    ci = lax.broadcasted_iota(i32, (tm, tm), 1)
    tri = jnp.where(ci < ri, 1.0, 0.0).astype(bf16)
    prefix = jnp.dot(tri, onehot.astype(bf16), preferred_element_type=f32) + carry[0:1, :]
    rank1 = jnp.sum(jnp.where(o1, prefix, 0.0), axis=-1, keepdims=True)
    rank2 = jnp.sum(jnp.where(o2, prefix, 0.0), axis=-1, keepdims=True)
    new_carry = carry[...] + jnp.sum(onehot, axis=0, keepdims=True)
    carry[...] = new_carry
    cnt_ref[...] = new_carry

    route = jnp.where(lane == 0.0, l1 - n_g, 0.0)
    route = jnp.where(lane == 1.0, l2 - n_g, route)
    route = jnp.where(lane == 2.0, rank1, route)
    route = jnp.where(lane == 3.0, rank2, route)
    route = jnp.where(lane == 4.0, gate1, route)
    route = jnp.where(lane == 5.0, gate2, route)
    route_ref[...] = route


def _outproj(a_p, b_p, a_s, b_s, x_p, x_s, ln_in_g, ln_in_b, w_out_b, ln1_g, ln1_b,
             wr_hi, wr_lo, b_r, alpha, tm):
    n_p, width = a_p.shape
    n_s = a_s.shape[0]
    d = x_p.shape[1]
    n_p_tiles = n_p // tm
    n_s_tiles = n_s // tm
    n_tiles = n_p_tiles + n_s_tiles

    def p_map(i):
        return (jnp.minimum(i, n_p_tiles - 1), 0)

    def s_map(i):
        return (jnp.clip(i - n_p_tiles, 0, n_s_tiles - 1), 0)

    def const(shape, **kw):
        return pl.BlockSpec(shape, lambda i: (0,) * len(shape), **kw)

    return pl.pallas_call(
        functools.partial(_outproj_kernel, split=OUTPROJ_SPLIT, n_p_tiles=n_p_tiles, alpha=alpha),
        grid=(n_tiles,),
        in_specs=[pl.BlockSpec((tm, width), p_map), pl.BlockSpec((tm, width), p_map),
                  pl.BlockSpec((tm, width), s_map), pl.BlockSpec((tm, width), s_map),
                  pl.BlockSpec((tm, d), p_map), pl.BlockSpec((tm, d), s_map),
                  const((1, d)), const((1, d)),
                  const((2 * width, d), pipeline_mode=pl.Buffered(1)),
                  const((1, d)), const((1, d)),
                  const((d, ROUTE_LANES)), const((d, ROUTE_LANES)), const((1, ROUTE_LANES))],
        out_specs=[pl.BlockSpec((tm, d), lambda i: (i, 0)),
                   pl.BlockSpec((tm * (d // 2 // LANES), LANES), lambda i: (i, 0)),
                   pl.BlockSpec((tm, ROUTE_LANES), lambda i: (i, 0)),
                   const((SUBLANES, ROUTE_LANES))],
        out_shape=[jax.ShapeDtypeStruct((n_p + n_s, d), f32),
                   jax.ShapeDtypeStruct(((n_p + n_s) * (d // 2 // LANES), LANES), jnp.uint32),
                   jax.ShapeDtypeStruct((n_p + n_s, ROUTE_LANES), f32),
                   jax.ShapeDtypeStruct((SUBLANES, ROUTE_LANES), f32)],
        scratch_shapes=[pltpu.VMEM((SUBLANES, ROUTE_LANES), f32)],
        compiler_params=pltpu.CompilerParams(dimension_semantics=("arbitrary",),
                                             vmem_limit_bytes=VMEM_LIMIT),
        name="outproj_router",
    )(a_p, b_p, a_s, b_s, x_p, x_s, ln_in_g, ln_in_b, w_out_b, ln1_g, ln1_b, wr_hi, wr_lo, b_r)


def _expert_kernel(src_ref, dst_ref, start_ref, nblk_ref,
                   h1_hbm, wg_ref, wu_ref, wd_ref, y2_hbm,
                   xbuf, ybuf, gsem, osem, *, blk, n_rows):
    e = pl.program_id(0)
    n_e = pl.num_programs(0)
    nb = nblk_ref[e]
    g0 = start_ref[e]

    xs = xbuf.shape[1] // blk
    ys = ybuf.shape[1] // blk

    def gather(g, slot):
        for r in range(blk):
            tok = src_ref[g * blk + r]
            pltpu.make_async_copy(h1_hbm.at[pl.ds(pl.multiple_of(tok * xs, xs), xs), :],
                                  xbuf.at[slot, pl.ds(r * xs, xs), :], gsem.at[slot]
                                  ).start(priority=GATHER_PRIORITY)

    def scatter(g, slot):
        for r in range(blk):
            row = dst_ref[(g + 1) * blk + r]
            pltpu.make_async_copy(ybuf.at[slot, pl.ds(r * ys, ys), :],
                                  y2_hbm.at[pl.ds(pl.multiple_of(row * ys, ys), ys), :],
                                  osem.at[slot]).start(priority=r % 2)

    def dump_copy(slot):
        dst = y2_hbm.at[pl.ds((n_rows + slot * blk) * ys, blk * ys), :]
        return pltpu.make_async_copy(ybuf.at[slot], dst, osem.at[slot])

    def wait_gather(slot):
        pltpu.make_async_copy(h1_hbm.at[pl.ds(0, blk * xs), :], xbuf.at[slot], gsem.at[slot]).wait()

    nx, ny = xbuf.shape[0], ybuf.shape[0]
    ahead = nx - 1

    @pl.when(e == 0)
    def _():
        ybuf[...] = jnp.zeros(ybuf.shape, ybuf.dtype)
        for s in range(ny - 1):
            dump_copy(s).start()
        for g in range(ahead):
            gather(g, g)

    @pl.when(nb > 0)
    def _():
        def chunk(c, carry):
            g = g0 + c
            slot = g % nx
            yslot = g % ny
            wait_gather(slot)
            dump_copy(yslot).wait()
            gather(g + ahead, (g + ahead) % nx)
            scatter(g - 1, (g + ny - 1) % ny)
            xw = [xbuf[slot, pl.ds(s, blk, stride=xs), :] for s in range(xs)]
            x_lo = [lax.bitcast_convert_type(lax.shift_left(w, jnp.uint32(16)), f32) for w in xw]
            x_hi = [lax.bitcast_convert_type(w & jnp.uint32(0xFFFF0000), f32) for w in xw]
            x = jnp.concatenate(x_lo + x_hi, axis=1).astype(bf16)
            hg = jnp.dot(x, wg_ref[...].astype(bf16), preferred_element_type=f32)
            hu = jnp.dot(x, wu_ref[...].astype(bf16), preferred_element_type=f32)
            hid = (hg * jax.nn.sigmoid(hg) * hu).astype(bf16)
            y = jnp.dot(hid, wd_ref[...].astype(bf16), preferred_element_type=f32)
            for s in range(ys):
                ybuf[yslot, pl.ds(s, blk, stride=ys), :] = y[:, s * LANES:(s + 1) * LANES]
            return carry

        lax.fori_loop(0, nb, chunk, 0)

    @pl.when(e == n_e - 1)
    def _():
        g_end = g0 + nb
        scatter(g_end - 1, (g_end + ny - 1) % ny)
        for s in range(ny):
            dump_copy(s).wait()
        for j in range(ahead):
            wait_gather((g_end + j) % nx)


def _experts(h1p, src_rows, dst_rows, pad_start, nblk, w_gate, w_up, w_down, blk):
    n_e, d, d_e = w_gate.shape
    xs, ys = d // 2 // LANES, d // LANES
    n = h1p.shape[0] // xs
    n_rows = TOP_K * n
    grid_spec = pltpu.PrefetchScalarGridSpec(
        num_scalar_prefetch=4,
        grid=(n_e,),
        in_specs=[pl.BlockSpec(memory_space=pl.ANY),
                  pl.BlockSpec((None, d, d_e), lambda e, *_: (e, 0, 0)),
                  pl.BlockSpec((None, d, d_e), lambda e, *_: (e, 0, 0)),
                  pl.BlockSpec((None, d_e, d), lambda e, *_: (e, 0, 0))],
        out_specs=pl.BlockSpec(memory_space=pl.ANY),
        scratch_shapes=[pltpu.VMEM((GATHER_SLOTS, blk * xs, LANES), jnp.uint32),
                        pltpu.VMEM((SCATTER_SLOTS, blk * ys, LANES), f32),
                        pltpu.SemaphoreType.DMA((GATHER_SLOTS,)),
                        pltpu.SemaphoreType.DMA((SCATTER_SLOTS,))],
    )
    return pl.pallas_call(
        functools.partial(_expert_kernel, blk=blk, n_rows=n_rows),
        grid_spec=grid_spec,
        out_shape=jax.ShapeDtypeStruct(((n_rows + SCATTER_SLOTS * blk) * ys, LANES), f32),
        compiler_params=pltpu.CompilerParams(dimension_semantics=("arbitrary",),
                                             vmem_limit_bytes=VMEM_LIMIT),
        name="experts",
    )(src_rows, dst_rows, pad_start, nblk, h1p, w_gate, w_up, w_down)


def _combine_kernel(h1_ref, route_ref, ya_ref, yb_ref, l2g_ref, l2b_ref, yp_ref, ysm_ref,
                    *, n_p_tiles, alpha):
    i = pl.program_id(0)
    route = route_ref[...]
    tm = route.shape[0]
    ys = ya_ref.shape[0] // tm

    def token_rows(ref):
        return jnp.concatenate([ref[pl.ds(s, tm, stride=ys), :] for s in range(ys)], axis=1)

    y = token_rows(ya_ref) * route[:, 4:5] + token_rows(yb_ref) * route[:, 5:6]
    out = _layer_norm(alpha * h1_ref[...] + y, l2g_ref[...], l2b_ref[...])

    @pl.when(i < n_p_tiles)
    def _():
        yp_ref[...] = out

    @pl.when(i >= n_p_tiles)
    def _():
        ysm_ref[...] = out


def _combine(h1, route, y2, ln2_g, ln2_b, n_p, alpha, tm):
    n, d = h1.shape
    n_s = n - n_p
    n_tiles = n // tm
    n_p_tiles = n_p // tm
    n_s_tiles = n_s // tm
    return pl.pallas_call(
        functools.partial(_combine_kernel, n_p_tiles=n_p_tiles, alpha=alpha),
        grid=(n_tiles,),
        in_specs=[pl.BlockSpec((tm, d), lambda i: (i, 0)),
                  pl.BlockSpec((tm, ROUTE_LANES), lambda i: (i, 0)),
                  pl.BlockSpec((tm * (d // LANES), LANES), lambda i: (i, 0)),
                  pl.BlockSpec((tm * (d // LANES), LANES), lambda i: (n_tiles + i, 0)),
                  pl.BlockSpec((1, d), lambda i: (0, 0)),
                  pl.BlockSpec((1, d), lambda i: (0, 0))],
        out_specs=[pl.BlockSpec((tm, d), lambda i: (jnp.minimum(i, n_p_tiles - 1), 0)),
                   pl.BlockSpec((tm, d), lambda i: (jnp.clip(i - n_p_tiles, 0, n_s_tiles - 1), 0))],
        out_shape=[jax.ShapeDtypeStruct((n_p, d), f32), jax.ShapeDtypeStruct((n_s, d), f32)],
        compiler_params=pltpu.CompilerParams(dimension_semantics=("arbitrary",),
                                             vmem_limit_bytes=VMEM_LIMIT),
        name="combine_ln2",
    )(h1, route, y2, y2, ln2_g, ln2_b)


def _shifted_taps(w, n_t, hist):
    k = w.shape[0]
    zero = jnp.zeros_like(w[0])
    wt = jnp.stack([jnp.stack([w[r - t] if 0 <= r - t < k else zero for r in range(hist)])
                    for t in range(n_t)])
    w2 = jnp.stack([jnp.stack([w[hist - t + j] if (j <= t and 0 <= hist - t + j < k) else zero
                               for j in range(n_t)]) for t in range(n_t)])
    return wt, w2


def kernel(x_prompt, x_sample, state_conv_a, state_conv_b, state_lru, meta_tokens, ln_in_g, ln_in_b, w_in, conv_a_w, conv_a_b, norm_a_g, norm_a_b, conv_b_w, conv_b_b, w_rg, b_rg, w_ig, b_ig, lru_lambda, w_out, ln1_g, ln1_b, w_router_group, b_router_group, w_router_expert, b_router_expert, w_gate, w_up, w_down, ln2_g, ln2_b):
    depth = w_in.shape[0]
    assert depth == 1, "single-layer trunk only"
    bp, tp, d = x_prompt.shape
    bs, ts, _ = x_sample.shape
    n_meta = meta_tokens.shape[0]
    width = conv_a_w.shape[2]
    ka, kb = conv_a_w.shape[1], conv_b_w.shape[1]
    n_e = w_gate.shape[1]
    alpha = (2.0 * depth) ** 0.25
    n_p, n_s = bp * tp, bs * ts
    n = n_p + n_s
    assert ts < ka and ts >= kb - 1
    assert n_p % TOKEN_TILE == 0 and n_s % TOKEN_TILE == 0 and tp % SEQ_TILE == 0
    assert bs % STEP_BATCH == 0 and n_p % STEP_BATCH == 0
    assert n_p % COMBINE_TILE == 0 and n_s % COMBINE_TILE == 0
    assert n_meta % BF16_ROWS == 0
    assert N_GROUPS * (1 + EXPERTS_PER_GROUP) <= ROUTE_LANES and n_e == N_GROUPS * EXPERTS_PER_GROUP

    row = lambda v: v.reshape(1, -1).astype(f32)
    ln_g, ln_b = row(ln_in_g), row(ln_in_b)
    w_in_b = w_in[0].astype(bf16)
    w_out_b = w_out[0].astype(bf16)
    mix_w = (conv_a_w[0], row(conv_a_b[0]), row(norm_a_g[0]), row(norm_a_b[0]),
             conv_b_w[0], row(conv_b_b[0]),
             jnp.concatenate([w_rg[0], w_ig[0]], axis=-1).astype(bf16),
             row(b_rg[0]), row(b_ig[0]), row(lru_lambda[0]))
    ha, hb = _round_up(ka - 1, SUBLANES), _round_up(kb - 1, SUBLANES)

    def pad_hist(s, h):
        return jnp.concatenate([jnp.zeros(s.shape[:1] + (h - s.shape[1], width), f32), s], axis=1)

    glu_m, bx_m, gate_m = _inproj(meta_tokens, meta_tokens, 0, ln_g, ln_b, w_in_b, n_meta)
    _, _, sa_m, sb_m, sh_m = _mixer_seq(
        glu_m, bx_m, gate_m, jnp.zeros((1, ha, width), f32), jnp.zeros((1, hb, width), f32),
        jnp.zeros((1, 1, width), f32), mix_w, 1, n_meta, n_meta)

    x_p = x_prompt.reshape(n_p, d)
    x_s = jnp.swapaxes(x_sample, 0, 1).reshape(n_s, d)
    glu, bx, gate = _inproj(x_p, x_s, n_s, ln_g, ln_b, w_in_b, TOKEN_TILE)

    bcast = lambda s: jnp.broadcast_to(s, (bp,) + s.shape[1:])
    a_p, b_p, nsa_p, nsb_p, nsh_p = _mixer_seq(glu, bx, gate, bcast(sa_m), bcast(sb_m), bcast(sh_m),
                                               mix_w, bp, tp, SEQ_TILE)

    nat = lambda v: jnp.swapaxes(v[n_p:].reshape(ts, bs, width), 0, 1)
    wat, wa2 = _shifted_taps(conv_a_w[0], ts, ka - 1)
    wbt, wb2 = _shifted_taps(conv_b_w[0], ts, kb - 1)
    a_s, b_s, nsa_s, nsb_s, nsh_s = _mixer_step(
        glu, bx, gate, n_p, nat(glu), nat(bx), state_conv_a[0], state_conv_b[0], state_lru[0],
        (wat, wa2, wbt, wb2), mix_w, bs, ts, STEP_BATCH)

    w_r = jnp.concatenate([w_router_group[0], w_router_expert[0]], axis=1)
    w_r = jnp.pad(w_r, ((0, 0), (0, ROUTE_LANES - w_r.shape[1])))
    wr_hi = w_r.astype(bf16)
    wr_lo = (w_r - wr_hi.astype(f32)).astype(bf16)
    b_r = jnp.concatenate([b_router_group[0], b_router_expert[0]])
    b_r = jnp.pad(b_r, (0, ROUTE_LANES - b_r.shape[0])).reshape(1, ROUTE_LANES)
    h1, h1p, route, cnt = _outproj(a_p, b_p, a_s, b_s, x_p, x_s, ln_g, ln_b, w_out_b,
                              row(ln1_g[0]), row(ln1_b[0]), wr_hi, wr_lo, b_r, alpha, TOKEN_TILE)

    blk = EXPERT_ROWS
    e_idx = route[:, 0:TOP_K].astype(i32)
    rank = route[:, 2:2 + TOP_K].astype(i32)
    counts = cnt[0, N_GROUPS:N_GROUPS + n_e].astype(i32)
    nblk = (counts + blk - 1) // blk
    pad_end = jnp.cumsum(nblk * blk)
    pad_start = pad_end - nblk * blk
    start_of = jnp.sum(jnp.where(e_idx[..., None] == jnp.arange(n_e, dtype=i32), pad_start, 0), axis=-1)
    dest = (start_of + rank).reshape(-1)
    p_max = _round_up(TOP_K * n + n_e * (blk - 1), blk)
    flat = jnp.full((p_max,), -1, i32).at[dest].set(jnp.arange(TOP_K * n, dtype=i32))
    pos = jnp.arange(p_max, dtype=i32)
    valid = flat >= 0
    src_rows = jnp.where(valid, flat // TOP_K, 0)
    dst_rows = jnp.where(valid, (flat % TOP_K) * n + flat // TOP_K, TOP_K * n + pos % (SCATTER_SLOTS * blk))
    src_rows = jnp.concatenate([src_rows, jnp.zeros(((GATHER_SLOTS - 1) * blk,), i32)])
    dst_rows = jnp.concatenate([TOP_K * n + (SCATTER_SLOTS - 1) * blk + jnp.arange(blk, dtype=i32), dst_rows])
    y2 = _experts(h1p, src_rows, dst_rows, (pad_start // blk).astype(i32), nblk,
                  w_gate[0], w_up[0], w_down[0], blk)
    y_p, y_s = _combine(h1, route, y2, row(ln2_g[0]), row(ln2_b[0]), n_p, alpha, TOKEN_TILE)

    y_prompt = y_p.reshape(bp, tp, d)
    y_sample = jnp.swapaxes(y_s.reshape(ts, bs, d), 0, 1)
    return (y_prompt, y_sample,
            nsa_p[:, ha - (ka - 1):][None], nsb_p[:, hb - (kb - 1):][None], nsh_p.reshape(1, bp, width),
            nsa_s[None], nsb_s[None], nsh_s[None])
```

```python
import functools

import jax
import jax.numpy as jnp
from jax import lax
from jax.experimental import pallas as pl
from jax.experimental.pallas import tpu as pltpu

f32 = jnp.float32
bf16 = jnp.bfloat16
i32 = jnp.int32

CONV_HEADS_A = 8
LRU_HEADS = 8
LRU_C = 8.0
N_GROUPS = 8
EXPERTS_PER_GROUP = 8
TOP_K = 2
LN_EPS = 1e-5

SUBLANES = 8
LANES = 128
BF16_ROWS = 16
VMEM_LIMIT = 56 * 1024 * 1024

TOKEN_TILE = 256
OUTPROJ_SPLIT = 1
SEQ_TILE = 256
CONV_ROWS = 32
STEP_BATCH = 32
EXPERT_ROWS = 160
COMBINE_TILE = 128
ROUTE_LANES = 128
GATHER_SLOTS = 4
SCATTER_SLOTS = 4
GATHER_PRIORITY = 1
NEG_BIG = -1e30


def _round_up(x, m):
    return (x + m - 1) // m * m


def _layer_norm(x, g, b):
    mu = jnp.mean(x, axis=-1, keepdims=True)
    xc = x - mu
    var = jnp.mean(xc * xc, axis=-1, keepdims=True)
    return xc * lax.rsqrt(var + LN_EPS) * g + b


def _group_norm_silu(x, g, b, group):
    parts = []
    for c in range(x.shape[1] // group):
        xg = x[:, c * group:(c + 1) * group]
        mu = jnp.mean(xg, axis=-1, keepdims=True)
        xc = xg - mu
        var = jnp.mean(xc * xc, axis=-1, keepdims=True)
        parts.append(xc * lax.rsqrt(var + LN_EPS))
    y = jnp.concatenate(parts, axis=-1) * g + b
    return y * jax.nn.sigmoid(y)


def _lru_gates(cb, wg_ref, brg, big, lam):
    heads, hd, _ = wg_ref.shape
    cbb = cb.astype(bf16)
    rs, iz = [], []
    for h in range(heads):
        z = jnp.dot(cbb[:, h * hd:(h + 1) * hd], wg_ref[h], preferred_element_type=f32)
        rs.append(z[:, :hd])
        iz.append(z[:, hd:])
    r = jax.nn.sigmoid(jnp.concatenate(rs, axis=-1) + brg)
    i = jax.nn.sigmoid(jnp.concatenate(iz, axis=-1) + big)
    log_a = -LRU_C * r * jax.nn.softplus(-lam)
    a = jnp.exp(log_a)
    u = jnp.sqrt(-jnp.tanh(log_a) * (a * a + 1.0)) * (i * cb)
    return a, u


def _inproj_kernel(xa_ref, xb_ref, g_ref, b_ref, w_ref, glu_ref, bx_ref, gate_ref, *, n_a_tiles):
    i = pl.program_id(0)
    x = jnp.where(i < n_a_tiles, xa_ref[...], xb_ref[...])
    xn = _layer_norm(x, g_ref[...], b_ref[...]).astype(bf16)
    width = glu_ref.shape[1]
    ch = min(width, 512)
    for c in range(width // ch):
        lo = c * ch
        av = jnp.dot(xn, w_ref[:, lo:lo + ch], preferred_element_type=f32)
        ag = jnp.dot(xn, w_ref[:, width + lo:width + lo + ch], preferred_element_type=f32)
        glu_ref[:, lo:lo + ch] = av * jax.nn.sigmoid(ag)
        bx_ref[:, lo:lo + ch] = jnp.dot(xn, w_ref[:, 2 * width + lo:2 * width + lo + ch],
                                        preferred_element_type=f32)
        bg = jnp.dot(xn, w_ref[:, 3 * width + lo:3 * width + lo + ch], preferred_element_type=f32)
        gate_ref[:, lo:lo + ch] = jax.nn.gelu(bg)


def _inproj(xa, xb, n_b_rows, ln_g, ln_b, w_in_b, tm):
    na, d = xa.shape
    width = w_in_b.shape[1] // 4
    n_a_tiles = na // tm
    n_tiles = (na + n_b_rows) // tm
    n_b_tiles = max(n_tiles - n_a_tiles, 1)
    row_out = pl.BlockSpec((tm, width), lambda i: (i, 0))
    return pl.pallas_call(
        functools.partial(_inproj_kernel, n_a_tiles=n_a_tiles),
        grid=(n_tiles,),
        in_specs=[
            pl.BlockSpec((tm, d), lambda i: (jnp.minimum(i, n_a_tiles - 1), 0)),
            pl.BlockSpec((tm, d), lambda i: (jnp.clip(i - n_a_tiles, 0, n_b_tiles - 1), 0)),
            pl.BlockSpec((1, d), lambda i: (0, 0)),
            pl.BlockSpec((1, d), lambda i: (0, 0)),
            pl.BlockSpec((d, 4 * width), lambda i: (0, 0), pipeline_mode=pl.Buffered(1)),
        ],
        out_specs=[row_out, row_out, row_out],
        out_shape=[jax.ShapeDtypeStruct((na + n_b_rows, width), f32)] * 3,
        compiler_params=pltpu.CompilerParams(dimension_semantics=("arbitrary",),
                                             vmem_limit_bytes=VMEM_LIMIT),
        name="inproj",
    )(xa, xb, ln_g, ln_b, w_in_b)


def _mixer_seq_kernel(glu_ref, bx_ref, gate_ref, sa_ref, sb_ref, sh_ref,
                      wa_ref, ba_ref, ng_ref, nb_ref, wb_ref, bb_ref, wg_ref, brg_ref, big_ref, lam_ref,
                      aout_ref, bout_ref, nsa_ref, nsb_ref, nsh_ref,
                      wina, winb, xsa, xsb, ca_s, cb_s, a_s, u_s, hcar):
    tau = pl.program_id(1)
    tt, width = glu_ref.shape
    ka, kb = wa_ref.shape[0], wb_ref.shape[0]
    ha, hb = sa_ref.shape[0], sb_ref.shape[0]

    @pl.when(tau == 0)
    def _():
        wina[0:ha, :] = sa_ref[...]
        winb[0:hb, :] = sb_ref[...]
        hcar[...] = jnp.broadcast_to(sh_ref[...], hcar.shape)

    wina[ha:ha + tt, :] = glu_ref[...]
    winb[hb:hb + tt, :] = bx_ref[...]
    for s in sorted({(ha - (ka - 1) + k) % SUBLANES for k in range(ka)} - {0}):
        xsa[s - 1] = wina[pl.ds(s, tt + ha - SUBLANES), :]
    for s in sorted({(hb - (kb - 1) + k) % SUBLANES for k in range(kb)} - {0}):
        xsb[s - 1] = winb[pl.ds(s, tt + hb - SUBLANES), :]

    def tap(win, xs, off, r0, lanes):
        q, s = divmod(off, SUBLANES)
        src = win if s == 0 else xs.at[s - 1]
        return src[pl.ds(r0 + SUBLANES * q, SUBLANES), lanes]

    rows = min(CONV_ROWS, tt)
    for g in range(width // LANES):
        lanes = pl.ds(g * LANES, LANES)
        bcast = lambda ref, k: jnp.broadcast_to(ref[k:k + 1, lanes], (SUBLANES, LANES))
        wa_k = [bcast(wa_ref, k) for k in range(ka)]
        wb_k = [bcast(wb_ref, k) for k in range(kb)]
        ba_g, bb_g = bcast(ba_ref, 0), bcast(bb_ref, 0)

        def conv_block(rb, carry):
            for j in range(rows // SUBLANES):
                r0 = pl.multiple_of(rb * rows, rows) + j * SUBLANES
                acc = ba_g
                for k in range(ka):
                    acc = acc + wa_k[k] * tap(wina, xsa, ha - (ka - 1) + k, r0, lanes)
                ca_s[pl.ds(r0, SUBLANES), lanes] = acc
                accb = bb_g
                for k in range(kb):
                    accb = accb + wb_k[k] * tap(winb, xsb, hb - (kb - 1) + k, r0, lanes)
                cb_s[pl.ds(r0, SUBLANES), lanes] = accb
            return carry

        lax.fori_loop(0, tt // rows, conv_block, 0)

    a_out = _group_norm_silu(ca_s[...], ng_ref[...], nb_ref[...], width // CONV_HEADS_A)
    aout_ref[...] = a_out.astype(aout_ref.dtype)

    a, u = _lru_gates(cb_s[...], wg_ref, brg_ref[...], big_ref[...], lam_ref[...])
    row8 = lax.broadcasted_iota(i32, (tt, width), 0) & (SUBLANES - 1)
    d = 1
    while d < SUBLANES:
        m = row8 >= d
        a_sh = jnp.where(m, pltpu.roll(a, d, axis=0), 1.0)
        u_sh = jnp.where(m, pltpu.roll(u, d, axis=0), 0.0)
        u = a * u_sh + u
        a = a * a_sh
        d *= 2
    a_s[...] = a
    u_s[...] = u

    def scan_block(j, h):
        r0 = pl.multiple_of(j * SUBLANES, SUBLANES)
        hblk = a_s[pl.ds(r0, SUBLANES), :] * h + u_s[pl.ds(r0, SUBLANES), :]
        u_s[pl.ds(r0, SUBLANES), :] = hblk
        return jnp.broadcast_to(hblk[SUBLANES - 1:SUBLANES, :], h.shape)

    h_last = lax.fori_loop(0, tt // SUBLANES, scan_block, hcar[...])
    hcar[...] = h_last
    bout_ref[...] = (u_s[...] * gate_ref[...]).astype(bout_ref.dtype)

    tail_a = wina[tt:tt + ha, :]
    tail_b = winb[tt:tt + hb, :]
    wina[0:ha, :] = tail_a
    winb[0:hb, :] = tail_b
    nsa_ref[...] = tail_a
    nsb_ref[...] = tail_b
    nsh_ref[...] = h_last[0:1, :]


def _mixer_seq(glu, bx, gate, sa_pad, sb_pad, sh, mix_w, n_seq, seq_len, tt):
    wa, ba, ng, nb, wb, bb, wg, brg, big, lam = mix_w
    width = glu.shape[1]
    ha, hb = sa_pad.shape[1], sb_pad.shape[1]
    n_t = seq_len // tt
    row_in = pl.BlockSpec((tt, width), lambda b, t: (b * n_t + t, 0))

    def const(shape):
        return pl.BlockSpec(shape, lambda b, t: (0,) * len(shape))

    def per_seq(r):
        return pl.BlockSpec((None, r, width), lambda b, t: (b, 0, 0))

    return pl.pallas_call(
        _mixer_seq_kernel,
        grid=(n_seq, n_t),
        in_specs=[row_in, row_in, row_in, per_seq(ha), per_seq(hb), per_seq(1),
                  const(wa.shape), const(ba.shape), const(ng.shape), const(nb.shape),
                  const(wb.shape), const(bb.shape), const(wg.shape), const(brg.shape),
                  const(big.shape), const(lam.shape)],
        out_specs=[row_in, row_in, per_seq(ha), per_seq(hb), per_seq(1)],
        out_shape=[jax.ShapeDtypeStruct((n_seq * seq_len, width), bf16),
                   jax.ShapeDtypeStruct((n_seq * seq_len, width), bf16),
                   jax.ShapeDtypeStruct((n_seq, ha, width), f32),
                   jax.ShapeDtypeStruct((n_seq, hb, width), f32),
                   jax.ShapeDtypeStruct((n_seq, 1, width), f32)],
        scratch_shapes=[pltpu.VMEM((ha + tt, width), f32),
                        pltpu.VMEM((hb + tt, width), f32),
                        pltpu.VMEM((SUBLANES - 1, tt + ha - SUBLANES, width), f32),
                        pltpu.VMEM((SUBLANES - 1, tt + hb - SUBLANES, width), f32),
                        pltpu.VMEM((tt, width), f32),
                        pltpu.VMEM((tt, width), f32),
                        pltpu.VMEM((tt, width), f32),
                        pltpu.VMEM((tt, width), f32),
                        pltpu.VMEM((SUBLANES, width), f32)],
        compiler_params=pltpu.CompilerParams(dimension_semantics=("arbitrary", "arbitrary"),
                                             vmem_limit_bytes=VMEM_LIMIT),
        name="mixer_seq",
    )(glu, bx, gate, sa_pad, sb_pad, sh, wa, ba, ng, nb, wb, bb, wg, brg, big, lam)


def _mixer_step_kernel(glu_ref, bx_ref, gate_ref, glun_ref, bxn_ref, sa_ref, sb_ref, sh_ref,
                       wat_ref, wa2_ref, wbt_ref, wb2_ref,
                       ba_ref, ng_ref, nb_ref, bb_ref, wg_ref, brg_ref, big_ref, lam_ref,
                       aout_ref, bout_ref, nsa_ref, nsb_ref, nsh_ref,
                       hista, histb, hcar):
    t = pl.program_id(1)
    n_t = hista.shape[0]
    width = glu_ref.shape[1]
    ka1, kb1 = sa_ref.shape[1], sb_ref.shape[1]

    @pl.when(t == 0)
    def _():
        hista[...] = jnp.zeros_like(hista)
        histb[...] = jnp.zeros_like(histb)
        hcar[...] = sh_ref[...]
        nsa_ref[:, 0:ka1 - n_t, :] = sa_ref[:, n_t:ka1, :]
        nsa_ref[:, ka1 - n_t:ka1, :] = glun_ref[...]
        nsb_ref[...] = bxn_ref[:, n_t - kb1:n_t, :]

    hista[t] = glu_ref[...]
    histb[t] = bx_ref[...]

    ca = ba_ref[...] + jnp.sum(sa_ref[...] * wat_ref[...][None], axis=1)
    cb = bb_ref[...] + jnp.sum(sb_ref[...] * wbt_ref[...][None], axis=1)
    for j in range(n_t):
        ca = ca + wa2_ref[j:j + 1, :] * hista[j]
        cb = cb + wb2_ref[j:j + 1, :] * histb[j]
    a_out = _group_norm_silu(ca, ng_ref[...], nb_ref[...], width // CONV_HEADS_A)
    aout_ref[...] = a_out.astype(aout_ref.dtype)

    a, u = _lru_gates(cb, wg_ref, brg_ref[...], big_ref[...], lam_ref[...])
    h = a * hcar[...] + u
    hcar[...] = h
    nsh_ref[...] = h
    bout_ref[...] = (h * gate_ref[...]).astype(bout_ref.dtype)


def _mixer_step(glu, bx, gate, row_off, glu_nat, bx_nat, sa, sb, sh, step_w, mix_w, n_seq, n_t, bt):
    wat, wa2, wbt, wb2 = step_w
    wa, ba, ng, nb, wb, bb, wg, brg, big, lam = mix_w
    width = glu.shape[1]
    ka1, kb1 = sa.shape[1], sb.shape[1]
    off = row_off // bt
    nb_tiles = n_seq // bt
    row_in = pl.BlockSpec((bt, width), lambda j, t: (off + t * nb_tiles + j, 0))
    row_out = pl.BlockSpec((bt, width), lambda j, t: (t * nb_tiles + j, 0))

    def const(shape):
        return pl.BlockSpec(shape, lambda j, t: (0,) * len(shape))

    def per_t(arr):
        return pl.BlockSpec((None,) + arr.shape[1:], lambda j, t: (t, 0, 0))

    def per_b(r):
        return pl.BlockSpec((bt, r, width), lambda j, t: (j, 0, 0))

    state_h = pl.BlockSpec((bt, width), lambda j, t: (j, 0))
    return pl.pallas_call(
        _mixer_step_kernel,
        grid=(nb_tiles, n_t),
        in_specs=[row_in, row_in, row_in, per_b(n_t), per_b(n_t), per_b(ka1), per_b(kb1), state_h,
                  per_t(wat), per_t(wa2), per_t(wbt), per_t(wb2),
                  const(ba.shape), const(ng.shape), const(nb.shape), const(bb.shape),
                  const(wg.shape), const(brg.shape), const(big.shape), const(lam.shape)],
        out_specs=[row_out, row_out, per_b(ka1), per_b(kb1), state_h],
        out_shape=[jax.ShapeDtypeStruct((n_seq * n_t, width), bf16),
                   jax.ShapeDtypeStruct((n_seq * n_t, width), bf16),
                   jax.ShapeDtypeStruct((n_seq, ka1, width), f32),
                   jax.ShapeDtypeStruct((n_seq, kb1, width), f32),
                   jax.ShapeDtypeStruct((n_seq, width), f32)],
        scratch_shapes=[pltpu.VMEM((n_t, bt, width), f32),
                        pltpu.VMEM((n_t, bt, width), f32),
                        pltpu.VMEM((bt, width), f32)],
        compiler_params=pltpu.CompilerParams(dimension_semantics=("arbitrary", "arbitrary"),
                                             vmem_limit_bytes=VMEM_LIMIT),
        name="mixer_step",
    )(glu, bx, gate, glu_nat, bx_nat, sa, sb, sh, wat, wa2, wbt, wb2,
      ba, ng, nb, bb, wg, brg, big, lam)


def _outproj_kernel(*refs, split, **kw):
    for h in range(split):
        _outproj_rows(h, split, *refs, **kw)


def _outproj_rows(h, split, ap_ref, bp_ref, as_ref, bs_ref, xp_ref, xs_ref, ling_ref, linb_ref, wo_ref,
                  l1g_ref, l1b_ref, wrh_ref, wrl_ref, br_ref,
                  h1_ref, h1p_ref, route_ref, cnt_ref, carry, *, n_p_tiles, alpha):
    i = pl.program_id(0)
    is_p = i < n_p_tiles
    sub = ap_ref.shape[0] // split
    per_tok = h1p_ref.shape[0] // ap_ref.shape[0]
    rows = pl.ds(h * sub, sub)
    ap_ref, bp_ref, as_ref, bs_ref = (r.at[rows] for r in (ap_ref, bp_ref, as_ref, bs_ref))
    xp_ref, xs_ref, h1_ref, route_ref = (r.at[rows] for r in (xp_ref, xs_ref, h1_ref, route_ref))
    h1p_ref = h1p_ref.at[pl.ds(h * sub * per_tok, sub * per_tok)]
    a = jnp.where(is_p, ap_ref[...], as_ref[...])
    b = jnp.where(is_p, bp_ref[...], bs_ref[...])
    x = jnp.where(is_p, xp_ref[...], xs_ref[...])
    width = a.shape[1]
    mix = (jnp.dot(a, wo_ref[0:width, :], preferred_element_type=f32)
           + jnp.dot(b, wo_ref[width:2 * width, :], preferred_element_type=f32))
    h0 = _layer_norm(x, ling_ref[...], linb_ref[...])
    h1 = _layer_norm(alpha * h0 + mix, l1g_ref[...], l1b_ref[...])
    h1_ref[...] = h1

    xh = h1.astype(bf16)
    xh32 = xh.astype(f32)
    half = h1.shape[1] // 2
    lo = lax.shift_right_logical(lax.bitcast_convert_type(xh32[:, :half], jnp.uint32), jnp.uint32(16))
    hi = lax.bitcast_convert_type(xh32[:, half:], jnp.uint32) & jnp.uint32(0xFFFF0000)
    word = lo | hi
    n_sub = half // LANES
    for s in range(n_sub):
        h1p_ref[pl.ds(s, word.shape[0], stride=n_sub), :] = word[:, s * LANES:(s + 1) * LANES]

    xl = (h1 - xh32).astype(bf16)
    z = (jnp.dot(xh, wrh_ref[...], preferred_element_type=f32)
         + jnp.dot(xl, wrh_ref[...], preferred_element_type=f32)
         + jnp.dot(xh, wrl_ref[...], preferred_element_type=f32)) + br_ref[...]
    tm = z.shape[0]
    lane = lax.broadcasted_iota(i32, z.shape, 1).astype(f32)
    n_g, epg = float(N_GROUPS), float(EXPERTS_PER_GROUP)
    far = float(2 * ROUTE_LANES)

    gm = lane < n_g
    zg = jnp.where(gm, z, NEG_BIG)
    pg = jnp.where(gm, jnp.exp(zg - jnp.max(zg, axis=-1, keepdims=True)), 0.0)
    pg = pg / jnp.sum(pg, axis=-1, keepdims=True)
    g_top = jnp.max(pg, axis=-1, keepdims=True)
    g_idx = jnp.min(jnp.where(gm & (pg == g_top), lane, far), axis=-1, keepdims=True)

    lo = n_g + g_idx * epg
    em = (lane >= lo) & (lane < lo + epg)
    ze = jnp.where(em, z, NEG_BIG)
    pe = jnp.where(em, jnp.exp(ze - jnp.max(ze, axis=-1, keepdims=True)), 0.0)
    pe = pe / jnp.sum(pe, axis=-1, keepdims=True)
    v1 = jnp.max(jnp.where(em, pe, -1.0), axis=-1, keepdims=True)
    l1 = jnp.min(jnp.where(em & (pe == v1), lane, far), axis=-1, keepdims=True)
    pe2 = jnp.where(em & (lane != l1), pe, -1.0)
    v2 = jnp.max(pe2, axis=-1, keepdims=True)
    l2 = jnp.min(jnp.where(pe2 == v2, lane, far), axis=-1, keepdims=True)
    den = v1 + v2
    gate1 = v1 / den * g_top
    gate2 = v2 / den * g_top

    @pl.when((i == 0) & (h == 0))
    def _():
        carry[...] = jnp.zeros_like(carry)

    o1 = lane == l1
    o2 = lane == l2
    onehot = jnp.where(o1 | o2, 1.0, 0.0)
    ri = lax.broadcasted_iota(i32, (tm, tm), 0)
    ci = lax.broadcasted_iota(i32, (tm, tm), 1)
    tri = jnp.where(ci < ri, 1.0, 0.0).astype(bf16)
    prefix = jnp.dot(tri, onehot.astype(bf16), preferred_element_type=f32) + carry[0:1, :]
    rank1 = jnp.sum(jnp.where(o1, prefix, 0.0), axis=-1, keepdims=True)
    rank2 = jnp.sum(jnp.where(o2, prefix, 0.0), axis=-1, keepdims=True)
    new_carry = carry[...] + jnp.sum(onehot, axis=0, keepdims=True)
    carry[...] = new_carry
    cnt_ref[...] = new_carry

    route = jnp.where(lane == 0.0, l1 - n_g, 0.0)
    route = jnp.where(lane == 1.0, l2 - n_g, route)
    route = jnp.where(lane == 2.0, rank1, route)
    route = jnp.where(lane == 3.0, rank2, route)
    route = jnp.where(lane == 4.0, gate1, route)
    route = jnp.where(lane == 5.0, gate2, route)
    route_ref[...] = route


def _outproj(a_p, b_p, a_s, b_s, x_p, x_s, ln_in_g, ln_in_b, w_out_b, ln1_g, ln1_b,
             wr_hi, wr_lo, b_r, alpha, tm):
    n_p, width = a_p.shape
    n_s = a_s.shape[0]
    d = x_p.shape[1]
    n_p_tiles = n_p // tm
    n_s_tiles = n_s // tm
    n_tiles = n_p_tiles + n_s_tiles

    def p_map(i):
        return (jnp.minimum(i, n_p_tiles - 1), 0)

    def s_map(i):
        return (jnp.clip(i - n_p_tiles, 0, n_s_tiles - 1), 0)

    def const(shape, **kw):
        return pl.BlockSpec(shape, lambda i: (0,) * len(shape), **kw)

    return pl.pallas_call(
        functools.partial(_outproj_kernel, split=OUTPROJ_SPLIT, n_p_tiles=n_p_tiles, alpha=alpha),
        grid=(n_tiles,),
        in_specs=[pl.BlockSpec((tm, width), p_map), pl.BlockSpec((tm, width), p_map),
                  pl.BlockSpec((tm, width), s_map), pl.BlockSpec((tm, width), s_map),
                  pl.BlockSpec((tm, d), p_map), pl.BlockSpec((tm, d), s_map),
                  const((1, d)), const((1, d)),
                  const((2 * width, d), pipeline_mode=pl.Buffered(1)),
                  const((1, d)), const((1, d)),
                  const((d, ROUTE_LANES)), const((d, ROUTE_LANES)), const((1, ROUTE_LANES))],
        out_specs=[pl.BlockSpec((tm, d), lambda i: (i, 0)),
                   pl.BlockSpec((tm * (d // 2 // LANES), LANES), lambda i: (i, 0)),
                   pl.BlockSpec((tm, ROUTE_LANES), lambda i: (i, 0)),
                   const((SUBLANES, ROUTE_LANES))],
        out_shape=[jax.ShapeDtypeStruct((n_p + n_s, d), f32),
                   jax.ShapeDtypeStruct(((n_p + n_s) * (d // 2 // LANES), LANES), jnp.uint32),
                   jax.ShapeDtypeStruct((n_p + n_s, ROUTE_LANES), f32),
                   jax.ShapeDtypeStruct((SUBLANES, ROUTE_LANES), f32)],
        scratch_shapes=[pltpu.VMEM((SUBLANES, ROUTE_LANES), f32)],
        compiler_params=pltpu.CompilerParams(dimension_semantics=("arbitrary",),
                                             vmem_limit_bytes=VMEM_LIMIT),
        name="outproj_router",
    )(a_p, b_p, a_s, b_s, x_p, x_s, ln_in_g, ln_in_b, w_out_b, ln1_g, ln1_b, wr_hi, wr_lo, b_r)


def _expert_kernel(src_ref, dst_ref, start_ref, nblk_ref,
                   h1_hbm, wg_ref, wu_ref, wd_ref, y2_hbm,
                   xbuf, ybuf, wgb, wub, wdb, gsem, osem, *, blk, n_rows):
    e = pl.program_id(0)
    n_e = pl.num_programs(0)
    nb = nblk_ref[e]
    g0 = start_ref[e]

    xs = xbuf.shape[1] // blk
    ys = ybuf.shape[1] // blk

    def gather(g, slot):
        for r in range(blk):
            tok = src_ref[g * blk + r]
            pltpu.make_async_copy(h1_hbm.at[pl.ds(pl.multiple_of(tok * xs, xs), xs), :],
                                  xbuf.at[slot, pl.ds(r * xs, xs), :], gsem.at[slot]
                                  ).start(priority=GATHER_PRIORITY)

    def scatter(g, slot):
        for r in range(blk):
            row = dst_ref[(g + 1) * blk + r]
            pltpu.make_async_copy(ybuf.at[slot, pl.ds(r * ys, ys), :],
                                  y2_hbm.at[pl.ds(pl.multiple_of(row * ys, ys), ys), :],
                                  osem.at[slot]).start(priority=r % 2)

    def dump_copy(slot):
        dst = y2_hbm.at[pl.ds((n_rows + slot * blk) * ys, blk * ys), :]
        return pltpu.make_async_copy(ybuf.at[slot], dst, osem.at[slot])

    def wait_gather(slot):
        pltpu.make_async_copy(h1_hbm.at[pl.ds(0, blk * xs), :], xbuf.at[slot], gsem.at[slot]).wait()

    nx, ny = xbuf.shape[0], ybuf.shape[0]
    ahead = nx - 1

    @pl.when(e == 0)
    def _():
        ybuf[...] = jnp.zeros(ybuf.shape, ybuf.dtype)
        for s in range(ny - 1):
            dump_copy(s).start()
        for g in range(ahead):
            gather(g, g)

    @pl.when(nb > 0)
    def _():
        wgb[...] = wg_ref[...].astype(bf16)
        wub[...] = wu_ref[...].astype(bf16)
        wdb[...] = wd_ref[...].astype(bf16)

        def chunk(c, carry):
            g = g0 + c
            slot = g % nx
            yslot = g % ny
            wait_gather(slot)
            dump_copy(yslot).wait()
            gather(g + ahead, (g + ahead) % nx)
            scatter(g - 1, (g + ny - 1) % ny)
            xw = [xbuf[slot, pl.ds(s, blk, stride=xs), :] for s in range(xs)]
            x_lo = [lax.bitcast_convert_type(lax.shift_left(w, jnp.uint32(16)), f32) for w in xw]
            x_hi = [lax.bitcast_convert_type(w & jnp.uint32(0xFFFF0000), f32) for w in xw]
            x = jnp.concatenate(x_lo + x_hi, axis=1).astype(bf16)
            hg = jnp.dot(x, wgb[...], preferred_element_type=f32)
            hu = jnp.dot(x, wub[...], preferred_element_type=f32)
            hid = (hg * jax.nn.sigmoid(hg) * hu).astype(bf16)
            y = jnp.dot(hid, wdb[...], preferred_element_type=f32)
            for s in range(ys):
                ybuf[yslot, pl.ds(s, blk, stride=ys), :] = y[:, s * LANES:(s + 1) * LANES]
            return carry

        lax.fori_loop(0, nb, chunk, 0)

    @pl.when(e == n_e - 1)
    def _():
        g_end = g0 + nb
        scatter(g_end - 1, (g_end + ny - 1) % ny)
        for s in range(ny):
            dump_copy(s).wait()
        for j in range(ahead):
            wait_gather((g_end + j) % nx)


def _experts(h1p, src_rows, dst_rows, pad_start, nblk, w_gate, w_up, w_down, blk):
    n_e, d, d_e = w_gate.shape
    xs, ys = d // 2 // LANES, d // LANES
    n = h1p.shape[0] // xs
    n_rows = TOP_K * n
    grid_spec = pltpu.PrefetchScalarGridSpec(
        num_scalar_prefetch=4,
        grid=(n_e,),
        in_specs=[pl.BlockSpec(memory_space=pl.ANY),
                  pl.BlockSpec((None, d, d_e), lambda e, *_: (e, 0, 0)),
                  pl.BlockSpec((None, d, d_e), lambda e, *_: (e, 0, 0)),
                  pl.BlockSpec((None, d_e, d), lambda e, *_: (e, 0, 0))],
        out_specs=pl.BlockSpec(memory_space=pl.ANY),
        scratch_shapes=[pltpu.VMEM((GATHER_SLOTS, blk * xs, LANES), jnp.uint32),
                        pltpu.VMEM((SCATTER_SLOTS, blk * ys, LANES), f32),
                        pltpu.VMEM((d, d_e), bf16),
                        pltpu.VMEM((d, d_e), bf16),
                        pltpu.VMEM((d_e, d), bf16),
                        pltpu.SemaphoreType.DMA((GATHER_SLOTS,)),
                        pltpu.SemaphoreType.DMA((SCATTER_SLOTS,))],
    )
    return pl.pallas_call(
        functools.partial(_expert_kernel, blk=blk, n_rows=n_rows),
        grid_spec=grid_spec,
        out_shape=jax.ShapeDtypeStruct(((n_rows + SCATTER_SLOTS * blk) * ys, LANES), f32),
        compiler_params=pltpu.CompilerParams(dimension_semantics=("arbitrary",),
                                             vmem_limit_bytes=VMEM_LIMIT),
        name="experts",
    )(src_rows, dst_rows, pad_start, nblk, h1p, w_gate, w_up, w_down)


def _combine_kernel(h1_ref, route_ref, ya_ref, yb_ref, l2g_ref, l2b_ref, yp_ref, ysm_ref,
                    *, n_p_tiles, alpha):
    i = pl.program_id(0)
    route = route_ref[...]
    tm = route.shape[0]
    ys = ya_ref.shape[0] // tm

    def token_rows(ref):
        return jnp.concatenate([ref[pl.ds(s, tm, stride=ys), :] for s in range(ys)], axis=1)

    y = token_rows(ya_ref) * route[:, 4:5] + token_rows(yb_ref) * route[:, 5:6]
    out = _layer_norm(alpha * h1_ref[...] + y, l2g_ref[...], l2b_ref[...])

    @pl.when(i < n_p_tiles)
    def _():
        yp_ref[...] = out

    @pl.when(i >= n_p_tiles)
    def _():
        ysm_ref[...] = out


def _combine(h1, route, y2, ln2_g, ln2_b, n_p, alpha, tm):
    n, d = h1.shape
    n_s = n - n_p
    n_tiles = n // tm
    n_p_tiles = n_p // tm
    n_s_tiles = n_s // tm
    return pl.pallas_call(
        functools.partial(_combine_kernel, n_p_tiles=n_p_tiles, alpha=alpha),
        grid=(n_tiles,),
        in_specs=[pl.BlockSpec((tm, d), lambda i: (i, 0)),
                  pl.BlockSpec((tm, ROUTE_LANES), lambda i: (i, 0)),
                  pl.BlockSpec((tm * (d // LANES), LANES), lambda i: (i, 0)),
                  pl.BlockSpec((tm * (d // LANES), LANES), lambda i: (n_tiles + i, 0)),
                  pl.BlockSpec((1, d), lambda i: (0, 0)),
                  pl.BlockSpec((1, d), lambda i: (0, 0))],
        out_specs=[pl.BlockSpec((tm, d), lambda i: (jnp.minimum(i, n_p_tiles - 1), 0)),
                   pl.BlockSpec((tm, d), lambda i: (jnp.clip(i - n_p_tiles, 0, n_s_tiles - 1), 0))],
        out_shape=[jax.ShapeDtypeStruct((n_p, d), f32), jax.ShapeDtypeStruct((n_s, d), f32)],
        compiler_params=pltpu.CompilerParams(dimension_semantics=("arbitrary",),
                                             vmem_limit_bytes=VMEM_LIMIT),
        name="combine_ln2",
    )(h1, route, y2, y2, ln2_g, ln2_b)


def _shifted_taps(w, n_t, hist):
    k = w.shape[0]
    zero = jnp.zeros_like(w[0])
    wt = jnp.stack([jnp.stack([w[r - t] if 0 <= r - t < k else zero for r in range(hist)])
                    for t in range(n_t)])
    w2 = jnp.stack([jnp.stack([w[hist - t + j] if (j <= t and 0 <= hist - t + j < k) else zero
                               for j in range(n_t)]) for t in range(n_t)])
    return wt, w2


def kernel(x_prompt, x_sample, state_conv_a, state_conv_b, state_lru, meta_tokens, ln_in_g, ln_in_b, w_in, conv_a_w, conv_a_b, norm_a_g, norm_a_b, conv_b_w, conv_b_b, w_rg, b_rg, w_ig, b_ig, lru_lambda, w_out, ln1_g, ln1_b, w_router_group, b_router_group, w_router_expert, b_router_expert, w_gate, w_up, w_down, ln2_g, ln2_b):
    depth = w_in.shape[0]
    assert depth == 1, "single-layer trunk only"
    bp, tp, d = x_prompt.shape
    bs, ts, _ = x_sample.shape
    n_meta = meta_tokens.shape[0]
    width = conv_a_w.shape[2]
    ka, kb = conv_a_w.shape[1], conv_b_w.shape[1]
    n_e = w_gate.shape[1]
    alpha = (2.0 * depth) ** 0.25
    n_p, n_s = bp * tp, bs * ts
    n = n_p + n_s
    assert ts < ka and ts >= kb - 1
    assert n_p % TOKEN_TILE == 0 and n_s % TOKEN_TILE == 0 and tp % SEQ_TILE == 0
    assert bs % STEP_BATCH == 0 and n_p % STEP_BATCH == 0
    assert n_p % COMBINE_TILE == 0 and n_s % COMBINE_TILE == 0
    assert n_meta % BF16_ROWS == 0
    assert N_GROUPS * (1 + EXPERTS_PER_GROUP) <= ROUTE_LANES and n_e == N_GROUPS * EXPERTS_PER_GROUP

    row = lambda v: v.reshape(1, -1).astype(f32)
    ln_g, ln_b = row(ln_in_g), row(ln_in_b)
    w_in_b = w_in[0].astype(bf16)
    w_out_b = w_out[0].astype(bf16)
    mix_w = (conv_a_w[0], row(conv_a_b[0]), row(norm_a_g[0]), row(norm_a_b[0]),
             conv_b_w[0], row(conv_b_b[0]),
             jnp.concatenate([w_rg[0], w_ig[0]], axis=-1).astype(bf16),
             row(b_rg[0]), row(b_ig[0]), row(lru_lambda[0]))
    ha, hb = _round_up(ka - 1, SUBLANES), _round_up(kb - 1, SUBLANES)

    def pad_hist(s, h):
        return jnp.concatenate([jnp.zeros(s.shape[:1] + (h - s.shape[1], width), f32), s], axis=1)

    glu_m, bx_m, gate_m = _inproj(meta_tokens, meta_tokens, 0, ln_g, ln_b, w_in_b, n_meta)
    _, _, sa_m, sb_m, sh_m = _mixer_seq(
        glu_m, bx_m, gate_m, jnp.zeros((1, ha, width), f32), jnp.zeros((1, hb, width), f32),
        jnp.zeros((1, 1, width), f32), mix_w, 1, n_meta, n_meta)

    x_p = x_prompt.reshape(n_p, d)
    x_s = jnp.swapaxes(x_sample, 0, 1).reshape(n_s, d)
    glu, bx, gate = _inproj(x_p, x_s, n_s, ln_g, ln_b, w_in_b, TOKEN_TILE)

    bcast = lambda s: jnp.broadcast_to(s, (bp,) + s.shape[1:])
    a_p, b_p, nsa_p, nsb_p, nsh_p = _mixer_seq(glu, bx, gate, bcast(sa_m), bcast(sb_m), bcast(sh_m),
                                               mix_w, bp, tp, SEQ_TILE)

    nat = lambda v: jnp.swapaxes(v[n_p:].reshape(ts, bs, width), 0, 1)
    wat, wa2 = _shifted_taps(conv_a_w[0], ts, ka - 1)
    wbt, wb2 = _shifted_taps(conv_b_w[0], ts, kb - 1)
    a_s, b_s, nsa_s, nsb_s, nsh_s = _mixer_step(
        glu, bx, gate, n_p, nat(glu), nat(bx), state_conv_a[0], state_conv_b[0], state_lru[0],
        (wat, wa2, wbt, wb2), mix_w, bs, ts, STEP_BATCH)

    w_r = jnp.concatenate([w_router_group[0], w_router_expert[0]], axis=1)
    w_r = jnp.pad(w_r, ((0, 0), (0, ROUTE_LANES - w_r.shape[1])))
    wr_hi = w_r.astype(bf16)
    wr_lo = (w_r - wr_hi.astype(f32)).astype(bf16)
    b_r = jnp.concatenate([b_router_group[0], b_router_expert[0]])
    b_r = jnp.pad(b_r, (0, ROUTE_LANES - b_r.shape[0])).reshape(1, ROUTE_LANES)
    h1, h1p, route, cnt = _outproj(a_p, b_p, a_s, b_s, x_p, x_s, ln_g, ln_b, w_out_b,
                              row(ln1_g[0]), row(ln1_b[0]), wr_hi, wr_lo, b_r, alpha, TOKEN_TILE)

    blk = EXPERT_ROWS
    e_idx = route[:, 0:TOP_K].astype(i32)
    rank = route[:, 2:2 + TOP_K].astype(i32)
    counts = cnt[0, N_GROUPS:N_GROUPS + n_e].astype(i32)
    nblk = (counts + blk - 1) // blk
    pad_end = jnp.cumsum(nblk * blk)
    pad_start = pad_end - nblk * blk
    start_of = jnp.sum(jnp.where(e_idx[..., None] == jnp.arange(n_e, dtype=i32), pad_start, 0), axis=-1)
    dest = (start_of + rank).reshape(-1)
    p_max = _round_up(TOP_K * n + n_e * (blk - 1), blk)
    flat = jnp.full((p_max,), -1, i32).at[dest].set(jnp.arange(TOP_K * n, dtype=i32))
    pos = jnp.arange(p_max, dtype=i32)
    valid = flat >= 0
    src_rows = jnp.where(valid, flat // TOP_K, 0)
    dst_rows = jnp.where(valid, (flat % TOP_K) * n + flat // TOP_K, TOP_K * n + pos % (SCATTER_SLOTS * blk))
    src_rows = jnp.concatenate([src_rows, jnp.zeros(((GATHER_SLOTS - 1) * blk,), i32)])
    dst_rows = jnp.concatenate([TOP_K * n + (SCATTER_SLOTS - 1) * blk + jnp.arange(blk, dtype=i32), dst_rows])
    y2 = _experts(h1p, src_rows, dst_rows, (pad_start // blk).astype(i32), nblk,
                  w_gate[0], w_up[0], w_down[0], blk)
    y_p, y_s = _combine(h1, route, y2, row(ln2_g[0]), row(ln2_b[0]), n_p, alpha, TOKEN_TILE)

    y_prompt = y_p.reshape(bp, tp, d)
    y_sample = jnp.swapaxes(y_s.reshape(ts, bs, d), 0, 1)
    return (y_prompt, y_sample,
            nsa_p[:, ha - (ka - 1):][None], nsb_p[:, hb - (kb - 1):][None], nsh_p.reshape(1, bp, width),
            nsa_s[None], nsb_s[None], nsh_s[None])
```

```python
import functools

import jax
import jax.numpy as jnp
from jax import lax
from jax.experimental import pallas as pl
from jax.experimental.pallas import tpu as pltpu

f32 = jnp.float32
bf16 = jnp.bfloat16
i32 = jnp.int32

CONV_HEADS_A = 8
LRU_HEADS = 8
LRU_C = 8.0
N_GROUPS = 8
EXPERTS_PER_GROUP = 8
TOP_K = 2
LN_EPS = 1e-5

SUBLANES = 8
LANES = 128
BF16_ROWS = 16
VMEM_LIMIT = 56 * 1024 * 1024

TOKEN_TILE = 256
SEQ_TILE = 256
CONV_ROWS = 32
STEP_BATCH = 32
EXPERT_ROWS = 128
ROUTE_LANES = 128
GATHER_SLOTS = 4
SCATTER_SLOTS = 4
GATHER_PRIORITY = 1
NEG_BIG = -1e30


def _round_up(x, m):
    return (x + m - 1) // m * m


def _layer_norm(x, g, b):
    mu = jnp.mean(x, axis=-1, keepdims=True)
    xc = x - mu
    var = jnp.mean(xc * xc, axis=-1, keepdims=True)
    return xc * lax.rsqrt(var + LN_EPS) * g + b


def _group_norm_silu(x, g, b, group):
    parts = []
    for c in range(x.shape[1] // group):
        xg = x[:, c * group:(c + 1) * group]
        mu = jnp.mean(xg, axis=-1, keepdims=True)
        xc = xg - mu
        var = jnp.mean(xc * xc, axis=-1, keepdims=True)
        parts.append(xc * lax.rsqrt(var + LN_EPS))
    y = jnp.concatenate(parts, axis=-1) * g + b
    return y * jax.nn.sigmoid(y)


def _lru_gates(cb, wg_ref, brg, big, lam):
    heads, hd, _ = wg_ref.shape
    cbb = cb.astype(bf16)
    rs, iz = [], []
    for h in range(heads):
        z = jnp.dot(cbb[:, h * hd:(h + 1) * hd], wg_ref[h], preferred_element_type=f32)
        rs.append(z[:, :hd])
        iz.append(z[:, hd:])
    r = jax.nn.sigmoid(jnp.concatenate(rs, axis=-1) + brg)
    i = jax.nn.sigmoid(jnp.concatenate(iz, axis=-1) + big)
    log_a = -LRU_C * r * jax.nn.softplus(-lam)
    a = jnp.exp(log_a)
    u = jnp.sqrt(-jnp.tanh(log_a) * (a * a + 1.0)) * (i * cb)
    return a, u


def _inproj_kernel(xa_ref, xb_ref, g_ref, b_ref, w_ref, glu_ref, bx_ref, gate_ref, *, n_a_tiles):
    i = pl.program_id(0)
    x = jnp.where(i < n_a_tiles, xa_ref[...], xb_ref[...])
    xn = _layer_norm(x, g_ref[...], b_ref[...]).astype(bf16)
    width = glu_ref.shape[1]
    ch = min(width, 512)
    for c in range(width // ch):
        lo = c * ch
        av = jnp.dot(xn, w_ref[:, lo:lo + ch], preferred_element_type=f32)
        ag = jnp.dot(xn, w_ref[:, width + lo:width + lo + ch], preferred_element_type=f32)
        glu_ref[:, lo:lo + ch] = av * jax.nn.sigmoid(ag)
        bx_ref[:, lo:lo + ch] = jnp.dot(xn, w_ref[:, 2 * width + lo:2 * width + lo + ch],
                                        preferred_element_type=f32)
        bg = jnp.dot(xn, w_ref[:, 3 * width + lo:3 * width + lo + ch], preferred_element_type=f32)
        gate_ref[:, lo:lo + ch] = jax.nn.gelu(bg)


def _inproj(xa, xb, n_b_rows, ln_g, ln_b, w_in_b, tm):
    na, d = xa.shape
    width = w_in_b.shape[1] // 4
    n_a_tiles = na // tm
    n_tiles = (na + n_b_rows) // tm
    n_b_tiles = max(n_tiles - n_a_tiles, 1)
    row_out = pl.BlockSpec((tm, width), lambda i: (i, 0))
    return pl.pallas_call(
        functools.partial(_inproj_kernel, n_a_tiles=n_a_tiles),
        grid=(n_tiles,),
        in_specs=[
            pl.BlockSpec((tm, d), lambda i: (jnp.minimum(i, n_a_tiles - 1), 0)),
            pl.BlockSpec((tm, d), lambda i: (jnp.clip(i - n_a_tiles, 0, n_b_tiles - 1), 0)),
            pl.BlockSpec((1, d), lambda i: (0, 0)),
            pl.BlockSpec((1, d), lambda i: (0, 0)),
            pl.BlockSpec((d, 4 * width), lambda i: (0, 0), pipeline_mode=pl.Buffered(1)),
        ],
        out_specs=[row_out, row_out, row_out],
        out_shape=[jax.ShapeDtypeStruct((na + n_b_rows, width), f32)] * 3,
        compiler_params=pltpu.CompilerParams(dimension_semantics=("arbitrary",),
                                             vmem_limit_bytes=VMEM_LIMIT),
        name="inproj",
    )(xa, xb, ln_g, ln_b, w_in_b)


def _mixer_seq_kernel(glu_ref, bx_ref, gate_ref, sa_ref, sb_ref, sh_ref,
                      wa_ref, ba_ref, ng_ref, nb_ref, wb_ref, bb_ref, wg_ref, brg_ref, big_ref, lam_ref,
                      aout_ref, bout_ref, nsa_ref, nsb_ref, nsh_ref,
                      wina, winb, xsa, xsb, ca_s, cb_s, a_s, u_s, hcar):
    tau = pl.program_id(1)
    tt, width = glu_ref.shape
    ka, kb = wa_ref.shape[0], wb_ref.shape[0]
    ha, hb = sa_ref.shape[0], sb_ref.shape[0]

    @pl.when(tau == 0)
    def _():
        wina[0:ha, :] = sa_ref[...]
        winb[0:hb, :] = sb_ref[...]
        hcar[...] = jnp.broadcast_to(sh_ref[...], hcar.shape)

    wina[ha:ha + tt, :] = glu_ref[...]
    winb[hb:hb + tt, :] = bx_ref[...]
    for s in sorted({(ha - (ka - 1) + k) % SUBLANES for k in range(ka)} - {0}):
        xsa[s - 1] = wina[pl.ds(s, tt + ha - SUBLANES), :]
    for s in sorted({(hb - (kb - 1) + k) % SUBLANES for k in range(kb)} - {0}):
        xsb[s - 1] = winb[pl.ds(s, tt + hb - SUBLANES), :]

    def tap(win, xs, off, r0, lanes):
        q, s = divmod(off, SUBLANES)
        src = win if s == 0 else xs.at[s - 1]
        return src[pl.ds(r0 + SUBLANES * q, SUBLANES), lanes]

    rows = min(CONV_ROWS, tt)
    for g in range(width // LANES):
        lanes = pl.ds(g * LANES, LANES)
        bcast = lambda ref, k: jnp.broadcast_to(ref[k:k + 1, lanes], (SUBLANES, LANES))
        wa_k = [bcast(wa_ref, k) for k in range(ka)]
        wb_k = [bcast(wb_ref, k) for k in range(kb)]
        ba_g, bb_g = bcast(ba_ref, 0), bcast(bb_ref, 0)

        def conv_block(rb, carry):
            for j in range(rows // SUBLANES):
                r0 = pl.multiple_of(rb * rows, rows) + j * SUBLANES
                acc = ba_g
                for k in range(ka):
                    acc = acc + wa_k[k] * tap(wina, xsa, ha - (ka - 1) + k, r0, lanes)
                ca_s[pl.ds(r0, SUBLANES), lanes] = acc
                accb = bb_g
                for k in range(kb):
                    accb = accb + wb_k[k] * tap(winb, xsb, hb - (kb - 1) + k, r0, lanes)
                cb_s[pl.ds(r0, SUBLANES), lanes] = accb
            return carry

        lax.fori_loop(0, tt // rows, conv_block, 0)

    a_out = _group_norm_silu(ca_s[...], ng_ref[...], nb_ref[...], width // CONV_HEADS_A)
    aout_ref[...] = a_out.astype(aout_ref.dtype)

    a, u = _lru_gates(cb_s[...], wg_ref, brg_ref[...], big_ref[...], lam_ref[...])
    row8 = lax.broadcasted_iota(i32, (tt, width), 0) & (SUBLANES - 1)
    d = 1
    while d < SUBLANES:
        m = row8 >= d
        a_sh = jnp.where(m, pltpu.roll(a, d, axis=0), 1.0)
        u_sh = jnp.where(m, pltpu.roll(u, d, axis=0), 0.0)
        u = a * u_sh + u
        a = a * a_sh
        d *= 2
    a_s[...] = a
    u_s[...] = u

    def scan_block(j, h):
        r0 = pl.multiple_of(j * SUBLANES, SUBLANES)
        hblk = a_s[pl.ds(r0, SUBLANES), :] * h + u_s[pl.ds(r0, SUBLANES), :]
        u_s[pl.ds(r0, SUBLANES), :] = hblk
        return jnp.broadcast_to(hblk[SUBLANES - 1:SUBLANES, :], h.shape)

    h_last = lax.fori_loop(0, tt // SUBLANES, scan_block, hcar[...])
    hcar[...] = h_last
    bout_ref[...] = (u_s[...] * gate_ref[...]).astype(bout_ref.dtype)

    tail_a = wina[tt:tt + ha, :]
    tail_b = winb[tt:tt + hb, :]
    wina[0:ha, :] = tail_a
    winb[0:hb, :] = tail_b
    nsa_ref[...] = tail_a
    nsb_ref[...] = tail_b
    nsh_ref[...] = h_last[0:1, :]


def _mixer_seq(glu, bx, gate, row_off, sa_pad, sb_pad, sh, mix_w, n_seq, seq_len, tt):
    wa, ba, ng, nb, wb, bb, wg, brg, big, lam = mix_w
    width = glu.shape[1]
    ha, hb = sa_pad.shape[1], sb_pad.shape[1]
    n_t = seq_len // tt
    off = row_off // tt
    row_in = pl.BlockSpec((tt, width), lambda b, t: (off + b * n_t + t, 0))
    row_out = pl.BlockSpec((tt, width), lambda b, t: (b * n_t + t, 0))

    def const(shape):
        return pl.BlockSpec(shape, lambda b, t: (0,) * len(shape))

    def per_seq(r):
        return pl.BlockSpec((None, r, width), lambda b, t: (b, 0, 0))

    return pl.pallas_call(
        _mixer_seq_kernel,
        grid=(n_seq, n_t),
        in_specs=[row_in, row_in, row_in, per_seq(ha), per_seq(hb), per_seq(1),
                  const(wa.shape), const(ba.shape), const(ng.shape), const(nb.shape),
                  const(wb.shape), const(bb.shape), const(wg.shape), const(brg.shape),
                  const(big.shape), const(lam.shape)],
        out_specs=[row_out, row_out, per_seq(ha), per_seq(hb), per_seq(1)],
        out_shape=[jax.ShapeDtypeStruct((n_seq * seq_len, width), bf16),
                   jax.ShapeDtypeStruct((n_seq * seq_len, width), bf16),
                   jax.ShapeDtypeStruct((n_seq, ha, width), f32),
                   jax.ShapeDtypeStruct((n_seq, hb, width), f32),
                   jax.ShapeDtypeStruct((n_seq, 1, width), f32)],
        scratch_shapes=[pltpu.VMEM((ha + tt, width), f32),
                        pltpu.VMEM((hb + tt, width), f32),
                        pltpu.VMEM((SUBLANES - 1, tt + ha - SUBLANES, width), f32),
                        pltpu.VMEM((SUBLANES - 1, tt + hb - SUBLANES, width), f32),
                        pltpu.VMEM((tt, width), f32),
                        pltpu.VMEM((tt, width), f32),
                        pltpu.VMEM((tt, width), f32),
                        pltpu.VMEM((tt, width), f32),
                        pltpu.VMEM((SUBLANES, width), f32)],
        compiler_params=pltpu.CompilerParams(dimension_semantics=("arbitrary", "arbitrary"),
                                             vmem_limit_bytes=VMEM_LIMIT),
        name="mixer_seq",
    )(glu, bx, gate, sa_pad, sb_pad, sh, wa, ba, ng, nb, wb, bb, wg, brg, big, lam)


def _mixer_step_kernel(glu_ref, bx_ref, gate_ref, glun_ref, bxn_ref, sa_ref, sb_ref, sh_ref,
                       wat_ref, wa2_ref, wbt_ref, wb2_ref,
                       ba_ref, ng_ref, nb_ref, bb_ref, wg_ref, brg_ref, big_ref, lam_ref,
                       aout_ref, bout_ref, nsa_ref, nsb_ref, nsh_ref,
                       hista, histb, hcar):
    t = pl.program_id(1)
    n_t = hista.shape[0]
    width = glu_ref.shape[1]
    ka1, kb1 = sa_ref.shape[1], sb_ref.shape[1]

    @pl.when(t == 0)
    def _():
        hista[...] = jnp.zeros_like(hista)
        histb[...] = jnp.zeros_like(histb)
        hcar[...] = sh_ref[...]
        nsa_ref[:, 0:ka1 - n_t, :] = sa_ref[:, n_t:ka1, :]
        nsa_ref[:, ka1 - n_t:ka1, :] = glun_ref[...]
        nsb_ref[...] = bxn_ref[:, n_t - kb1:n_t, :]

    hista[t] = glu_ref[...]
    histb[t] = bx_ref[...]

    ca = ba_ref[...] + jnp.sum(sa_ref[...] * wat_ref[...][None], axis=1)
    cb = bb_ref[...] + jnp.sum(sb_ref[...] * wbt_ref[...][None], axis=1)
    for j in range(n_t):
        ca = ca + wa2_ref[j:j + 1, :] * hista[j]
        cb = cb + wb2_ref[j:j + 1, :] * histb[j]
    a_out = _group_norm_silu(ca, ng_ref[...], nb_ref[...], width // CONV_HEADS_A)
    aout_ref[...] = a_out.astype(aout_ref.dtype)

    a, u = _lru_gates(cb, wg_ref, brg_ref[...], big_ref[...], lam_ref[...])
    h = a * hcar[...] + u
    hcar[...] = h
    nsh_ref[...] = h
    bout_ref[...] = (h * gate_ref[...]).astype(bout_ref.dtype)


def _mixer_step(glu, bx, gate, row_off, glu_nat, bx_nat, sa, sb, sh, step_w, mix_w, n_seq, n_t, bt):
    wat, wa2, wbt, wb2 = step_w
    wa, ba, ng, nb, wb, bb, wg, brg, big, lam = mix_w
    width = glu.shape[1]
    ka1, kb1 = sa.shape[2], sb.shape[2]
    off = row_off // bt
    nb_tiles = n_seq // bt
    row_in = pl.BlockSpec((bt, width), lambda j, t: (off + t * nb_tiles + j, 0))
    row_out = pl.BlockSpec((bt, width), lambda j, t: (t * nb_tiles + j, 0))

    def const(shape):
        return pl.BlockSpec(shape, lambda j, t: (0,) * len(shape))

    def per_t(arr):
        return pl.BlockSpec((None,) + arr.shape[1:], lambda j, t: (t, 0, 0))

    def per_b(r):
        return pl.BlockSpec((bt, r, width), lambda j, t: (j, 0, 0))

    def state(r):
        return pl.BlockSpec((None, bt, r, width), lambda j, t: (0, j, 0, 0))

    state_h = pl.BlockSpec((None, bt, width), lambda j, t: (0, j, 0))
    return pl.pallas_call(
        _mixer_step_kernel,
        grid=(nb_tiles, n_t),
        in_specs=[row_in, row_in, row_in, per_b(n_t), per_b(n_t), state(ka1), state(kb1), state_h,
                  per_t(wat), per_t(wa2), per_t(wbt), per_t(wb2),
                  const(ba.shape), const(ng.shape), const(nb.shape), const(bb.shape),
                  const(wg.shape), const(brg.shape), const(big.shape), const(lam.shape)],
        out_specs=[row_out, row_out, state(ka1), state(kb1), state_h],
        out_shape=[jax.ShapeDtypeStruct((n_seq * n_t, width), bf16),
                   jax.ShapeDtypeStruct((n_seq * n_t, width), bf16),
                   jax.ShapeDtypeStruct((1, n_seq, ka1, width), f32),
                   jax.ShapeDtypeStruct((1, n_seq, kb1, width), f32),
                   jax.ShapeDtypeStruct((1, n_seq, width), f32)],
        scratch_shapes=[pltpu.VMEM((n_t, bt, width), f32),
                        pltpu.VMEM((n_t, bt, width), f32),
                        pltpu.VMEM((bt, width), f32)],
        compiler_params=pltpu.CompilerParams(dimension_semantics=("arbitrary", "arbitrary"),
                                             vmem_limit_bytes=VMEM_LIMIT),
        name="mixer_step",
    )(glu, bx, gate, glu_nat, bx_nat, sa, sb, sh, wat, wa2, wbt, wb2,
      ba, ng, nb, bb, wg, brg, big, lam)


def _outproj_kernel(ap_ref, bp_ref, as_ref, bs_ref, xp_ref, xs_ref, ling_ref, linb_ref, wo_ref,
                    l1g_ref, l1b_ref, wrh_ref, wrl_ref, br_ref,
                    h1_ref, h1p_ref, route_ref, routet_ref, cnt_ref, carry, *, n_p_tiles, alpha):
    i = pl.program_id(0)
    is_p = i < n_p_tiles
    a = jnp.where(is_p, ap_ref[...], as_ref[...])
    b = jnp.where(is_p, bp_ref[...], bs_ref[...])
    x = jnp.where(is_p, xp_ref[...], xs_ref[...])
    width = a.shape[1]
    mix = (jnp.dot(a, wo_ref[0:width, :], preferred_element_type=f32)
           + jnp.dot(b, wo_ref[width:2 * width, :], preferred_element_type=f32))
    h0 = _layer_norm(x, ling_ref[...], linb_ref[...])
    h1 = _layer_norm(alpha * h0 + mix, l1g_ref[...], l1b_ref[...])
    h1_ref[...] = h1

    xh = h1.astype(bf16)
    xh32 = xh.astype(f32)
    half = h1.shape[1] // 2
    lo = lax.shift_right_logical(lax.bitcast_convert_type(xh32[:, :half], jnp.uint32), jnp.uint32(16))
    hi = lax.bitcast_convert_type(xh32[:, half:], jnp.uint32) & jnp.uint32(0xFFFF0000)
    word = lo | hi
    n_sub = half // LANES
    for s in range(n_sub):
        h1p_ref[pl.ds(s, word.shape[0], stride=n_sub), :] = word[:, s * LANES:(s + 1) * LANES]

    xl = (h1 - xh32).astype(bf16)
    z = (jnp.dot(xh, wrh_ref[...], preferred_element_type=f32)
         + jnp.dot(xl, wrh_ref[...], preferred_element_type=f32)
         + jnp.dot(xh, wrl_ref[...], preferred_element_type=f32)) + br_ref[...]
    tm = z.shape[0]
    lane = lax.broadcasted_iota(i32, z.shape, 1).astype(f32)
    n_g, epg = float(N_GROUPS), float(EXPERTS_PER_GROUP)
    far = float(2 * ROUTE_LANES)

    gm = lane < n_g
    zg = jnp.where(gm, z, NEG_BIG)
    pg = jnp.where(gm, jnp.exp(zg - jnp.max(zg, axis=-1, keepdims=True)), 0.0)
    pg = pg / jnp.sum(pg, axis=-1, keepdims=True)
    g_top = jnp.max(pg, axis=-1, keepdims=True)
    g_idx = jnp.min(jnp.where(gm & (pg == g_top), lane, far), axis=-1, keepdims=True)

    lo = n_g + g_idx * epg
    em = (lane >= lo) & (lane < lo + epg)
    ze = jnp.where(em, z, NEG_BIG)
    pe = jnp.where(em, jnp.exp(ze - jnp.max(ze, axis=-1, keepdims=True)), 0.0)
    pe = pe / jnp.sum(pe, axis=-1, keepdims=True)
    v1 = jnp.max(jnp.where(em, pe, -1.0), axis=-1, keepdims=True)
    l1 = jnp.min(jnp.where(em & (pe == v1), lane, far), axis=-1, keepdims=True)
    pe2 = jnp.where(em & (lane != l1), pe, -1.0)
    v2 = jnp.max(pe2, axis=-1, keepdims=True)
    l2 = jnp.min(jnp.where(pe2 == v2, lane, far), axis=-1, keepdims=True)
    den = v1 + v2
    gate1 = v1 / den * g_top
    gate2 = v2 / den * g_top

    @pl.when(i == 0)
    def _():
        carry[...] = jnp.zeros_like(carry)

    o1 = lane == l1
    o2 = lane == l2
    onehot = jnp.where(o1 | o2, 1.0, 0.0)
    ri = lax.broadcasted_iota(i32, (tm, tm), 0)
    ci = lax.broadcasted_iota(i32, (tm, tm), 1)
    tri = jnp.where(ci < ri, 1.0, 0.0).astype(bf16)
    prefix = jnp.dot(tri, onehot.astype(bf16), preferred_element_type=f32) + carry[0:1, :]
    rank1 = jnp.sum(jnp.where(o1, prefix, 0.0), axis=-1, keepdims=True)
    rank2 = jnp.sum(jnp.where(o2, prefix, 0.0), axis=-1, keepdims=True)
    new_carry = carry[...] + jnp.sum(onehot, axis=0, keepdims=True)
    carry[...] = new_carry
    cnt_ref[...] = new_carry

    route = jnp.where(lane == 0.0, l1 - n_g, 0.0)
    route = jnp.where(lane == 1.0, l2 - n_g, route)
    route = jnp.where(lane == 2.0, rank1, route)
    route = jnp.where(lane == 3.0, rank2, route)
    route = jnp.where(lane == 4.0, gate1, route)
    route = jnp.where(lane == 5.0, gate2, route)
    route_ref[...] = route
    routet_ref[...] = route.T[0:routet_ref.shape[0], :]


def _outproj(a_p, b_p, a_s, b_s, x_p, x_s, ln_in_g, ln_in_b, w_out_b, ln1_g, ln1_b,
             wr_hi, wr_lo, b_r, alpha, tm):
    n_p, width = a_p.shape
    n_s = a_s.shape[0]
    d = x_p.shape[1]
    n_p_tiles = n_p // tm
    n_s_tiles = n_s // tm
    n_tiles = n_p_tiles + n_s_tiles

    def p_map(i):
        return (jnp.minimum(i, n_p_tiles - 1), 0)

    def s_map(i):
        return (jnp.clip(i - n_p_tiles, 0, n_s_tiles - 1), 0)

    def const(shape, **kw):
        return pl.BlockSpec(shape, lambda i: (0,) * len(shape), **kw)

    return pl.pallas_call(
        functools.partial(_outproj_kernel, n_p_tiles=n_p_tiles, alpha=alpha),
        grid=(n_tiles,),
        in_specs=[pl.BlockSpec((tm, width), p_map), pl.BlockSpec((tm, width), p_map),
                  pl.BlockSpec((tm, width), s_map), pl.BlockSpec((tm, width), s_map),
                  pl.BlockSpec((tm, d), p_map), pl.BlockSpec((tm, d), s_map),
                  const((1, d)), const((1, d)),
                  const((2 * width, d), pipeline_mode=pl.Buffered(1)),
                  const((1, d)), const((1, d)),
                  const((d, ROUTE_LANES)), const((d, ROUTE_LANES)), const((1, ROUTE_LANES))],
        out_specs=[pl.BlockSpec((tm, d), lambda i: (i, 0)),
                   pl.BlockSpec((tm * (d // 2 // LANES), LANES), lambda i: (i, 0)),
                   pl.BlockSpec((tm, ROUTE_LANES), lambda i: (i, 0)),
                   pl.BlockSpec((SUBLANES, tm), lambda i: (0, i)),
                   const((SUBLANES, ROUTE_LANES))],
        out_shape=[jax.ShapeDtypeStruct((n_p + n_s, d), f32),
                   jax.ShapeDtypeStruct(((n_p + n_s) * (d // 2 // LANES), LANES), jnp.uint32),
                   jax.ShapeDtypeStruct((n_p + n_s, ROUTE_LANES), f32),
                   jax.ShapeDtypeStruct((SUBLANES, n_p + n_s), f32),
                   jax.ShapeDtypeStruct((SUBLANES, ROUTE_LANES), f32)],
        scratch_shapes=[pltpu.VMEM((SUBLANES, ROUTE_LANES), f32)],
        compiler_params=pltpu.CompilerParams(dimension_semantics=("arbitrary",),
                                             vmem_limit_bytes=VMEM_LIMIT),
        name="outproj_router",
    )(a_p, b_p, a_s, b_s, x_p, x_s, ln_in_g, ln_in_b, w_out_b, ln1_g, ln1_b, wr_hi, wr_lo, b_r)


def _expert_kernel(src_ref, dst_ref, start_ref, nblk_ref,
                   h1_hbm, wg_ref, wu_ref, wd_ref, y2_hbm,
                   xbuf, ybuf, wgb, wub, wdb, gsem, osem, *, blk, n_rows):
    e = pl.program_id(0)
    n_e = pl.num_programs(0)
    nb = nblk_ref[e]
    g0 = start_ref[e]

    xs = xbuf.shape[1] // blk
    ys = ybuf.shape[1] // blk

    def gather(g, slot):
        for r in range(blk):
            tok = src_ref[g * blk + r]
            pltpu.make_async_copy(h1_hbm.at[pl.ds(pl.multiple_of(tok * xs, xs), xs), :],
                                  xbuf.at[slot, pl.ds(r * xs, xs), :], gsem.at[slot]
                                  ).start(priority=GATHER_PRIORITY)

    def scatter(g, slot):
        for r in range(blk):
            row = dst_ref[(g + 1) * blk + r]
            pltpu.make_async_copy(ybuf.at[slot, pl.ds(r * ys, ys), :],
                                  y2_hbm.at[pl.ds(pl.multiple_of(row * ys, ys), ys), :],
                                  osem.at[slot]).start(priority=r % 2)

    def dump_copy(slot):
        dst = y2_hbm.at[pl.ds((n_rows + slot * blk) * ys, blk * ys), :]
        return pltpu.make_async_copy(ybuf.at[slot], dst, osem.at[slot])

    def wait_gather(slot):
        pltpu.make_async_copy(h1_hbm.at[pl.ds(0, blk * xs), :], xbuf.at[slot], gsem.at[slot]).wait()

    nx, ny = xbuf.shape[0], ybuf.shape[0]
    ahead = nx - 1

    @pl.when(e == 0)
    def _():
        ybuf[...] = jnp.zeros(ybuf.shape, ybuf.dtype)
        for s in range(ny - 1):
            dump_copy(s).start()
        for g in range(ahead):
            gather(g, g)

    @pl.when(nb > 0)
    def _():
        wgb[...] = wg_ref[...].astype(bf16)
        wub[...] = wu_ref[...].astype(bf16)
        wdb[...] = wd_ref[...].astype(bf16)

        def chunk(c, carry):
            g = g0 + c
            slot = g % nx
            yslot = g % ny
            wait_gather(slot)
            dump_copy(yslot).wait()
            gather(g + ahead, (g + ahead) % nx)
            scatter(g - 1, (g + ny - 1) % ny)
            xw = [xbuf[slot, pl.ds(s, blk, stride=xs), :] for s in range(xs)]
            x_lo = [lax.bitcast_convert_type(lax.shift_left(w, jnp.uint32(16)), f32) for w in xw]
            x_hi = [lax.bitcast_convert_type(w & jnp.uint32(0xFFFF0000), f32) for w in xw]
            x = jnp.concatenate(x_lo + x_hi, axis=1).astype(bf16)
            hg = jnp.dot(x, wgb[...], preferred_element_type=f32)
            hu = jnp.dot(x, wub[...], preferred_element_type=f32)
            hid = (hg * jax.nn.sigmoid(hg) * hu).astype(bf16)
            y = jnp.dot(hid, wdb[...], preferred_element_type=f32)
            for s in range(ys):
                ybuf[yslot, pl.ds(s, blk, stride=ys), :] = y[:, s * LANES:(s + 1) * LANES]
            return carry

        lax.fori_loop(0, nb, chunk, 0)

    @pl.when(e == n_e - 1)
    def _():
        g_end = g0 + nb
        scatter(g_end - 1, (g_end + ny - 1) % ny)
        for s in range(ny):
            dump_copy(s).wait()
        for j in range(ahead):
            wait_gather((g_end + j) % nx)


def _experts(h1p, src_rows, dst_rows, pad_start, nblk, w_gate, w_up, w_down, blk):
    n_e, d, d_e = w_gate.shape
    xs, ys = d // 2 // LANES, d // LANES
    n = h1p.shape[0] // xs
    n_rows = TOP_K * n
    grid_spec = pltpu.PrefetchScalarGridSpec(
        num_scalar_prefetch=4,
        grid=(n_e,),
        in_specs=[pl.BlockSpec(memory_space=pl.ANY),
                  pl.BlockSpec((None, d, d_e), lambda e, *_: (e, 0, 0)),
                  pl.BlockSpec((None, d, d_e), lambda e, *_: (e, 0, 0)),
                  pl.BlockSpec((None, d_e, d), lambda e, *_: (e, 0, 0))],
        out_specs=pl.BlockSpec(memory_space=pl.ANY),
        scratch_shapes=[pltpu.VMEM((GATHER_SLOTS, blk * xs, LANES), jnp.uint32),
                        pltpu.VMEM((SCATTER_SLOTS, blk * ys, LANES), f32),
                        pltpu.VMEM((d, d_e), bf16),
                        pltpu.VMEM((d, d_e), bf16),
                        pltpu.VMEM((d_e, d), bf16),
                        pltpu.SemaphoreType.DMA((GATHER_SLOTS,)),
                        pltpu.SemaphoreType.DMA((SCATTER_SLOTS,))],
    )
    return pl.pallas_call(
        functools.partial(_expert_kernel, blk=blk, n_rows=n_rows),
        grid_spec=grid_spec,
        out_shape=jax.ShapeDtypeStruct(((n_rows + SCATTER_SLOTS * blk) * ys, LANES), f32),
        compiler_params=pltpu.CompilerParams(dimension_semantics=("arbitrary",),
                                             vmem_limit_bytes=VMEM_LIMIT),
        name="experts",
    )(src_rows, dst_rows, pad_start, nblk, h1p, w_gate, w_up, w_down)


def _combine_kernel(h1_ref, route_ref, ya_ref, yb_ref, l2g_ref, l2b_ref, yp_ref, ysm_ref,
                    *, n_p_tiles, alpha):
    i = pl.program_id(0)
    route = route_ref[...]
    tm = route.shape[0]
    ys = ya_ref.shape[0] // tm

    def token_rows(ref):
        return jnp.concatenate([ref[pl.ds(s, tm, stride=ys), :] for s in range(ys)], axis=1)

    y = token_rows(ya_ref) * route[:, 4:5] + token_rows(yb_ref) * route[:, 5:6]
    out = _layer_norm(alpha * h1_ref[...] + y, l2g_ref[...], l2b_ref[...])

    @pl.when(i < n_p_tiles)
    def _():
        yp_ref[...] = out

    @pl.when(i >= n_p_tiles)
    def _():
        ysm_ref[...] = out


def _combine(h1, route, y2, ln2_g, ln2_b, n_p, alpha, tm):
    n, d = h1.shape
    n_s = n - n_p
    n_tiles = n // tm
    n_p_tiles = n_p // tm
    n_s_tiles = n_s // tm
    return pl.pallas_call(
        functools.partial(_combine_kernel, n_p_tiles=n_p_tiles, alpha=alpha),
        grid=(n_tiles,),
        in_specs=[pl.BlockSpec((tm, d), lambda i: (i, 0)),
                  pl.BlockSpec((tm, ROUTE_LANES), lambda i: (i, 0)),
                  pl.BlockSpec((tm * (d // LANES), LANES), lambda i: (i, 0)),
                  pl.BlockSpec((tm * (d // LANES), LANES), lambda i: (n_tiles + i, 0)),
                  pl.BlockSpec((1, d), lambda i: (0, 0)),
                  pl.BlockSpec((1, d), lambda i: (0, 0))],
        out_specs=[pl.BlockSpec((tm, d), lambda i: (jnp.minimum(i, n_p_tiles - 1), 0)),
                   pl.BlockSpec((tm, d), lambda i: (jnp.clip(i - n_p_tiles, 0, n_s_tiles - 1), 0))],
        out_shape=[jax.ShapeDtypeStruct((n_p, d), f32), jax.ShapeDtypeStruct((n_s, d), f32)],
        compiler_params=pltpu.CompilerParams(dimension_semantics=("arbitrary",),
                                             vmem_limit_bytes=VMEM_LIMIT),
        name="combine_ln2",
    )(h1, route, y2, y2, ln2_g, ln2_b)


def _shifted_taps(w, n_t, hist):
    k = w.shape[0]
    zero = jnp.zeros_like(w[0])
    wt = jnp.stack([jnp.stack([w[r - t] if 0 <= r - t < k else zero for r in range(hist)])
                    for t in range(n_t)])
    w2 = jnp.stack([jnp.stack([w[hist - t + j] if (j <= t and 0 <= hist - t + j < k) else zero
                               for j in range(n_t)]) for t in range(n_t)])
    return wt, w2


def kernel(x_prompt, x_sample, state_conv_a, state_conv_b, state_lru, meta_tokens, ln_in_g, ln_in_b, w_in, conv_a_w, conv_a_b, norm_a_g, norm_a_b, conv_b_w, conv_b_b, w_rg, b_rg, w_ig, b_ig, lru_lambda, w_out, ln1_g, ln1_b, w_router_group, b_router_group, w_router_expert, b_router_expert, w_gate, w_up, w_down, ln2_g, ln2_b):
    depth = w_in.shape[0]
    assert depth == 1, "single-layer trunk only"
    bp, tp, d = x_prompt.shape
    bs, ts, _ = x_sample.shape
    n_meta = meta_tokens.shape[0]
    width = conv_a_w.shape[2]
    ka, kb = conv_a_w.shape[1], conv_b_w.shape[1]
    n_e = w_gate.shape[1]
    alpha = (2.0 * depth) ** 0.25
    n_p, n_s = bp * tp, bs * ts
    n = n_p + n_s
    assert ts < ka and ts >= kb - 1
    assert n_p % TOKEN_TILE == 0 and n_s % TOKEN_TILE == 0 and tp % SEQ_TILE == 0
    assert bs % STEP_BATCH == 0 and n_p % STEP_BATCH == 0
    assert n_meta % BF16_ROWS == 0 and n_meta <= TOKEN_TILE and n % n_meta == 0
    assert N_GROUPS * (1 + EXPERTS_PER_GROUP) <= ROUTE_LANES and n_e == N_GROUPS * EXPERTS_PER_GROUP

    row = lambda v: v.reshape(1, -1).astype(f32)
    ln_g, ln_b = row(ln_in_g), row(ln_in_b)
    w_in_b = w_in[0].astype(bf16)
    w_out_b = w_out[0].astype(bf16)
    mix_w = (conv_a_w[0], row(conv_a_b[0]), row(norm_a_g[0]), row(norm_a_b[0]),
             conv_b_w[0], row(conv_b_b[0]),
             jnp.concatenate([w_rg[0], w_ig[0]], axis=-1).astype(bf16),
             row(b_rg[0]), row(b_ig[0]), row(lru_lambda[0]))
    ha, hb = _round_up(ka - 1, SUBLANES), _round_up(kb - 1, SUBLANES)

    x_p = x_prompt.reshape(n_p, d)
    x_s = jnp.concatenate([jnp.swapaxes(x_sample, 0, 1).reshape(n_s, d),
                           jnp.pad(meta_tokens, ((0, TOKEN_TILE - n_meta), (0, 0)))], axis=0)
    glu, bx, gate = _inproj(x_p, x_s, n_s + TOKEN_TILE, ln_g, ln_b, w_in_b, TOKEN_TILE)

    _, _, sa_m, sb_m, sh_m = _mixer_seq(
        glu, bx, gate, n, jnp.zeros((1, ha, width), f32), jnp.zeros((1, hb, width), f32),
        jnp.zeros((1, 1, width), f32), mix_w, 1, n_meta, n_meta)

    bcast = lambda s: jnp.broadcast_to(s, (bp,) + s.shape[1:])
    a_p, b_p, nsa_p, nsb_p, nsh_p = _mixer_seq(glu, bx, gate, 0, bcast(sa_m), bcast(sb_m), bcast(sh_m),
                                               mix_w, bp, tp, SEQ_TILE)

    nat = lambda v: jnp.swapaxes(v[n_p:n].reshape(ts, bs, width), 0, 1)
    wat, wa2 = _shifted_taps(conv_a_w[0], ts, ka - 1)
    wbt, wb2 = _shifted_taps(conv_b_w[0], ts, kb - 1)
    a_s, b_s, nsa_s, nsb_s, nsh_s = _mixer_step(
        glu, bx, gate, n_p, nat(glu), nat(bx), state_conv_a, state_conv_b, state_lru,
        (wat, wa2, wbt, wb2), mix_w, bs, ts, STEP_BATCH)

    w_r = jnp.concatenate([w_router_group[0], w_router_expert[0]], axis=1)
    w_r = jnp.pad(w_r, ((0, 0), (0, ROUTE_LANES - w_r.shape[1])))
    wr_hi = w_r.astype(bf16)
    wr_lo = (w_r - wr_hi.astype(f32)).astype(bf16)
    b_r = jnp.concatenate([b_router_group[0], b_router_expert[0]])
    b_r = jnp.pad(b_r, (0, ROUTE_LANES - b_r.shape[0])).reshape(1, ROUTE_LANES)
    h1, h1p, route, route_t, cnt = _outproj(a_p, b_p, a_s, b_s, x_p, x_s, ln_g, ln_b, w_out_b,
                                            row(ln1_g[0]), row(ln1_b[0]), wr_hi, wr_lo, b_r, alpha, TOKEN_TILE)

    blk = EXPERT_ROWS
    e_idx = route_t[0:TOP_K].astype(i32)
    rank = route_t[TOP_K:2 * TOP_K].astype(i32)
    counts = cnt[0, N_GROUPS:N_GROUPS + n_e].astype(i32)
    nblk = (counts + blk - 1) // blk
    pad_end = jnp.cumsum(nblk * blk)
    pad_start = pad_end - nblk * blk
    start_of = jnp.sum(jnp.where(e_idx[None] == jnp.arange(n_e, dtype=i32)[:, None, None],
                                 pad_start[:, None, None], 0), axis=0)
    dest = (start_of + rank).reshape(-1)
    p_max = _round_up(TOP_K * n + n_e * (blk - 1), blk)
    flat = jnp.full((p_max,), -1, i32).at[dest].set(jnp.arange(TOP_K * n, dtype=i32))
    pos = jnp.arange(p_max, dtype=i32)
    valid = flat >= 0
    src_rows = jnp.where(valid, flat % n, 0)
    dst_rows = jnp.where(valid, flat, TOP_K * n + pos % (SCATTER_SLOTS * blk))
    src_rows = jnp.concatenate([src_rows, jnp.zeros(((GATHER_SLOTS - 1) * blk,), i32)])
    dst_rows = jnp.concatenate([TOP_K * n + (SCATTER_SLOTS - 1) * blk + jnp.arange(blk, dtype=i32), dst_rows])
    y2 = _experts(h1p, src_rows, dst_rows, (pad_start // blk).astype(i32), nblk,
                  w_gate[0], w_up[0], w_down[0], blk)
    y_p, y_s = _combine(h1, route, y2, row(ln2_g[0]), row(ln2_b[0]), n_p, alpha, TOKEN_TILE)

    y_prompt = y_p.reshape(bp, tp, d)
    y_sample = jnp.swapaxes(y_s.reshape(ts, bs, d), 0, 1)
    return (y_prompt, y_sample,
            nsa_p[:, ha - (ka - 1):][None], nsb_p[:, hb - (kb - 1):][None], nsh_p.reshape(1, bp, width),
            nsa_s, nsb_s, nsh_s)
```

```python
import functools

import jax
import jax.numpy as jnp
from jax import lax
from jax.experimental import pallas as pl
from jax.experimental.pallas import tpu as pltpu

f32 = jnp.float32
bf16 = jnp.bfloat16
i32 = jnp.int32

CONV_HEADS_A = 8
LRU_HEADS = 8
LRU_C = 8.0
N_GROUPS = 8
EXPERTS_PER_GROUP = 8
TOP_K = 2
LN_EPS = 1e-5

SUBLANES = 8
LANES = 128
BF16_ROWS = 16
VMEM_LIMIT = 56 * 1024 * 1024

TOKEN_TILE = 256
SEQ_TILE = 256
CONV_ROWS = 64
STEP_BATCH = 32
EXPERT_ROWS = 128
ROUTE_LANES = 128
ROUTE_SUB = 128
ROUTE_MAX_TILE = 2304
GATHER_SLOTS = 4
SCATTER_SLOTS = 4
GATHER_PRIORITY = 1
NEG_BIG = -1e30


def _round_up(x, m):
    return (x + m - 1) // m * m


def _layer_norm(x, g, b):
    mu = jnp.mean(x, axis=-1, keepdims=True)
    xc = x - mu
    var = jnp.mean(xc * xc, axis=-1, keepdims=True)
    return xc * lax.rsqrt(var + LN_EPS) * g + b


def _group_norm_silu(x, g, b, group):
    parts = []
    for c in range(x.shape[1] // group):
        xg = x[:, c * group:(c + 1) * group]
        mu = jnp.mean(xg, axis=-1, keepdims=True)
        xc = xg - mu
        var = jnp.mean(xc * xc, axis=-1, keepdims=True)
        parts.append(xc * lax.rsqrt(var + LN_EPS))
    y = jnp.concatenate(parts, axis=-1) * g + b
    return y * jax.nn.sigmoid(y)


def _lru_gates(cb, wg_ref, brg, big, lam):
    heads, hd, _ = wg_ref.shape
    cbb = cb.astype(bf16)
    rs, iz = [], []
    for h in range(heads):
        z = jnp.dot(cbb[:, h * hd:(h + 1) * hd], wg_ref[h], preferred_element_type=f32)
        rs.append(z[:, :hd])
        iz.append(z[:, hd:])
    r = jax.nn.sigmoid(jnp.concatenate(rs, axis=-1) + brg)
    i = jax.nn.sigmoid(jnp.concatenate(iz, axis=-1) + big)
    log_a = -LRU_C * r * jax.nn.softplus(-lam)
    a = jnp.exp(log_a)
    u = jnp.sqrt(-jnp.tanh(log_a) * (a * a + 1.0)) * (i * cb)
    return a, u


def _inproj_kernel(xa_ref, xb_ref, g_ref, b_ref, w_ref, h_ref, glu_ref, bx_ref, gate_ref, *, n_a_tiles):
    i = pl.program_id(0)
    x = jnp.where(i < n_a_tiles, xa_ref[...], xb_ref[...])
    h = _layer_norm(x, g_ref[...], b_ref[...])
    h_ref[...] = h
    xn = h.astype(bf16)
    width = glu_ref.shape[1]
    ch = min(width, 512)
    for c in range(width // ch):
        lo = c * ch
        av = jnp.dot(xn, w_ref[:, lo:lo + ch], preferred_element_type=f32)
        ag = jnp.dot(xn, w_ref[:, width + lo:width + lo + ch], preferred_element_type=f32)
        glu_ref[:, lo:lo + ch] = av * jax.nn.sigmoid(ag)
        bx_ref[:, lo:lo + ch] = jnp.dot(xn, w_ref[:, 2 * width + lo:2 * width + lo + ch],
                                        preferred_element_type=f32)
        bg = jnp.dot(xn, w_ref[:, 3 * width + lo:3 * width + lo + ch], preferred_element_type=f32)
        gate_ref[:, lo:lo + ch] = jax.nn.gelu(bg)


def _inproj(xa, xb, n_b_rows, ln_g, ln_b, w_in_b, tm):
    na, d = xa.shape
    width = w_in_b.shape[1] // 4
    n_a_tiles = na // tm
    n_tiles = (na + n_b_rows) // tm
    n_b_tiles = max(n_tiles - n_a_tiles, 1)
    row_out = pl.BlockSpec((tm, width), lambda i: (i, 0))
    return pl.pallas_call(
        functools.partial(_inproj_kernel, n_a_tiles=n_a_tiles),
        grid=(n_tiles,),
        in_specs=[
            pl.BlockSpec((tm, d), lambda i: (jnp.minimum(i, n_a_tiles - 1), 0)),
            pl.BlockSpec((tm, d), lambda i: (jnp.clip(i - n_a_tiles, 0, n_b_tiles - 1), 0)),
            pl.BlockSpec((1, d), lambda i: (0, 0)),
            pl.BlockSpec((1, d), lambda i: (0, 0)),
            pl.BlockSpec((d, 4 * width), lambda i: (0, 0), pipeline_mode=pl.Buffered(1)),
        ],
        out_specs=[pl.BlockSpec((tm, d), lambda i: (i, 0)), row_out, row_out, row_out],
        out_shape=[jax.ShapeDtypeStruct((na + n_b_rows, d), f32)]
        + [jax.ShapeDtypeStruct((na + n_b_rows, width), f32)] * 3,
        compiler_params=pltpu.CompilerParams(dimension_semantics=("arbitrary",),
                                             vmem_limit_bytes=VMEM_LIMIT),
        name="inproj",
    )(xa, xb, ln_g, ln_b, w_in_b)


def _mixer_seq_kernel(glu_ref, bx_ref, gate_ref, sa_ref, sb_ref, sh_ref,
                      wa_ref, ba_ref, ng_ref, nb_ref, wb_ref, bb_ref, wg_ref, brg_ref, big_ref, lam_ref,
                      aout_ref, bout_ref, nsa_ref, nsb_ref, nsh_ref,
                      wina, winb, xsa, xsb, ca_s, cb_s, a_s, u_s, hcar):
    tau = pl.program_id(1)
    tt, width = glu_ref.shape
    ka, kb = wa_ref.shape[0], wb_ref.shape[0]
    ha, hb = sa_ref.shape[0], sb_ref.shape[0]

    @pl.when(tau == 0)
    def _():
        wina[0:ha, :] = sa_ref[...]
        winb[0:hb, :] = sb_ref[...]
        hcar[...] = jnp.broadcast_to(sh_ref[...], hcar.shape)

    wina[ha:ha + tt, :] = glu_ref[...]
    winb[hb:hb + tt, :] = bx_ref[...]
    for s in sorted({(ha - (ka - 1) + k) % SUBLANES for k in range(ka)} - {0}):
        xsa[s - 1] = wina[pl.ds(s, tt + ha - SUBLANES), :]
    for s in sorted({(hb - (kb - 1) + k) % SUBLANES for k in range(kb)} - {0}):
        xsb[s - 1] = winb[pl.ds(s, tt + hb - SUBLANES), :]

    def tap(win, xs, off, r0, lanes):
        q, s = divmod(off, SUBLANES)
        src = win if s == 0 else xs.at[s - 1]
        return src[pl.ds(r0 + SUBLANES * q, SUBLANES), lanes]

    rows = min(CONV_ROWS, tt)
    for g in range(width // LANES):
        lanes = pl.ds(g * LANES, LANES)
        bcast = lambda ref, k: jnp.broadcast_to(ref[k:k + 1, lanes], (SUBLANES, LANES))
        wa_k = [bcast(wa_ref, k) for k in range(ka)]
        wb_k = [bcast(wb_ref, k) for k in range(kb)]
        ba_g, bb_g = bcast(ba_ref, 0), bcast(bb_ref, 0)

        def conv_block(rb, carry):
            for j in range(rows // SUBLANES):
                r0 = pl.multiple_of(rb * rows, rows) + j * SUBLANES
                acc = ba_g
                for k in range(ka):
                    acc = acc + wa_k[k] * tap(wina, xsa, ha - (ka - 1) + k, r0, lanes)
                ca_s[pl.ds(r0, SUBLANES), lanes] = acc
                accb = bb_g
                for k in range(kb):
                    accb = accb + wb_k[k] * tap(winb, xsb, hb - (kb - 1) + k, r0, lanes)
                cb_s[pl.ds(r0, SUBLANES), lanes] = accb
            return carry

        lax.fori_loop(0, tt // rows, conv_block, 0)

    a_out = _group_norm_silu(ca_s[...], ng_ref[...], nb_ref[...], width // CONV_HEADS_A)
    aout_ref[...] = a_out.astype(aout_ref.dtype)

    a, u = _lru_gates(cb_s[...], wg_ref, brg_ref[...], big_ref[...], lam_ref[...])
    row8 = lax.broadcasted_iota(i32, (tt, width), 0) & (SUBLANES - 1)
    d = 1
    while d < SUBLANES:
        m = row8 >= d
        a_sh = jnp.where(m, pltpu.roll(a, d, axis=0), 1.0)
        u_sh = jnp.where(m, pltpu.roll(u, d, axis=0), 0.0)
        u = a * u_sh + u
        a = a * a_sh
        d *= 2
    a_s[...] = a
    u_s[...] = u

    def scan_block(j, h):
        r0 = pl.multiple_of(j * SUBLANES, SUBLANES)
        hblk = a_s[pl.ds(r0, SUBLANES), :] * h + u_s[pl.ds(r0, SUBLANES), :]
        u_s[pl.ds(r0, SUBLANES), :] = hblk
        return jnp.broadcast_to(hblk[SUBLANES - 1:SUBLANES, :], h.shape)

    h_last = lax.fori_loop(0, tt // SUBLANES, scan_block, hcar[...])
    hcar[...] = h_last
    bout_ref[...] = (u_s[...] * gate_ref[...]).astype(bout_ref.dtype)

    tail_a = wina[tt:tt + ha, :]
    tail_b = winb[tt:tt + hb, :]
    wina[0:ha, :] = tail_a
    winb[0:hb, :] = tail_b
    nsa_ref[...] = tail_a
    nsb_ref[...] = tail_b
    nsh_ref[...] = h_last[0:1, :]


def _mixer_seq(glu, bx, gate, row_off, sa_pad, sb_pad, sh, mix_w, n_seq, seq_len, tt):
    wa, ba, ng, nb, wb, bb, wg, brg, big, lam = mix_w
    width = glu.shape[1]
    ha, hb = sa_pad.shape[1], sb_pad.shape[1]
    n_t = seq_len // tt
    off = row_off // tt
    row_in = pl.BlockSpec((tt, width), lambda b, t: (off + b * n_t + t, 0))
    row_out = pl.BlockSpec((tt, width), lambda b, t: (b * n_t + t, 0))

    def const(shape):
        return pl.BlockSpec(shape, lambda b, t: (0,) * len(shape))

    def per_seq(r):
        return pl.BlockSpec((None, r, width), lambda b, t: (b, 0, 0))

    return pl.pallas_call(
        _mixer_seq_kernel,
        grid=(n_seq, n_t),
        in_specs=[row_in, row_in, row_in, per_seq(ha), per_seq(hb), per_seq(1),
                  const(wa.shape), const(ba.shape), const(ng.shape), const(nb.shape),
                  const(wb.shape), const(bb.shape), const(wg.shape), const(brg.shape),
                  const(big.shape), const(lam.shape)],
        out_specs=[row_out, row_out, per_seq(ha), per_seq(hb), per_seq(1)],
        out_shape=[jax.ShapeDtypeStruct((n_seq * seq_len, width), bf16),
                   jax.ShapeDtypeStruct((n_seq * seq_len, width), bf16),
                   jax.ShapeDtypeStruct((n_seq, ha, width), f32),
                   jax.ShapeDtypeStruct((n_seq, hb, width), f32),
                   jax.ShapeDtypeStruct((n_seq, 1, width), f32)],
        scratch_shapes=[pltpu.VMEM((ha + tt, width), f32),
                        pltpu.VMEM((hb + tt, width), f32),
                        pltpu.VMEM((SUBLANES - 1, tt + ha - SUBLANES, width), f32),
                        pltpu.VMEM((SUBLANES - 1, tt + hb - SUBLANES, width), f32),
                        pltpu.VMEM((tt, width), f32),
                        pltpu.VMEM((tt, width), f32),
                        pltpu.VMEM((tt, width), f32),
                        pltpu.VMEM((tt, width), f32),
                        pltpu.VMEM((SUBLANES, width), f32)],
        compiler_params=pltpu.CompilerParams(dimension_semantics=("arbitrary", "arbitrary"),
                                             vmem_limit_bytes=VMEM_LIMIT),
        name="mixer_seq",
    )(glu, bx, gate, sa_pad, sb_pad, sh, wa, ba, ng, nb, wb, bb, wg, brg, big, lam)


def _mixer_step_kernel(glu_ref, bx_ref, gate_ref, sa_ref, sb_ref, sh_ref,
                       wat_ref, wa2_ref, wbt_ref, wb2_ref,
                       ba_ref, ng_ref, nb_ref, bb_ref, wg_ref, brg_ref, big_ref, lam_ref,
                       aout_ref, bout_ref, nsa_ref, nsb_ref, nsh_ref,
                       hista, histb, hcar):
    t = pl.program_id(1)
    n_t = hista.shape[0]
    width = glu_ref.shape[1]

    def roll_buffer(s_ref, ns_ref, x_ref):
        shift = n_t - s_ref.shape[0]

        @pl.when(t == 0)
        def _():
            for r in range(max(-shift, 0)):
                ns_ref[r] = s_ref[r + n_t]

        if shift <= 0:
            ns_ref[t - shift] = x_ref[...]
        else:
            @pl.when(t >= shift)
            def _():
                ns_ref[t - shift] = x_ref[...]

    @pl.when(t == 0)
    def _():
        hista[...] = jnp.zeros_like(hista)
        histb[...] = jnp.zeros_like(histb)
        hcar[...] = sh_ref[...]

    roll_buffer(sa_ref, nsa_ref, glu_ref)
    roll_buffer(sb_ref, nsb_ref, bx_ref)
    hista[t] = glu_ref[...]
    histb[t] = bx_ref[...]

    ca = jnp.broadcast_to(ba_ref[...], glu_ref.shape)
    for r in range(sa_ref.shape[0]):
        ca = ca + wat_ref[r:r + 1, :] * sa_ref[r]
    cb = jnp.broadcast_to(bb_ref[...], bx_ref.shape)
    for r in range(sb_ref.shape[0]):
        cb = cb + wbt_ref[r:r + 1, :] * sb_ref[r]
    for j in range(n_t):
        ca = ca + wa2_ref[j:j + 1, :] * hista[j]
        cb = cb + wb2_ref[j:j + 1, :] * histb[j]
    a_out = _group_norm_silu(ca, ng_ref[...], nb_ref[...], width // CONV_HEADS_A)
    aout_ref[...] = a_out.astype(aout_ref.dtype)

    a, u = _lru_gates(cb, wg_ref, brg_ref[...], big_ref[...], lam_ref[...])
    h = a * hcar[...] + u
    hcar[...] = h
    nsh_ref[...] = h
    bout_ref[...] = (h * gate_ref[...]).astype(bout_ref.dtype)


def _mixer_step(glu, bx, gate, row_off, sa, sb, sh, step_w, mix_w, n_seq, n_t, bt):
    wat, wa2, wbt, wb2 = step_w
    wa, ba, ng, nb, wb, bb, wg, brg, big, lam = mix_w
    width = glu.shape[1]
    ka1, kb1 = sa.shape[1], sb.shape[1]
    off = row_off // bt
    nb_tiles = n_seq // bt
    row_in = pl.BlockSpec((bt, width), lambda j, t: (off + t * nb_tiles + j, 0))
    row_out = pl.BlockSpec((bt, width), lambda j, t: (t * nb_tiles + j, 0))

    def const(shape):
        return pl.BlockSpec(shape, lambda j, t: (0,) * len(shape))

    def per_t(arr):
        return pl.BlockSpec((None,) + arr.shape[1:], lambda j, t: (t, 0, 0))

    def state(r):
        return pl.BlockSpec((None, r, bt, width), lambda j, t: (0, 0, j, 0))

    state_h = pl.BlockSpec((None, bt, width), lambda j, t: (0, j, 0))
    return pl.pallas_call(
        _mixer_step_kernel,
        grid=(nb_tiles, n_t),
        in_specs=[row_in, row_in, row_in, state(ka1), state(kb1), state_h,
                  per_t(wat), per_t(wa2), per_t(wbt), per_t(wb2),
                  const(ba.shape), const(ng.shape), const(nb.shape), const(bb.shape),
                  const(wg.shape), const(brg.shape), const(big.shape), const(lam.shape)],
        out_specs=[row_out, row_out, state(ka1), state(kb1), state_h],
        out_shape=[jax.ShapeDtypeStruct((n_seq * n_t, width), bf16),
                   jax.ShapeDtypeStruct((n_seq * n_t, width), bf16),
                   jax.ShapeDtypeStruct((1, ka1, n_seq, width), f32),
                   jax.ShapeDtypeStruct((1, kb1, n_seq, width), f32),
                   jax.ShapeDtypeStruct((1, n_seq, width), f32)],
        scratch_shapes=[pltpu.VMEM((n_t, bt, width), f32),
                        pltpu.VMEM((n_t, bt, width), f32),
                        pltpu.VMEM((bt, width), f32)],
        compiler_params=pltpu.CompilerParams(dimension_semantics=("arbitrary", "arbitrary"),
                                             vmem_limit_bytes=VMEM_LIMIT),
        name="mixer_step",
    )(glu, bx, gate, sa, sb, sh, wat, wa2, wbt, wb2,
      ba, ng, nb, bb, wg, brg, big, lam)


def _outproj_kernel(ap_ref, bp_ref, as_ref, bs_ref, h_ref, wo_ref,
                    l1g_ref, l1b_ref, wrh_ref, wrl_ref, br_ref,
                    h1_ref, h1p_ref, z_ref, *, n_p_tiles, alpha):
    i = pl.program_id(0)
    is_p = i < n_p_tiles
    a = jnp.where(is_p, ap_ref[...], as_ref[...])
    b = jnp.where(is_p, bp_ref[...], bs_ref[...])
    width = a.shape[1]
    mix = (jnp.dot(a, wo_ref[0:width, :], preferred_element_type=f32)
           + jnp.dot(b, wo_ref[width:2 * width, :], preferred_element_type=f32))
    h1 = _layer_norm(alpha * h_ref[...] + mix, l1g_ref[...], l1b_ref[...])
    h1_ref[...] = h1

    xh = h1.astype(bf16)
    xh32 = xh.astype(f32)
    half = h1.shape[1] // 2
    lo = lax.shift_right_logical(lax.bitcast_convert_type(xh32[:, :half], jnp.uint32), jnp.uint32(16))
    hi = lax.bitcast_convert_type(xh32[:, half:], jnp.uint32) & jnp.uint32(0xFFFF0000)
    word = lo | hi
    n_sub = half // LANES
    for s in range(n_sub):
        h1p_ref[pl.ds(s, word.shape[0], stride=n_sub), :] = word[:, s * LANES:(s + 1) * LANES]

    xl = (h1 - xh32).astype(bf16)
    z_ref[...] = (jnp.dot(xh, wrh_ref[...], preferred_element_type=f32)
                  + jnp.dot(xl, wrh_ref[...], preferred_element_type=f32)
                  + jnp.dot(xh, wrl_ref[...], preferred_element_type=f32)) + br_ref[...]


def _route_kernel(z_ref, route_ref, routet_ref, cnt_ref, carry, *, sub):
    i = pl.program_id(0)
    z = z_ref[...]
    lane = lax.broadcasted_iota(i32, z.shape, 1).astype(f32)
    n_g, epg = float(N_GROUPS), float(EXPERTS_PER_GROUP)
    far = float(2 * ROUTE_LANES)

    gm = lane < n_g
    zg = jnp.where(gm, z, NEG_BIG)
    pg = jnp.where(gm, jnp.exp(zg - jnp.max(zg, axis=-1, keepdims=True)), 0.0)
    pg = pg / jnp.sum(pg, axis=-1, keepdims=True)
    g_top = jnp.max(pg, axis=-1, keepdims=True)
    g_idx = jnp.min(jnp.where(gm & (pg == g_top), lane, far), axis=-1, keepdims=True)

    lo = n_g + g_idx * epg
    em = (lane >= lo) & (lane < lo + epg)
    ze = jnp.where(em, z, NEG_BIG)
    pe = jnp.where(em, jnp.exp(ze - jnp.max(ze, axis=-1, keepdims=True)), 0.0)
    pe = pe / jnp.sum(pe, axis=-1, keepdims=True)
    v1 = jnp.max(jnp.where(em, pe, -1.0), axis=-1, keepdims=True)
    l1 = jnp.min(jnp.where(em & (pe == v1), lane, far), axis=-1, keepdims=True)
    pe2 = jnp.where(em & (lane != l1), pe, -1.0)
    v2 = jnp.max(pe2, axis=-1, keepdims=True)
    l2 = jnp.min(jnp.where(pe2 == v2, lane, far), axis=-1, keepdims=True)
    den = v1 + v2
    gate1 = v1 / den * g_top
    gate2 = v2 / den * g_top

    @pl.when(i == 0)
    def _():
        carry[...] = jnp.zeros_like(carry)

    o1 = lane == l1
    o2 = lane == l2
    onehot = jnp.where(o1 | o2, 1.0, 0.0)
    ri = lax.broadcasted_iota(i32, (sub, sub), 0)
    ci = lax.broadcasted_iota(i32, (sub, sub), 1)
    tri = jnp.where(ci < ri, 1.0, 0.0).astype(bf16)
    counts = carry[0:1, :]
    prefix = []
    for s in range(z.shape[0] // sub):
        oh = onehot[s * sub:(s + 1) * sub, :]
        prefix.append(jnp.dot(tri, oh.astype(bf16), preferred_element_type=f32) + counts)
        counts = counts + jnp.sum(oh, axis=0, keepdims=True)
    prefix = jnp.concatenate(prefix, axis=0)
    rank1 = jnp.sum(jnp.where(o1, prefix, 0.0), axis=-1, keepdims=True)
    rank2 = jnp.sum(jnp.where(o2, prefix, 0.0), axis=-1, keepdims=True)
    new_carry = jnp.broadcast_to(counts, carry.shape)
    carry[...] = new_carry
    cnt_ref[...] = new_carry

    route = jnp.where(lane == 0.0, l1 - n_g, 0.0)
    route = jnp.where(lane == 1.0, l2 - n_g, route)
    route = jnp.where(lane == 2.0, rank1, route)
    route = jnp.where(lane == 3.0, rank2, route)
    route = jnp.where(lane == 4.0, gate1, route)
    route = jnp.where(lane == 5.0, gate2, route)
    route_ref[...] = route
    routet_ref[...] = route.T[0:routet_ref.shape[0], :]


def _outproj(a_p, b_p, a_s, b_s, h, w_out_b, ln1_g, ln1_b,
             wr_hi, wr_lo, b_r, alpha, tm):
    n_p, width = a_p.shape
    n_s = a_s.shape[0]
    d = h.shape[1]
    n_p_tiles = n_p // tm
    n_s_tiles = n_s // tm
    n_tiles = n_p_tiles + n_s_tiles

    def p_map(i):
        return (jnp.minimum(i, n_p_tiles - 1), 0)

    def s_map(i):
        return (jnp.clip(i - n_p_tiles, 0, n_s_tiles - 1), 0)

    def const(shape, **kw):
        return pl.BlockSpec(shape, lambda i: (0,) * len(shape), **kw)

    return pl.pallas_call(
        functools.partial(_outproj_kernel, n_p_tiles=n_p_tiles, alpha=alpha),
        grid=(n_tiles,),
        in_specs=[pl.BlockSpec((tm, width), p_map), pl.BlockSpec((tm, width), p_map),
                  pl.BlockSpec((tm, width), s_map), pl.BlockSpec((tm, width), s_map),
                  pl.BlockSpec((tm, d), lambda i: (i, 0)),
                  const((2 * width, d), pipeline_mode=pl.Buffered(1)),
                  const((1, d)), const((1, d)),
                  const((d, ROUTE_LANES)), const((d, ROUTE_LANES)), const((1, ROUTE_LANES))],
        out_specs=[pl.BlockSpec((tm, d), lambda i: (i, 0)),
                   pl.BlockSpec((tm * (d // 2 // LANES), LANES), lambda i: (i, 0)),
                   pl.BlockSpec((tm, ROUTE_LANES), lambda i: (i, 0))],
        out_shape=[jax.ShapeDtypeStruct((n_p + n_s, d), f32),
                   jax.ShapeDtypeStruct(((n_p + n_s) * (d // 2 // LANES), LANES), jnp.uint32),
                   jax.ShapeDtypeStruct((n_p + n_s, ROUTE_LANES), f32)],
        compiler_params=pltpu.CompilerParams(dimension_semantics=("arbitrary",),
                                             vmem_limit_bytes=VMEM_LIMIT),
        name="outproj",
    )(a_p, b_p, a_s, b_s, h, w_out_b, ln1_g, ln1_b, wr_hi, wr_lo, b_r)


def _route_tile(n):
    return max(t for t in range(ROUTE_SUB, ROUTE_MAX_TILE + 1, ROUTE_SUB) if n % t == 0)


def _route(z):
    n = z.shape[0]
    tr = _route_tile(n)
    return pl.pallas_call(
        functools.partial(_route_kernel, sub=ROUTE_SUB),
        grid=(n // tr,),
        in_specs=[pl.BlockSpec((tr, ROUTE_LANES), lambda i: (i, 0))],
        out_specs=[pl.BlockSpec((tr, ROUTE_LANES), lambda i: (i, 0)),
                   pl.BlockSpec((SUBLANES, tr), lambda i: (0, i)),
                   pl.BlockSpec((SUBLANES, ROUTE_LANES), lambda i: (0, 0))],
        out_shape=[jax.ShapeDtypeStruct((n, ROUTE_LANES), f32),
                   jax.ShapeDtypeStruct((SUBLANES, n), f32),
                   jax.ShapeDtypeStruct((SUBLANES, ROUTE_LANES), f32)],
        scratch_shapes=[pltpu.VMEM((SUBLANES, ROUTE_LANES), f32)],
        compiler_params=pltpu.CompilerParams(dimension_semantics=("arbitrary",),
                                             vmem_limit_bytes=VMEM_LIMIT),
        name="route",
    )(z)


def _expert_kernel(src_ref, dst_ref, start_ref, nblk_ref,
                   h1_hbm, wg_ref, wu_ref, wd_ref, y2_hbm,
                   xbuf, ybuf, wgb, wub, wdb, gsem, osem, *, blk, n_rows):
    e = pl.program_id(0)
    n_e = pl.num_programs(0)
    nb = nblk_ref[e]
    g0 = start_ref[e]

    xs = xbuf.shape[1] // blk
    ys = ybuf.shape[1] // blk

    def gather(g, slot):
        for r in range(blk):
            tok = src_ref[g * blk + r]
            pltpu.make_async_copy(h1_hbm.at[pl.ds(pl.multiple_of(tok * xs, xs), xs), :],
                                  xbuf.at[slot, pl.ds(r * xs, xs), :], gsem.at[slot]
                                  ).start(priority=GATHER_PRIORITY)

    def scatter(g, slot):
        for r in range(blk):
            row = dst_ref[(g + 1) * blk + r]
            pltpu.make_async_copy(ybuf.at[slot, pl.ds(r * ys, ys), :],
                                  y2_hbm.at[pl.ds(pl.multiple_of(row * ys, ys), ys), :],
                                  osem.at[slot]).start(priority=r % 2)

    def dump_copy(slot):
        dst = y2_hbm.at[pl.ds((n_rows + slot * blk) * ys, blk * ys), :]
        return pltpu.make_async_copy(ybuf.at[slot], dst, osem.at[slot])

    def wait_gather(slot):
        pltpu.make_async_copy(h1_hbm.at[pl.ds(0, blk * xs), :], xbuf.at[slot], gsem.at[slot]).wait()

    nx, ny = xbuf.shape[0], ybuf.shape[0]
    ahead = nx - 1

    @pl.when(e == 0)
    def _():
        ybuf[...] = jnp.zeros(ybuf.shape, ybuf.dtype)
        for s in range(ny - 1):
            dump_copy(s).start()
        for g in range(ahead):
            gather(g, g)

    @pl.when(nb > 0)
    def _():
        wgb[...] = wg_ref[...].astype(bf16)
        wub[...] = wu_ref[...].astype(bf16)
        wdb[...] = wd_ref[...].astype(bf16)

        def chunk(c, carry):
            g = g0 + c
            slot = g % nx
            yslot = g % ny
            wait_gather(slot)
            dump_copy(yslot).wait()
            gather(g + ahead, (g + ahead) % nx)
            scatter(g - 1, (g + ny - 1) % ny)
            xw = [xbuf[slot, pl.ds(s, blk, stride=xs), :] for s in range(xs)]
            x_lo = [lax.bitcast_convert_type(lax.shift_left(w, jnp.uint32(16)), f32) for w in xw]
            x_hi = [lax.bitcast_convert_type(w & jnp.uint32(0xFFFF0000), f32) for w in xw]
            x = jnp.concatenate(x_lo + x_hi, axis=1).astype(bf16)
            hg = jnp.dot(x, wgb[...], preferred_element_type=f32)
            hu = jnp.dot(x, wub[...], preferred_element_type=f32)
            hid = (hg * jax.nn.sigmoid(hg) * hu).astype(bf16)
            y = jnp.dot(hid, wdb[...], preferred_element_type=f32)
            for s in range(ys):
                ybuf[yslot, pl.ds(s, blk, stride=ys), :] = y[:, s * LANES:(s + 1) * LANES]
            return carry

        lax.fori_loop(0, nb, chunk, 0)

    @pl.when(e == n_e - 1)
    def _():
        g_end = g0 + nb
        scatter(g_end - 1, (g_end + ny - 1) % ny)
        for s in range(ny):
            dump_copy(s).wait()
        for j in range(ahead):
            wait_gather((g_end + j) % nx)


def _experts(h1p, src_rows, dst_rows, pad_start, nblk, w_gate, w_up, w_down, blk):
    n_e, d, d_e = w_gate.shape
    xs, ys = d // 2 // LANES, d // LANES
    n = h1p.shape[0] // xs
    n_rows = TOP_K * n
    grid_spec = pltpu.PrefetchScalarGridSpec(
        num_scalar_prefetch=4,
        grid=(n_e,),
        in_specs=[pl.BlockSpec(memory_space=pl.ANY),
                  pl.BlockSpec((None, d, d_e), lambda e, *_: (e, 0, 0)),
                  pl.BlockSpec((None, d, d_e), lambda e, *_: (e, 0, 0)),
                  pl.BlockSpec((None, d_e, d), lambda e, *_: (e, 0, 0))],
        out_specs=pl.BlockSpec(memory_space=pl.ANY),
        scratch_shapes=[pltpu.VMEM((GATHER_SLOTS, blk * xs, LANES), jnp.uint32),
                        pltpu.VMEM((SCATTER_SLOTS, blk * ys, LANES), f32),
                        pltpu.VMEM((d, d_e), bf16),
                        pltpu.VMEM((d, d_e), bf16),
                        pltpu.VMEM((d_e, d), bf16),
                        pltpu.SemaphoreType.DMA((GATHER_SLOTS,)),
                        pltpu.SemaphoreType.DMA((SCATTER_SLOTS,))],
    )
    return pl.pallas_call(
        functools.partial(_expert_kernel, blk=blk, n_rows=n_rows),
        grid_spec=grid_spec,
        out_shape=jax.ShapeDtypeStruct(((n_rows + SCATTER_SLOTS * blk) * ys, LANES), f32),
        compiler_params=pltpu.CompilerParams(dimension_semantics=("arbitrary",),
                                             vmem_limit_bytes=VMEM_LIMIT),
        name="experts",
    )(src_rows, dst_rows, pad_start, nblk, h1p, w_gate, w_up, w_down)


def _combine_kernel(h1_ref, route_ref, ya_ref, yb_ref, l2g_ref, l2b_ref, yp_ref, ysm_ref,
                    *, n_p_tiles, alpha):
    i = pl.program_id(0)
    route = route_ref[...]
    tm = route.shape[0]
    ys = ya_ref.shape[0] // tm

    def token_rows(ref):
        return jnp.concatenate([ref[pl.ds(s, tm, stride=ys), :] for s in range(ys)], axis=1)

    y = token_rows(ya_ref) * route[:, 4:5] + token_rows(yb_ref) * route[:, 5:6]
    out = _layer_norm(alpha * h1_ref[...] + y, l2g_ref[...], l2b_ref[...])

    @pl.when(i < n_p_tiles)
    def _():
        yp_ref[...] = out

    @pl.when(i >= n_p_tiles)
    def _():
        ysm_ref[...] = out


def _combine(h1, route, y2, ln2_g, ln2_b, n_p, alpha, tm):
    n, d = h1.shape
    n_s = n - n_p
    n_tiles = n // tm
    n_p_tiles = n_p // tm
    n_s_tiles = n_s // tm
    return pl.pallas_call(
        functools.partial(_combine_kernel, n_p_tiles=n_p_tiles, alpha=alpha),
        grid=(n_tiles,),
        in_specs=[pl.BlockSpec((tm, d), lambda i: (i, 0)),
                  pl.BlockSpec((tm, ROUTE_LANES), lambda i: (i, 0)),
                  pl.BlockSpec((tm * (d // LANES), LANES), lambda i: (i, 0)),
                  pl.BlockSpec((tm * (d // LANES), LANES), lambda i: (n_tiles + i, 0)),
                  pl.BlockSpec((1, d), lambda i: (0, 0)),
                  pl.BlockSpec((1, d), lambda i: (0, 0))],
        out_specs=[pl.BlockSpec((tm, d), lambda i: (jnp.minimum(i, n_p_tiles - 1), 0)),
                   pl.BlockSpec((tm, d), lambda i: (jnp.clip(i - n_p_tiles, 0, n_s_tiles - 1), 0))],
        out_shape=[jax.ShapeDtypeStruct((n_p, d), f32), jax.ShapeDtypeStruct((n_s, d), f32)],
        compiler_params=pltpu.CompilerParams(dimension_semantics=("arbitrary",),
                                             vmem_limit_bytes=VMEM_LIMIT),
        name="combine_ln2",
    )(h1, route, y2, y2, ln2_g, ln2_b)


def _shifted_taps(w, n_t, hist):
    k = w.shape[0]
    zero = jnp.zeros_like(w[0])
    wt = jnp.stack([jnp.stack([w[r - t] if 0 <= r - t < k else zero for r in range(hist)])
                    for t in range(n_t)])
    w2 = jnp.stack([jnp.stack([w[hist - t + j] if (j <= t and 0 <= hist - t + j < k) else zero
                               for j in range(n_t)]) for t in range(n_t)])
    return wt, w2


def kernel(x_prompt, x_sample, state_conv_a, state_conv_b, state_lru, meta_tokens, ln_in_g, ln_in_b, w_in, conv_a_w, conv_a_b, norm_a_g, norm_a_b, conv_b_w, conv_b_b, w_rg, b_rg, w_ig, b_ig, lru_lambda, w_out, ln1_g, ln1_b, w_router_group, b_router_group, w_router_expert, b_router_expert, w_gate, w_up, w_down, ln2_g, ln2_b):
    depth = w_in.shape[0]
    assert depth == 1, "single-layer trunk only"
    bp, tp, d = x_prompt.shape
    bs, ts, _ = x_sample.shape
    n_meta = meta_tokens.shape[0]
    width = conv_a_w.shape[2]
    ka, kb = conv_a_w.shape[1], conv_b_w.shape[1]
    n_e = w_gate.shape[1]
    alpha = (2.0 * depth) ** 0.25
    n_p, n_s = bp * tp, bs * ts
    n = n_p + n_s
    assert ts < ka and ts >= kb - 1
    assert n_p % TOKEN_TILE == 0 and n_s % TOKEN_TILE == 0 and tp % SEQ_TILE == 0
    assert bs % STEP_BATCH == 0 and n_p % STEP_BATCH == 0
    assert n_meta % BF16_ROWS == 0 and n_meta <= TOKEN_TILE and n % n_meta == 0
    assert N_GROUPS * (1 + EXPERTS_PER_GROUP) <= ROUTE_LANES and n_e == N_GROUPS * EXPERTS_PER_GROUP

    row = lambda v: v.reshape(1, -1).astype(f32)
    ln_g, ln_b = row(ln_in_g), row(ln_in_b)
    w_in_b = w_in[0].astype(bf16)
    w_out_b = w_out[0].astype(bf16)
    mix_w = (conv_a_w[0], row(conv_a_b[0]), row(norm_a_g[0]), row(norm_a_b[0]),
             conv_b_w[0], row(conv_b_b[0]),
             jnp.concatenate([w_rg[0], w_ig[0]], axis=-1).astype(bf16),
             row(b_rg[0]), row(b_ig[0]), row(lru_lambda[0]))
    ha, hb = _round_up(ka - 1, SUBLANES), _round_up(kb - 1, SUBLANES)

    x_p = x_prompt.reshape(n_p, d)
    x_s = jnp.concatenate([jnp.swapaxes(x_sample, 0, 1).reshape(n_s, d),
                           jnp.pad(meta_tokens, ((0, TOKEN_TILE - n_meta), (0, 0)))], axis=0)
    h, glu, bx, gate = _inproj(x_p, x_s, n_s + TOKEN_TILE, ln_g, ln_b, w_in_b, TOKEN_TILE)

    _, _, sa_m, sb_m, sh_m = _mixer_seq(
        glu, bx, gate, n, jnp.zeros((1, ha, width), f32), jnp.zeros((1, hb, width), f32),
        jnp.zeros((1, 1, width), f32), mix_w, 1, n_meta, n_meta)

    bcast = lambda s: jnp.broadcast_to(s, (bp,) + s.shape[1:])
    a_p, b_p, nsa_p, nsb_p, nsh_p = _mixer_seq(glu, bx, gate, 0, bcast(sa_m), bcast(sb_m), bcast(sh_m),
                                               mix_w, bp, tp, SEQ_TILE)

    row_major = lambda s: jnp.swapaxes(s, 1, 2)
    wat, wa2 = _shifted_taps(conv_a_w[0], ts, ka - 1)
    wbt, wb2 = _shifted_taps(conv_b_w[0], ts, kb - 1)
    a_s, b_s, nsa_s, nsb_s, nsh_s = _mixer_step(
        glu, bx, gate, n_p, row_major(state_conv_a), row_major(state_conv_b), state_lru,
        (wat, wa2, wbt, wb2), mix_w, bs, ts, STEP_BATCH)

    w_r = jnp.concatenate([w_router_group[0], w_router_expert[0]], axis=1)
    w_r = jnp.pad(w_r, ((0, 0), (0, ROUTE_LANES - w_r.shape[1])))
    wr_hi = w_r.astype(bf16)
    wr_lo = (w_r - wr_hi.astype(f32)).astype(bf16)
    b_r = jnp.concatenate([b_router_group[0], b_router_expert[0]])
    b_r = jnp.pad(b_r, (0, ROUTE_LANES - b_r.shape[0])).reshape(1, ROUTE_LANES)
    h1, h1p, logits = _outproj(a_p, b_p, a_s, b_s, h, w_out_b,
                               row(ln1_g[0]), row(ln1_b[0]), wr_hi, wr_lo, b_r, alpha, TOKEN_TILE)
    route, route_t, cnt = _route(logits)

    blk = EXPERT_ROWS
    e_idx = route_t[0:TOP_K].astype(i32)
    rank = route_t[TOP_K:2 * TOP_K].astype(i32)
    counts = cnt[0, N_GROUPS:N_GROUPS + n_e].astype(i32)
    nblk = (counts + blk - 1) // blk
    pad_end = jnp.cumsum(nblk * blk)
    pad_start = pad_end - nblk * blk
    start_of = jnp.sum(jnp.where(e_idx[None] == jnp.arange(n_e, dtype=i32)[:, None, None],
                                 pad_start[:, None, None], 0), axis=0)
    dest = (start_of + rank).reshape(-1)
    p_max = _round_up(TOP_K * n + n_e * (blk - 1), blk)
    flat = jnp.full((p_max,), -1, i32).at[dest].set(jnp.arange(TOP_K * n, dtype=i32))
    pos = jnp.arange(p_max, dtype=i32)
    valid = flat >= 0
    src_rows = jnp.where(valid, flat % n, 0)
    dst_rows = jnp.where(valid, flat, TOP_K * n + pos % (SCATTER_SLOTS * blk))
    src_rows = jnp.concatenate([src_rows, jnp.zeros(((GATHER_SLOTS - 1) * blk,), i32)])
    dst_rows = jnp.concatenate([TOP_K * n + (SCATTER_SLOTS - 1) * blk + jnp.arange(blk, dtype=i32), dst_rows])
    y2 = _experts(h1p, src_rows, dst_rows, (pad_start // blk).astype(i32), nblk,
                  w_gate[0], w_up[0], w_down[0], blk)
    y_p, y_s = _combine(h1, route, y2, row(ln2_g[0]), row(ln2_b[0]), n_p, alpha, TOKEN_TILE)

    y_prompt = y_p.reshape(bp, tp, d)
    y_sample = jnp.swapaxes(y_s.reshape(ts, bs, d), 0, 1)
    return (y_prompt, y_sample,
            nsa_p[:, ha - (ka - 1):][None], nsb_p[:, hb - (kb - 1):][None], nsh_p.reshape(1, bp, width),
            row_major(nsa_s), row_major(nsb_s), nsh_s)
```

```python
import functools

import jax
import jax.numpy as jnp
from jax import lax
from jax.experimental import pallas as pl
from jax.experimental.pallas import tpu as pltpu

f32 = jnp.float32
bf16 = jnp.bfloat16
i32 = jnp.int32

CONV_HEADS_A = 8
LRU_HEADS = 8
LRU_C = 8.0
N_GROUPS = 8
EXPERTS_PER_GROUP = 8
TOP_K = 2
LN_EPS = 1e-5

SUBLANES = 8
LANES = 128
BF16_ROWS = 16
VMEM_LIMIT = 56 * 1024 * 1024

TOKEN_TILE = 256
SEQ_TILE = 256
CONV_ROWS = 64
STEP_BATCH = 32
EXPERT_ROWS = 384
ROUTE_LANES = 128
ROUTE_SUB = 128
ROUTE_MAX_TILE = 2304
GATHER_SLOTS = 3
SCATTER_SLOTS = 3
GATHER_PRIORITY = 1
NEG_BIG = -1e30


def _round_up(x, m):
    return (x + m - 1) // m * m


def _layer_norm(x, g, b):
    mu = jnp.mean(x, axis=-1, keepdims=True)
    xc = x - mu
    var = jnp.mean(xc * xc, axis=-1, keepdims=True)
    return xc * lax.rsqrt(var + LN_EPS) * g + b


def _group_norm_silu(x, g, b, group):
    parts = []
    for c in range(x.shape[1] // group):
        xg = x[:, c * group:(c + 1) * group]
        mu = jnp.mean(xg, axis=-1, keepdims=True)
        xc = xg - mu
        var = jnp.mean(xc * xc, axis=-1, keepdims=True)
        parts.append(xc * lax.rsqrt(var + LN_EPS))
    y = jnp.concatenate(parts, axis=-1) * g + b
    return y * jax.nn.sigmoid(y)


def _lru_gates(cb, wg_ref, brg, big, lam):
    heads, hd, _ = wg_ref.shape
    cbb = cb.astype(bf16)
    rs, iz = [], []
    for h in range(heads):
        z = jnp.dot(cbb[:, h * hd:(h + 1) * hd], wg_ref[h], preferred_element_type=f32)
        rs.append(z[:, :hd])
        iz.append(z[:, hd:])
    r = jax.nn.sigmoid(jnp.concatenate(rs, axis=-1) + brg)
    i = jax.nn.sigmoid(jnp.concatenate(iz, axis=-1) + big)
    log_a = -LRU_C * r * jax.nn.softplus(-lam)
    a = jnp.exp(log_a)
    u = jnp.sqrt(-jnp.tanh(log_a) * (a * a + 1.0)) * (i * cb)
    return a, u


def _inproj_kernel(xa_ref, xb_ref, g_ref, b_ref, w_ref, h_ref, glu_ref, bx_ref, gate_ref, *, n_a_tiles):
    i = pl.program_id(0)
    x = jnp.where(i < n_a_tiles, xa_ref[...], xb_ref[...])
    h = _layer_norm(x, g_ref[...], b_ref[...])
    h_ref[...] = h
    xn = h.astype(bf16)
    width = glu_ref.shape[1]
    ch = min(width, 512)
    for c in range(width // ch):
        lo = c * ch
        av = jnp.dot(xn, w_ref[:, lo:lo + ch], preferred_element_type=f32)
        ag = jnp.dot(xn, w_ref[:, width + lo:width + lo + ch], preferred_element_type=f32)
        glu_ref[:, lo:lo + ch] = av * jax.nn.sigmoid(ag)
        bx_ref[:, lo:lo + ch] = jnp.dot(xn, w_ref[:, 2 * width + lo:2 * width + lo + ch],
                                        preferred_element_type=f32)
        bg = jnp.dot(xn, w_ref[:, 3 * width + lo:3 * width + lo + ch], preferred_element_type=f32)
        gate_ref[:, lo:lo + ch] = jax.nn.gelu(bg)


def _inproj(xa, xb, n_b_rows, ln_g, ln_b, w_in_b, tm):
    na, d = xa.shape
    width = w_in_b.shape[1] // 4
    n_a_tiles = na // tm
    n_tiles = (na + n_b_rows) // tm
    n_b_tiles = max(n_tiles - n_a_tiles, 1)
    row_out = pl.BlockSpec((tm, width), lambda i: (i, 0))
    return pl.pallas_call(
        functools.partial(_inproj_kernel, n_a_tiles=n_a_tiles),
        grid=(n_tiles,),
        in_specs=[
            pl.BlockSpec((tm, d), lambda i: (jnp.minimum(i, n_a_tiles - 1), 0)),
            pl.BlockSpec((tm, d), lambda i: (jnp.clip(i - n_a_tiles, 0, n_b_tiles - 1), 0)),
            pl.BlockSpec((1, d), lambda i: (0, 0)),
            pl.BlockSpec((1, d), lambda i: (0, 0)),
            pl.BlockSpec((d, 4 * width), lambda i: (0, 0), pipeline_mode=pl.Buffered(1)),
        ],
        out_specs=[pl.BlockSpec((tm, d), lambda i: (i, 0)), row_out, row_out, row_out],
        out_shape=[jax.ShapeDtypeStruct((na + n_b_rows, d), f32)]
        + [jax.ShapeDtypeStruct((na + n_b_rows, width), f32)] * 3,
        compiler_params=pltpu.CompilerParams(dimension_semantics=("arbitrary",),
                                             vmem_limit_bytes=VMEM_LIMIT),
        name="inproj",
    )(xa, xb, ln_g, ln_b, w_in_b)


def _mixer_seq_kernel(glu_ref, bx_ref, gate_ref, sa_ref, sb_ref, sh_ref,
                      wa_ref, ba_ref, ng_ref, nb_ref, wb_ref, bb_ref, wg_ref, brg_ref, big_ref, lam_ref,
                      aout_ref, bout_ref, nsa_ref, nsb_ref, nsh_ref,
                      wina, winb, xsa, xsb, ca_s, cb_s, a_s, u_s, hcar):
    tau = pl.program_id(1)
    tt, width = glu_ref.shape
    ka, kb = wa_ref.shape[0], wb_ref.shape[0]
    ha, hb = sa_ref.shape[0], sb_ref.shape[0]

    @pl.when(tau == 0)
    def _():
        wina[0:ha, :] = sa_ref[...]
        winb[0:hb, :] = sb_ref[...]
        hcar[...] = jnp.broadcast_to(sh_ref[...], hcar.shape)

    wina[ha:ha + tt, :] = glu_ref[...]
    winb[hb:hb + tt, :] = bx_ref[...]
    for s in sorted({(ha - (ka - 1) + k) % SUBLANES for k in range(ka)} - {0}):
        xsa[s - 1] = wina[pl.ds(s, tt + ha - SUBLANES), :]
    for s in sorted({(hb - (kb - 1) + k) % SUBLANES for k in range(kb)} - {0}):
        xsb[s - 1] = winb[pl.ds(s, tt + hb - SUBLANES), :]

    def tap(win, xs, off, r0, lanes):
        q, s = divmod(off, SUBLANES)
        src = win if s == 0 else xs.at[s - 1]
        return src[pl.ds(r0 + SUBLANES * q, SUBLANES), lanes]

    rows = min(CONV_ROWS, tt)
    for g in range(width // LANES):
        lanes = pl.ds(g * LANES, LANES)
        bcast = lambda ref, k: jnp.broadcast_to(ref[k:k + 1, lanes], (SUBLANES, LANES))
        wa_k = [bcast(wa_ref, k) for k in range(ka)]
        wb_k = [bcast(wb_ref, k) for k in range(kb)]
        ba_g, bb_g = bcast(ba_ref, 0), bcast(bb_ref, 0)

        def conv_block(rb, carry):
            for j in range(rows // SUBLANES):
                r0 = pl.multiple_of(rb * rows, rows) + j * SUBLANES
                acc = ba_g
                for k in range(ka):
                    acc = acc + wa_k[k] * tap(wina, xsa, ha - (ka - 1) + k, r0, lanes)
                ca_s[pl.ds(r0, SUBLANES), lanes] = acc
                accb = bb_g
                for k in range(kb):
                    accb = accb + wb_k[k] * tap(winb, xsb, hb - (kb - 1) + k, r0, lanes)
                cb_s[pl.ds(r0, SUBLANES), lanes] = accb
            return carry

        lax.fori_loop(0, tt // rows, conv_block, 0)

    a_out = _group_norm_silu(ca_s[...], ng_ref[...], nb_ref[...], width // CONV_HEADS_A)
    aout_ref[...] = a_out.astype(aout_ref.dtype)

    a, u = _lru_gates(cb_s[...], wg_ref, brg_ref[...], big_ref[...], lam_ref[...])
    row8 = lax.broadcasted_iota(i32, (tt, width), 0) & (SUBLANES - 1)
    d = 1
    while d < SUBLANES:
        m = row8 >= d
        a_sh = jnp.where(m, pltpu.roll(a, d, axis=0), 1.0)
        u_sh = jnp.where(m, pltpu.roll(u, d, axis=0), 0.0)
        u = a * u_sh + u
        a = a * a_sh
        d *= 2
    a_s[...] = a
    u_s[...] = u

    def scan_block(j, h):
        r0 = pl.multiple_of(j * SUBLANES, SUBLANES)
        hblk = a_s[pl.ds(r0, SUBLANES), :] * h + u_s[pl.ds(r0, SUBLANES), :]
        u_s[pl.ds(r0, SUBLANES), :] = hblk
        return jnp.broadcast_to(hblk[SUBLANES - 1:SUBLANES, :], h.shape)

    h_last = lax.fori_loop(0, tt // SUBLANES, scan_block, hcar[...])
    hcar[...] = h_last
    bout_ref[...] = (u_s[...] * gate_ref[...]).astype(bout_ref.dtype)

    tail_a = wina[tt:tt + ha, :]
    tail_b = winb[tt:tt + hb, :]
    wina[0:ha, :] = tail_a
    winb[0:hb, :] = tail_b
    nsa_ref[...] = tail_a
    nsb_ref[...] = tail_b
    nsh_ref[...] = h_last[0:1, :]


def _mixer_seq(glu, bx, gate, row_off, sa_pad, sb_pad, sh, mix_w, n_seq, seq_len, tt):
    wa, ba, ng, nb, wb, bb, wg, brg, big, lam = mix_w
    width = glu.shape[1]
    ha, hb = sa_pad.shape[1], sb_pad.shape[1]
    n_t = seq_len // tt
    off = row_off // tt
    row_in = pl.BlockSpec((tt, width), lambda b, t: (off + b * n_t + t, 0))
    row_out = pl.BlockSpec((tt, width), lambda b, t: (b * n_t + t, 0))

    def const(shape):
        return pl.BlockSpec(shape, lambda b, t: (0,) * len(shape))

    def per_seq(r):
        return pl.BlockSpec((None, r, width), lambda b, t: (b, 0, 0))

    return pl.pallas_call(
        _mixer_seq_kernel,
        grid=(n_seq, n_t),
        in_specs=[row_in, row_in, row_in, per_seq(ha), per_seq(hb), per_seq(1),
                  const(wa.shape), const(ba.shape), const(ng.shape), const(nb.shape),
                  const(wb.shape), const(bb.shape), const(wg.shape), const(brg.shape),
                  const(big.shape), const(lam.shape)],
        out_specs=[row_out, row_out, per_seq(ha), per_seq(hb), per_seq(1)],
        out_shape=[jax.ShapeDtypeStruct((n_seq * seq_len, width), bf16),
                   jax.ShapeDtypeStruct((n_seq * seq_len, width), bf16),
                   jax.ShapeDtypeStruct((n_seq, ha, width), f32),
                   jax.ShapeDtypeStruct((n_seq, hb, width), f32),
                   jax.ShapeDtypeStruct((n_seq, 1, width), f32)],
        scratch_shapes=[pltpu.VMEM((ha + tt, width), f32),
                        pltpu.VMEM((hb + tt, width), f32),
                        pltpu.VMEM((SUBLANES - 1, tt + ha - SUBLANES, width), f32),
                        pltpu.VMEM((SUBLANES - 1, tt + hb - SUBLANES, width), f32),
                        pltpu.VMEM((tt, width), f32),
                        pltpu.VMEM((tt, width), f32),
                        pltpu.VMEM((tt, width), f32),
                        pltpu.VMEM((tt, width), f32),
                        pltpu.VMEM((SUBLANES, width), f32)],
        compiler_params=pltpu.CompilerParams(dimension_semantics=("arbitrary", "arbitrary"),
                                             vmem_limit_bytes=VMEM_LIMIT),
        name="mixer_seq",
    )(glu, bx, gate, sa_pad, sb_pad, sh, wa, ba, ng, nb, wb, bb, wg, brg, big, lam)


def _mixer_step_kernel(glu_ref, bx_ref, gate_ref, sa_ref, sb_ref, sh_ref,
                       wat_ref, wa2_ref, wbt_ref, wb2_ref,
                       ba_ref, ng_ref, nb_ref, bb_ref, wg_ref, brg_ref, big_ref, lam_ref,
                       aout_ref, bout_ref, nsa_ref, nsb_ref, nsh_ref,
                       hista, histb, hcar):
    t = pl.program_id(1)
    n_t = hista.shape[0]
    width = glu_ref.shape[1]

    def roll_buffer(s_ref, ns_ref, x_ref):
        shift = n_t - s_ref.shape[0]

        @pl.when(t == 0)
        def _():
            for r in range(max(-shift, 0)):
                ns_ref[r] = s_ref[r + n_t]

        if shift <= 0:
            ns_ref[t - shift] = x_ref[...]
        else:
            @pl.when(t >= shift)
            def _():
                ns_ref[t - shift] = x_ref[...]

    @pl.when(t == 0)
    def _():
        hista[...] = jnp.zeros_like(hista)
        histb[...] = jnp.zeros_like(histb)
        hcar[...] = sh_ref[...]

    roll_buffer(sa_ref, nsa_ref, glu_ref)
    roll_buffer(sb_ref, nsb_ref, bx_ref)
    hista[t] = glu_ref[...]
    histb[t] = bx_ref[...]

    ca = jnp.broadcast_to(ba_ref[...], glu_ref.shape)
    for r in range(sa_ref.shape[0]):
        ca = ca + wat_ref[r:r + 1, :] * sa_ref[r]
    cb = jnp.broadcast_to(bb_ref[...], bx_ref.shape)
    for r in range(sb_ref.shape[0]):
        cb = cb + wbt_ref[r:r + 1, :] * sb_ref[r]
    for j in range(n_t):
        ca = ca + wa2_ref[j:j + 1, :] * hista[j]
        cb = cb + wb2_ref[j:j + 1, :] * histb[j]
    a_out = _group_norm_silu(ca, ng_ref[...], nb_ref[...], width // CONV_HEADS_A)
    aout_ref[...] = a_out.astype(aout_ref.dtype)

    a, u = _lru_gates(cb, wg_ref, brg_ref[...], big_ref[...], lam_ref[...])
    h = a * hcar[...] + u
    hcar[...] = h
    nsh_ref[...] = h
    bout_ref[...] = (h * gate_ref[...]).astype(bout_ref.dtype)


def _mixer_step(glu, bx, gate, row_off, sa, sb, sh, step_w, mix_w, n_seq, n_t, bt):
    wat, wa2, wbt, wb2 = step_w
    wa, ba, ng, nb, wb, bb, wg, brg, big, lam = mix_w
    width = glu.shape[1]
    ka1, kb1 = sa.shape[1], sb.shape[1]
    off = row_off // bt
    nb_tiles = n_seq // bt
    row_in = pl.BlockSpec((bt, width), lambda j, t: (off + t * nb_tiles + j, 0))
    row_out = pl.BlockSpec((bt, width), lambda j, t: (t * nb_tiles + j, 0))

    def const(shape):
        return pl.BlockSpec(shape, lambda j, t: (0,) * len(shape))

    def per_t(arr):
        return pl.BlockSpec((None,) + arr.shape[1:], lambda j, t: (t, 0, 0))

    def state(r):
        return pl.BlockSpec((None, r, bt, width), lambda j, t: (0, 0, j, 0))

    state_h = pl.BlockSpec((None, bt, width), lambda j, t: (0, j, 0))
    return pl.pallas_call(
        _mixer_step_kernel,
        grid=(nb_tiles, n_t),
        in_specs=[row_in, row_in, row_in, state(ka1), state(kb1), state_h,
                  per_t(wat), per_t(wa2), per_t(wbt), per_t(wb2),
                  const(ba.shape), const(ng.shape), const(nb.shape), const(bb.shape),
                  const(wg.shape), const(brg.shape), const(big.shape), const(lam.shape)],
        out_specs=[row_out, row_out, state(ka1), state(kb1), state_h],
        out_shape=[jax.ShapeDtypeStruct((n_seq * n_t, width), bf16),
                   jax.ShapeDtypeStruct((n_seq * n_t, width), bf16),
                   jax.ShapeDtypeStruct((1, ka1, n_seq, width), f32),
                   jax.ShapeDtypeStruct((1, kb1, n_seq, width), f32),
                   jax.ShapeDtypeStruct((1, n_seq, width), f32)],
        scratch_shapes=[pltpu.VMEM((n_t, bt, width), f32),
                        pltpu.VMEM((n_t, bt, width), f32),
                        pltpu.VMEM((bt, width), f32)],
        compiler_params=pltpu.CompilerParams(dimension_semantics=("arbitrary", "arbitrary"),
                                             vmem_limit_bytes=VMEM_LIMIT),
        name="mixer_step",
    )(glu, bx, gate, sa, sb, sh, wat, wa2, wbt, wb2,
      ba, ng, nb, bb, wg, brg, big, lam)


def _outproj_kernel(ap_ref, bp_ref, as_ref, bs_ref, h_ref, wo_ref,
                    l1g_ref, l1b_ref, wrh_ref, wrl_ref, br_ref,
                    h1_ref, h1p_ref, z_ref, *, n_p_tiles, alpha):
    i = pl.program_id(0)
    is_p = i < n_p_tiles
    a = jnp.where(is_p, ap_ref[...], as_ref[...])
    b = jnp.where(is_p, bp_ref[...], bs_ref[...])
    width = a.shape[1]
    mix = (jnp.dot(a, wo_ref[0:width, :], preferred_element_type=f32)
           + jnp.dot(b, wo_ref[width:2 * width, :], preferred_element_type=f32))
    h1 = _layer_norm(alpha * h_ref[...] + mix, l1g_ref[...], l1b_ref[...])
    h1_ref[...] = h1

    xh = h1.astype(bf16)
    xh32 = xh.astype(f32)
    half = h1.shape[1] // 2
    lo = lax.shift_right_logical(lax.bitcast_convert_type(xh32[:, :half], jnp.uint32), jnp.uint32(16))
    hi = lax.bitcast_convert_type(xh32[:, half:], jnp.uint32) & jnp.uint32(0xFFFF0000)
    word = lo | hi
    n_sub = half // LANES
    for s in range(n_sub):
        h1p_ref[pl.ds(s, word.shape[0], stride=n_sub), :] = word[:, s * LANES:(s + 1) * LANES]

    xl = (h1 - xh32).astype(bf16)
    z_ref[...] = (jnp.dot(xh, wrh_ref[...], preferred_element_type=f32)
                  + jnp.dot(xl, wrh_ref[...], preferred_element_type=f32)
                  + jnp.dot(xh, wrl_ref[...], preferred_element_type=f32)) + br_ref[...]


def _route_kernel(z_ref, route_ref, routet_ref, cnt_ref, carry, *, sub):
    i = pl.program_id(0)
    z = z_ref[...]
    lane = lax.broadcasted_iota(i32, z.shape, 1).astype(f32)
    n_g, epg = float(N_GROUPS), float(EXPERTS_PER_GROUP)
    far = float(2 * ROUTE_LANES)

    gm = lane < n_g
    zg = jnp.where(gm, z, NEG_BIG)
    pg = jnp.where(gm, jnp.exp(zg - jnp.max(zg, axis=-1, keepdims=True)), 0.0)
    pg = pg / jnp.sum(pg, axis=-1, keepdims=True)
    g_top = jnp.max(pg, axis=-1, keepdims=True)
    g_idx = jnp.min(jnp.where(gm & (pg == g_top), lane, far), axis=-1, keepdims=True)

    lo = n_g + g_idx * epg
    em = (lane >= lo) & (lane < lo + epg)
    ze = jnp.where(em, z, NEG_BIG)
    pe = jnp.where(em, jnp.exp(ze - jnp.max(ze, axis=-1, keepdims=True)), 0.0)
    pe = pe / jnp.sum(pe, axis=-1, keepdims=True)
    v1 = jnp.max(jnp.where(em, pe, -1.0), axis=-1, keepdims=True)
    l1 = jnp.min(jnp.where(em & (pe == v1), lane, far), axis=-1, keepdims=True)
    pe2 = jnp.where(em & (lane != l1), pe, -1.0)
    v2 = jnp.max(pe2, axis=-1, keepdims=True)
    l2 = jnp.min(jnp.where(pe2 == v2, lane, far), axis=-1, keepdims=True)
    den = v1 + v2
    gate1 = v1 / den * g_top
    gate2 = v2 / den * g_top

    @pl.when(i == 0)
    def _():
        carry[...] = jnp.zeros_like(carry)

    o1 = lane == l1
    o2 = lane == l2
    onehot = jnp.where(o1 | o2, 1.0, 0.0)
    ri = lax.broadcasted_iota(i32, (sub, sub), 0)
    ci = lax.broadcasted_iota(i32, (sub, sub), 1)
    tri = jnp.where(ci < ri, 1.0, 0.0).astype(bf16)
    counts = carry[0:1, :]
    prefix = []
    for s in range(z.shape[0] // sub):
        oh = onehot[s * sub:(s + 1) * sub, :]
        prefix.append(jnp.dot(tri, oh.astype(bf16), preferred_element_type=f32) + counts)
        counts = counts + jnp.sum(oh, axis=0, keepdims=True)
    prefix = jnp.concatenate(prefix, axis=0)
    rank1 = jnp.sum(jnp.where(o1, prefix, 0.0), axis=-1, keepdims=True)
    rank2 = jnp.sum(jnp.where(o2, prefix, 0.0), axis=-1, keepdims=True)
    new_carry = jnp.broadcast_to(counts, carry.shape)
    carry[...] = new_carry
    cnt_ref[...] = new_carry

    route = jnp.where(lane == 0.0, l1 - n_g, 0.0)
    route = jnp.where(lane == 1.0, l2 - n_g, route)
    route = jnp.where(lane == 2.0, rank1, route)
    route = jnp.where(lane == 3.0, rank2, route)
    route = jnp.where(lane == 4.0, gate1, route)
    route = jnp.where(lane == 5.0, gate2, route)
    route_ref[...] = route
    routet_ref[...] = route.T[0:routet_ref.shape[0], :]


def _outproj(a_p, b_p, a_s, b_s, h, w_out_b, ln1_g, ln1_b,
             wr_hi, wr_lo, b_r, alpha, tm):
    n_p, width = a_p.shape
    n_s = a_s.shape[0]
    d = h.shape[1]
    n_p_tiles = n_p // tm
    n_s_tiles = n_s // tm
    n_tiles = n_p_tiles + n_s_tiles

    def p_map(i):
        return (jnp.minimum(i, n_p_tiles - 1), 0)

    def s_map(i):
        return (jnp.clip(i - n_p_tiles, 0, n_s_tiles - 1), 0)

    def const(shape, **kw):
        return pl.BlockSpec(shape, lambda i: (0,) * len(shape), **kw)

    return pl.pallas_call(
        functools.partial(_outproj_kernel, n_p_tiles=n_p_tiles, alpha=alpha),
        grid=(n_tiles,),
        in_specs=[pl.BlockSpec((tm, width), p_map), pl.BlockSpec((tm, width), p_map),
                  pl.BlockSpec((tm, width), s_map), pl.BlockSpec((tm, width), s_map),
                  pl.BlockSpec((tm, d), lambda i: (i, 0)),
                  const((2 * width, d), pipeline_mode=pl.Buffered(1)),
                  const((1, d)), const((1, d)),
                  const((d, ROUTE_LANES)), const((d, ROUTE_LANES)), const((1, ROUTE_LANES))],
        out_specs=[pl.BlockSpec((tm, d), lambda i: (i, 0)),
                   pl.BlockSpec((tm * (d // 2 // LANES), LANES), lambda i: (i, 0)),
                   pl.BlockSpec((tm, ROUTE_LANES), lambda i: (i, 0))],
        out_shape=[jax.ShapeDtypeStruct((n_p + n_s, d), f32),
                   jax.ShapeDtypeStruct(((n_p + n_s) * (d // 2 // LANES), LANES), jnp.uint32),
                   jax.ShapeDtypeStruct((n_p + n_s, ROUTE_LANES), f32)],
        compiler_params=pltpu.CompilerParams(dimension_semantics=("arbitrary",),
                                             vmem_limit_bytes=VMEM_LIMIT),
        name="outproj",
    )(a_p, b_p, a_s, b_s, h, w_out_b, ln1_g, ln1_b, wr_hi, wr_lo, b_r)


def _route_tile(n):
    return max(t for t in range(ROUTE_SUB, ROUTE_MAX_TILE + 1, ROUTE_SUB) if n % t == 0)


def _route(z):
    n = z.shape[0]
    tr = _route_tile(n)
    return pl.pallas_call(
        functools.partial(_route_kernel, sub=ROUTE_SUB),
        grid=(n // tr,),
        in_specs=[pl.BlockSpec((tr, ROUTE_LANES), lambda i: (i, 0))],
        out_specs=[pl.BlockSpec((tr, ROUTE_LANES), lambda i: (i, 0)),
                   pl.BlockSpec((SUBLANES, tr), lambda i: (0, i)),
                   pl.BlockSpec((SUBLANES, ROUTE_LANES), lambda i: (0, 0))],
        out_shape=[jax.ShapeDtypeStruct((n, ROUTE_LANES), f32),
                   jax.ShapeDtypeStruct((SUBLANES, n), f32),
                   jax.ShapeDtypeStruct((SUBLANES, ROUTE_LANES), f32)],
        scratch_shapes=[pltpu.VMEM((SUBLANES, ROUTE_LANES), f32)],
        compiler_params=pltpu.CompilerParams(dimension_semantics=("arbitrary",),
                                             vmem_limit_bytes=VMEM_LIMIT),
        name="route",
    )(z)


def _expert_kernel(src_ref, dst_ref, start_ref, nblk_ref,
                   h1_hbm, wg_ref, wu_ref, wd_ref, y2_hbm,
                   xbuf, ybuf, wgb, wub, wdb, gsem, osem, *, blk, n_rows):
    e = pl.program_id(0)
    n_e = pl.num_programs(0)
    nb = nblk_ref[e]
    g0 = start_ref[e]

    xs = xbuf.shape[1] // blk
    ys = ybuf.shape[1] // blk

    def gather(g, slot):
        for r in range(blk):
            tok = src_ref[g * blk + r]
            pltpu.make_async_copy(h1_hbm.at[pl.ds(pl.multiple_of(tok * xs, xs), xs), :],
                                  xbuf.at[slot, pl.ds(r * xs, xs), :], gsem.at[slot]
                                  ).start(priority=GATHER_PRIORITY)

    def scatter(g, slot):
        for r in range(blk):
            row = dst_ref[(g + 1) * blk + r]
            pltpu.make_async_copy(ybuf.at[slot, pl.ds(r * ys, ys), :],
                                  y2_hbm.at[pl.ds(pl.multiple_of(row * ys, ys), ys), :],
                                  osem.at[slot]).start(priority=r % 2)

    def dump_copy(slot):
        dst = y2_hbm.at[pl.ds((n_rows + slot * blk) * ys, blk * ys), :]
        return pltpu.make_async_copy(ybuf.at[slot], dst, osem.at[slot])

    def wait_gather(slot):
        pltpu.make_async_copy(h1_hbm.at[pl.ds(0, blk * xs), :], xbuf.at[slot], gsem.at[slot]).wait()

    nx, ny = xbuf.shape[0], ybuf.shape[0]
    ahead = nx - 1

    @pl.when(e == 0)
    def _():
        ybuf[...] = jnp.zeros(ybuf.shape, ybuf.dtype)
        for s in range(ny - 1):
            dump_copy(s).start()
        for g in range(ahead):
            gather(g, g)

    @pl.when(nb > 0)
    def _():
        wgb[...] = wg_ref[...].astype(bf16)
        wub[...] = wu_ref[...].astype(bf16)
        wdb[...] = wd_ref[...].astype(bf16)

        def chunk(c, carry):
            g = g0 + c
            slot = g % nx
            yslot = g % ny
            wait_gather(slot)
            dump_copy(yslot).wait()
            gather(g + ahead, (g + ahead) % nx)
            scatter(g - 1, (g + ny - 1) % ny)
            xw = [xbuf[slot, pl.ds(s, blk, stride=xs), :] for s in range(xs)]
            x_lo = [lax.bitcast_convert_type(lax.shift_left(w, jnp.uint32(16)), f32) for w in xw]
            x_hi = [lax.bitcast_convert_type(w & jnp.uint32(0xFFFF0000), f32) for w in xw]
            x = jnp.concatenate(x_lo + x_hi, axis=1).astype(bf16)
            hg = jnp.dot(x, wgb[...], preferred_element_type=f32)
            hu = jnp.dot(x, wub[...], preferred_element_type=f32)
            hid = (hg * jax.nn.sigmoid(hg) * hu).astype(bf16)
            y = jnp.dot(hid, wdb[...], preferred_element_type=f32)
            for s in range(ys):
                ybuf[yslot, pl.ds(s, blk, stride=ys), :] = y[:, s * LANES:(s + 1) * LANES]
            return carry

        lax.fori_loop(0, nb, chunk, 0)

    @pl.when(e == n_e - 1)
    def _():
        g_end = g0 + nb
        scatter(g_end - 1, (g_end + ny - 1) % ny)
        for s in range(ny):
            dump_copy(s).wait()
        for j in range(ahead):
            wait_gather((g_end + j) % nx)


def _experts(h1p, src_rows, dst_rows, pad_start, nblk, w_gate, w_up, w_down, blk):
    n_e, d, d_e = w_gate.shape
    xs, ys = d // 2 // LANES, d // LANES
    n = h1p.shape[0] // xs
    n_rows = TOP_K * n
    grid_spec = pltpu.PrefetchScalarGridSpec(
        num_scalar_prefetch=4,
        grid=(n_e,),
        in_specs=[pl.BlockSpec(memory_space=pl.ANY),
                  pl.BlockSpec((None, d, d_e), lambda e, *_: (e, 0, 0)),
                  pl.BlockSpec((None, d, d_e), lambda e, *_: (e, 0, 0)),
                  pl.BlockSpec((None, d_e, d), lambda e, *_: (e, 0, 0))],
        out_specs=pl.BlockSpec(memory_space=pl.ANY),
        scratch_shapes=[pltpu.VMEM((GATHER_SLOTS, blk * xs, LANES), jnp.uint32),
                        pltpu.VMEM((SCATTER_SLOTS, blk * ys, LANES), f32),
                        pltpu.VMEM((d, d_e), bf16),
                        pltpu.VMEM((d, d_e), bf16),
                        pltpu.VMEM((d_e, d), bf16),
                        pltpu.SemaphoreType.DMA((GATHER_SLOTS,)),
                        pltpu.SemaphoreType.DMA((SCATTER_SLOTS,))],
    )
    return pl.pallas_call(
        functools.partial(_expert_kernel, blk=blk, n_rows=n_rows),
        grid_spec=grid_spec,
        out_shape=jax.ShapeDtypeStruct(((n_rows + SCATTER_SLOTS * blk) * ys, LANES), f32),
        compiler_params=pltpu.CompilerParams(dimension_semantics=("arbitrary",),
                                             vmem_limit_bytes=VMEM_LIMIT),
        name="experts",
    )(src_rows, dst_rows, pad_start, nblk, h1p, w_gate, w_up, w_down)


def _combine_kernel(h1_ref, route_ref, ya_ref, yb_ref, l2g_ref, l2b_ref, yp_ref, ysm_ref,
                    *, n_p_tiles, alpha):
    i = pl.program_id(0)
    route = route_ref[...]
    tm = route.shape[0]
    ys = ya_ref.shape[0] // tm

    def token_rows(ref):
        return jnp.concatenate([ref[pl.ds(s, tm, stride=ys), :] for s in range(ys)], axis=1)

    y = token_rows(ya_ref) * route[:, 4:5] + token_rows(yb_ref) * route[:, 5:6]
    out = _layer_norm(alpha * h1_ref[...] + y, l2g_ref[...], l2b_ref[...])

    @pl.when(i < n_p_tiles)
    def _():
        yp_ref[...] = out

    @pl.when(i >= n_p_tiles)
    def _():
        ysm_ref[...] = out


def _combine(h1, route, y2, ln2_g, ln2_b, n_p, alpha, tm):
    n, d = h1.shape
    n_s = n - n_p
    n_tiles = n // tm
    n_p_tiles = n_p // tm
    n_s_tiles = n_s // tm
    return pl.pallas_call(
        functools.partial(_combine_kernel, n_p_tiles=n_p_tiles, alpha=alpha),
        grid=(n_tiles,),
        in_specs=[pl.BlockSpec((tm, d), lambda i: (i, 0)),
                  pl.BlockSpec((tm, ROUTE_LANES), lambda i: (i, 0)),
                  pl.BlockSpec((tm * (d // LANES), LANES), lambda i: (i, 0)),
                  pl.BlockSpec((tm * (d // LANES), LANES), lambda i: (n_tiles + i, 0)),
                  pl.BlockSpec((1, d), lambda i: (0, 0)),
                  pl.BlockSpec((1, d), lambda i: (0, 0))],
        out_specs=[pl.BlockSpec((tm, d), lambda i: (jnp.minimum(i, n_p_tiles - 1), 0)),
                   pl.BlockSpec((tm, d), lambda i: (jnp.clip(i - n_p_tiles, 0, n_s_tiles - 1), 0))],
        out_shape=[jax.ShapeDtypeStruct((n_p, d), f32), jax.ShapeDtypeStruct((n_s, d), f32)],
        compiler_params=pltpu.CompilerParams(dimension_semantics=("arbitrary",),
                                             vmem_limit_bytes=VMEM_LIMIT),
        name="combine_ln2",
    )(h1, route, y2, y2, ln2_g, ln2_b)


def _shifted_taps(w, n_t, hist):
    k = w.shape[0]
    zero = jnp.zeros_like(w[0])
    wt = jnp.stack([jnp.stack([w[r - t] if 0 <= r - t < k else zero for r in range(hist)])
                    for t in range(n_t)])
    w2 = jnp.stack([jnp.stack([w[hist - t + j] if (j <= t and 0 <= hist - t + j < k) else zero
                               for j in range(n_t)]) for t in range(n_t)])
    return wt, w2


def kernel(x_prompt, x_sample, state_conv_a, state_conv_b, state_lru, meta_tokens, ln_in_g, ln_in_b, w_in, conv_a_w, conv_a_b, norm_a_g, norm_a_b, conv_b_w, conv_b_b, w_rg, b_rg, w_ig, b_ig, lru_lambda, w_out, ln1_g, ln1_b, w_router_group, b_router_group, w_router_expert, b_router_expert, w_gate, w_up, w_down, ln2_g, ln2_b):
    depth = w_in.shape[0]
    assert depth == 1, "single-layer trunk only"
    bp, tp, d = x_prompt.shape
    bs, ts, _ = x_sample.shape
    n_meta = meta_tokens.shape[0]
    width = conv_a_w.shape[2]
    ka, kb = conv_a_w.shape[1], conv_b_w.shape[1]
    n_e = w_gate.shape[1]
    alpha = (2.0 * depth) ** 0.25
    n_p, n_s = bp * tp, bs * ts
    n = n_p + n_s
    assert ts < ka and ts >= kb - 1
    assert n_p % TOKEN_TILE == 0 and n_s % TOKEN_TILE == 0 and tp % SEQ_TILE == 0
    assert bs % STEP_BATCH == 0 and n_p % STEP_BATCH == 0
    assert n_meta % BF16_ROWS == 0 and n_meta <= TOKEN_TILE and n % n_meta == 0
    assert N_GROUPS * (1 + EXPERTS_PER_GROUP) <= ROUTE_LANES and n_e == N_GROUPS * EXPERTS_PER_GROUP

    row = lambda v: v.reshape(1, -1).astype(f32)
    ln_g, ln_b = row(ln_in_g), row(ln_in_b)
    w_in_b = w_in[0].astype(bf16)
    w_out_b = w_out[0].astype(bf16)
    mix_w = (conv_a_w[0], row(conv_a_b[0]), row(norm_a_g[0]), row(norm_a_b[0]),
             conv_b_w[0], row(conv_b_b[0]),
             jnp.concatenate([w_rg[0], w_ig[0]], axis=-1).astype(bf16),
             row(b_rg[0]), row(b_ig[0]), row(lru_lambda[0]))
    ha, hb = _round_up(ka - 1, SUBLANES), _round_up(kb - 1, SUBLANES)

    x_p = x_prompt.reshape(n_p, d)
    x_s = jnp.concatenate([jnp.swapaxes(x_sample, 0, 1).reshape(n_s, d),
                           jnp.pad(meta_tokens, ((0, TOKEN_TILE - n_meta), (0, 0)))], axis=0)
    h, glu, bx, gate = _inproj(x_p, x_s, n_s + TOKEN_TILE, ln_g, ln_b, w_in_b, TOKEN_TILE)

    _, _, sa_m, sb_m, sh_m = _mixer_seq(
        glu, bx, gate, n, jnp.zeros((1, ha, width), f32), jnp.zeros((1, hb, width), f32),
        jnp.zeros((1, 1, width), f32), mix_w, 1, n_meta, n_meta)

    bcast = lambda s: jnp.broadcast_to(s, (bp,) + s.shape[1:])
    a_p, b_p, nsa_p, nsb_p, nsh_p = _mixer_seq(glu, bx, gate, 0, bcast(sa_m), bcast(sb_m), bcast(sh_m),
                                               mix_w, bp, tp, SEQ_TILE)

    row_major = lambda s: jnp.swapaxes(s, 1, 2)
    wat, wa2 = _shifted_taps(conv_a_w[0], ts, ka - 1)
    wbt, wb2 = _shifted_taps(conv_b_w[0], ts, kb - 1)
    a_s, b_s, nsa_s, nsb_s, nsh_s = _mixer_step(
        glu, bx, gate, n_p, row_major(state_conv_a), row_major(state_conv_b), state_lru,
        (wat, wa2, wbt, wb2), mix_w, bs, ts, STEP_BATCH)

    w_r = jnp.concatenate([w_router_group[0], w_router_expert[0]], axis=1)
    w_r = jnp.pad(w_r, ((0, 0), (0, ROUTE_LANES - w_r.shape[1])))
    wr_hi = w_r.astype(bf16)
    wr_lo = (w_r - wr_hi.astype(f32)).astype(bf16)
    b_r = jnp.concatenate([b_router_group[0], b_router_expert[0]])
    b_r = jnp.pad(b_r, (0, ROUTE_LANES - b_r.shape[0])).reshape(1, ROUTE_LANES)
    h1, h1p, logits = _outproj(a_p, b_p, a_s, b_s, h, w_out_b,
                               row(ln1_g[0]), row(ln1_b[0]), wr_hi, wr_lo, b_r, alpha, TOKEN_TILE)
    route, route_t, cnt = _route(logits)

    blk = EXPERT_ROWS
    e_idx = route_t[0:TOP_K].astype(i32)
    rank = route_t[TOP_K:2 * TOP_K].astype(i32)
    counts = cnt[0, N_GROUPS:N_GROUPS + n_e].astype(i32)
    nblk = (counts + blk - 1) // blk
    pad_end = jnp.cumsum(nblk * blk)
    pad_start = pad_end - nblk * blk
    start_of = jnp.sum(jnp.where(e_idx[None] == jnp.arange(n_e, dtype=i32)[:, None, None],
                                 pad_start[:, None, None], 0), axis=0)
    dest = (start_of + rank).reshape(-1)
    p_max = _round_up(TOP_K * n + n_e * (blk - 1), blk)
    flat = jnp.full((p_max,), -1, i32).at[dest].set(jnp.arange(TOP_K * n, dtype=i32))
    pos = jnp.arange(p_max, dtype=i32)
    valid = flat >= 0
    src_rows = jnp.where(valid, flat % n, 0)
    dst_rows = jnp.where(valid, flat, TOP_K * n + pos % (SCATTER_SLOTS * blk))
    src_rows = jnp.concatenate([src_rows, jnp.zeros(((GATHER_SLOTS - 1) * blk,), i32)])
    dst_rows = jnp.concatenate([TOP_K * n + (SCATTER_SLOTS - 1) * blk + jnp.arange(blk, dtype=i32), dst_rows])
    y2 = _experts(h1p, src_rows, dst_rows, (pad_start // blk).astype(i32), nblk,
                  w_gate[0], w_up[0], w_down[0], blk)
    y_p, y_s = _combine(h1, route, y2, row(ln2_g[0]), row(ln2_b[0]), n_p, alpha, TOKEN_TILE)

    y_prompt = y_p.reshape(bp, tp, d)
    y_sample = jnp.swapaxes(y_s.reshape(ts, bs, d), 0, 1)
    return (y_prompt, y_sample,
            nsa_p[:, ha - (ka - 1):][None], nsb_p[:, hb - (kb - 1):][None], nsh_p.reshape(1, bp, width),
            row_major(nsa_s), row_major(nsb_s), nsh_s)
```

```python
import functools

import jax
import jax.numpy as jnp
from jax import lax
from jax.experimental import pallas as pl
from jax.experimental.pallas import tpu as pltpu

f32 = jnp.float32
bf16 = jnp.bfloat16
i32 = jnp.int32

CONV_HEADS_A = 8
LRU_HEADS = 8
LRU_C = 8.0
N_GROUPS = 8
EXPERTS_PER_GROUP = 8
TOP_K = 2
LN_EPS = 1e-5

SUBLANES = 8
LANES = 128
BF16_ROWS = 16
VMEM_LIMIT = 56 * 1024 * 1024

TOKEN_TILE = 256
SEQ_TILE = 256
CONV_ROWS = 64
STEP_BATCH = 32
EXPERT_ROWS = 128
ROUTE_LANES = 128
ROUTE_SUB = 128
ROUTE_MAX_TILE = 2304
GATHER_SLOTS = 4
SCATTER_SLOTS = 4
NEG_BIG = -1e30


def _round_up(x, m):
    return (x + m - 1) // m * m


def _layer_norm(x, g, b):
    mu = jnp.mean(x, axis=-1, keepdims=True)
    xc = x - mu
    var = jnp.mean(xc * xc, axis=-1, keepdims=True)
    return xc * lax.rsqrt(var + LN_EPS) * g + b


def _group_norm_silu(x, g, b, group):
    parts = []
    for c in range(x.shape[1] // group):
        xg = x[:, c * group:(c + 1) * group]
        mu = jnp.mean(xg, axis=-1, keepdims=True)
        xc = xg - mu
        var = jnp.mean(xc * xc, axis=-1, keepdims=True)
        parts.append(xc * lax.rsqrt(var + LN_EPS))
    y = jnp.concatenate(parts, axis=-1) * g + b
    return y * jax.nn.sigmoid(y)


def _lru_gates(cb, wg_ref, brg, big, lam):
    heads, hd, _ = wg_ref.shape
    cbb = cb.astype(bf16)
    rs, iz = [], []
    for h in range(heads):
        z = jnp.dot(cbb[:, h * hd:(h + 1) * hd], wg_ref[h], preferred_element_type=f32)
        rs.append(z[:, :hd])
        iz.append(z[:, hd:])
    r = jax.nn.sigmoid(jnp.concatenate(rs, axis=-1) + brg)
    i = jax.nn.sigmoid(jnp.concatenate(iz, axis=-1) + big)
    log_a = -LRU_C * r * jax.nn.softplus(-lam)
    a = jnp.exp(log_a)
    u = jnp.sqrt(-jnp.tanh(log_a) * (a * a + 1.0)) * (i * cb)
    return a, u


def _inproj_kernel(xa_ref, xb_ref, g_ref, b_ref, w_ref, h_ref, glu_ref, bx_ref, gate_ref, *, n_a_tiles):
    i = pl.program_id(0)
    x = jnp.where(i < n_a_tiles, xa_ref[...], xb_ref[...])
    h = _layer_norm(x, g_ref[...], b_ref[...])
    h_ref[...] = h
    xn = h.astype(bf16)
    width = glu_ref.shape[1]
    ch = min(width, 512)
    for c in range(width // ch):
        lo = c * ch
        av = jnp.dot(xn, w_ref[:, lo:lo + ch], preferred_element_type=f32)
        ag = jnp.dot(xn, w_ref[:, width + lo:width + lo + ch], preferred_element_type=f32)
        glu_ref[:, lo:lo + ch] = av * jax.nn.sigmoid(ag)
        bx_ref[:, lo:lo + ch] = jnp.dot(xn, w_ref[:, 2 * width + lo:2 * width + lo + ch],
                                        preferred_element_type=f32)
        bg = jnp.dot(xn, w_ref[:, 3 * width + lo:3 * width + lo + ch], preferred_element_type=f32)
        gate_ref[:, lo:lo + ch] = jax.nn.gelu(bg)


def _inproj(xa, xb, n_b_rows, ln_g, ln_b, w_in_b, tm):
    na, d = xa.shape
    width = w_in_b.shape[1] // 4
    n_a_tiles = na // tm
    n_tiles = (na + n_b_rows) // tm
    n_b_tiles = max(n_tiles - n_a_tiles, 1)
    row_out = pl.BlockSpec((tm, width), lambda i: (i, 0))
    return pl.pallas_call(
        functools.partial(_inproj_kernel, n_a_tiles=n_a_tiles),
        grid=(n_tiles,),
        in_specs=[
            pl.BlockSpec((tm, d), lambda i: (jnp.minimum(i, n_a_tiles - 1), 0)),
            pl.BlockSpec((tm, d), lambda i: (jnp.clip(i - n_a_tiles, 0, n_b_tiles - 1), 0)),
            pl.BlockSpec((1, d), lambda i: (0, 0)),
            pl.BlockSpec((1, d), lambda i: (0, 0)),
            pl.BlockSpec((d, 4 * width), lambda i: (0, 0), pipeline_mode=pl.Buffered(1)),
        ],
        out_specs=[pl.BlockSpec((tm, d), lambda i: (i, 0)), row_out, row_out, row_out],
        out_shape=[jax.ShapeDtypeStruct((na + n_b_rows, d), f32)]
        + [jax.ShapeDtypeStruct((na + n_b_rows, width), f32)] * 3,
        compiler_params=pltpu.CompilerParams(dimension_semantics=("arbitrary",),
                                             vmem_limit_bytes=VMEM_LIMIT),
        name="inproj",
    )(xa, xb, ln_g, ln_b, w_in_b)


def _mixer_seq_kernel(glu_ref, bx_ref, gate_ref, sa_ref, sb_ref, sh_ref,
                      wa_ref, ba_ref, ng_ref, nb_ref, wb_ref, bb_ref, wg_ref, brg_ref, big_ref, lam_ref,
                      aout_ref, bout_ref, nsa_ref, nsb_ref, nsh_ref,
                      wina, winb, xsa, xsb, ca_s, cb_s, a_s, u_s, hcar):
    tau = pl.program_id(1)
    tt, width = glu_ref.shape
    ka, kb = wa_ref.shape[0], wb_ref.shape[0]
    ha, hb = sa_ref.shape[0], sb_ref.shape[0]

    @pl.when(tau == 0)
    def _():
        wina[0:ha, :] = sa_ref[...]
        winb[0:hb, :] = sb_ref[...]
        hcar[...] = jnp.broadcast_to(sh_ref[...], hcar.shape)

    wina[ha:ha + tt, :] = glu_ref[...]
    winb[hb:hb + tt, :] = bx_ref[...]
    for s in sorted({(ha - (ka - 1) + k) % SUBLANES for k in range(ka)} - {0}):
        xsa[s - 1] = wina[pl.ds(s, tt + ha - SUBLANES), :]
    for s in sorted({(hb - (kb - 1) + k) % SUBLANES for k in range(kb)} - {0}):
        xsb[s - 1] = winb[pl.ds(s, tt + hb - SUBLANES), :]

    def tap(win, xs, off, r0, lanes):
        q, s = divmod(off, SUBLANES)
        src = win if s == 0 else xs.at[s - 1]
        return src[pl.ds(r0 + SUBLANES * q, SUBLANES), lanes]

    rows = min(CONV_ROWS, tt)
    for g in range(width // LANES):
        lanes = pl.ds(g * LANES, LANES)
        bcast = lambda ref, k: jnp.broadcast_to(ref[k:k + 1, lanes], (SUBLANES, LANES))
        wa_k = [bcast(wa_ref, k) for k in range(ka)]
        wb_k = [bcast(wb_ref, k) for k in range(kb)]
        ba_g, bb_g = bcast(ba_ref, 0), bcast(bb_ref, 0)

        def conv_block(rb, carry):
            for j in range(rows // SUBLANES):
                r0 = pl.multiple_of(rb * rows, rows) + j * SUBLANES
                acc = ba_g
                for k in range(ka):
                    acc = acc + wa_k[k] * tap(wina, xsa, ha - (ka - 1) + k, r0, lanes)
                ca_s[pl.ds(r0, SUBLANES), lanes] = acc
                accb = bb_g
                for k in range(kb):
                    accb = accb + wb_k[k] * tap(winb, xsb, hb - (kb - 1) + k, r0, lanes)
                cb_s[pl.ds(r0, SUBLANES), lanes] = accb
            return carry

        lax.fori_loop(0, tt // rows, conv_block, 0)

    a_out = _group_norm_silu(ca_s[...], ng_ref[...], nb_ref[...], width // CONV_HEADS_A)
    aout_ref[...] = a_out.astype(aout_ref.dtype)

    a, u = _lru_gates(cb_s[...], wg_ref, brg_ref[...], big_ref[...], lam_ref[...])
    row8 = lax.broadcasted_iota(i32, (tt, width), 0) & (SUBLANES - 1)
    d = 1
    while d < SUBLANES:
        m = row8 >= d
        a_sh = jnp.where(m, pltpu.roll(a, d, axis=0), 1.0)
        u_sh = jnp.where(m, pltpu.roll(u, d, axis=0), 0.0)
        u = a * u_sh + u
        a = a * a_sh
        d *= 2
    a_s[...] = a
    u_s[...] = u

    def scan_block(j, h):
        r0 = pl.multiple_of(j * SUBLANES, SUBLANES)
        hblk = a_s[pl.ds(r0, SUBLANES), :] * h + u_s[pl.ds(r0, SUBLANES), :]
        u_s[pl.ds(r0, SUBLANES), :] = hblk
        return jnp.broadcast_to(hblk[SUBLANES - 1:SUBLANES, :], h.shape)

    h_last = lax.fori_loop(0, tt // SUBLANES, scan_block, hcar[...])
    hcar[...] = h_last
    bout_ref[...] = (u_s[...] * gate_ref[...]).astype(bout_ref.dtype)

    tail_a = wina[tt:tt + ha, :]
    tail_b = winb[tt:tt + hb, :]
    wina[0:ha, :] = tail_a
    winb[0:hb, :] = tail_b
    nsa_ref[...] = tail_a
    nsb_ref[...] = tail_b
    nsh_ref[...] = h_last[0:1, :]


def _mixer_seq(glu, bx, gate, row_off, sa_pad, sb_pad, sh, mix_w, n_seq, seq_len, tt):
    wa, ba, ng, nb, wb, bb, wg, brg, big, lam = mix_w
    width = glu.shape[1]
    ha, hb = sa_pad.shape[1], sb_pad.shape[1]
    n_t = seq_len // tt
    off = row_off // tt
    row_in = pl.BlockSpec((tt, width), lambda b, t: (off + b * n_t + t, 0))
    row_out = pl.BlockSpec((tt, width), lambda b, t: (b * n_t + t, 0))

    def const(shape):
        return pl.BlockSpec(shape, lambda b, t: (0,) * len(shape))

    def per_seq(r):
        return pl.BlockSpec((None, r, width), lambda b, t: (b, 0, 0))

    return pl.pallas_call(
        _mixer_seq_kernel,
        grid=(n_seq, n_t),
        in_specs=[row_in, row_in, row_in, per_seq(ha), per_seq(hb), per_seq(1),
                  const(wa.shape), const(ba.shape), const(ng.shape), const(nb.shape),
                  const(wb.shape), const(bb.shape), const(wg.shape), const(brg.shape),
                  const(big.shape), const(lam.shape)],
        out_specs=[row_out, row_out, per_seq(ha), per_seq(hb), per_seq(1)],
        out_shape=[jax.ShapeDtypeStruct((n_seq * seq_len, width), bf16),
                   jax.ShapeDtypeStruct((n_seq * seq_len, width), bf16),
                   jax.ShapeDtypeStruct((n_seq, ha, width), f32),
                   jax.ShapeDtypeStruct((n_seq, hb, width), f32),
                   jax.ShapeDtypeStruct((n_seq, 1, width), f32)],
        scratch_shapes=[pltpu.VMEM((ha + tt, width), f32),
                        pltpu.VMEM((hb + tt, width), f32),
                        pltpu.VMEM((SUBLANES - 1, tt + ha - SUBLANES, width), f32),
                        pltpu.VMEM((SUBLANES - 1, tt + hb - SUBLANES, width), f32),
                        pltpu.VMEM((tt, width), f32),
                        pltpu.VMEM((tt, width), f32),
                        pltpu.VMEM((tt, width), f32),
                        pltpu.VMEM((tt, width), f32),
                        pltpu.VMEM((SUBLANES, width), f32)],
        compiler_params=pltpu.CompilerParams(dimension_semantics=("arbitrary", "arbitrary"),
                                             vmem_limit_bytes=VMEM_LIMIT),
        name="mixer_seq",
    )(glu, bx, gate, sa_pad, sb_pad, sh, wa, ba, ng, nb, wb, bb, wg, brg, big, lam)


def _mixer_step_kernel(glu_ref, bx_ref, gate_ref, sa_ref, sb_ref, sh_ref,
                       wat_ref, wa2_ref, wbt_ref, wb2_ref,
                       ba_ref, ng_ref, nb_ref, bb_ref, wg_ref, brg_ref, big_ref, lam_ref,
                       aout_ref, bout_ref, nsa_ref, nsb_ref, nsh_ref,
                       hista, histb, hcar):
    t = pl.program_id(1)
    n_t = hista.shape[0]
    width = glu_ref.shape[1]

    def roll_buffer(s_ref, ns_ref, x_ref):
        shift = n_t - s_ref.shape[0]

        @pl.when(t == 0)
        def _():
            for r in range(max(-shift, 0)):
                ns_ref[r] = s_ref[r + n_t]

        if shift <= 0:
            ns_ref[t - shift] = x_ref[...]
        else:
            @pl.when(t >= shift)
            def _():
                ns_ref[t - shift] = x_ref[...]

    @pl.when(t == 0)
    def _():
        hista[...] = jnp.zeros_like(hista)
        histb[...] = jnp.zeros_like(histb)
        hcar[...] = sh_ref[...]

    roll_buffer(sa_ref, nsa_ref, glu_ref)
    roll_buffer(sb_ref, nsb_ref, bx_ref)
    hista[t] = glu_ref[...]
    histb[t] = bx_ref[...]

    ca = jnp.broadcast_to(ba_ref[...], glu_ref.shape)
    for r in range(sa_ref.shape[0]):
        ca = ca + wat_ref[r:r + 1, :] * sa_ref[r]
    cb = jnp.broadcast_to(bb_ref[...], bx_ref.shape)
    for r in range(sb_ref.shape[0]):
        cb = cb + wbt_ref[r:r + 1, :] * sb_ref[r]
    for j in range(n_t):
        ca = ca + wa2_ref[j:j + 1, :] * hista[j]
        cb = cb + wb2_ref[j:j + 1, :] * histb[j]
    a_out = _group_norm_silu(ca, ng_ref[...], nb_ref[...], width // CONV_HEADS_A)
    aout_ref[...] = a_out.astype(aout_ref.dtype)

    a, u = _lru_gates(cb, wg_ref, brg_ref[...], big_ref[...], lam_ref[...])
    h = a * hcar[...] + u
    hcar[...] = h
    nsh_ref[...] = h
    bout_ref[...] = (h * gate_ref[...]).astype(bout_ref.dtype)


def _mixer_step(glu, bx, gate, row_off, sa, sb, sh, step_w, mix_w, n_seq, n_t, bt):
    wat, wa2, wbt, wb2 = step_w
    wa, ba, ng, nb, wb, bb, wg, brg, big, lam = mix_w
    width = glu.shape[1]
    ka1, kb1 = sa.shape[1], sb.shape[1]
    off = row_off // bt
    nb_tiles = n_seq // bt
    row_in = pl.BlockSpec((bt, width), lambda j, t: (off + t * nb_tiles + j, 0))
    row_out = pl.BlockSpec((bt, width), lambda j, t: (t * nb_tiles + j, 0))

    def const(shape):
        return pl.BlockSpec(shape, lambda j, t: (0,) * len(shape))

    def per_t(arr):
        return pl.BlockSpec((None,) + arr.shape[1:], lambda j, t: (t, 0, 0))

    def state(r):
        return pl.BlockSpec((None, r, bt, width), lambda j, t: (0, 0, j, 0))

    state_h = pl.BlockSpec((None, bt, width), lambda j, t: (0, j, 0))
    return pl.pallas_call(
        _mixer_step_kernel,
        grid=(nb_tiles, n_t),
        in_specs=[row_in, row_in, row_in, state(ka1), state(kb1), state_h,
                  per_t(wat), per_t(wa2), per_t(wbt), per_t(wb2),
                  const(ba.shape), const(ng.shape), const(nb.shape), const(bb.shape),
                  const(wg.shape), const(brg.shape), const(big.shape), const(lam.shape)],
        out_specs=[row_out, row_out, state(ka1), state(kb1), state_h],
        out_shape=[jax.ShapeDtypeStruct((n_seq * n_t, width), bf16),
                   jax.ShapeDtypeStruct((n_seq * n_t, width), bf16),
                   jax.ShapeDtypeStruct((1, ka1, n_seq, width), f32),
                   jax.ShapeDtypeStruct((1, kb1, n_seq, width), f32),
                   jax.ShapeDtypeStruct((1, n_seq, width), f32)],
        scratch_shapes=[pltpu.VMEM((n_t, bt, width), f32),
                        pltpu.VMEM((n_t, bt, width), f32),
                        pltpu.VMEM((bt, width), f32)],
        compiler_params=pltpu.CompilerParams(dimension_semantics=("arbitrary", "arbitrary"),
                                             vmem_limit_bytes=VMEM_LIMIT),
        name="mixer_step",
    )(glu, bx, gate, sa, sb, sh, wat, wa2, wbt, wb2,
      ba, ng, nb, bb, wg, brg, big, lam)


def _outproj_kernel(ap_ref, bp_ref, as_ref, bs_ref, h_ref, wo_ref,
                    l1g_ref, l1b_ref, wrh_ref, wrl_ref, br_ref,
                    h1_ref, h1p_ref, z_ref, *, n_p_tiles, alpha):
    i = pl.program_id(0)
    is_p = i < n_p_tiles
    a = jnp.where(is_p, ap_ref[...], as_ref[...])
    b = jnp.where(is_p, bp_ref[...], bs_ref[...])
    width = a.shape[1]
    mix = (jnp.dot(a, wo_ref[0:width, :], preferred_element_type=f32)
           + jnp.dot(b, wo_ref[width:2 * width, :], preferred_element_type=f32))
    h1 = _layer_norm(alpha * h_ref[...] + mix, l1g_ref[...], l1b_ref[...])
    h1_ref[...] = h1

    xh = h1.astype(bf16)
    xh32 = xh.astype(f32)
    half = h1.shape[1] // 2
    lo = lax.shift_right_logical(lax.bitcast_convert_type(xh32[:, :half], jnp.uint32), jnp.uint32(16))
    hi = lax.bitcast_convert_type(xh32[:, half:], jnp.uint32) & jnp.uint32(0xFFFF0000)
    word = lo | hi
    n_sub = half // LANES
    for s in range(n_sub):
        h1p_ref[pl.ds(s, word.shape[0], stride=n_sub), :] = word[:, s * LANES:(s + 1) * LANES]

    xl = (h1 - xh32).astype(bf16)
    z_ref[...] = (jnp.dot(xh, wrh_ref[...], preferred_element_type=f32)
                  + jnp.dot(xl, wrh_ref[...], preferred_element_type=f32)
                  + jnp.dot(xh, wrl_ref[...], preferred_element_type=f32)) + br_ref[...]


def _route_kernel(z_ref, route_ref, routet_ref, cnt_ref, carry, *, sub):
    i = pl.program_id(0)
    z = z_ref[...]
    lane = lax.broadcasted_iota(i32, z.shape, 1).astype(f32)
    n_g, epg = float(N_GROUPS), float(EXPERTS_PER_GROUP)
    far = float(2 * ROUTE_LANES)

    gm = lane < n_g
    zg = jnp.where(gm, z, NEG_BIG)
    pg = jnp.where(gm, jnp.exp(zg - jnp.max(zg, axis=-1, keepdims=True)), 0.0)
    pg = pg / jnp.sum(pg, axis=-1, keepdims=True)
    g_top = jnp.max(pg, axis=-1, keepdims=True)
    g_idx = jnp.min(jnp.where(gm & (pg == g_top), lane, far), axis=-1, keepdims=True)

    lo = n_g + g_idx * epg
    em = (lane >= lo) & (lane < lo + epg)
    ze = jnp.where(em, z, NEG_BIG)
    pe = jnp.where(em, jnp.exp(ze - jnp.max(ze, axis=-1, keepdims=True)), 0.0)
    pe = pe / jnp.sum(pe, axis=-1, keepdims=True)
    v1 = jnp.max(jnp.where(em, pe, -1.0), axis=-1, keepdims=True)
    l1 = jnp.min(jnp.where(em & (pe == v1), lane, far), axis=-1, keepdims=True)
    pe2 = jnp.where(em & (lane != l1), pe, -1.0)
    v2 = jnp.max(pe2, axis=-1, keepdims=True)
    l2 = jnp.min(jnp.where(pe2 == v2, lane, far), axis=-1, keepdims=True)
    den = v1 + v2
    gate1 = v1 / den * g_top
    gate2 = v2 / den * g_top

    @pl.when(i == 0)
    def _():
        carry[...] = jnp.zeros_like(carry)

    o1 = lane == l1
    o2 = lane == l2
    onehot = jnp.where(o1 | o2, 1.0, 0.0)
    ri = lax.broadcasted_iota(i32, (sub, sub), 0)
    ci = lax.broadcasted_iota(i32, (sub, sub), 1)
    tri = jnp.where(ci < ri, 1.0, 0.0).astype(bf16)
    counts = carry[0:1, :]
    prefix = []
    for s in range(z.shape[0] // sub):
        oh = onehot[s * sub:(s + 1) * sub, :]
        prefix.append(jnp.dot(tri, oh.astype(bf16), preferred_element_type=f32) + counts)
        counts = counts + jnp.sum(oh, axis=0, keepdims=True)
    prefix = jnp.concatenate(prefix, axis=0)
    rank1 = jnp.sum(jnp.where(o1, prefix, 0.0), axis=-1, keepdims=True)
    rank2 = jnp.sum(jnp.where(o2, prefix, 0.0), axis=-1, keepdims=True)
    new_carry = jnp.broadcast_to(counts, carry.shape)
    carry[...] = new_carry
    cnt_ref[...] = new_carry

    route = jnp.where(lane == 0.0, l1 - n_g, 0.0)
    route = jnp.where(lane == 1.0, l2 - n_g, route)
    route = jnp.where(lane == 2.0, rank1, route)
    route = jnp.where(lane == 3.0, rank2, route)
    route = jnp.where(lane == 4.0, gate1, route)
    route = jnp.where(lane == 5.0, gate2, route)
    route_ref[...] = route
    routet_ref[...] = route.T[0:routet_ref.shape[0], :]


def _outproj(a_p, b_p, a_s, b_s, h, w_out_b, ln1_g, ln1_b,
             wr_hi, wr_lo, b_r, alpha, tm):
    n_p, width = a_p.shape
    n_s = a_s.shape[0]
    d = h.shape[1]
    n_p_tiles = n_p // tm
    n_s_tiles = n_s // tm
    n_tiles = n_p_tiles + n_s_tiles

    def p_map(i):
        return (jnp.minimum(i, n_p_tiles - 1), 0)

    def s_map(i):
        return (jnp.clip(i - n_p_tiles, 0, n_s_tiles - 1), 0)

    def const(shape, **kw):
        return pl.BlockSpec(shape, lambda i: (0,) * len(shape), **kw)

    return pl.pallas_call(
        functools.partial(_outproj_kernel, n_p_tiles=n_p_tiles, alpha=alpha),
        grid=(n_tiles,),
        in_specs=[pl.BlockSpec((tm, width), p_map), pl.BlockSpec((tm, width), p_map),
                  pl.BlockSpec((tm, width), s_map), pl.BlockSpec((tm, width), s_map),
                  pl.BlockSpec((tm, d), lambda i: (i, 0)),
                  const((2 * width, d), pipeline_mode=pl.Buffered(1)),
                  const((1, d)), const((1, d)),
                  const((d, ROUTE_LANES)), const((d, ROUTE_LANES)), const((1, ROUTE_LANES))],
        out_specs=[pl.BlockSpec((tm, d), lambda i: (i, 0)),
                   pl.BlockSpec((tm * (d // 2 // LANES), LANES), lambda i: (i, 0)),
                   pl.BlockSpec((tm, ROUTE_LANES), lambda i: (i, 0))],
        out_shape=[jax.ShapeDtypeStruct((n_p + n_s, d), f32),
                   jax.ShapeDtypeStruct(((n_p + n_s) * (d // 2 // LANES), LANES), jnp.uint32),
                   jax.ShapeDtypeStruct((n_p + n_s, ROUTE_LANES), f32)],
        compiler_params=pltpu.CompilerParams(dimension_semantics=("arbitrary",),
                                             vmem_limit_bytes=VMEM_LIMIT),
        name="outproj",
    )(a_p, b_p, a_s, b_s, h, w_out_b, ln1_g, ln1_b, wr_hi, wr_lo, b_r)


def _route_tile(n):
    return max(t for t in range(ROUTE_SUB, ROUTE_MAX_TILE + 1, ROUTE_SUB) if n % t == 0)


def _route(z):
    n = z.shape[0]
    tr = _route_tile(n)
    return pl.pallas_call(
        functools.partial(_route_kernel, sub=ROUTE_SUB),
        grid=(n // tr,),
        in_specs=[pl.BlockSpec((tr, ROUTE_LANES), lambda i: (i, 0))],
        out_specs=[pl.BlockSpec((tr, ROUTE_LANES), lambda i: (i, 0)),
                   pl.BlockSpec((SUBLANES, tr), lambda i: (0, i)),
                   pl.BlockSpec((SUBLANES, ROUTE_LANES), lambda i: (0, 0))],
        out_shape=[jax.ShapeDtypeStruct((n, ROUTE_LANES), f32),
                   jax.ShapeDtypeStruct((SUBLANES, n), f32),
                   jax.ShapeDtypeStruct((SUBLANES, ROUTE_LANES), f32)],
        scratch_shapes=[pltpu.VMEM((SUBLANES, ROUTE_LANES), f32)],
        compiler_params=pltpu.CompilerParams(dimension_semantics=("arbitrary",),
                                             vmem_limit_bytes=VMEM_LIMIT),
        name="route",
    )(z)


def _expert_kernel(src_ref, dst_ref, start_ref, nblk_ref,
                   h1_hbm, wg_ref, wu_ref, wd_ref, y2_hbm,
                   xbuf, ybuf, wgb, wub, wdb, gsem, osem, *, blk, n_rows):
    e = pl.program_id(0)
    n_e = pl.num_programs(0)
    nb = nblk_ref[e]
    g0 = start_ref[e]

    xs = xbuf.shape[1] // blk
    ys = ybuf.shape[1] // blk

    def gather(g, slot):
        for r in range(blk):
            tok = src_ref[g * blk + r]
            pltpu.make_async_copy(h1_hbm.at[pl.ds(pl.multiple_of(tok * xs, xs), xs), :],
                                  xbuf.at[slot, pl.ds(r * xs, xs), :], gsem.at[slot]
                                  ).start(priority=r % 2)

    def scatter(g, slot):
        for r in range(blk):
            row = dst_ref[(g + 1) * blk + r]
            pltpu.make_async_copy(ybuf.at[slot, pl.ds(r * ys, ys), :],
                                  y2_hbm.at[pl.ds(pl.multiple_of(row * ys, ys), ys), :],
                                  osem.at[slot]).start(priority=r % 2)

    def dump_copy(slot):
        dst = y2_hbm.at[pl.ds((n_rows + slot * blk) * ys, blk * ys), :]
        return pltpu.make_async_copy(ybuf.at[slot], dst, osem.at[slot])

    def wait_gather(slot):
        pltpu.make_async_copy(h1_hbm.at[pl.ds(0, blk * xs), :], xbuf.at[slot], gsem.at[slot]).wait()

    nx, ny = xbuf.shape[0], ybuf.shape[0]
    ahead = nx - 1

    @pl.when(e == 0)
    def _():
        ybuf[...] = jnp.zeros(ybuf.shape, ybuf.dtype)
        for s in range(ny - 1):
            dump_copy(s).start()
        for g in range(ahead):
            gather(g, g)

    @pl.when(nb > 0)
    def _():
        wgb[...] = wg_ref[...].astype(bf16)
        wub[...] = wu_ref[...].astype(bf16)
        wdb[...] = wd_ref[...].astype(bf16)

        def chunk(c, carry):
            g = g0 + c
            slot = g % nx
            yslot = g % ny
            wait_gather(slot)
            dump_copy(yslot).wait()
            gather(g + ahead, (g + ahead) % nx)
            scatter(g - 1, (g + ny - 1) % ny)
            xw = [xbuf[slot, pl.ds(s, blk, stride=xs), :] for s in range(xs)]
            x_lo = [lax.bitcast_convert_type(lax.shift_left(w, jnp.uint32(16)), f32) for w in xw]
            x_hi = [lax.bitcast_convert_type(w & jnp.uint32(0xFFFF0000), f32) for w in xw]
            x = jnp.concatenate(x_lo + x_hi, axis=1).astype(bf16)
            hg = jnp.dot(x, wgb[...], preferred_element_type=f32)
            hu = jnp.dot(x, wub[...], preferred_element_type=f32)
            hid = (hg * jax.nn.sigmoid(hg) * hu).astype(bf16)
            y = jnp.dot(hid, wdb[...], preferred_element_type=f32)
            for s in range(ys):
                ybuf[yslot, pl.ds(s, blk, stride=ys), :] = y[:, s * LANES:(s + 1) * LANES]
            return carry

        lax.fori_loop(0, nb, chunk, 0)

    @pl.when(e == n_e - 1)
    def _():
        g_end = g0 + nb
        scatter(g_end - 1, (g_end + ny - 1) % ny)
        for s in range(ny):
            dump_copy(s).wait()
        for j in range(ahead):
            wait_gather((g_end + j) % nx)


def _experts(h1p, src_rows, dst_rows, pad_start, nblk, w_gate, w_up, w_down, blk):
    n_e, d, d_e = w_gate.shape
    xs, ys = d // 2 // LANES, d // LANES
    n = h1p.shape[0] // xs
    n_rows = TOP_K * n
    grid_spec = pltpu.PrefetchScalarGridSpec(
        num_scalar_prefetch=4,
        grid=(n_e,),
        in_specs=[pl.BlockSpec(memory_space=pl.ANY),
                  pl.BlockSpec((None, d, d_e), lambda e, *_: (e, 0, 0)),
                  pl.BlockSpec((None, d, d_e), lambda e, *_: (e, 0, 0)),
                  pl.BlockSpec((None, d_e, d), lambda e, *_: (e, 0, 0))],
        out_specs=pl.BlockSpec(memory_space=pl.ANY),
        scratch_shapes=[pltpu.VMEM((GATHER_SLOTS, blk * xs, LANES), jnp.uint32),
                        pltpu.VMEM((SCATTER_SLOTS, blk * ys, LANES), f32),
                        pltpu.VMEM((d, d_e), bf16),
                        pltpu.VMEM((d, d_e), bf16),
                        pltpu.VMEM((d_e, d), bf16),
                        pltpu.SemaphoreType.DMA((GATHER_SLOTS,)),
                        pltpu.SemaphoreType.DMA((SCATTER_SLOTS,))],
    )
    return pl.pallas_call(
        functools.partial(_expert_kernel, blk=blk, n_rows=n_rows),
        grid_spec=grid_spec,
        out_shape=jax.ShapeDtypeStruct(((n_rows + SCATTER_SLOTS * blk) * ys, LANES), f32),
        compiler_params=pltpu.CompilerParams(dimension_semantics=("arbitrary",),
                                             vmem_limit_bytes=VMEM_LIMIT),
        name="experts",
    )(src_rows, dst_rows, pad_start, nblk, h1p, w_gate, w_up, w_down)


def _combine_kernel(h1_ref, route_ref, ya_ref, yb_ref, l2g_ref, l2b_ref, yp_ref, ysm_ref,
                    *, n_p_tiles, alpha):
    i = pl.program_id(0)
    route = route_ref[...]
    tm = route.shape[0]
    ys = ya_ref.shape[0] // tm

    def token_rows(ref):
        return jnp.concatenate([ref[pl.ds(s, tm, stride=ys), :] for s in range(ys)], axis=1)

    y = token_rows(ya_ref) * route[:, 4:5] + token_rows(yb_ref) * route[:, 5:6]
    out = _layer_norm(alpha * h1_ref[...] + y, l2g_ref[...], l2b_ref[...])

    @pl.when(i < n_p_tiles)
    def _():
        yp_ref[...] = out

    @pl.when(i >= n_p_tiles)
    def _():
        ysm_ref[...] = out


def _combine(h1, route, y2, ln2_g, ln2_b, n_p, alpha, tm):
    n, d = h1.shape
    n_s = n - n_p
    n_tiles = n // tm
    n_p_tiles = n_p // tm
    n_s_tiles = n_s // tm
    return pl.pallas_call(
        functools.partial(_combine_kernel, n_p_tiles=n_p_tiles, alpha=alpha),
        grid=(n_tiles,),
        in_specs=[pl.BlockSpec((tm, d), lambda i: (i, 0)),
                  pl.BlockSpec((tm, ROUTE_LANES), lambda i: (i, 0)),
                  pl.BlockSpec((tm * (d // LANES), LANES), lambda i: (i, 0)),
                  pl.BlockSpec((tm * (d // LANES), LANES), lambda i: (n_tiles + i, 0)),
                  pl.BlockSpec((1, d), lambda i: (0, 0)),
                  pl.BlockSpec((1, d), lambda i: (0, 0))],
        out_specs=[pl.BlockSpec((tm, d), lambda i: (jnp.minimum(i, n_p_tiles - 1), 0)),
                   pl.BlockSpec((tm, d), lambda i: (jnp.clip(i - n_p_tiles, 0, n_s_tiles - 1), 0))],
        out_shape=[jax.ShapeDtypeStruct((n_p, d), f32), jax.ShapeDtypeStruct((n_s, d), f32)],
        compiler_params=pltpu.CompilerParams(dimension_semantics=("arbitrary",),
                                             vmem_limit_bytes=VMEM_LIMIT),
        name="combine_ln2",
    )(h1, route, y2, y2, ln2_g, ln2_b)


def _shifted_taps(w, n_t, hist):
    k = w.shape[0]
    zero = jnp.zeros_like(w[0])
    wt = jnp.stack([jnp.stack([w[r - t] if 0 <= r - t < k else zero for r in range(hist)])
                    for t in range(n_t)])
    w2 = jnp.stack([jnp.stack([w[hist - t + j] if (j <= t and 0 <= hist - t + j < k) else zero
                               for j in range(n_t)]) for t in range(n_t)])
    return wt, w2


def kernel(x_prompt, x_sample, state_conv_a, state_conv_b, state_lru, meta_tokens, ln_in_g, ln_in_b, w_in, conv_a_w, conv_a_b, norm_a_g, norm_a_b, conv_b_w, conv_b_b, w_rg, b_rg, w_ig, b_ig, lru_lambda, w_out, ln1_g, ln1_b, w_router_group, b_router_group, w_router_expert, b_router_expert, w_gate, w_up, w_down, ln2_g, ln2_b):
    depth = w_in.shape[0]
    assert depth == 1, "single-layer trunk only"
    bp, tp, d = x_prompt.shape
    bs, ts, _ = x_sample.shape
    n_meta = meta_tokens.shape[0]
    width = conv_a_w.shape[2]
    ka, kb = conv_a_w.shape[1], conv_b_w.shape[1]
    n_e = w_gate.shape[1]
    alpha = (2.0 * depth) ** 0.25
    n_p, n_s = bp * tp, bs * ts
    n = n_p + n_s
    assert ts < ka and ts >= kb - 1
    assert n_p % TOKEN_TILE == 0 and n_s % TOKEN_TILE == 0 and tp % SEQ_TILE == 0
    assert bs % STEP_BATCH == 0 and n_p % STEP_BATCH == 0
    assert n_meta % BF16_ROWS == 0 and n_meta <= TOKEN_TILE and n % n_meta == 0
    assert N_GROUPS * (1 + EXPERTS_PER_GROUP) <= ROUTE_LANES and n_e == N_GROUPS * EXPERTS_PER_GROUP

    row = lambda v: v.reshape(1, -1).astype(f32)
    ln_g, ln_b = row(ln_in_g), row(ln_in_b)
    w_in_b = w_in[0].astype(bf16)
    w_out_b = w_out[0].astype(bf16)
    mix_w = (conv_a_w[0], row(conv_a_b[0]), row(norm_a_g[0]), row(norm_a_b[0]),
             conv_b_w[0], row(conv_b_b[0]),
             jnp.concatenate([w_rg[0], w_ig[0]], axis=-1).astype(bf16),
             row(b_rg[0]), row(b_ig[0]), row(lru_lambda[0]))
    ha, hb = _round_up(ka - 1, SUBLANES), _round_up(kb - 1, SUBLANES)

    x_p = x_prompt.reshape(n_p, d)
    x_s = jnp.concatenate([jnp.swapaxes(x_sample, 0, 1).reshape(n_s, d),
                           jnp.pad(meta_tokens, ((0, TOKEN_TILE - n_meta), (0, 0)))], axis=0)
    h, glu, bx, gate = _inproj(x_p, x_s, n_s + TOKEN_TILE, ln_g, ln_b, w_in_b, TOKEN_TILE)

    _, _, sa_m, sb_m, sh_m = _mixer_seq(
        glu, bx, gate, n, jnp.zeros((1, ha, width), f32), jnp.zeros((1, hb, width), f32),
        jnp.zeros((1, 1, width), f32), mix_w, 1, n_meta, n_meta)

    bcast = lambda s: jnp.broadcast_to(s, (bp,) + s.shape[1:])
    a_p, b_p, nsa_p, nsb_p, nsh_p = _mixer_seq(glu, bx, gate, 0, bcast(sa_m), bcast(sb_m), bcast(sh_m),
                                               mix_w, bp, tp, SEQ_TILE)

    row_major = lambda s: jnp.swapaxes(s, 1, 2)
    wat, wa2 = _shifted_taps(conv_a_w[0], ts, ka - 1)
    wbt, wb2 = _shifted_taps(conv_b_w[0], ts, kb - 1)
    a_s, b_s, nsa_s, nsb_s, nsh_s = _mixer_step(
        glu, bx, gate, n_p, row_major(state_conv_a), row_major(state_conv_b), state_lru,
        (wat, wa2, wbt, wb2), mix_w, bs, ts, STEP_BATCH)

    w_r = jnp.concatenate([w_router_group[0], w_router_expert[0]], axis=1)
    w_r = jnp.pad(w_r, ((0, 0), (0, ROUTE_LANES - w_r.shape[1])))
    wr_hi = w_r.astype(bf16)
    wr_lo = (w_r - wr_hi.astype(f32)).astype(bf16)
    b_r = jnp.concatenate([b_router_group[0], b_router_expert[0]])
    b_r = jnp.pad(b_r, (0, ROUTE_LANES - b_r.shape[0])).reshape(1, ROUTE_LANES)
    h1, h1p, logits = _outproj(a_p, b_p, a_s, b_s, h, w_out_b,
                               row(ln1_g[0]), row(ln1_b[0]), wr_hi, wr_lo, b_r, alpha, TOKEN_TILE)
    route, route_t, cnt = _route(logits)

    blk = EXPERT_ROWS
    e_idx = route_t[0:TOP_K].astype(i32)
    rank = route_t[TOP_K:2 * TOP_K].astype(i32)
    counts = cnt[0, N_GROUPS:N_GROUPS + n_e].astype(i32)
    nblk = (counts + blk - 1) // blk
    pad_end = jnp.cumsum(nblk * blk)
    pad_start = pad_end - nblk * blk
    start_of = jnp.sum(jnp.where(e_idx[None] == jnp.arange(n_e, dtype=i32)[:, None, None],
                                 pad_start[:, None, None], 0), axis=0)
    dest = (start_of + rank).reshape(-1)
    p_max = _round_up(TOP_K * n + n_e * (blk - 1), blk)
    flat = jnp.full((p_max,), -1, i32).at[dest].set(jnp.arange(TOP_K * n, dtype=i32))
    pos = jnp.arange(p_max, dtype=i32)
    valid = flat >= 0
    src_rows = jnp.where(valid, flat % n, 0)
    dst_rows = jnp.where(valid, flat, TOP_K * n + pos % (SCATTER_SLOTS * blk))
    src_rows = jnp.concatenate([src_rows, jnp.zeros(((GATHER_SLOTS - 1) * blk,), i32)])
    dst_rows = jnp.concatenate([TOP_K * n + (SCATTER_SLOTS - 1) * blk + jnp.arange(blk, dtype=i32), dst_rows])
    y2 = _experts(h1p, src_rows, dst_rows, (pad_start // blk).astype(i32), nblk,
                  w_gate[0], w_up[0], w_down[0], blk)
    y_p, y_s = _combine(h1, route, y2, row(ln2_g[0]), row(ln2_b[0]), n_p, alpha, TOKEN_TILE)

    y_prompt = y_p.reshape(bp, tp, d)
    y_sample = jnp.swapaxes(y_s.reshape(ts, bs, d), 0, 1)
    return (y_prompt, y_sample,
            nsa_p[:, ha - (ka - 1):][None], nsb_p[:, hb - (kb - 1):][None], nsh_p.reshape(1, bp, width),
            row_major(nsa_s), row_major(nsb_s), nsh_s)
```

```python
import functools

import jax
import jax.numpy as jnp
from jax import lax
from jax.experimental import pallas as pl
from jax.experimental.pallas import tpu as pltpu

f32 = jnp.float32
bf16 = jnp.bfloat16
i32 = jnp.int32

CONV_HEADS_A = 8
LRU_HEADS = 8
LRU_C = 8.0
N_GROUPS = 8
EXPERTS_PER_GROUP = 8
TOP_K = 2
LN_EPS = 1e-5

SUBLANES = 8
LANES = 128
BF16_ROWS = 16
VMEM_LIMIT = 56 * 1024 * 1024

TOKEN_TILE = 256
SEQ_TILE = 256
CONV_ROWS = 64
STEP_BATCH = 32
EXPERT_ROWS = 128
ROUTE_LANES = 128
ROUTE_SUB = 128
ROUTE_MAX_TILE = 2304
GATHER_SLOTS = 4
SCATTER_SLOTS = 4
NEG_BIG = -1e30


def _round_up(x, m):
    return (x + m - 1) // m * m


def _layer_norm(x, g, b):
    mu = jnp.mean(x, axis=-1, keepdims=True)
    xc = x - mu
    var = jnp.mean(xc * xc, axis=-1, keepdims=True)
    return xc * lax.rsqrt(var + LN_EPS) * g + b


def _group_norm_silu(x, g, b, group):
    parts = []
    for c in range(x.shape[1] // group):
        xg = x[:, c * group:(c + 1) * group]
        mu = jnp.mean(xg, axis=-1, keepdims=True)
        xc = xg - mu
        var = jnp.mean(xc * xc, axis=-1, keepdims=True)
        parts.append(xc * lax.rsqrt(var + LN_EPS))
    y = jnp.concatenate(parts, axis=-1) * g + b
    return y * jax.nn.sigmoid(y)


def _lru_gates(cb, wg_ref, brg, big, lam):
    heads, hd, _ = wg_ref.shape
    cbb = cb.astype(bf16)
    rs, iz = [], []
    for h in range(heads):
        z = jnp.dot(cbb[:, h * hd:(h + 1) * hd], wg_ref[h], preferred_element_type=f32)
        rs.append(z[:, :hd])
        iz.append(z[:, hd:])
    r = jax.nn.sigmoid(jnp.concatenate(rs, axis=-1) + brg)
    i = jax.nn.sigmoid(jnp.concatenate(iz, axis=-1) + big)
    log_a = -LRU_C * r * jax.nn.softplus(-lam)
    a = jnp.exp(log_a)
    u = jnp.sqrt(-jnp.tanh(log_a) * (a * a + 1.0)) * (i * cb)
    return a, u


def _inproj_kernel(xa_ref, xb_ref, g_ref, b_ref, w_ref, h_ref, glu_ref, bx_ref, gate_ref, *, n_a_tiles):
    i = pl.program_id(0)
    x = jnp.where(i < n_a_tiles, xa_ref[...], xb_ref[...])
    h = _layer_norm(x, g_ref[...], b_ref[...])
    h_ref[...] = h
    xn = h.astype(bf16)
    width = glu_ref.shape[1]
    ch = min(width, 512)
    for c in range(width // ch):
        lo = c * ch
        av = jnp.dot(xn, w_ref[:, lo:lo + ch], preferred_element_type=f32)
        ag = jnp.dot(xn, w_ref[:, width + lo:width + lo + ch], preferred_element_type=f32)
        glu_ref[:, lo:lo + ch] = av * jax.nn.sigmoid(ag)
        bx_ref[:, lo:lo + ch] = jnp.dot(xn, w_ref[:, 2 * width + lo:2 * width + lo + ch],
                                        preferred_element_type=f32)
        bg = jnp.dot(xn, w_ref[:, 3 * width + lo:3 * width + lo + ch], preferred_element_type=f32)
        gate_ref[:, lo:lo + ch] = jax.nn.gelu(bg)


def _inproj(xa, xb, n_b_rows, ln_g, ln_b, w_in_b, tm):
    na, d = xa.shape
    width = w_in_b.shape[1] // 4
    n_a_tiles = na // tm
    n_tiles = (na + n_b_rows) // tm
    n_b_tiles = max(n_tiles - n_a_tiles, 1)
    row_out = pl.BlockSpec((tm, width), lambda i: (i, 0))
    return pl.pallas_call(
        functools.partial(_inproj_kernel, n_a_tiles=n_a_tiles),
        grid=(n_tiles,),
        in_specs=[
            pl.BlockSpec((tm, d), lambda i: (jnp.minimum(i, n_a_tiles - 1), 0)),
            pl.BlockSpec((tm, d), lambda i: (jnp.clip(i - n_a_tiles, 0, n_b_tiles - 1), 0)),
            pl.BlockSpec((1, d), lambda i: (0, 0)),
            pl.BlockSpec((1, d), lambda i: (0, 0)),
            pl.BlockSpec((d, 4 * width), lambda i: (0, 0), pipeline_mode=pl.Buffered(1)),
        ],
        out_specs=[pl.BlockSpec((tm, d), lambda i: (i, 0)), row_out, row_out, row_out],
        out_shape=[jax.ShapeDtypeStruct((na + n_b_rows, d), f32)]
        + [jax.ShapeDtypeStruct((na + n_b_rows, width), f32)] * 3,
        compiler_params=pltpu.CompilerParams(dimension_semantics=("arbitrary",),
                                             vmem_limit_bytes=VMEM_LIMIT),
        name="inproj",
    )(xa, xb, ln_g, ln_b, w_in_b)


def _mixer_seq_kernel(glu_ref, bx_ref, gate_ref, sa_ref, sb_ref, sh_ref,
                      wa_ref, ba_ref, ng_ref, nb_ref, wb_ref, bb_ref, wg_ref, brg_ref, big_ref, lam_ref,
                      aout_ref, bout_ref, nsa_ref, nsb_ref, nsh_ref,
                      wina, winb, xsa, xsb, ca_s, cb_s, a_s, u_s, hcar):
    tau = pl.program_id(1)
    tt, width = glu_ref.shape
    ka, kb = wa_ref.shape[0], wb_ref.shape[0]
    ha, hb = sa_ref.shape[0], sb_ref.shape[0]

    @pl.when(tau == 0)
    def _():
        wina[0:ha, :] = sa_ref[...]
        winb[0:hb, :] = sb_ref[...]
        hcar[...] = jnp.broadcast_to(sh_ref[...], hcar.shape)

    wina[ha:ha + tt, :] = glu_ref[...]
    winb[hb:hb + tt, :] = bx_ref[...]
    for s in sorted({(ha - (ka - 1) + k) % SUBLANES for k in range(ka)} - {0}):
        xsa[s - 1] = wina[pl.ds(s, tt + ha - SUBLANES), :]
    for s in sorted({(hb - (kb - 1) + k) % SUBLANES for k in range(kb)} - {0}):
        xsb[s - 1] = winb[pl.ds(s, tt + hb - SUBLANES), :]

    def tap(win, xs, off, r0, lanes):
        q, s = divmod(off, SUBLANES)
        src = win if s == 0 else xs.at[s - 1]
        return src[pl.ds(r0 + SUBLANES * q, SUBLANES), lanes]

    rows = min(CONV_ROWS, tt)
    for g in range(width // LANES):
        lanes = pl.ds(g * LANES, LANES)
        bcast = lambda ref, k: jnp.broadcast_to(ref[k:k + 1, lanes], (SUBLANES, LANES))
        wa_k = [bcast(wa_ref, k) for k in range(ka)]
        wb_k = [bcast(wb_ref, k) for k in range(kb)]
        ba_g, bb_g = bcast(ba_ref, 0), bcast(bb_ref, 0)

        def conv_block(rb, carry):
            for j in range(rows // SUBLANES):
                r0 = pl.multiple_of(rb * rows, rows) + j * SUBLANES
                acc = ba_g
                for k in range(ka):
                    acc = acc + wa_k[k] * tap(wina, xsa, ha - (ka - 1) + k, r0, lanes)
                ca_s[pl.ds(r0, SUBLANES), lanes] = acc
                accb = bb_g
                for k in range(kb):
                    accb = accb + wb_k[k] * tap(winb, xsb, hb - (kb - 1) + k, r0, lanes)
                cb_s[pl.ds(r0, SUBLANES), lanes] = accb
            return carry

        lax.fori_loop(0, tt // rows, conv_block, 0)

    a_out = _group_norm_silu(ca_s[...], ng_ref[...], nb_ref[...], width // CONV_HEADS_A)
    aout_ref[...] = a_out.astype(aout_ref.dtype)

    a, u = _lru_gates(cb_s[...], wg_ref, brg_ref[...], big_ref[...], lam_ref[...])
    row8 = lax.broadcasted_iota(i32, (tt, width), 0) & (SUBLANES - 1)
    d = 1
    while d < SUBLANES:
        m = row8 >= d
        a_sh = jnp.where(m, pltpu.roll(a, d, axis=0), 1.0)
        u_sh = jnp.where(m, pltpu.roll(u, d, axis=0), 0.0)
        u = a * u_sh + u
        a = a * a_sh
        d *= 2
    a_s[...] = a
    u_s[...] = u

    def scan_block(j, h):
        r0 = pl.multiple_of(j * SUBLANES, SUBLANES)
        hblk = a_s[pl.ds(r0, SUBLANES), :] * h + u_s[pl.ds(r0, SUBLANES), :]
        u_s[pl.ds(r0, SUBLANES), :] = hblk
        return jnp.broadcast_to(hblk[SUBLANES - 1:SUBLANES, :], h.shape)

    h_last = lax.fori_loop(0, tt // SUBLANES, scan_block, hcar[...])
    hcar[...] = h_last
    bout_ref[...] = (u_s[...] * gate_ref[...]).astype(bout_ref.dtype)

    tail_a = wina[tt:tt + ha, :]
    tail_b = winb[tt:tt + hb, :]
    wina[0:ha, :] = tail_a
    winb[0:hb, :] = tail_b
    nsa_ref[...] = tail_a
    nsb_ref[...] = tail_b
    nsh_ref[...] = h_last[0:1, :]


def _mixer_seq(glu, bx, gate, row_off, sa_pad, sb_pad, sh, mix_w, n_seq, seq_len, tt):
    wa, ba, ng, nb, wb, bb, wg, brg, big, lam = mix_w
    width = glu.shape[1]
    ha, hb = sa_pad.shape[1], sb_pad.shape[1]
    n_t = seq_len // tt
    off = row_off // tt
    row_in = pl.BlockSpec((tt, width), lambda b, t: (off + b * n_t + t, 0))
    row_out = pl.BlockSpec((tt, width), lambda b, t: (b * n_t + t, 0))

    def const(shape):
        return pl.BlockSpec(shape, lambda b, t: (0,) * len(shape))

    def per_seq(r):
        return pl.BlockSpec((None, r, width), lambda b, t: (b, 0, 0))

    return pl.pallas_call(
        _mixer_seq_kernel,
        grid=(n_seq, n_t),
        in_specs=[row_in, row_in, row_in, per_seq(ha), per_seq(hb), per_seq(1),
                  const(wa.shape), const(ba.shape), const(ng.shape), const(nb.shape),
                  const(wb.shape), const(bb.shape), const(wg.shape), const(brg.shape),
                  const(big.shape), const(lam.shape)],
        out_specs=[row_out, row_out, per_seq(ha), per_seq(hb), per_seq(1)],
        out_shape=[jax.ShapeDtypeStruct((n_seq * seq_len, width), bf16),
                   jax.ShapeDtypeStruct((n_seq * seq_len, width), bf16),
                   jax.ShapeDtypeStruct((n_seq, ha, width), f32),
                   jax.ShapeDtypeStruct((n_seq, hb, width), f32),
                   jax.ShapeDtypeStruct((n_seq, 1, width), f32)],
        scratch_shapes=[pltpu.VMEM((ha + tt, width), f32),
                        pltpu.VMEM((hb + tt, width), f32),
                        pltpu.VMEM((SUBLANES - 1, tt + ha - SUBLANES, width), f32),
                        pltpu.VMEM((SUBLANES - 1, tt + hb - SUBLANES, width), f32),
                        pltpu.VMEM((tt, width), f32),
                        pltpu.VMEM((tt, width), f32),
                        pltpu.VMEM((tt, width), f32),
                        pltpu.VMEM((tt, width), f32),
                        pltpu.VMEM((SUBLANES, width), f32)],
        compiler_params=pltpu.CompilerParams(dimension_semantics=("arbitrary", "arbitrary"),
                                             vmem_limit_bytes=VMEM_LIMIT),
        name="mixer_seq",
    )(glu, bx, gate, sa_pad, sb_pad, sh, wa, ba, ng, nb, wb, bb, wg, brg, big, lam)


def _mixer_step_kernel(glu_ref, bx_ref, gate_ref, sa_ref, sb_ref, sh_ref,
                       wat_ref, wa2_ref, wbt_ref, wb2_ref,
                       ba_ref, ng_ref, nb_ref, bb_ref, wg_ref, brg_ref, big_ref, lam_ref,
                       aout_ref, bout_ref, nsa_ref, nsb_ref, nsh_ref,
                       hista, histb, hcar):
    t = pl.program_id(1)
    n_t = hista.shape[0]
    width = glu_ref.shape[1]

    def roll_buffer(s_ref, ns_ref, x_ref):
        shift = n_t - s_ref.shape[0]

        @pl.when(t == 0)
        def _():
            for r in range(max(-shift, 0)):
                ns_ref[r] = s_ref[r + n_t]

        if shift <= 0:
            ns_ref[t - shift] = x_ref[...]
        else:
            @pl.when(t >= shift)
            def _():
                ns_ref[t - shift] = x_ref[...]

    @pl.when(t == 0)
    def _():
        hista[...] = jnp.zeros_like(hista)
        histb[...] = jnp.zeros_like(histb)
        hcar[...] = sh_ref[...]

    roll_buffer(sa_ref, nsa_ref, glu_ref)
    roll_buffer(sb_ref, nsb_ref, bx_ref)
    hista[t] = glu_ref[...]
    histb[t] = bx_ref[...]

    ca = jnp.broadcast_to(ba_ref[...], glu_ref.shape)
    for r in range(sa_ref.shape[0]):
        ca = ca + wat_ref[r:r + 1, :] * sa_ref[r]
    cb = jnp.broadcast_to(bb_ref[...], bx_ref.shape)
    for r in range(sb_ref.shape[0]):
        cb = cb + wbt_ref[r:r + 1, :] * sb_ref[r]
    for j in range(n_t):
        ca = ca + wa2_ref[j:j + 1, :] * hista[j]
        cb = cb + wb2_ref[j:j + 1, :] * histb[j]
    a_out = _group_norm_silu(ca, ng_ref[...], nb_ref[...], width // CONV_HEADS_A)
    aout_ref[...] = a_out.astype(aout_ref.dtype)

    a, u = _lru_gates(cb, wg_ref, brg_ref[...], big_ref[...], lam_ref[...])
    h = a * hcar[...] + u
    hcar[...] = h
    nsh_ref[...] = h
    bout_ref[...] = (h * gate_ref[...]).astype(bout_ref.dtype)


def _mixer_step(glu, bx, gate, row_off, sa, sb, sh, step_w, mix_w, n_seq, n_t, bt):
    wat, wa2, wbt, wb2 = step_w
    wa, ba, ng, nb, wb, bb, wg, brg, big, lam = mix_w
    width = glu.shape[1]
    ka1, kb1 = sa.shape[1], sb.shape[1]
    off = row_off // bt
    nb_tiles = n_seq // bt
    row_in = pl.BlockSpec((bt, width), lambda j, t: (off + t * nb_tiles + j, 0))
    row_out = pl.BlockSpec((bt, width), lambda j, t: (t * nb_tiles + j, 0))

    def const(shape):
        return pl.BlockSpec(shape, lambda j, t: (0,) * len(shape))

    def per_t(arr):
        return pl.BlockSpec((None,) + arr.shape[1:], lambda j, t: (t, 0, 0))

    def state(r):
        return pl.BlockSpec((None, r, bt, width), lambda j, t: (0, 0, j, 0))

    state_h = pl.BlockSpec((None, bt, width), lambda j, t: (0, j, 0))
    return pl.pallas_call(
        _mixer_step_kernel,
        grid=(nb_tiles, n_t),
        in_specs=[row_in, row_in, row_in, state(ka1), state(kb1), state_h,
                  per_t(wat), per_t(wa2), per_t(wbt), per_t(wb2),
                  const(ba.shape), const(ng.shape), const(nb.shape), const(bb.shape),
                  const(wg.shape), const(brg.shape), const(big.shape), const(lam.shape)],
        out_specs=[row_out, row_out, state(ka1), state(kb1), state_h],
        out_shape=[jax.ShapeDtypeStruct((n_seq * n_t, width), bf16),
                   jax.ShapeDtypeStruct((n_seq * n_t, width), bf16),
                   jax.ShapeDtypeStruct((1, ka1, n_seq, width), f32),
                   jax.ShapeDtypeStruct((1, kb1, n_seq, width), f32),
                   jax.ShapeDtypeStruct((1, n_seq, width), f32)],
        scratch_shapes=[pltpu.VMEM((n_t, bt, width), f32),
                        pltpu.VMEM((n_t, bt, width), f32),
                        pltpu.VMEM((bt, width), f32)],
        compiler_params=pltpu.CompilerParams(dimension_semantics=("arbitrary", "arbitrary"),
                                             vmem_limit_bytes=VMEM_LIMIT),
        name="mixer_step",
    )(glu, bx, gate, sa, sb, sh, wat, wa2, wbt, wb2,
      ba, ng, nb, bb, wg, brg, big, lam)


def _outproj_kernel(ap_ref, bp_ref, as_ref, bs_ref, h_ref, wo_ref,
                    l1g_ref, l1b_ref, wrh_ref, wrl_ref, br_ref,
                    h1_ref, h1p_ref, z_ref, *, n_p_tiles, alpha):
    i = pl.program_id(0)
    is_p = i < n_p_tiles
    a = jnp.where(is_p, ap_ref[...], as_ref[...])
    b = jnp.where(is_p, bp_ref[...], bs_ref[...])
    width = a.shape[1]
    mix = (jnp.dot(a, wo_ref[0:width, :], preferred_element_type=f32)
           + jnp.dot(b, wo_ref[width:2 * width, :], preferred_element_type=f32))
    h1 = _layer_norm(alpha * h_ref[...] + mix, l1g_ref[...], l1b_ref[...])
    h1_ref[...] = h1

    xh = h1.astype(bf16)
    xh32 = xh.astype(f32)
    half = h1.shape[1] // 2
    lo = lax.shift_right_logical(lax.bitcast_convert_type(xh32[:, :half], jnp.uint32), jnp.uint32(16))
    hi = lax.bitcast_convert_type(xh32[:, half:], jnp.uint32) & jnp.uint32(0xFFFF0000)
    word = lo | hi
    n_sub = half // LANES
    for s in range(n_sub):
        h1p_ref[pl.ds(s, word.shape[0], stride=n_sub), :] = word[:, s * LANES:(s + 1) * LANES]

    xl = (h1 - xh32).astype(bf16)
    z_ref[...] = (jnp.dot(xh, wrh_ref[...], preferred_element_type=f32)
                  + jnp.dot(xl, wrh_ref[...], preferred_element_type=f32)
                  + jnp.dot(xh, wrl_ref[...], preferred_element_type=f32)) + br_ref[...]


def _route_kernel(z_ref, route_ref, routet_ref, cnt_ref, carry, *, sub):
    i = pl.program_id(0)
    z = z_ref[...]
    lane = lax.broadcasted_iota(i32, z.shape, 1).astype(f32)
    n_g, epg = float(N_GROUPS), float(EXPERTS_PER_GROUP)
    far = float(2 * ROUTE_LANES)

    gm = lane < n_g
    zg = jnp.where(gm, z, NEG_BIG)
    pg = jnp.where(gm, jnp.exp(zg - jnp.max(zg, axis=-1, keepdims=True)), 0.0)
    pg = pg / jnp.sum(pg, axis=-1, keepdims=True)
    g_top = jnp.max(pg, axis=-1, keepdims=True)
    g_idx = jnp.min(jnp.where(gm & (pg == g_top), lane, far), axis=-1, keepdims=True)

    lo = n_g + g_idx * epg
    em = (lane >= lo) & (lane < lo + epg)
    ze = jnp.where(em, z, NEG_BIG)
    pe = jnp.where(em, jnp.exp(ze - jnp.max(ze, axis=-1, keepdims=True)), 0.0)
    pe = pe / jnp.sum(pe, axis=-1, keepdims=True)
    v1 = jnp.max(jnp.where(em, pe, -1.0), axis=-1, keepdims=True)
    l1 = jnp.min(jnp.where(em & (pe == v1), lane, far), axis=-1, keepdims=True)
    pe2 = jnp.where(em & (lane != l1), pe, -1.0)
    v2 = jnp.max(pe2, axis=-1, keepdims=True)
    l2 = jnp.min(jnp.where(pe2 == v2, lane, far), axis=-1, keepdims=True)
    den = v1 + v2
    gate1 = v1 / den * g_top
    gate2 = v2 / den * g_top

    @pl.when(i == 0)
    def _():
        carry[...] = jnp.zeros_like(carry)

    o1 = lane == l1
    o2 = lane == l2
    onehot = jnp.where(o1 | o2, 1.0, 0.0)
    ri = lax.broadcasted_iota(i32, (sub, sub), 0)
    ci = lax.broadcasted_iota(i32, (sub, sub), 1)
    tri = jnp.where(ci < ri, 1.0, 0.0).astype(bf16)
    counts = carry[0:1, :]
    prefix = []
    for s in range(z.shape[0] // sub):
        oh = onehot[s * sub:(s + 1) * sub, :]
        prefix.append(jnp.dot(tri, oh.astype(bf16), preferred_element_type=f32) + counts)
        counts = counts + jnp.sum(oh, axis=0, keepdims=True)
    prefix = jnp.concatenate(prefix, axis=0)
    rank1 = jnp.sum(jnp.where(o1, prefix, 0.0), axis=-1, keepdims=True)
    rank2 = jnp.sum(jnp.where(o2, prefix, 0.0), axis=-1, keepdims=True)
    new_carry = jnp.broadcast_to(counts, carry.shape)
    carry[...] = new_carry
    cnt_ref[...] = new_carry

    route = jnp.where(lane == 0.0, l1 - n_g, 0.0)
    route = jnp.where(lane == 1.0, l2 - n_g, route)
    route = jnp.where(lane == 2.0, rank1, route)
    route = jnp.where(lane == 3.0, rank2, route)
    route = jnp.where(lane == 4.0, gate1, route)
    route = jnp.where(lane == 5.0, gate2, route)
    route_ref[...] = route
    routet_ref[...] = route.T[0:routet_ref.shape[0], :]


def _outproj(a_p, b_p, a_s, b_s, h, w_out_b, ln1_g, ln1_b,
             wr_hi, wr_lo, b_r, alpha, tm):
    n_p, width = a_p.shape
    n_s = a_s.shape[0]
    d = h.shape[1]
    n_p_tiles = n_p // tm
    n_s_tiles = n_s // tm
    n_tiles = n_p_tiles + n_s_tiles

    def p_map(i):
        return (jnp.minimum(i, n_p_tiles - 1), 0)

    def s_map(i):
        return (jnp.clip(i - n_p_tiles, 0, n_s_tiles - 1), 0)

    def const(shape, **kw):
        return pl.BlockSpec(shape, lambda i: (0,) * len(shape), **kw)

    return pl.pallas_call(
        functools.partial(_outproj_kernel, n_p_tiles=n_p_tiles, alpha=alpha),
        grid=(n_tiles,),
        in_specs=[pl.BlockSpec((tm, width), p_map), pl.BlockSpec((tm, width), p_map),
                  pl.BlockSpec((tm, width), s_map), pl.BlockSpec((tm, width), s_map),
                  pl.BlockSpec((tm, d), lambda i: (i, 0)),
                  const((2 * width, d), pipeline_mode=pl.Buffered(1)),
                  const((1, d)), const((1, d)),
                  const((d, ROUTE_LANES)), const((d, ROUTE_LANES)), const((1, ROUTE_LANES))],
        out_specs=[pl.BlockSpec((tm, d), lambda i: (i, 0)),
                   pl.BlockSpec((tm * (d // 2 // LANES), LANES), lambda i: (i, 0)),
                   pl.BlockSpec((tm, ROUTE_LANES), lambda i: (i, 0))],
        out_shape=[jax.ShapeDtypeStruct((n_p + n_s, d), f32),
                   jax.ShapeDtypeStruct(((n_p + n_s) * (d // 2 // LANES), LANES), jnp.uint32),
                   jax.ShapeDtypeStruct((n_p + n_s, ROUTE_LANES), f32)],
        compiler_params=pltpu.CompilerParams(dimension_semantics=("arbitrary",),
                                             vmem_limit_bytes=VMEM_LIMIT),
        name="outproj",
    )(a_p, b_p, a_s, b_s, h, w_out_b, ln1_g, ln1_b, wr_hi, wr_lo, b_r)


def _route_tile(n):
    return max(t for t in range(ROUTE_SUB, ROUTE_MAX_TILE + 1, ROUTE_SUB) if n % t == 0)


def _route(z):
    n = z.shape[0]
    tr = _route_tile(n)
    return pl.pallas_call(
        functools.partial(_route_kernel, sub=ROUTE_SUB),
        grid=(n // tr,),
        in_specs=[pl.BlockSpec((tr, ROUTE_LANES), lambda i: (i, 0))],
        out_specs=[pl.BlockSpec((tr, ROUTE_LANES), lambda i: (i, 0)),
                   pl.BlockSpec((SUBLANES, tr), lambda i: (0, i)),
                   pl.BlockSpec((SUBLANES, ROUTE_LANES), lambda i: (0, 0))],
        out_shape=[jax.ShapeDtypeStruct((n, ROUTE_LANES), f32),
                   jax.ShapeDtypeStruct((SUBLANES, n), f32),
                   jax.ShapeDtypeStruct((SUBLANES, ROUTE_LANES), f32)],
        scratch_shapes=[pltpu.VMEM((SUBLANES, ROUTE_LANES), f32)],
        compiler_params=pltpu.CompilerParams(dimension_semantics=("arbitrary",),
                                             vmem_limit_bytes=VMEM_LIMIT),
        name="route",
    )(z)


def _expert_kernel(src_ref, dst_ref, start_ref, nblk_ref,
                   h1_hbm, wg_ref, wu_ref, wd_ref, y2_hbm,
                   xbuf, ybuf, wgb, wub, wdb, gsem, osem, *, blk, n_rows):
    e = pl.program_id(0)
    n_e = pl.num_programs(0)
    nb = nblk_ref[e]
    g0 = start_ref[e]

    xs = xbuf.shape[1] // blk
    ys = ybuf.shape[1] // blk

    def gather(g, slot):
        for r in range(blk):
            tok = src_ref[g * blk + r]
            pltpu.make_async_copy(h1_hbm.at[pl.ds(pl.multiple_of(tok * xs, xs), xs), :],
                                  xbuf.at[slot, pl.ds(r * xs, xs), :], gsem.at[slot]
                                  ).start(priority=r % 2)

    def scatter(g, slot):
        for r in range(blk):
            row = dst_ref[(g + 1) * blk + r]
            pltpu.make_async_copy(ybuf.at[slot, pl.ds(r * ys, ys), :],
                                  y2_hbm.at[pl.ds(pl.multiple_of(row * ys, ys), ys), :],
                                  osem.at[slot]).start(priority=r % 2)

    def dump_copy(slot):
        dst = y2_hbm.at[pl.ds((n_rows + slot * blk) * ys, blk * ys), :]
        return pltpu.make_async_copy(ybuf.at[slot], dst, osem.at[slot])

    def wait_gather(slot):
        pltpu.make_async_copy(h1_hbm.at[pl.ds(0, blk * xs), :], xbuf.at[slot], gsem.at[slot]).wait()

    nx, ny = xbuf.shape[0], ybuf.shape[0]
    ahead = nx - 1

    @pl.when(e == 0)
    def _():
        ybuf[...] = jnp.zeros(ybuf.shape, ybuf.dtype)
        for s in range(ny - 1):
            dump_copy(s).start()
        for g in range(ahead):
            gather(g, g)

    @pl.when(nb > 0)
    def _():
        wgb[...] = wg_ref[...].astype(bf16)
        wub[...] = wu_ref[...].astype(bf16)
        wdb[...] = wd_ref[...].astype(bf16)

        def chunk(c, carry):
            g = g0 + c
            slot = g % nx
            yslot = g % ny
            wait_gather(slot)
            xw = [xbuf[slot, pl.ds(s, blk, stride=xs), :] for s in range(xs)]
            x_lo = [lax.bitcast_convert_type(lax.shift_left(w, jnp.uint32(16)), f32) for w in xw]
            x_hi = [lax.bitcast_convert_type(w & jnp.uint32(0xFFFF0000), f32) for w in xw]
            x = jnp.concatenate(x_lo + x_hi, axis=1).astype(bf16)
            hg = jnp.dot(x, wgb[...], preferred_element_type=f32)
            hu = jnp.dot(x, wub[...], preferred_element_type=f32)
            dump_copy(yslot).wait()
            gather(g + ahead, (g + ahead) % nx)
            scatter(g - 1, (g + ny - 1) % ny)
            hid = (hg * jax.nn.sigmoid(hg) * hu).astype(bf16)
            y = jnp.dot(hid, wdb[...], preferred_element_type=f32)
            for s in range(ys):
                ybuf[yslot, pl.ds(s, blk, stride=ys), :] = y[:, s * LANES:(s + 1) * LANES]
            return carry

        lax.fori_loop(0, nb, chunk, 0)

    @pl.when(e == n_e - 1)
    def _():
        g_end = g0 + nb
        scatter(g_end - 1, (g_end + ny - 1) % ny)
        for s in range(ny):
            dump_copy(s).wait()
        for j in range(ahead):
            wait_gather((g_end + j) % nx)


def _experts(h1p, src_rows, dst_rows, pad_start, nblk, w_gate, w_up, w_down, blk):
    n_e, d, d_e = w_gate.shape
    xs, ys = d // 2 // LANES, d // LANES
    n = h1p.shape[0] // xs
    n_rows = TOP_K * n
    grid_spec = pltpu.PrefetchScalarGridSpec(
        num_scalar_prefetch=4,
        grid=(n_e,),
        in_specs=[pl.BlockSpec(memory_space=pl.ANY),
                  pl.BlockSpec((None, d, d_e), lambda e, *_: (e, 0, 0)),
                  pl.BlockSpec((None, d, d_e), lambda e, *_: (e, 0, 0)),
                  pl.BlockSpec((None, d_e, d), lambda e, *_: (e, 0, 0))],
        out_specs=pl.BlockSpec(memory_space=pl.ANY),
        scratch_shapes=[pltpu.VMEM((GATHER_SLOTS, blk * xs, LANES), jnp.uint32),
                        pltpu.VMEM((SCATTER_SLOTS, blk * ys, LANES), f32),
                        pltpu.VMEM((d, d_e), bf16),
                        pltpu.VMEM((d, d_e), bf16),
                        pltpu.VMEM((d_e, d), bf16),
                        pltpu.SemaphoreType.DMA((GATHER_SLOTS,)),
                        pltpu.SemaphoreType.DMA((SCATTER_SLOTS,))],
    )
    return pl.pallas_call(
        functools.partial(_expert_kernel, blk=blk, n_rows=n_rows),
        grid_spec=grid_spec,
        out_shape=jax.ShapeDtypeStruct(((n_rows + SCATTER_SLOTS * blk) * ys, LANES), f32),
        compiler_params=pltpu.CompilerParams(dimension_semantics=("arbitrary",),
                                             vmem_limit_bytes=VMEM_LIMIT),
        name="experts",
    )(src_rows, dst_rows, pad_start, nblk, h1p, w_gate, w_up, w_down)


def _combine_kernel(h1_ref, route_ref, ya_ref, yb_ref, l2g_ref, l2b_ref, yp_ref, ysm_ref,
                    *, n_p_tiles, alpha):
    i = pl.program_id(0)
    route = route_ref[...]
    tm = route.shape[0]
    ys = ya_ref.shape[0] // tm

    def token_rows(ref):
        return jnp.concatenate([ref[pl.ds(s, tm, stride=ys), :] for s in range(ys)], axis=1)

    y = token_rows(ya_ref) * route[:, 4:5] + token_rows(yb_ref) * route[:, 5:6]
    out = _layer_norm(alpha * h1_ref[...] + y, l2g_ref[...], l2b_ref[...])

    @pl.when(i < n_p_tiles)
    def _():
        yp_ref[...] = out

    @pl.when(i >= n_p_tiles)
    def _():
        ysm_ref[...] = out


def _combine(h1, route, y2, ln2_g, ln2_b, n_p, alpha, tm):
    n, d = h1.shape
    n_s = n - n_p
    n_tiles = n // tm
    n_p_tiles = n_p // tm
    n_s_tiles = n_s // tm
    return pl.pallas_call(
        functools.partial(_combine_kernel, n_p_tiles=n_p_tiles, alpha=alpha),
        grid=(n_tiles,),
        in_specs=[pl.BlockSpec((tm, d), lambda i: (i, 0)),
                  pl.BlockSpec((tm, ROUTE_LANES), lambda i: (i, 0)),
                  pl.BlockSpec((tm * (d // LANES), LANES), lambda i: (i, 0)),
                  pl.BlockSpec((tm * (d // LANES), LANES), lambda i: (n_tiles + i, 0)),
                  pl.BlockSpec((1, d), lambda i: (0, 0)),
                  pl.BlockSpec((1, d), lambda i: (0, 0))],
        out_specs=[pl.BlockSpec((tm, d), lambda i: (jnp.minimum(i, n_p_tiles - 1), 0)),
                   pl.BlockSpec((tm, d), lambda i: (jnp.clip(i - n_p_tiles, 0, n_s_tiles - 1), 0))],
        out_shape=[jax.ShapeDtypeStruct((n_p, d), f32), jax.ShapeDtypeStruct((n_s, d), f32)],
        compiler_params=pltpu.CompilerParams(dimension_semantics=("arbitrary",),
                                             vmem_limit_bytes=VMEM_LIMIT),
        name="combine_ln2",
    )(h1, route, y2, y2, ln2_g, ln2_b)


def _shifted_taps(w, n_t, hist):
    k = w.shape[0]
    zero = jnp.zeros_like(w[0])
    wt = jnp.stack([jnp.stack([w[r - t] if 0 <= r - t < k else zero for r in range(hist)])
                    for t in range(n_t)])
    w2 = jnp.stack([jnp.stack([w[hist - t + j] if (j <= t and 0 <= hist - t + j < k) else zero
                               for j in range(n_t)]) for t in range(n_t)])
    return wt, w2


def kernel(x_prompt, x_sample, state_conv_a, state_conv_b, state_lru, meta_tokens, ln_in_g, ln_in_b, w_in, conv_a_w, conv_a_b, norm_a_g, norm_a_b, conv_b_w, conv_b_b, w_rg, b_rg, w_ig, b_ig, lru_lambda, w_out, ln1_g, ln1_b, w_router_group, b_router_group, w_router_expert, b_router_expert, w_gate, w_up, w_down, ln2_g, ln2_b):
    depth = w_in.shape[0]
    assert depth == 1, "single-layer trunk only"
    bp, tp, d = x_prompt.shape
    bs, ts, _ = x_sample.shape
    n_meta = meta_tokens.shape[0]
    width = conv_a_w.shape[2]
    ka, kb = conv_a_w.shape[1], conv_b_w.shape[1]
    n_e = w_gate.shape[1]
    alpha = (2.0 * depth) ** 0.25
    n_p, n_s = bp * tp, bs * ts
    n = n_p + n_s
    assert ts < ka and ts >= kb - 1
    assert n_p % TOKEN_TILE == 0 and n_s % TOKEN_TILE == 0 and tp % SEQ_TILE == 0
    assert bs % STEP_BATCH == 0 and n_p % STEP_BATCH == 0
    assert n_meta % BF16_ROWS == 0 and n_meta <= TOKEN_TILE and n % n_meta == 0
    assert N_GROUPS * (1 + EXPERTS_PER_GROUP) <= ROUTE_LANES and n_e == N_GROUPS * EXPERTS_PER_GROUP

    row = lambda v: v.reshape(1, -1).astype(f32)
    ln_g, ln_b = row(ln_in_g), row(ln_in_b)
    w_in_b = w_in[0].astype(bf16)
    w_out_b = w_out[0].astype(bf16)
    mix_w = (conv_a_w[0], row(conv_a_b[0]), row(norm_a_g[0]), row(norm_a_b[0]),
             conv_b_w[0], row(conv_b_b[0]),
             jnp.concatenate([w_rg[0], w_ig[0]], axis=-1).astype(bf16),
             row(b_rg[0]), row(b_ig[0]), row(lru_lambda[0]))
    ha, hb = _round_up(ka - 1, SUBLANES), _round_up(kb - 1, SUBLANES)

    x_p = x_prompt.reshape(n_p, d)
    x_s = jnp.concatenate([jnp.swapaxes(x_sample, 0, 1).reshape(n_s, d),
                           jnp.pad(meta_tokens, ((0, TOKEN_TILE - n_meta), (0, 0)))], axis=0)
    h, glu, bx, gate = _inproj(x_p, x_s, n_s + TOKEN_TILE, ln_g, ln_b, w_in_b, TOKEN_TILE)

    _, _, sa_m, sb_m, sh_m = _mixer_seq(
        glu, bx, gate, n, jnp.zeros((1, ha, width), f32), jnp.zeros((1, hb, width), f32),
        jnp.zeros((1, 1, width), f32), mix_w, 1, n_meta, n_meta)

    bcast = lambda s: jnp.broadcast_to(s, (bp,) + s.shape[1:])
    a_p, b_p, nsa_p, nsb_p, nsh_p = _mixer_seq(glu, bx, gate, 0, bcast(sa_m), bcast(sb_m), bcast(sh_m),
                                               mix_w, bp, tp, SEQ_TILE)

    row_major = lambda s: jnp.swapaxes(s, 1, 2)
    wat, wa2 = _shifted_taps(conv_a_w[0], ts, ka - 1)
    wbt, wb2 = _shifted_taps(conv_b_w[0], ts, kb - 1)
    a_s, b_s, nsa_s, nsb_s, nsh_s = _mixer_step(
        glu, bx, gate, n_p, row_major(state_conv_a), row_major(state_conv_b), state_lru,
        (wat, wa2, wbt, wb2), mix_w, bs, ts, STEP_BATCH)

    w_r = jnp.concatenate([w_router_group[0], w_router_expert[0]], axis=1)
    w_r = jnp.pad(w_r, ((0, 0), (0, ROUTE_LANES - w_r.shape[1])))
    wr_hi = w_r.astype(bf16)
    wr_lo = (w_r - wr_hi.astype(f32)).astype(bf16)
    b_r = jnp.concatenate([b_router_group[0], b_router_expert[0]])
    b_r = jnp.pad(b_r, (0, ROUTE_LANES - b_r.shape[0])).reshape(1, ROUTE_LANES)
    h1, h1p, logits = _outproj(a_p, b_p, a_s, b_s, h, w_out_b,
                               row(ln1_g[0]), row(ln1_b[0]), wr_hi, wr_lo, b_r, alpha, TOKEN_TILE)
    route, route_t, cnt = _route(logits)

    blk = EXPERT_ROWS
    e_idx = route_t[0:TOP_K].astype(i32)
    rank = route_t[TOP_K:2 * TOP_K].astype(i32)
    counts = cnt[0, N_GROUPS:N_GROUPS + n_e].astype(i32)
    nblk = (counts + blk - 1) // blk
    pad_end = jnp.cumsum(nblk * blk)
    pad_start = pad_end - nblk * blk
    start_of = jnp.sum(jnp.where(e_idx[None] == jnp.arange(n_e, dtype=i32)[:, None, None],
                                 pad_start[:, None, None], 0), axis=0)
    dest = (start_of + rank).reshape(-1)
    p_max = _round_up(TOP_K * n + n_e * (blk - 1), blk)
    flat = jnp.full((p_max,), -1, i32).at[dest].set(jnp.arange(TOP_K * n, dtype=i32))
    pos = jnp.arange(p_max, dtype=i32)
    valid = flat >= 0
    src_rows = jnp.where(valid, flat % n, 0)
    dst_rows = jnp.where(valid, flat, TOP_K * n + pos % (SCATTER_SLOTS * blk))
    src_rows = jnp.concatenate([src_rows, jnp.zeros(((GATHER_SLOTS - 1) * blk,), i32)])
    dst_rows = jnp.concatenate([TOP_K * n + (SCATTER_SLOTS - 1) * blk + jnp.arange(blk, dtype=i32), dst_rows])
    y2 = _experts(h1p, src_rows, dst_rows, (pad_start // blk).astype(i32), nblk,
                  w_gate[0], w_up[0], w_down[0], blk)
    y_p, y_s = _combine(h1, route, y2, row(ln2_g[0]), row(ln2_b[0]), n_p, alpha, TOKEN_TILE)

    y_prompt = y_p.reshape(bp, tp, d)
    y_sample = jnp.swapaxes(y_s.reshape(ts, bs, d), 0, 1)
    return (y_prompt, y_sample,
            nsa_p[:, ha - (ka - 1):][None], nsb_p[:, hb - (kb - 1):][None], nsh_p.reshape(1, bp, width),
            row_major(nsa_s), row_major(nsb_s), nsh_s)
```

```python
import functools

import jax
import jax.numpy as jnp
from jax import lax
from jax.experimental import pallas as pl
from jax.experimental.pallas import tpu as pltpu

f32 = jnp.float32
bf16 = jnp.bfloat16
i32 = jnp.int32

CONV_HEADS_A = 8
LRU_HEADS = 8
LRU_C = 8.0
N_GROUPS = 8
EXPERTS_PER_GROUP = 8
TOP_K = 2
LN_EPS = 1e-5

SUBLANES = 8
LANES = 128
BF16_ROWS = 16
VMEM_LIMIT = 56 * 1024 * 1024

TOKEN_TILE = 256
COMBINE_TILE = 512
SEQ_TILE = 256
CONV_ROWS = 64
STEP_BATCH = 32
EXPERT_ROWS = 128
ROUTE_LANES = 128
ROUTE_SUB = 128
ROUTE_MAX_TILE = 2304
GATHER_SLOTS = 4
SCATTER_SLOTS = 4
NEG_BIG = -1e30


def _round_up(x, m):
    return (x + m - 1) // m * m


def _layer_norm(x, g, b):
    mu = jnp.mean(x, axis=-1, keepdims=True)
    xc = x - mu
    var = jnp.mean(xc * xc, axis=-1, keepdims=True)
    return xc * lax.rsqrt(var + LN_EPS) * g + b


def _group_norm_silu(x, g, b, group):
    parts = []
    for c in range(x.shape[1] // group):
        xg = x[:, c * group:(c + 1) * group]
        mu = jnp.mean(xg, axis=-1, keepdims=True)
        xc = xg - mu
        var = jnp.mean(xc * xc, axis=-1, keepdims=True)
        parts.append(xc * lax.rsqrt(var + LN_EPS))
    y = jnp.concatenate(parts, axis=-1) * g + b
    return y * jax.nn.sigmoid(y)


def _lru_gates(cb, wg_ref, brg, big, lam):
    heads, hd, _ = wg_ref.shape
    cbb = cb.astype(bf16)
    rs, iz = [], []
    for h in range(heads):
        z = jnp.dot(cbb[:, h * hd:(h + 1) * hd], wg_ref[h], preferred_element_type=f32)
        rs.append(z[:, :hd])
        iz.append(z[:, hd:])
    r = jax.nn.sigmoid(jnp.concatenate(rs, axis=-1) + brg)
    i = jax.nn.sigmoid(jnp.concatenate(iz, axis=-1) + big)
    log_a = -LRU_C * r * jax.nn.softplus(-lam)
    a = jnp.exp(log_a)
    u = jnp.sqrt(-jnp.tanh(log_a) * (a * a + 1.0)) * (i * cb)
    return a, u


def _inproj_kernel(xa_ref, xb_ref, g_ref, b_ref, w_ref, h_ref, glu_ref, bx_ref, gate_ref, *, n_a_tiles):
    i = pl.program_id(0)
    x = jnp.where(i < n_a_tiles, xa_ref[...], xb_ref[...])
    h = _layer_norm(x, g_ref[...], b_ref[...])
    h_ref[...] = h
    xn = h.astype(bf16)
    width = glu_ref.shape[1]
    ch = min(width, 512)
    for c in range(width // ch):
        lo = c * ch
        av = jnp.dot(xn, w_ref[:, lo:lo + ch], preferred_element_type=f32)
        ag = jnp.dot(xn, w_ref[:, width + lo:width + lo + ch], preferred_element_type=f32)
        glu_ref[:, lo:lo + ch] = av * jax.nn.sigmoid(ag)
        bx_ref[:, lo:lo + ch] = jnp.dot(xn, w_ref[:, 2 * width + lo:2 * width + lo + ch],
                                        preferred_element_type=f32)
        bg = jnp.dot(xn, w_ref[:, 3 * width + lo:3 * width + lo + ch], preferred_element_type=f32)
        gate_ref[:, lo:lo + ch] = jax.nn.gelu(bg)


def _inproj(xa, xb, n_b_rows, ln_g, ln_b, w_in_b, tm):
    na, d = xa.shape
    width = w_in_b.shape[1] // 4
    n_a_tiles = na // tm
    n_tiles = (na + n_b_rows) // tm
    n_b_tiles = max(n_tiles - n_a_tiles, 1)
    row_out = pl.BlockSpec((tm, width), lambda i: (i, 0))
    return pl.pallas_call(
        functools.partial(_inproj_kernel, n_a_tiles=n_a_tiles),
        grid=(n_tiles,),
        in_specs=[
            pl.BlockSpec((tm, d), lambda i: (jnp.minimum(i, n_a_tiles - 1), 0)),
            pl.BlockSpec((tm, d), lambda i: (jnp.clip(i - n_a_tiles, 0, n_b_tiles - 1), 0)),
            pl.BlockSpec((1, d), lambda i: (0, 0)),
            pl.BlockSpec((1, d), lambda i: (0, 0)),
            pl.BlockSpec((d, 4 * width), lambda i: (0, 0), pipeline_mode=pl.Buffered(1)),
        ],
        out_specs=[pl.BlockSpec((tm, d), lambda i: (i, 0)), row_out, row_out, row_out],
        out_shape=[jax.ShapeDtypeStruct((na + n_b_rows, d), f32)]
        + [jax.ShapeDtypeStruct((na + n_b_rows, width), f32)] * 3,
        compiler_params=pltpu.CompilerParams(dimension_semantics=("arbitrary",),
                                             vmem_limit_bytes=VMEM_LIMIT),
        name="inproj",
    )(xa, xb, ln_g, ln_b, w_in_b)


def _mixer_seq_kernel(glu_ref, bx_ref, gate_ref, sa_ref, sb_ref, sh_ref,
                      wa_ref, ba_ref, ng_ref, nb_ref, wb_ref, bb_ref, wg_ref, brg_ref, big_ref, lam_ref,
                      aout_ref, bout_ref, nsa_ref, nsb_ref, nsh_ref,
                      wina, winb, xsa, xsb, ca_s, cb_s, a_s, u_s, hcar):
    tau = pl.program_id(1)
    tt, width = glu_ref.shape
    ka, kb = wa_ref.shape[0], wb_ref.shape[0]
    ha, hb = sa_ref.shape[0], sb_ref.shape[0]

    @pl.when(tau == 0)
    def _():
        wina[0:ha, :] = sa_ref[...]
        winb[0:hb, :] = sb_ref[...]
        hcar[...] = jnp.broadcast_to(sh_ref[...], hcar.shape)

    wina[ha:ha + tt, :] = glu_ref[...]
    winb[hb:hb + tt, :] = bx_ref[...]
    for s in sorted({(ha - (ka - 1) + k) % SUBLANES for k in range(ka)} - {0}):
        xsa[s - 1] = wina[pl.ds(s, tt + ha - SUBLANES), :]
    for s in sorted({(hb - (kb - 1) + k) % SUBLANES for k in range(kb)} - {0}):
        xsb[s - 1] = winb[pl.ds(s, tt + hb - SUBLANES), :]

    def tap(win, xs, off, r0, lanes):
        q, s = divmod(off, SUBLANES)
        src = win if s == 0 else xs.at[s - 1]
        return src[pl.ds(r0 + SUBLANES * q, SUBLANES), lanes]

    rows = min(CONV_ROWS, tt)
    for g in range(width // LANES):
        lanes = pl.ds(g * LANES, LANES)
        bcast = lambda ref, k: jnp.broadcast_to(ref[k:k + 1, lanes], (SUBLANES, LANES))
        wa_k = [bcast(wa_ref, k) for k in range(ka)]
        wb_k = [bcast(wb_ref, k) for k in range(kb)]
        ba_g, bb_g = bcast(ba_ref, 0), bcast(bb_ref, 0)

        def conv_block(rb, carry):
            for j in range(rows // SUBLANES):
                r0 = pl.multiple_of(rb * rows, rows) + j * SUBLANES
                acc = ba_g
                for k in range(ka):
                    acc = acc + wa_k[k] * tap(wina, xsa, ha - (ka - 1) + k, r0, lanes)
                ca_s[pl.ds(r0, SUBLANES), lanes] = acc
                accb = bb_g
                for k in range(kb):
                    accb = accb + wb_k[k] * tap(winb, xsb, hb - (kb - 1) + k, r0, lanes)
                cb_s[pl.ds(r0, SUBLANES), lanes] = accb
            return carry

        lax.fori_loop(0, tt // rows, conv_block, 0)

    a_out = _group_norm_silu(ca_s[...], ng_ref[...], nb_ref[...], width // CONV_HEADS_A)
    aout_ref[...] = a_out.astype(aout_ref.dtype)

    a, u = _lru_gates(cb_s[...], wg_ref, brg_ref[...], big_ref[...], lam_ref[...])
    row8 = lax.broadcasted_iota(i32, (tt, width), 0) & (SUBLANES - 1)
    d = 1
    while d < SUBLANES:
        m = row8 >= d
        a_sh = jnp.where(m, pltpu.roll(a, d, axis=0), 1.0)
        u_sh = jnp.where(m, pltpu.roll(u, d, axis=0), 0.0)
        u = a * u_sh + u
        a = a * a_sh
        d *= 2
    a_s[...] = a
    u_s[...] = u

    def scan_block(j, h):
        r0 = pl.multiple_of(j * SUBLANES, SUBLANES)
        hblk = a_s[pl.ds(r0, SUBLANES), :] * h + u_s[pl.ds(r0, SUBLANES), :]
        u_s[pl.ds(r0, SUBLANES), :] = hblk
        return jnp.broadcast_to(hblk[SUBLANES - 1:SUBLANES, :], h.shape)

    h_last = lax.fori_loop(0, tt // SUBLANES, scan_block, hcar[...])
    hcar[...] = h_last
    bout_ref[...] = (u_s[...] * gate_ref[...]).astype(bout_ref.dtype)

    tail_a = wina[tt:tt + ha, :]
    tail_b = winb[tt:tt + hb, :]
    wina[0:ha, :] = tail_a
    winb[0:hb, :] = tail_b
    nsa_ref[...] = tail_a
    nsb_ref[...] = tail_b
    nsh_ref[...] = h_last[0:1, :]


def _mixer_seq(glu, bx, gate, row_off, sa_pad, sb_pad, sh, mix_w, n_seq, seq_len, tt):
    wa, ba, ng, nb, wb, bb, wg, brg, big, lam = mix_w
    width = glu.shape[1]
    ha, hb = sa_pad.shape[1], sb_pad.shape[1]
    n_t = seq_len // tt
    off = row_off // tt
    row_in = pl.BlockSpec((tt, width), lambda b, t: (off + b * n_t + t, 0))
    row_out = pl.BlockSpec((tt, width), lambda b, t: (b * n_t + t, 0))

    def const(shape):
        return pl.BlockSpec(shape, lambda b, t: (0,) * len(shape))

    def per_seq(r):
        return pl.BlockSpec((None, r, width), lambda b, t: (b, 0, 0))

    return pl.pallas_call(
        _mixer_seq_kernel,
        grid=(n_seq, n_t),
        in_specs=[row_in, row_in, row_in, per_seq(ha), per_seq(hb), per_seq(1),
                  const(wa.shape), const(ba.shape), const(ng.shape), const(nb.shape),
                  const(wb.shape), const(bb.shape), const(wg.shape), const(brg.shape),
                  const(big.shape), const(lam.shape)],
        out_specs=[row_out, row_out, per_seq(ha), per_seq(hb), per_seq(1)],
        out_shape=[jax.ShapeDtypeStruct((n_seq * seq_len, width), bf16),
                   jax.ShapeDtypeStruct((n_seq * seq_len, width), bf16),
                   jax.ShapeDtypeStruct((n_seq, ha, width), f32),
                   jax.ShapeDtypeStruct((n_seq, hb, width), f32),
                   jax.ShapeDtypeStruct((n_seq, 1, width), f32)],
        scratch_shapes=[pltpu.VMEM((ha + tt, width), f32),
                        pltpu.VMEM((hb + tt, width), f32),
                        pltpu.VMEM((SUBLANES - 1, tt + ha - SUBLANES, width), f32),
                        pltpu.VMEM((SUBLANES - 1, tt + hb - SUBLANES, width), f32),
                        pltpu.VMEM((tt, width), f32),
                        pltpu.VMEM((tt, width), f32),
                        pltpu.VMEM((tt, width), f32),
                        pltpu.VMEM((tt, width), f32),
                        pltpu.VMEM((SUBLANES, width), f32)],
        compiler_params=pltpu.CompilerParams(dimension_semantics=("arbitrary", "arbitrary"),
                                             vmem_limit_bytes=VMEM_LIMIT),
        name="mixer_seq",
    )(glu, bx, gate, sa_pad, sb_pad, sh, wa, ba, ng, nb, wb, bb, wg, brg, big, lam)


def _mixer_step_kernel(glu_ref, bx_ref, gate_ref, sa_ref, sb_ref, sh_ref,
                       wat_ref, wa2_ref, wbt_ref, wb2_ref,
                       ba_ref, ng_ref, nb_ref, bb_ref, wg_ref, brg_ref, big_ref, lam_ref,
                       aout_ref, bout_ref, nsa_ref, nsb_ref, nsh_ref,
                       hista, histb, hcar):
    t = pl.program_id(1)
    n_t = hista.shape[0]
    width = glu_ref.shape[1]

    def roll_buffer(s_ref, ns_ref, x_ref):
        shift = n_t - s_ref.shape[0]

        @pl.when(t == 0)
        def _():
            for r in range(max(-shift, 0)):
                ns_ref[r] = s_ref[r + n_t]

        if shift <= 0:
            ns_ref[t - shift] = x_ref[...]
        else:
            @pl.when(t >= shift)
            def _():
                ns_ref[t - shift] = x_ref[...]

    @pl.when(t == 0)
    def _():
        hista[...] = jnp.zeros_like(hista)
        histb[...] = jnp.zeros_like(histb)
        hcar[...] = sh_ref[...]

    roll_buffer(sa_ref, nsa_ref, glu_ref)
    roll_buffer(sb_ref, nsb_ref, bx_ref)
    hista[t] = glu_ref[...]
    histb[t] = bx_ref[...]

    ca = jnp.broadcast_to(ba_ref[...], glu_ref.shape)
    for r in range(sa_ref.shape[0]):
        ca = ca + wat_ref[r:r + 1, :] * sa_ref[r]
    cb = jnp.broadcast_to(bb_ref[...], bx_ref.shape)
    for r in range(sb_ref.shape[0]):
        cb = cb + wbt_ref[r:r + 1, :] * sb_ref[r]
    for j in range(n_t):
        ca = ca + wa2_ref[j:j + 1, :] * hista[j]
        cb = cb + wb2_ref[j:j + 1, :] * histb[j]
    a_out = _group_norm_silu(ca, ng_ref[...], nb_ref[...], width // CONV_HEADS_A)
    aout_ref[...] = a_out.astype(aout_ref.dtype)

    a, u = _lru_gates(cb, wg_ref, brg_ref[...], big_ref[...], lam_ref[...])
    h = a * hcar[...] + u
    hcar[...] = h
    nsh_ref[...] = h
    bout_ref[...] = (h * gate_ref[...]).astype(bout_ref.dtype)


def _mixer_step(glu, bx, gate, row_off, sa, sb, sh, step_w, mix_w, n_seq, n_t, bt):
    wat, wa2, wbt, wb2 = step_w
    wa, ba, ng, nb, wb, bb, wg, brg, big, lam = mix_w
    width = glu.shape[1]
    ka1, kb1 = sa.shape[1], sb.shape[1]
    off = row_off // bt
    nb_tiles = n_seq // bt
    row_in = pl.BlockSpec((bt, width), lambda j, t: (off + t * nb_tiles + j, 0))
    row_out = pl.BlockSpec((bt, width), lambda j, t: (t * nb_tiles + j, 0))

    def const(shape):
        return pl.BlockSpec(shape, lambda j, t: (0,) * len(shape))

    def per_t(arr):
        return pl.BlockSpec((None,) + arr.shape[1:], lambda j, t: (t, 0, 0))

    def state(r):
        return pl.BlockSpec((None, r, bt, width), lambda j, t: (0, 0, j, 0))

    state_h = pl.BlockSpec((None, bt, width), lambda j, t: (0, j, 0))
    return pl.pallas_call(
        _mixer_step_kernel,
        grid=(nb_tiles, n_t),
        in_specs=[row_in, row_in, row_in, state(ka1), state(kb1), state_h,
                  per_t(wat), per_t(wa2), per_t(wbt), per_t(wb2),
                  const(ba.shape), const(ng.shape), const(nb.shape), const(bb.shape),
                  const(wg.shape), const(brg.shape), const(big.shape), const(lam.shape)],
        out_specs=[row_out, row_out, state(ka1), state(kb1), state_h],
        out_shape=[jax.ShapeDtypeStruct((n_seq * n_t, width), bf16),
                   jax.ShapeDtypeStruct((n_seq * n_t, width), bf16),
                   jax.ShapeDtypeStruct((1, ka1, n_seq, width), f32),
                   jax.ShapeDtypeStruct((1, kb1, n_seq, width), f32),
                   jax.ShapeDtypeStruct((1, n_seq, width), f32)],
        scratch_shapes=[pltpu.VMEM((n_t, bt, width), f32),
                        pltpu.VMEM((n_t, bt, width), f32),
                        pltpu.VMEM((bt, width), f32)],
        compiler_params=pltpu.CompilerParams(dimension_semantics=("arbitrary", "arbitrary"),
                                             vmem_limit_bytes=VMEM_LIMIT),
        name="mixer_step",
    )(glu, bx, gate, sa, sb, sh, wat, wa2, wbt, wb2,
      ba, ng, nb, bb, wg, brg, big, lam)


def _outproj_kernel(ap_ref, bp_ref, as_ref, bs_ref, h_ref, wo_ref,
                    l1g_ref, l1b_ref, wrh_ref, wrl_ref, br_ref,
                    h1_ref, h1p_ref, z_ref, *, n_p_tiles, alpha):
    i = pl.program_id(0)
    is_p = i < n_p_tiles
    a = jnp.where(is_p, ap_ref[...], as_ref[...])
    b = jnp.where(is_p, bp_ref[...], bs_ref[...])
    width = a.shape[1]
    mix = (jnp.dot(a, wo_ref[0:width, :], preferred_element_type=f32)
           + jnp.dot(b, wo_ref[width:2 * width, :], preferred_element_type=f32))
    h1 = _layer_norm(alpha * h_ref[...] + mix, l1g_ref[...], l1b_ref[...])
    h1_ref[...] = h1

    xh = h1.astype(bf16)
    xh32 = xh.astype(f32)
    half = h1.shape[1] // 2
    lo = lax.shift_right_logical(lax.bitcast_convert_type(xh32[:, :half], jnp.uint32), jnp.uint32(16))
    hi = lax.bitcast_convert_type(xh32[:, half:], jnp.uint32) & jnp.uint32(0xFFFF0000)
    word = lo | hi
    n_sub = half // LANES
    for s in range(n_sub):
        h1p_ref[pl.ds(s, word.shape[0], stride=n_sub), :] = word[:, s * LANES:(s + 1) * LANES]

    xl = (h1 - xh32).astype(bf16)
    z_ref[...] = (jnp.dot(xh, wrh_ref[...], preferred_element_type=f32)
                  + jnp.dot(xl, wrh_ref[...], preferred_element_type=f32)
                  + jnp.dot(xh, wrl_ref[...], preferred_element_type=f32)) + br_ref[...]


def _route_kernel(z_ref, route_ref, routet_ref, cnt_ref, carry, *, sub):
    i = pl.program_id(0)
    z = z_ref[...]
    lane = lax.broadcasted_iota(i32, z.shape, 1).astype(f32)
    n_g, epg = float(N_GROUPS), float(EXPERTS_PER_GROUP)
    far = float(2 * ROUTE_LANES)

    gm = lane < n_g
    zg = jnp.where(gm, z, NEG_BIG)
    pg = jnp.where(gm, jnp.exp(zg - jnp.max(zg, axis=-1, keepdims=True)), 0.0)
    pg = pg / jnp.sum(pg, axis=-1, keepdims=True)
    g_top = jnp.max(pg, axis=-1, keepdims=True)
    g_idx = jnp.min(jnp.where(gm & (pg == g_top), lane, far), axis=-1, keepdims=True)

    lo = n_g + g_idx * epg
    em = (lane >= lo) & (lane < lo + epg)
    ze = jnp.where(em, z, NEG_BIG)
    pe = jnp.where(em, jnp.exp(ze - jnp.max(ze, axis=-1, keepdims=True)), 0.0)
    pe = pe / jnp.sum(pe, axis=-1, keepdims=True)
    v1 = jnp.max(jnp.where(em, pe, -1.0), axis=-1, keepdims=True)
    l1 = jnp.min(jnp.where(em & (pe == v1), lane, far), axis=-1, keepdims=True)
    pe2 = jnp.where(em & (lane != l1), pe, -1.0)
    v2 = jnp.max(pe2, axis=-1, keepdims=True)
    l2 = jnp.min(jnp.where(pe2 == v2, lane, far), axis=-1, keepdims=True)
    den = v1 + v2
    gate1 = v1 / den * g_top
    gate2 = v2 / den * g_top

    @pl.when(i == 0)
    def _():
        carry[...] = jnp.zeros_like(carry)

    o1 = lane == l1
    o2 = lane == l2
    onehot = jnp.where(o1 | o2, 1.0, 0.0)
    ri = lax.broadcasted_iota(i32, (sub, sub), 0)
    ci = lax.broadcasted_iota(i32, (sub, sub), 1)
    tri = jnp.where(ci < ri, 1.0, 0.0).astype(bf16)
    counts = carry[0:1, :]
    prefix = []
    for s in range(z.shape[0] // sub):
        oh = onehot[s * sub:(s + 1) * sub, :]
        prefix.append(jnp.dot(tri, oh.astype(bf16), preferred_element_type=f32) + counts)
        counts = counts + jnp.sum(oh, axis=0, keepdims=True)
    prefix = jnp.concatenate(prefix, axis=0)
    rank1 = jnp.sum(jnp.where(o1, prefix, 0.0), axis=-1, keepdims=True)
    rank2 = jnp.sum(jnp.where(o2, prefix, 0.0), axis=-1, keepdims=True)
    new_carry = jnp.broadcast_to(counts, carry.shape)
    carry[...] = new_carry
    cnt_ref[...] = new_carry

    route = jnp.where(lane == 0.0, l1 - n_g, 0.0)
    route = jnp.where(lane == 1.0, l2 - n_g, route)
    route = jnp.where(lane == 2.0, rank1, route)
    route = jnp.where(lane == 3.0, rank2, route)
    route = jnp.where(lane == 4.0, gate1, route)
    route = jnp.where(lane == 5.0, gate2, route)
    route_ref[...] = route
    routet_ref[...] = route.T[0:routet_ref.shape[0], :]


def _outproj(a_p, b_p, a_s, b_s, h, w_out_b, ln1_g, ln1_b,
             wr_hi, wr_lo, b_r, alpha, tm):
    n_p, width = a_p.shape
    n_s = a_s.shape[0]
    d = h.shape[1]
    n_p_tiles = n_p // tm
    n_s_tiles = n_s // tm
    n_tiles = n_p_tiles + n_s_tiles

    def p_map(i):
        return (jnp.minimum(i, n_p_tiles - 1), 0)

    def s_map(i):
        return (jnp.clip(i - n_p_tiles, 0, n_s_tiles - 1), 0)

    def const(shape, **kw):
        return pl.BlockSpec(shape, lambda i: (0,) * len(shape), **kw)

    return pl.pallas_call(
        functools.partial(_outproj_kernel, n_p_tiles=n_p_tiles, alpha=alpha),
        grid=(n_tiles,),
        in_specs=[pl.BlockSpec((tm, width), p_map), pl.BlockSpec((tm, width), p_map),
                  pl.BlockSpec((tm, width), s_map), pl.BlockSpec((tm, width), s_map),
                  pl.BlockSpec((tm, d), lambda i: (i, 0)),
                  const((2 * width, d), pipeline_mode=pl.Buffered(1)),
                  const((1, d)), const((1, d)),
                  const((d, ROUTE_LANES)), const((d, ROUTE_LANES)), const((1, ROUTE_LANES))],
        out_specs=[pl.BlockSpec((tm, d), lambda i: (i, 0)),
                   pl.BlockSpec((tm * (d // 2 // LANES), LANES), lambda i: (i, 0)),
                   pl.BlockSpec((tm, ROUTE_LANES), lambda i: (i, 0))],
        out_shape=[jax.ShapeDtypeStruct((n_p + n_s, d), f32),
                   jax.ShapeDtypeStruct(((n_p + n_s) * (d // 2 // LANES), LANES), jnp.uint32),
                   jax.ShapeDtypeStruct((n_p + n_s, ROUTE_LANES), f32)],
        compiler_params=pltpu.CompilerParams(dimension_semantics=("arbitrary",),
                                             vmem_limit_bytes=VMEM_LIMIT),
        name="outproj",
    )(a_p, b_p, a_s, b_s, h, w_out_b, ln1_g, ln1_b, wr_hi, wr_lo, b_r)


def _route_tile(n):
    return max(t for t in range(ROUTE_SUB, ROUTE_MAX_TILE + 1, ROUTE_SUB) if n % t == 0)


def _route(z):
    n = z.shape[0]
    tr = _route_tile(n)
    return pl.pallas_call(
        functools.partial(_route_kernel, sub=ROUTE_SUB),
        grid=(n // tr,),
        in_specs=[pl.BlockSpec((tr, ROUTE_LANES), lambda i: (i, 0))],
        out_specs=[pl.BlockSpec((tr, ROUTE_LANES), lambda i: (i, 0)),
                   pl.BlockSpec((SUBLANES, tr), lambda i: (0, i)),
                   pl.BlockSpec((SUBLANES, ROUTE_LANES), lambda i: (0, 0))],
        out_shape=[jax.ShapeDtypeStruct((n, ROUTE_LANES), f32),
                   jax.ShapeDtypeStruct((SUBLANES, n), f32),
                   jax.ShapeDtypeStruct((SUBLANES, ROUTE_LANES), f32)],
        scratch_shapes=[pltpu.VMEM((SUBLANES, ROUTE_LANES), f32)],
        compiler_params=pltpu.CompilerParams(dimension_semantics=("arbitrary",),
                                             vmem_limit_bytes=VMEM_LIMIT),
        name="route",
    )(z)


def _expert_kernel(src_ref, dst_ref, start_ref, nblk_ref,
                   h1_hbm, wg_ref, wu_ref, wd_ref, y2_hbm,
                   xbuf, ybuf, wgb, wub, wdb, gsem, osem, *, blk, n_rows):
    e = pl.program_id(0)
    n_e = pl.num_programs(0)
    nb = nblk_ref[e]
    g0 = start_ref[e]

    xs = xbuf.shape[1] // blk
    ys = ybuf.shape[1] // blk

    def gather(g, slot):
        for r in range(blk):
            tok = src_ref[g * blk + r]
            pltpu.make_async_copy(h1_hbm.at[pl.ds(pl.multiple_of(tok * xs, xs), xs), :],
                                  xbuf.at[slot, pl.ds(r * xs, xs), :], gsem.at[slot]
                                  ).start(priority=r % 2)

    def scatter(g, slot):
        for r in range(blk):
            row = dst_ref[(g + 1) * blk + r]
            pltpu.make_async_copy(ybuf.at[slot, pl.ds(r * ys, ys), :],
                                  y2_hbm.at[pl.ds(pl.multiple_of(row * ys, ys), ys), :],
                                  osem.at[slot]).start(priority=r % 2)

    def dump_copy(slot):
        dst = y2_hbm.at[pl.ds((n_rows + slot * blk) * ys, blk * ys), :]
        return pltpu.make_async_copy(ybuf.at[slot], dst, osem.at[slot])

    def wait_gather(slot):
        pltpu.make_async_copy(h1_hbm.at[pl.ds(0, blk * xs), :], xbuf.at[slot], gsem.at[slot]).wait()

    nx, ny = xbuf.shape[0], ybuf.shape[0]
    ahead = nx - 1

    @pl.when(e == 0)
    def _():
        ybuf[...] = jnp.zeros(ybuf.shape, ybuf.dtype)
        for s in range(ny - 1):
            dump_copy(s).start()
        for g in range(ahead):
            gather(g, g)

    @pl.when(nb > 0)
    def _():
        wgb[...] = wg_ref[...].astype(bf16)
        wub[...] = wu_ref[...].astype(bf16)
        wdb[...] = wd_ref[...].astype(bf16)

        def chunk(c, carry):
            g = g0 + c
            slot = g % nx
            yslot = g % ny
            wait_gather(slot)
            dump_copy(yslot).wait()
            gather(g + ahead, (g + ahead) % nx)
            scatter(g - 1, (g + ny - 1) % ny)
            xw = [xbuf[slot, pl.ds(s, blk, stride=xs), :] for s in range(xs)]
            x_lo = [lax.bitcast_convert_type(lax.shift_left(w, jnp.uint32(16)), f32) for w in xw]
            x_hi = [lax.bitcast_convert_type(w & jnp.uint32(0xFFFF0000), f32) for w in xw]
            x = jnp.concatenate(x_lo + x_hi, axis=1).astype(bf16)
            hg = jnp.dot(x, wgb[...], preferred_element_type=f32)
            hu = jnp.dot(x, wub[...], preferred_element_type=f32)
            hid = (hg * jax.nn.sigmoid(hg) * hu).astype(bf16)
            y = jnp.dot(hid, wdb[...], preferred_element_type=f32)
            for s in range(ys):
                ybuf[yslot, pl.ds(s, blk, stride=ys), :] = y[:, s * LANES:(s + 1) * LANES]
            return carry

        lax.fori_loop(0, nb, chunk, 0)

    @pl.when(e == n_e - 1)
    def _():
        g_end = g0 + nb
        scatter(g_end - 1, (g_end + ny - 1) % ny)
        for s in range(ny):
            dump_copy(s).wait()
        for j in range(ahead):
            wait_gather((g_end + j) % nx)


def _experts(h1p, src_rows, dst_rows, pad_start, nblk, w_gate, w_up, w_down, blk):
    n_e, d, d_e = w_gate.shape
    xs, ys = d // 2 // LANES, d // LANES
    n = h1p.shape[0] // xs
    n_rows = TOP_K * n
    grid_spec = pltpu.PrefetchScalarGridSpec(
        num_scalar_prefetch=4,
        grid=(n_e,),
        in_specs=[pl.BlockSpec(memory_space=pl.ANY),
                  pl.BlockSpec((None, d, d_e), lambda e, *_: (e, 0, 0)),
                  pl.BlockSpec((None, d, d_e), lambda e, *_: (e, 0, 0)),
                  pl.BlockSpec((None, d_e, d), lambda e, *_: (e, 0, 0))],
        out_specs=pl.BlockSpec(memory_space=pl.ANY),
        scratch_shapes=[pltpu.VMEM((GATHER_SLOTS, blk * xs, LANES), jnp.uint32),
                        pltpu.VMEM((SCATTER_SLOTS, blk * ys, LANES), f32),
                        pltpu.VMEM((d, d_e), bf16),
                        pltpu.VMEM((d, d_e), bf16),
                        pltpu.VMEM((d_e, d), bf16),
                        pltpu.SemaphoreType.DMA((GATHER_SLOTS,)),
                        pltpu.SemaphoreType.DMA((SCATTER_SLOTS,))],
    )
    return pl.pallas_call(
        functools.partial(_expert_kernel, blk=blk, n_rows=n_rows),
        grid_spec=grid_spec,
        out_shape=jax.ShapeDtypeStruct(((n_rows + SCATTER_SLOTS * blk) * ys, LANES), f32),
        compiler_params=pltpu.CompilerParams(dimension_semantics=("arbitrary",),
                                             vmem_limit_bytes=VMEM_LIMIT),
        name="experts",
    )(src_rows, dst_rows, pad_start, nblk, h1p, w_gate, w_up, w_down)


def _combine_kernel(h1_ref, route_ref, ya_ref, yb_ref, l2g_ref, l2b_ref, yp_ref, ysm_ref,
                    *, n_p_tiles, alpha):
    i = pl.program_id(0)
    route = route_ref[...]
    tm = route.shape[0]
    ys = ya_ref.shape[0] // tm

    def token_rows(ref):
        return jnp.concatenate([ref[pl.ds(s, tm, stride=ys), :] for s in range(ys)], axis=1)

    y = token_rows(ya_ref) * route[:, 4:5] + token_rows(yb_ref) * route[:, 5:6]
    out = _layer_norm(alpha * h1_ref[...] + y, l2g_ref[...], l2b_ref[...])

    @pl.when(i < n_p_tiles)
    def _():
        yp_ref[...] = out

    @pl.when(i >= n_p_tiles)
    def _():
        ysm_ref[...] = out


def _combine(h1, route, y2, ln2_g, ln2_b, n_p, alpha, tm):
    n, d = h1.shape
    n_s = n - n_p
    n_tiles = n // tm
    n_p_tiles = n_p // tm
    n_s_tiles = n_s // tm
    return pl.pallas_call(
        functools.partial(_combine_kernel, n_p_tiles=n_p_tiles, alpha=alpha),
        grid=(n_tiles,),
        in_specs=[pl.BlockSpec((tm, d), lambda i: (i, 0)),
                  pl.BlockSpec((tm, ROUTE_LANES), lambda i: (i, 0)),
                  pl.BlockSpec((tm * (d // LANES), LANES), lambda i: (i, 0)),
                  pl.BlockSpec((tm * (d // LANES), LANES), lambda i: (n_tiles + i, 0)),
                  pl.BlockSpec((1, d), lambda i: (0, 0)),
                  pl.BlockSpec((1, d), lambda i: (0, 0))],
        out_specs=[pl.BlockSpec((tm, d), lambda i: (jnp.minimum(i, n_p_tiles - 1), 0)),
                   pl.BlockSpec((tm, d), lambda i: (jnp.clip(i - n_p_tiles, 0, n_s_tiles - 1), 0))],
        out_shape=[jax.ShapeDtypeStruct((n_p, d), f32), jax.ShapeDtypeStruct((n_s, d), f32)],
        compiler_params=pltpu.CompilerParams(dimension_semantics=("arbitrary",),
                                             vmem_limit_bytes=VMEM_LIMIT),
        name="combine_ln2",
    )(h1, route, y2, y2, ln2_g, ln2_b)


def _shifted_taps(w, n_t, hist):
    k = w.shape[0]
    zero = jnp.zeros_like(w[0])
    wt = jnp.stack([jnp.stack([w[r - t] if 0 <= r - t < k else zero for r in range(hist)])
                    for t in range(n_t)])
    w2 = jnp.stack([jnp.stack([w[hist - t + j] if (j <= t and 0 <= hist - t + j < k) else zero
                               for j in range(n_t)]) for t in range(n_t)])
    return wt, w2


def kernel(x_prompt, x_sample, state_conv_a, state_conv_b, state_lru, meta_tokens, ln_in_g, ln_in_b, w_in, conv_a_w, conv_a_b, norm_a_g, norm_a_b, conv_b_w, conv_b_b, w_rg, b_rg, w_ig, b_ig, lru_lambda, w_out, ln1_g, ln1_b, w_router_group, b_router_group, w_router_expert, b_router_expert, w_gate, w_up, w_down, ln2_g, ln2_b):
    depth = w_in.shape[0]
    assert depth == 1, "single-layer trunk only"
    bp, tp, d = x_prompt.shape
    bs, ts, _ = x_sample.shape
    n_meta = meta_tokens.shape[0]
    width = conv_a_w.shape[2]
    ka, kb = conv_a_w.shape[1], conv_b_w.shape[1]
    n_e = w_gate.shape[1]
    alpha = (2.0 * depth) ** 0.25
    n_p, n_s = bp * tp, bs * ts
    n = n_p + n_s
    assert ts < ka and ts >= kb - 1
    assert n_p % TOKEN_TILE == 0 and n_s % TOKEN_TILE == 0 and tp % SEQ_TILE == 0
    assert bs % STEP_BATCH == 0 and n_p % STEP_BATCH == 0
    assert n_meta % BF16_ROWS == 0 and n_meta <= TOKEN_TILE and n % n_meta == 0
    assert N_GROUPS * (1 + EXPERTS_PER_GROUP) <= ROUTE_LANES and n_e == N_GROUPS * EXPERTS_PER_GROUP

    row = lambda v: v.reshape(1, -1).astype(f32)
    ln_g, ln_b = row(ln_in_g), row(ln_in_b)
    w_in_b = w_in[0].astype(bf16)
    w_out_b = w_out[0].astype(bf16)
    mix_w = (conv_a_w[0], row(conv_a_b[0]), row(norm_a_g[0]), row(norm_a_b[0]),
             conv_b_w[0], row(conv_b_b[0]),
             jnp.concatenate([w_rg[0], w_ig[0]], axis=-1).astype(bf16),
             row(b_rg[0]), row(b_ig[0]), row(lru_lambda[0]))
    ha, hb = _round_up(ka - 1, SUBLANES), _round_up(kb - 1, SUBLANES)

    x_p = x_prompt.reshape(n_p, d)
    x_s = jnp.concatenate([jnp.swapaxes(x_sample, 0, 1).reshape(n_s, d),
                           jnp.pad(meta_tokens, ((0, TOKEN_TILE - n_meta), (0, 0)))], axis=0)
    h, glu, bx, gate = _inproj(x_p, x_s, n_s + TOKEN_TILE, ln_g, ln_b, w_in_b, TOKEN_TILE)

    _, _, sa_m, sb_m, sh_m = _mixer_seq(
        glu, bx, gate, n, jnp.zeros((1, ha, width), f32), jnp.zeros((1, hb, width), f32),
        jnp.zeros((1, 1, width), f32), mix_w, 1, n_meta, n_meta)

    bcast = lambda s: jnp.broadcast_to(s, (bp,) + s.shape[1:])
    a_p, b_p, nsa_p, nsb_p, nsh_p = _mixer_seq(glu, bx, gate, 0, bcast(sa_m), bcast(sb_m), bcast(sh_m),
                                               mix_w, bp, tp, SEQ_TILE)

    row_major = lambda s: jnp.swapaxes(s, 1, 2)
    wat, wa2 = _shifted_taps(conv_a_w[0], ts, ka - 1)
    wbt, wb2 = _shifted_taps(conv_b_w[0], ts, kb - 1)
    a_s, b_s, nsa_s, nsb_s, nsh_s = _mixer_step(
        glu, bx, gate, n_p, row_major(state_conv_a), row_major(state_conv_b), state_lru,
        (wat, wa2, wbt, wb2), mix_w, bs, ts, STEP_BATCH)

    w_r = jnp.concatenate([w_router_group[0], w_router_expert[0]], axis=1)
    w_r = jnp.pad(w_r, ((0, 0), (0, ROUTE_LANES - w_r.shape[1])))
    wr_hi = w_r.astype(bf16)
    wr_lo = (w_r - wr_hi.astype(f32)).astype(bf16)
    b_r = jnp.concatenate([b_router_group[0], b_router_expert[0]])
    b_r = jnp.pad(b_r, (0, ROUTE_LANES - b_r.shape[0])).reshape(1, ROUTE_LANES)
    h1, h1p, logits = _outproj(a_p, b_p, a_s, b_s, h, w_out_b,
                               row(ln1_g[0]), row(ln1_b[0]), wr_hi, wr_lo, b_r, alpha, TOKEN_TILE)
    route, route_t, cnt = _route(logits)

    blk = EXPERT_ROWS
    e_idx = route_t[0:TOP_K].astype(i32)
    rank = route_t[TOP_K:2 * TOP_K].astype(i32)
    counts = cnt[0, N_GROUPS:N_GROUPS + n_e].astype(i32)
    nblk = (counts + blk - 1) // blk
    pad_end = jnp.cumsum(nblk * blk)
    pad_start = pad_end - nblk * blk
    start_of = jnp.sum(jnp.where(e_idx[None] == jnp.arange(n_e, dtype=i32)[:, None, None],
                                 pad_start[:, None, None], 0), axis=0)
    dest = (start_of + rank).reshape(-1)
    p_max = _round_up(TOP_K * n + n_e * (blk - 1), blk)
    flat = jnp.full((p_max,), -1, i32).at[dest].set(jnp.arange(TOP_K * n, dtype=i32))
    pos = jnp.arange(p_max, dtype=i32)
    valid = flat >= 0
    src_rows = jnp.where(valid, flat % n, 0)
    dst_rows = jnp.where(valid, flat, TOP_K * n + pos % (SCATTER_SLOTS * blk))
    src_rows = jnp.concatenate([src_rows, jnp.zeros(((GATHER_SLOTS - 1) * blk,), i32)])
    dst_rows = jnp.concatenate([TOP_K * n + (SCATTER_SLOTS - 1) * blk + jnp.arange(blk, dtype=i32), dst_rows])
    y2 = _experts(h1p, src_rows, dst_rows, (pad_start // blk).astype(i32), nblk,
                  w_gate[0], w_up[0], w_down[0], blk)
    combine_tile = max(t for t in (TOKEN_TILE, COMBINE_TILE) if n_p % t == 0 and n_s % t == 0)
    y_p, y_s = _combine(h1, route, y2, row(ln2_g[0]), row(ln2_b[0]), n_p, alpha, combine_tile)

    y_prompt = y_p.reshape(bp, tp, d)
    y_sample = jnp.swapaxes(y_s.reshape(ts, bs, d), 0, 1)
    return (y_prompt, y_sample,
            nsa_p[:, ha - (ka - 1):][None], nsb_p[:, hb - (kb - 1):][None], nsh_p.reshape(1, bp, width),
            row_major(nsa_s), row_major(nsb_s), nsh_s)
```

```python
import functools

import jax
import jax.numpy as jnp
from jax import lax
from jax.experimental import pallas as pl
from jax.experimental.pallas import tpu as pltpu

f32 = jnp.float32
bf16 = jnp.bfloat16
i32 = jnp.int32

CONV_HEADS_A = 8
LRU_HEADS = 8
LRU_C = 8.0
N_GROUPS = 8
EXPERTS_PER_GROUP = 8
TOP_K = 2
LN_EPS = 1e-5

SUBLANES = 8
LANES = 128
BF16_ROWS = 16
VMEM_LIMIT = 56 * 1024 * 1024

TOKEN_TILE = 256
SEQ_TILE = 256
CONV_ROWS = 64
STEP_BATCH = 32
EXPERT_ROWS = 128
ROUTE_LANES = 128
ROUTE_SUB = 128
ROUTE_MAX_TILE = 2304
GATHER_SLOTS = 4
SCATTER_SLOTS = 4
NEG_BIG = -1e30


def _round_up(x, m):
    return (x + m - 1) // m * m


def _layer_norm(x, g, b):
    mu = jnp.mean(x, axis=-1, keepdims=True)
    xc = x - mu
    var = jnp.mean(xc * xc, axis=-1, keepdims=True)
    return xc * lax.rsqrt(var + LN_EPS) * g + b


def _group_norm_silu(x, g, b, group):
    parts = []
    for c in range(x.shape[1] // group):
        xg = x[:, c * group:(c + 1) * group]
        mu = jnp.mean(xg, axis=-1, keepdims=True)
        xc = xg - mu
        var = jnp.mean(xc * xc, axis=-1, keepdims=True)
        parts.append(xc * lax.rsqrt(var + LN_EPS))
    y = jnp.concatenate(parts, axis=-1) * g + b
    return y * jax.nn.sigmoid(y)


def _lru_gates(cb, wg_ref, brg, big, lam):
    heads, hd, _ = wg_ref.shape
    cbb = cb.astype(bf16)
    rs, iz = [], []
    for h in range(heads):
        z = jnp.dot(cbb[:, h * hd:(h + 1) * hd], wg_ref[h], preferred_element_type=f32)
        rs.append(z[:, :hd])
        iz.append(z[:, hd:])
    r = jax.nn.sigmoid(jnp.concatenate(rs, axis=-1) + brg)
    i = jax.nn.sigmoid(jnp.concatenate(iz, axis=-1) + big)
    log_a = -LRU_C * r * jax.nn.softplus(-lam)
    a = jnp.exp(log_a)
    u = jnp.sqrt(-jnp.tanh(log_a) * (a * a + 1.0)) * (i * cb)
    return a, u


def _inproj_kernel(xa_ref, xb_ref, g_ref, b_ref, w_ref, h_ref, glu_ref, bx_ref, gate_ref, *, n_a_tiles):
    i = pl.program_id(0)
    x = jnp.where(i < n_a_tiles, xa_ref[...], xb_ref[...])
    h = _layer_norm(x, g_ref[...], b_ref[...])
    h_ref[...] = h
    xn = h.astype(bf16)
    width = glu_ref.shape[1]
    ch = min(width, 512)
    for c in range(width // ch):
        lo = c * ch
        av = jnp.dot(xn, w_ref[:, lo:lo + ch], preferred_element_type=f32)
        ag = jnp.dot(xn, w_ref[:, width + lo:width + lo + ch], preferred_element_type=f32)
        glu_ref[:, lo:lo + ch] = av * jax.nn.sigmoid(ag)
        bx_ref[:, lo:lo + ch] = jnp.dot(xn, w_ref[:, 2 * width + lo:2 * width + lo + ch],
                                        preferred_element_type=f32)
        bg = jnp.dot(xn, w_ref[:, 3 * width + lo:3 * width + lo + ch], preferred_element_type=f32)
        gate_ref[:, lo:lo + ch] = jax.nn.gelu(bg)


def _inproj(xa, xb, n_b_rows, ln_g, ln_b, w_in_b, tm):
    na, d = xa.shape
    width = w_in_b.shape[1] // 4
    n_a_tiles = na // tm
    n_tiles = (na + n_b_rows) // tm
    n_b_tiles = max(n_tiles - n_a_tiles, 1)
    row_out = pl.BlockSpec((tm, width), lambda i: (i, 0))
    return pl.pallas_call(
        functools.partial(_inproj_kernel, n_a_tiles=n_a_tiles),
        grid=(n_tiles,),
        in_specs=[
            pl.BlockSpec((tm, d), lambda i: (jnp.minimum(i, n_a_tiles - 1), 0)),
            pl.BlockSpec((tm, d), lambda i: (jnp.clip(i - n_a_tiles, 0, n_b_tiles - 1), 0)),
            pl.BlockSpec((1, d), lambda i: (0, 0)),
            pl.BlockSpec((1, d), lambda i: (0, 0)),
            pl.BlockSpec((d, 4 * width), lambda i: (0, 0), pipeline_mode=pl.Buffered(1)),
        ],
        out_specs=[pl.BlockSpec((tm, d), lambda i: (i, 0)), row_out, row_out, row_out],
        out_shape=[jax.ShapeDtypeStruct((na + n_b_rows, d), f32)]
        + [jax.ShapeDtypeStruct((na + n_b_rows, width), f32)] * 3,
        compiler_params=pltpu.CompilerParams(dimension_semantics=("arbitrary",),
                                             vmem_limit_bytes=VMEM_LIMIT),
        name="inproj",
    )(xa, xb, ln_g, ln_b, w_in_b)


def _mixer_seq_kernel(glu_ref, bx_ref, gate_ref, sa_ref, sb_ref, sh_ref,
                      wa_ref, ba_ref, ng_ref, nb_ref, wb_ref, bb_ref, wg_ref, brg_ref, big_ref, lam_ref,
                      aout_ref, bout_ref, nsa_ref, nsb_ref, nsh_ref,
                      wina, winb, xsa, xsb, ca_s, cb_s, a_s, u_s, hcar):
    tau = pl.program_id(1)
    tt, width = glu_ref.shape
    ka, kb = wa_ref.shape[0], wb_ref.shape[0]
    ha, hb = sa_ref.shape[0], sb_ref.shape[0]

    @pl.when(tau == 0)
    def _():
        wina[0:ha, :] = sa_ref[...]
        winb[0:hb, :] = sb_ref[...]
        hcar[...] = jnp.broadcast_to(sh_ref[...], hcar.shape)

    wina[ha:ha + tt, :] = glu_ref[...]
    winb[hb:hb + tt, :] = bx_ref[...]
    for s in sorted({(ha - (ka - 1) + k) % SUBLANES for k in range(ka)} - {0}):
        xsa[s - 1] = wina[pl.ds(s, tt + ha - SUBLANES), :]
    for s in sorted({(hb - (kb - 1) + k) % SUBLANES for k in range(kb)} - {0}):
        xsb[s - 1] = winb[pl.ds(s, tt + hb - SUBLANES), :]

    def tap(win, xs, off, r0, lanes):
        q, s = divmod(off, SUBLANES)
        src = win if s == 0 else xs.at[s - 1]
        return src[pl.ds(r0 + SUBLANES * q, SUBLANES), lanes]

    rows = min(CONV_ROWS, tt)
    for g in range(width // LANES):
        lanes = pl.ds(g * LANES, LANES)
        bcast = lambda ref, k: jnp.broadcast_to(ref[k:k + 1, lanes], (SUBLANES, LANES))
        wa_k = [bcast(wa_ref, k) for k in range(ka)]
        wb_k = [bcast(wb_ref, k) for k in range(kb)]
        ba_g, bb_g = bcast(ba_ref, 0), bcast(bb_ref, 0)

        def conv_block(rb, carry):
            for j in range(rows // SUBLANES):
                r0 = pl.multiple_of(rb * rows, rows) + j * SUBLANES
                acc = ba_g
                for k in range(ka):
                    acc = acc + wa_k[k] * tap(wina, xsa, ha - (ka - 1) + k, r0, lanes)
                ca_s[pl.ds(r0, SUBLANES), lanes] = acc
                accb = bb_g
                for k in range(kb):
                    accb = accb + wb_k[k] * tap(winb, xsb, hb - (kb - 1) + k, r0, lanes)
                cb_s[pl.ds(r0, SUBLANES), lanes] = accb
            return carry

        lax.fori_loop(0, tt // rows, conv_block, 0)

    a_out = _group_norm_silu(ca_s[...], ng_ref[...], nb_ref[...], width // CONV_HEADS_A)
    aout_ref[...] = a_out.astype(aout_ref.dtype)

    a, u = _lru_gates(cb_s[...], wg_ref, brg_ref[...], big_ref[...], lam_ref[...])
    row8 = lax.broadcasted_iota(i32, (tt, width), 0) & (SUBLANES - 1)
    d = 1
    while d < SUBLANES:
        m = row8 >= d
        a_sh = jnp.where(m, pltpu.roll(a, d, axis=0), 1.0)
        u_sh = jnp.where(m, pltpu.roll(u, d, axis=0), 0.0)
        u = a * u_sh + u
        a = a * a_sh
        d *= 2
    a_s[...] = a
    u_s[...] = u

    def scan_block(j, h):
        r0 = pl.multiple_of(j * SUBLANES, SUBLANES)
        hblk = a_s[pl.ds(r0, SUBLANES), :] * h + u_s[pl.ds(r0, SUBLANES), :]
        u_s[pl.ds(r0, SUBLANES), :] = hblk
        return jnp.broadcast_to(hblk[SUBLANES - 1:SUBLANES, :], h.shape)

    h_last = lax.fori_loop(0, tt // SUBLANES, scan_block, hcar[...])
    hcar[...] = h_last
    bout_ref[...] = (u_s[...] * gate_ref[...]).astype(bout_ref.dtype)

    tail_a = wina[tt:tt + ha, :]
    tail_b = winb[tt:tt + hb, :]
    wina[0:ha, :] = tail_a
    winb[0:hb, :] = tail_b
    nsa_ref[...] = tail_a
    nsb_ref[...] = tail_b
    nsh_ref[...] = h_last[0:1, :]


def _mixer_seq(glu, bx, gate, row_off, sa_pad, sb_pad, sh, mix_w, n_seq, seq_len, tt):
    wa, ba, ng, nb, wb, bb, wg, brg, big, lam = mix_w
    width = glu.shape[1]
    ha, hb = sa_pad.shape[1], sb_pad.shape[1]
    n_t = seq_len // tt
    off = row_off // tt
    row_in = pl.BlockSpec((tt, width), lambda b, t: (off + b * n_t + t, 0))
    row_out = pl.BlockSpec((tt, width), lambda b, t: (b * n_t + t, 0))

    def const(shape):
        return pl.BlockSpec(shape, lambda b, t: (0,) * len(shape))

    def per_seq(r):
        return pl.BlockSpec((None, r, width), lambda b, t: (b, 0, 0))

    return pl.pallas_call(
        _mixer_seq_kernel,
        grid=(n_seq, n_t),
        in_specs=[row_in, row_in, row_in, per_seq(ha), per_seq(hb), per_seq(1),
                  const(wa.shape), const(ba.shape), const(ng.shape), const(nb.shape),
                  const(wb.shape), const(bb.shape), const(wg.shape), const(brg.shape),
                  const(big.shape), const(lam.shape)],
        out_specs=[row_out, row_out, per_seq(ha), per_seq(hb), per_seq(1)],
        out_shape=[jax.ShapeDtypeStruct((n_seq * seq_len, width), bf16),
                   jax.ShapeDtypeStruct((n_seq * seq_len, width), bf16),
                   jax.ShapeDtypeStruct((n_seq, ha, width), f32),
                   jax.ShapeDtypeStruct((n_seq, hb, width), f32),
                   jax.ShapeDtypeStruct((n_seq, 1, width), f32)],
        scratch_shapes=[pltpu.VMEM((ha + tt, width), f32),
                        pltpu.VMEM((hb + tt, width), f32),
                        pltpu.VMEM((SUBLANES - 1, tt + ha - SUBLANES, width), f32),
                        pltpu.VMEM((SUBLANES - 1, tt + hb - SUBLANES, width), f32),
                        pltpu.VMEM((tt, width), f32),
                        pltpu.VMEM((tt, width), f32),
                        pltpu.VMEM((tt, width), f32),
                        pltpu.VMEM((tt, width), f32),
                        pltpu.VMEM((SUBLANES, width), f32)],
        compiler_params=pltpu.CompilerParams(dimension_semantics=("arbitrary", "arbitrary"),
                                             vmem_limit_bytes=VMEM_LIMIT),
        name="mixer_seq",
    )(glu, bx, gate, sa_pad, sb_pad, sh, wa, ba, ng, nb, wb, bb, wg, brg, big, lam)


def _mixer_step_kernel(glu_ref, bx_ref, gate_ref, sa_ref, sb_ref, sh_ref,
                       wat_ref, wa2_ref, wbt_ref, wb2_ref,
                       ba_ref, ng_ref, nb_ref, bb_ref, wg_ref, brg_ref, big_ref, lam_ref,
                       aout_ref, bout_ref, nsa_ref, nsb_ref, nsh_ref,
                       hista, histb, hcar):
    t = pl.program_id(1)
    n_t = hista.shape[0]
    width = glu_ref.shape[1]

    def roll_buffer(s_ref, ns_ref, x_ref):
        shift = n_t - s_ref.shape[0]

        @pl.when(t == 0)
        def _():
            for r in range(max(-shift, 0)):
                ns_ref[r] = s_ref[r + n_t]

        if shift <= 0:
            ns_ref[t - shift] = x_ref[...]
        else:
            @pl.when(t >= shift)
            def _():
                ns_ref[t - shift] = x_ref[...]

    @pl.when(t == 0)
    def _():
        hista[...] = jnp.zeros_like(hista)
        histb[...] = jnp.zeros_like(histb)
        hcar[...] = sh_ref[...]

    roll_buffer(sa_ref, nsa_ref, glu_ref)
    roll_buffer(sb_ref, nsb_ref, bx_ref)
    hista[t] = glu_ref[...]
    histb[t] = bx_ref[...]

    ca = jnp.broadcast_to(ba_ref[...], glu_ref.shape)
    for r in range(sa_ref.shape[0]):
        ca = ca + wat_ref[r:r + 1, :] * sa_ref[r]
    cb = jnp.broadcast_to(bb_ref[...], bx_ref.shape)
    for r in range(sb_ref.shape[0]):
        cb = cb + wbt_ref[r:r + 1, :] * sb_ref[r]
    for j in range(n_t):
        ca = ca + wa2_ref[j:j + 1, :] * hista[j]
        cb = cb + wb2_ref[j:j + 1, :] * histb[j]
    a_out = _group_norm_silu(ca, ng_ref[...], nb_ref[...], width // CONV_HEADS_A)
    aout_ref[...] = a_out.astype(aout_ref.dtype)

    a, u = _lru_gates(cb, wg_ref, brg_ref[...], big_ref[...], lam_ref[...])
    h = a * hcar[...] + u
    hcar[...] = h
    nsh_ref[...] = h
    bout_ref[...] = (h * gate_ref[...]).astype(bout_ref.dtype)


def _mixer_step(glu, bx, gate, row_off, sa, sb, sh, step_w, mix_w, n_seq, n_t, bt):
    wat, wa2, wbt, wb2 = step_w
    wa, ba, ng, nb, wb, bb, wg, brg, big, lam = mix_w
    width = glu.shape[1]
    ka1, kb1 = sa.shape[1], sb.shape[1]
    off = row_off // bt
    nb_tiles = n_seq // bt
    row_in = pl.BlockSpec((bt, width), lambda j, t: (off + t * nb_tiles + j, 0))
    row_out = pl.BlockSpec((bt, width), lambda j, t: (t * nb_tiles + j, 0))

    def const(shape):
        return pl.BlockSpec(shape, lambda j, t: (0,) * len(shape))

    def per_t(arr):
        return pl.BlockSpec((None,) + arr.shape[1:], lambda j, t: (t, 0, 0))

    def state(r):
        return pl.BlockSpec((None, r, bt, width), lambda j, t: (0, 0, j, 0))

    state_h = pl.BlockSpec((None, bt, width), lambda j, t: (0, j, 0))
    return pl.pallas_call(
        _mixer_step_kernel,
        grid=(nb_tiles, n_t),
        in_specs=[row_in, row_in, row_in, state(ka1), state(kb1), state_h,
                  per_t(wat), per_t(wa2), per_t(wbt), per_t(wb2),
                  const(ba.shape), const(ng.shape), const(nb.shape), const(bb.shape),
                  const(wg.shape), const(brg.shape), const(big.shape), const(lam.shape)],
        out_specs=[row_out, row_out, state(ka1), state(kb1), state_h],
        out_shape=[jax.ShapeDtypeStruct((n_seq * n_t, width), bf16),
                   jax.ShapeDtypeStruct((n_seq * n_t, width), bf16),
                   jax.ShapeDtypeStruct((1, ka1, n_seq, width), f32),
                   jax.ShapeDtypeStruct((1, kb1, n_seq, width), f32),
                   jax.ShapeDtypeStruct((1, n_seq, width), f32)],
        scratch_shapes=[pltpu.VMEM((n_t, bt, width), f32),
                        pltpu.VMEM((n_t, bt, width), f32),
                        pltpu.VMEM((bt, width), f32)],
        compiler_params=pltpu.CompilerParams(dimension_semantics=("arbitrary", "arbitrary"),
                                             vmem_limit_bytes=VMEM_LIMIT),
        name="mixer_step",
    )(glu, bx, gate, sa, sb, sh, wat, wa2, wbt, wb2,
      ba, ng, nb, bb, wg, brg, big, lam)


def _outproj_kernel(ap_ref, bp_ref, as_ref, bs_ref, h_ref, wo_ref,
                    l1g_ref, l1b_ref, wrh_ref, wrl_ref, br_ref,
                    h1_ref, h1p_ref, z_ref, *, n_p_tiles, alpha):
    i = pl.program_id(0)
    is_p = i < n_p_tiles
    a = jnp.where(is_p, ap_ref[...], as_ref[...])
    b = jnp.where(is_p, bp_ref[...], bs_ref[...])
    width = a.shape[1]
    mix = (jnp.dot(a, wo_ref[0:width, :], preferred_element_type=f32)
           + jnp.dot(b, wo_ref[width:2 * width, :], preferred_element_type=f32))
    h1 = _layer_norm(alpha * h_ref[...] + mix, l1g_ref[...], l1b_ref[...])
    h1_ref[...] = h1

    xh = h1.astype(bf16)
    xh32 = xh.astype(f32)
    half = h1.shape[1] // 2
    lo = lax.shift_right_logical(lax.bitcast_convert_type(xh32[:, :half], jnp.uint32), jnp.uint32(16))
    hi = lax.bitcast_convert_type(xh32[:, half:], jnp.uint32) & jnp.uint32(0xFFFF0000)
    word = lo | hi
    n_sub = half // LANES
    for s in range(n_sub):
        h1p_ref[pl.ds(s, word.shape[0], stride=n_sub), :] = word[:, s * LANES:(s + 1) * LANES]

    xl = (h1 - xh32).astype(bf16)
    z_ref[...] = (jnp.dot(xh, wrh_ref[...], preferred_element_type=f32)
                  + jnp.dot(xl, wrh_ref[...], preferred_element_type=f32)
                  + jnp.dot(xh, wrl_ref[...], preferred_element_type=f32)) + br_ref[...]


def _route_kernel(z_ref, route_ref, routet_ref, cnt_ref, carry, *, sub):
    i = pl.program_id(0)
    z = z_ref[...]
    lane = lax.broadcasted_iota(i32, z.shape, 1).astype(f32)
    n_g, epg = float(N_GROUPS), float(EXPERTS_PER_GROUP)
    far = float(2 * ROUTE_LANES)

    gm = lane < n_g
    zg = jnp.where(gm, z, NEG_BIG)
    pg = jnp.where(gm, jnp.exp(zg - jnp.max(zg, axis=-1, keepdims=True)), 0.0)
    pg = pg / jnp.sum(pg, axis=-1, keepdims=True)
    g_top = jnp.max(pg, axis=-1, keepdims=True)
    g_idx = jnp.min(jnp.where(gm & (pg == g_top), lane, far), axis=-1, keepdims=True)

    lo = n_g + g_idx * epg
    em = (lane >= lo) & (lane < lo + epg)
    ze = jnp.where(em, z, NEG_BIG)
    pe = jnp.where(em, jnp.exp(ze - jnp.max(ze, axis=-1, keepdims=True)), 0.0)
    pe = pe / jnp.sum(pe, axis=-1, keepdims=True)
    v1 = jnp.max(jnp.where(em, pe, -1.0), axis=-1, keepdims=True)
    l1 = jnp.min(jnp.where(em & (pe == v1), lane, far), axis=-1, keepdims=True)
    pe2 = jnp.where(em & (lane != l1), pe, -1.0)
    v2 = jnp.max(pe2, axis=-1, keepdims=True)
    l2 = jnp.min(jnp.where(pe2 == v2, lane, far), axis=-1, keepdims=True)
    den = v1 + v2
    gate1 = v1 / den * g_top
    gate2 = v2 / den * g_top

    @pl.when(i == 0)
    def _():
        carry[...] = jnp.zeros_like(carry)

    o1 = lane == l1
    o2 = lane == l2
    onehot = jnp.where(o1 | o2, 1.0, 0.0)
    ri = lax.broadcasted_iota(i32, (sub, sub), 0)
    ci = lax.broadcasted_iota(i32, (sub, sub), 1)
    tri = jnp.where(ci < ri, 1.0, 0.0).astype(bf16)
    counts = carry[0:1, :]
    prefix = []
    for s in range(z.shape[0] // sub):
        oh = onehot[s * sub:(s + 1) * sub, :]
        prefix.append(jnp.dot(tri, oh.astype(bf16), preferred_element_type=f32) + counts)
        counts = counts + jnp.sum(oh, axis=0, keepdims=True)
    prefix = jnp.concatenate(prefix, axis=0)
    rank1 = jnp.sum(jnp.where(o1, prefix, 0.0), axis=-1, keepdims=True)
    rank2 = jnp.sum(jnp.where(o2, prefix, 0.0), axis=-1, keepdims=True)
    new_carry = jnp.broadcast_to(counts, carry.shape)
    carry[...] = new_carry
    cnt_ref[...] = new_carry

    route = jnp.where(lane == 0.0, l1 - n_g, 0.0)
    route = jnp.where(lane == 1.0, l2 - n_g, route)
    route = jnp.where(lane == 2.0, rank1, route)
    route = jnp.where(lane == 3.0, rank2, route)
    route = jnp.where(lane == 4.0, gate1, route)
    route = jnp.where(lane == 5.0, gate2, route)
    route_ref[...] = route
    routet_ref[...] = route.T[0:routet_ref.shape[0], :]


def _outproj(a_p, b_p, a_s, b_s, h, w_out_b, ln1_g, ln1_b,
             wr_hi, wr_lo, b_r, alpha, tm):
    n_p, width = a_p.shape
    n_s = a_s.shape[0]
    d = h.shape[1]
    n_p_tiles = n_p // tm
    n_s_tiles = n_s // tm
    n_tiles = n_p_tiles + n_s_tiles

    def p_map(i):
        return (jnp.minimum(i, n_p_tiles - 1), 0)

    def s_map(i):
        return (jnp.clip(i - n_p_tiles, 0, n_s_tiles - 1), 0)

    def const(shape, **kw):
        return pl.BlockSpec(shape, lambda i: (0,) * len(shape), **kw)

    return pl.pallas_call(
        functools.partial(_outproj_kernel, n_p_tiles=n_p_tiles, alpha=alpha),
        grid=(n_tiles,),
        in_specs=[pl.BlockSpec((tm, width), p_map), pl.BlockSpec((tm, width), p_map),
                  pl.BlockSpec((tm, width), s_map), pl.BlockSpec((tm, width), s_map),
                  pl.BlockSpec((tm, d), lambda i: (i, 0)),
                  const((2 * width, d), pipeline_mode=pl.Buffered(1)),
                  const((1, d)), const((1, d)),
                  const((d, ROUTE_LANES)), const((d, ROUTE_LANES)), const((1, ROUTE_LANES))],
        out_specs=[pl.BlockSpec((tm, d), lambda i: (i, 0)),
                   pl.BlockSpec((tm * (d // 2 // LANES), LANES), lambda i: (i, 0)),
                   pl.BlockSpec((tm, ROUTE_LANES), lambda i: (i, 0))],
        out_shape=[jax.ShapeDtypeStruct((n_p + n_s, d), f32),
                   jax.ShapeDtypeStruct(((n_p + n_s) * (d // 2 // LANES), LANES), jnp.uint32),
                   jax.ShapeDtypeStruct((n_p + n_s, ROUTE_LANES), f32)],
        compiler_params=pltpu.CompilerParams(dimension_semantics=("arbitrary",),
                                             vmem_limit_bytes=VMEM_LIMIT),
        name="outproj",
    )(a_p, b_p, a_s, b_s, h, w_out_b, ln1_g, ln1_b, wr_hi, wr_lo, b_r)


def _route_tile(n):
    return max(t for t in range(ROUTE_SUB, ROUTE_MAX_TILE + 1, ROUTE_SUB) if n % t == 0)


def _route(z):
    n = z.shape[0]
    tr = _route_tile(n)
    return pl.pallas_call(
        functools.partial(_route_kernel, sub=ROUTE_SUB),
        grid=(n // tr,),
        in_specs=[pl.BlockSpec((tr, ROUTE_LANES), lambda i: (i, 0))],
        out_specs=[pl.BlockSpec((tr, ROUTE_LANES), lambda i: (i, 0)),
                   pl.BlockSpec((SUBLANES, tr), lambda i: (0, i)),
                   pl.BlockSpec((SUBLANES, ROUTE_LANES), lambda i: (0, 0))],
        out_shape=[jax.ShapeDtypeStruct((n, ROUTE_LANES), f32),
                   jax.ShapeDtypeStruct((SUBLANES, n), f32),
                   jax.ShapeDtypeStruct((SUBLANES, ROUTE_LANES), f32)],
        scratch_shapes=[pltpu.VMEM((SUBLANES, ROUTE_LANES), f32)],
        compiler_params=pltpu.CompilerParams(dimension_semantics=("arbitrary",),
                                             vmem_limit_bytes=VMEM_LIMIT),
        name="route",
    )(z)


def _expert_kernel(src_ref, dst_ref, start_ref, nblk_ref,
                   h1_hbm, wg_ref, wu_ref, wd_ref, y2_hbm,
                   xbuf, ybuf, wgb, wub, wdb, gsem, osem, *, blk, n_rows):
    e = pl.program_id(0)
    n_e = pl.num_programs(0)
    nb = nblk_ref[e]
    g0 = start_ref[e]

    xs = xbuf.shape[1] // blk
    ys = ybuf.shape[1] // blk

    def gather(g, slot):
        for r in range(blk):
            tok = src_ref[g * blk + r]
            pltpu.make_async_copy(h1_hbm.at[pl.ds(pl.multiple_of(tok * xs, xs), xs), :],
                                  xbuf.at[slot, pl.ds(r * xs, xs), :], gsem.at[slot]
                                  ).start(priority=r % 2)

    def scatter(g, slot):
        for r in range(blk):
            row = dst_ref[(g + 1) * blk + r]
            pltpu.make_async_copy(ybuf.at[slot, pl.ds(r * ys, ys), :],
                                  y2_hbm.at[pl.ds(pl.multiple_of(row * ys, ys), ys), :],
                                  osem.at[slot]).start(priority=r % 2)

    def dump_copy(slot):
        dst = y2_hbm.at[pl.ds((n_rows + slot * blk) * ys, blk * ys), :]
        return pltpu.make_async_copy(ybuf.at[slot], dst, osem.at[slot])

    def wait_gather(slot):
        pltpu.make_async_copy(h1_hbm.at[pl.ds(0, blk * xs), :], xbuf.at[slot], gsem.at[slot]).wait()

    nx, ny = xbuf.shape[0], ybuf.shape[0]
    ahead = nx - 1

    @pl.when(e == 0)
    def _():
        ybuf[...] = jnp.zeros(ybuf.shape, ybuf.dtype)
        for s in range(ny - 1):
            dump_copy(s).start()
        for g in range(ahead):
            gather(g, g)

    @pl.when(nb > 0)
    def _():
        def chunk(c, carry, cast_weights=False):
            g = g0 + c
            slot = g % nx
            yslot = g % ny
            wait_gather(slot)
            dump_copy(yslot).wait()
            gather(g + ahead, (g + ahead) % nx)
            scatter(g - 1, (g + ny - 1) % ny)
            xw = [xbuf[slot, pl.ds(s, blk, stride=xs), :] for s in range(xs)]
            x_lo = [lax.bitcast_convert_type(lax.shift_left(w, jnp.uint32(16)), f32) for w in xw]
            x_hi = [lax.bitcast_convert_type(w & jnp.uint32(0xFFFF0000), f32) for w in xw]
            x = jnp.concatenate(x_lo + x_hi, axis=1).astype(bf16)
            if cast_weights:
                wgb[...] = wg_ref[...].astype(bf16)
            hg = jnp.dot(x, wgb[...], preferred_element_type=f32)
            if cast_weights:
                wub[...] = wu_ref[...].astype(bf16)
            hu = jnp.dot(x, wub[...], preferred_element_type=f32)
            if cast_weights:
                wdb[...] = wd_ref[...].astype(bf16)
            hid = (hg * jax.nn.sigmoid(hg) * hu).astype(bf16)
            y = jnp.dot(hid, wdb[...], preferred_element_type=f32)
            for s in range(ys):
                ybuf[yslot, pl.ds(s, blk, stride=ys), :] = y[:, s * LANES:(s + 1) * LANES]
            return carry

        chunk(0, 0, cast_weights=True)
        lax.fori_loop(1, nb, chunk, 0)

    @pl.when(e == n_e - 1)
    def _():
        g_end = g0 + nb
        scatter(g_end - 1, (g_end + ny - 1) % ny)
        for s in range(ny):
            dump_copy(s).wait()
        for j in range(ahead):
            wait_gather((g_end + j) % nx)


def _experts(h1p, src_rows, dst_rows, pad_start, nblk, w_gate, w_up, w_down, blk):
    n_e, d, d_e = w_gate.shape
    xs, ys = d // 2 // LANES, d // LANES
    n = h1p.shape[0] // xs
    n_rows = TOP_K * n
    grid_spec = pltpu.PrefetchScalarGridSpec(
        num_scalar_prefetch=4,
        grid=(n_e,),
        in_specs=[pl.BlockSpec(memory_space=pl.ANY),
                  pl.BlockSpec((None, d, d_e), lambda e, *_: (e, 0, 0)),
                  pl.BlockSpec((None, d, d_e), lambda e, *_: (e, 0, 0)),
                  pl.BlockSpec((None, d_e, d), lambda e, *_: (e, 0, 0))],
        out_specs=pl.BlockSpec(memory_space=pl.ANY),
        scratch_shapes=[pltpu.VMEM((GATHER_SLOTS, blk * xs, LANES), jnp.uint32),
                        pltpu.VMEM((SCATTER_SLOTS, blk * ys, LANES), f32),
                        pltpu.VMEM((d, d_e), bf16),
                        pltpu.VMEM((d, d_e), bf16),
                        pltpu.VMEM((d_e, d), bf16),
                        pltpu.SemaphoreType.DMA((GATHER_SLOTS,)),
                        pltpu.SemaphoreType.DMA((SCATTER_SLOTS,))],
    )
    return pl.pallas_call(
        functools.partial(_expert_kernel, blk=blk, n_rows=n_rows),
        grid_spec=grid_spec,
        out_shape=jax.ShapeDtypeStruct(((n_rows + SCATTER_SLOTS * blk) * ys, LANES), f32),
        compiler_params=pltpu.CompilerParams(dimension_semantics=("arbitrary",),
                                             vmem_limit_bytes=VMEM_LIMIT),
        name="experts",
    )(src_rows, dst_rows, pad_start, nblk, h1p, w_gate, w_up, w_down)


def _combine_kernel(h1_ref, route_ref, ya_ref, yb_ref, l2g_ref, l2b_ref, yp_ref, ysm_ref,
                    *, n_p_tiles, alpha):
    i = pl.program_id(0)
    route = route_ref[...]
    tm = route.shape[0]
    ys = ya_ref.shape[0] // tm

    def token_rows(ref):
        return jnp.concatenate([ref[pl.ds(s, tm, stride=ys), :] for s in range(ys)], axis=1)

    y = token_rows(ya_ref) * route[:, 4:5] + token_rows(yb_ref) * route[:, 5:6]
    out = _layer_norm(alpha * h1_ref[...] + y, l2g_ref[...], l2b_ref[...])

    @pl.when(i < n_p_tiles)
    def _():
        yp_ref[...] = out

    @pl.when(i >= n_p_tiles)
    def _():
        ysm_ref[...] = out


def _combine(h1, route, y2, ln2_g, ln2_b, n_p, alpha, tm):
    n, d = h1.shape
    n_s = n - n_p
    n_tiles = n // tm
    n_p_tiles = n_p // tm
    n_s_tiles = n_s // tm
    return pl.pallas_call(
        functools.partial(_combine_kernel, n_p_tiles=n_p_tiles, alpha=alpha),
        grid=(n_tiles,),
        in_specs=[pl.BlockSpec((tm, d), lambda i: (i, 0)),
                  pl.BlockSpec((tm, ROUTE_LANES), lambda i: (i, 0)),
                  pl.BlockSpec((tm * (d // LANES), LANES), lambda i: (i, 0)),
                  pl.BlockSpec((tm * (d // LANES), LANES), lambda i: (n_tiles + i, 0)),
                  pl.BlockSpec((1, d), lambda i: (0, 0)),
                  pl.BlockSpec((1, d), lambda i: (0, 0))],
        out_specs=[pl.BlockSpec((tm, d), lambda i: (jnp.minimum(i, n_p_tiles - 1), 0)),
                   pl.BlockSpec((tm, d), lambda i: (jnp.clip(i - n_p_tiles, 0, n_s_tiles - 1), 0))],
        out_shape=[jax.ShapeDtypeStruct((n_p, d), f32), jax.ShapeDtypeStruct((n_s, d), f32)],
        compiler_params=pltpu.CompilerParams(dimension_semantics=("arbitrary",),
                                             vmem_limit_bytes=VMEM_LIMIT),
        name="combine_ln2",
    )(h1, route, y2, y2, ln2_g, ln2_b)


def _shifted_taps(w, n_t, hist):
    k = w.shape[0]
    zero = jnp.zeros_like(w[0])
    wt = jnp.stack([jnp.stack([w[r - t] if 0 <= r - t < k else zero for r in range(hist)])
                    for t in range(n_t)])
    w2 = jnp.stack([jnp.stack([w[hist - t + j] if (j <= t and 0 <= hist - t + j < k) else zero
                               for j in range(n_t)]) for t in range(n_t)])
    return wt, w2


def kernel(x_prompt, x_sample, state_conv_a, state_conv_b, state_lru, meta_tokens, ln_in_g, ln_in_b, w_in, conv_a_w, conv_a_b, norm_a_g, norm_a_b, conv_b_w, conv_b_b, w_rg, b_rg, w_ig, b_ig, lru_lambda, w_out, ln1_g, ln1_b, w_router_group, b_router_group, w_router_expert, b_router_expert, w_gate, w_up, w_down, ln2_g, ln2_b):
    depth = w_in.shape[0]
    assert depth == 1, "single-layer trunk only"
    bp, tp, d = x_prompt.shape
    bs, ts, _ = x_sample.shape
    n_meta = meta_tokens.shape[0]
    width = conv_a_w.shape[2]
    ka, kb = conv_a_w.shape[1], conv_b_w.shape[1]
    n_e = w_gate.shape[1]
    alpha = (2.0 * depth) ** 0.25
    n_p, n_s = bp * tp, bs * ts
    n = n_p + n_s
    assert ts < ka and ts >= kb - 1
    assert n_p % TOKEN_TILE == 0 and n_s % TOKEN_TILE == 0 and tp % SEQ_TILE == 0
    assert bs % STEP_BATCH == 0 and n_p % STEP_BATCH == 0
    assert n_meta % BF16_ROWS == 0 and n_meta <= TOKEN_TILE and n % n_meta == 0
    assert N_GROUPS * (1 + EXPERTS_PER_GROUP) <= ROUTE_LANES and n_e == N_GROUPS * EXPERTS_PER_GROUP

    row = lambda v: v.reshape(1, -1).astype(f32)
    ln_g, ln_b = row(ln_in_g), row(ln_in_b)
    w_in_b = w_in[0].astype(bf16)
    w_out_b = w_out[0].astype(bf16)
    mix_w = (conv_a_w[0], row(conv_a_b[0]), row(norm_a_g[0]), row(norm_a_b[0]),
             conv_b_w[0], row(conv_b_b[0]),
             jnp.concatenate([w_rg[0], w_ig[0]], axis=-1).astype(bf16),
             row(b_rg[0]), row(b_ig[0]), row(lru_lambda[0]))
    ha, hb = _round_up(ka - 1, SUBLANES), _round_up(kb - 1, SUBLANES)

    x_p = x_prompt.reshape(n_p, d)
    x_s = jnp.concatenate([jnp.swapaxes(x_sample, 0, 1).reshape(n_s, d),
                           jnp.pad(meta_tokens, ((0, TOKEN_TILE - n_meta), (0, 0)))], axis=0)
    h, glu, bx, gate = _inproj(x_p, x_s, n_s + TOKEN_TILE, ln_g, ln_b, w_in_b, TOKEN_TILE)

    _, _, sa_m, sb_m, sh_m = _mixer_seq(
        glu, bx, gate, n, jnp.zeros((1, ha, width), f32), jnp.zeros((1, hb, width), f32),
        jnp.zeros((1, 1, width), f32), mix_w, 1, n_meta, n_meta)

    bcast = lambda s: jnp.broadcast_to(s, (bp,) + s.shape[1:])
    a_p, b_p, nsa_p, nsb_p, nsh_p = _mixer_seq(glu, bx, gate, 0, bcast(sa_m), bcast(sb_m), bcast(sh_m),
                                               mix_w, bp, tp, SEQ_TILE)

    row_major = lambda s: jnp.swapaxes(s, 1, 2)
    wat, wa2 = _shifted_taps(conv_a_w[0], ts, ka - 1)
    wbt, wb2 = _shifted_taps(conv_b_w[0], ts, kb - 1)
    a_s, b_s, nsa_s, nsb_s, nsh_s = _mixer_step(
        glu, bx, gate, n_p, row_major(state_conv_a), row_major(state_conv_b), state_lru,
        (wat, wa2, wbt, wb2), mix_w, bs, ts, STEP_BATCH)

    w_r = jnp.concatenate([w_router_group[0], w_router_expert[0]], axis=1)
    w_r = jnp.pad(w_r, ((0, 0), (0, ROUTE_LANES - w_r.shape[1])))
    wr_hi = w_r.astype(bf16)
    wr_lo = (w_r - wr_hi.astype(f32)).astype(bf16)
    b_r = jnp.concatenate([b_router_group[0], b_router_expert[0]])
    b_r = jnp.pad(b_r, (0, ROUTE_LANES - b_r.shape[0])).reshape(1, ROUTE_LANES)
    h1, h1p, logits = _outproj(a_p, b_p, a_s, b_s, h, w_out_b,
                               row(ln1_g[0]), row(ln1_b[0]), wr_hi, wr_lo, b_r, alpha, TOKEN_TILE)
    route, route_t, cnt = _route(logits)

    blk = EXPERT_ROWS
    e_idx = route_t[0:TOP_K].astype(i32)
    rank = route_t[TOP_K:2 * TOP_K].astype(i32)
    counts = cnt[0, N_GROUPS:N_GROUPS + n_e].astype(i32)
    nblk = (counts + blk - 1) // blk
    pad_end = jnp.cumsum(nblk * blk)
    pad_start = pad_end - nblk * blk
    start_of = jnp.sum(jnp.where(e_idx[None] == jnp.arange(n_e, dtype=i32)[:, None, None],
                                 pad_start[:, None, None], 0), axis=0)
    dest = (start_of + rank).reshape(-1)
    p_max = _round_up(TOP_K * n + n_e * (blk - 1), blk)
    flat = jnp.full((p_max,), -1, i32).at[dest].set(jnp.arange(TOP_K * n, dtype=i32))
    pos = jnp.arange(p_max, dtype=i32)
    valid = flat >= 0
    src_rows = jnp.where(valid, flat % n, 0)
    dst_rows = jnp.where(valid, flat, TOP_K * n + pos % (SCATTER_SLOTS * blk))
    src_rows = jnp.concatenate([src_rows, jnp.zeros(((GATHER_SLOTS - 1) * blk,), i32)])
    dst_rows = jnp.concatenate([TOP_K * n + (SCATTER_SLOTS - 1) * blk + jnp.arange(blk, dtype=i32), dst_rows])
    y2 = _experts(h1p, src_rows, dst_rows, (pad_start // blk).astype(i32), nblk,
                  w_gate[0], w_up[0], w_down[0], blk)
    y_p, y_s = _combine(h1, route, y2, row(ln2_g[0]), row(ln2_b[0]), n_p, alpha, TOKEN_TILE)

    y_prompt = y_p.reshape(bp, tp, d)
    y_sample = jnp.swapaxes(y_s.reshape(ts, bs, d), 0, 1)
    return (y_prompt, y_sample,
            nsa_p[:, ha - (ka - 1):][None], nsb_p[:, hb - (kb - 1):][None], nsh_p.reshape(1, bp, width),
            row_major(nsa_s), row_major(nsb_s), nsh_s)
```

```python
import functools

import jax
import jax.numpy as jnp
from jax import lax
from jax.experimental import pallas as pl
from jax.experimental.pallas import tpu as pltpu

f32 = jnp.float32
bf16 = jnp.bfloat16
i32 = jnp.int32

CONV_HEADS_A = 8
LRU_HEADS = 8
LRU_C = 8.0
N_GROUPS = 8
EXPERTS_PER_GROUP = 8
TOP_K = 2
LN_EPS = 1e-5

SUBLANES = 8
LANES = 128
BF16_ROWS = 16
VMEM_LIMIT = 56 * 1024 * 1024

TOKEN_TILE = 256
SEQ_TILE = 256
CONV_ROWS = 64
STEP_BATCH = 64
EXPERT_ROWS = 128
ROUTE_LANES = 128
ROUTE_SUB = 128
ROUTE_MAX_TILE = 2304
GATHER_SLOTS = 4
SCATTER_SLOTS = 4
SCALAR_UNROLL = 8
NEG_BIG = -1e30


def _round_up(x, m):
    return (x + m - 1) // m * m


def _layer_norm(x, g, b):
    mu = jnp.mean(x, axis=-1, keepdims=True)
    xc = x - mu
    var = jnp.mean(xc * xc, axis=-1, keepdims=True)
    return xc * lax.rsqrt(var + LN_EPS) * g + b


def _group_norm_silu(x, g, b, group):
    parts = []
    for c in range(x.shape[1] // group):
        xg = x[:, c * group:(c + 1) * group]
        mu = jnp.mean(xg, axis=-1, keepdims=True)
        xc = xg - mu
        var = jnp.mean(xc * xc, axis=-1, keepdims=True)
        parts.append(xc * lax.rsqrt(var + LN_EPS))
    y = jnp.concatenate(parts, axis=-1) * g + b
    return y * jax.nn.sigmoid(y)


def _lru_gates(cb, wg_ref, brg, big, lam):
    heads, hd, _ = wg_ref.shape
    cbb = cb.astype(bf16)
    rs, iz = [], []
    for h in range(heads):
        z = jnp.dot(cbb[:, h * hd:(h + 1) * hd], wg_ref[h], preferred_element_type=f32)
        rs.append(z[:, :hd])
        iz.append(z[:, hd:])
    r = jax.nn.sigmoid(jnp.concatenate(rs, axis=-1) + brg)
    i = jax.nn.sigmoid(jnp.concatenate(iz, axis=-1) + big)
    log_a = -LRU_C * r * jax.nn.softplus(-lam)
    a = jnp.exp(log_a)
    u = jnp.sqrt(-jnp.tanh(log_a) * (a * a + 1.0)) * (i * cb)
    return a, u


def _inproj_kernel(xa_ref, xb_ref, g_ref, b_ref, w_ref, h_ref, glu_ref, bx_ref, gate_ref, *, n_a_tiles):
    i = pl.program_id(0)
    x = jnp.where(i < n_a_tiles, xa_ref[...], xb_ref[...])
    h = _layer_norm(x, g_ref[...], b_ref[...])
    h_ref[...] = h
    xn = h.astype(bf16)
    width = glu_ref.shape[1]
    ch = min(width, 512)
    for c in range(width // ch):
        lo = c * ch
        av = jnp.dot(xn, w_ref[:, lo:lo + ch], preferred_element_type=f32)
        ag = jnp.dot(xn, w_ref[:, width + lo:width + lo + ch], preferred_element_type=f32)
        glu_ref[:, lo:lo + ch] = av * jax.nn.sigmoid(ag)
        bx_ref[:, lo:lo + ch] = jnp.dot(xn, w_ref[:, 2 * width + lo:2 * width + lo + ch],
                                        preferred_element_type=f32)
        bg = jnp.dot(xn, w_ref[:, 3 * width + lo:3 * width + lo + ch], preferred_element_type=f32)
        gate_ref[:, lo:lo + ch] = jax.nn.gelu(bg)


def _inproj(xa, xb, n_b_rows, ln_g, ln_b, w_in_b, tm):
    na, d = xa.shape
    width = w_in_b.shape[1] // 4
    n_a_tiles = na // tm
    n_tiles = (na + n_b_rows) // tm
    n_b_tiles = max(n_tiles - n_a_tiles, 1)
    row_out = pl.BlockSpec((tm, width), lambda i: (i, 0))
    return pl.pallas_call(
        functools.partial(_inproj_kernel, n_a_tiles=n_a_tiles),
        grid=(n_tiles,),
        in_specs=[
            pl.BlockSpec((tm, d), lambda i: (jnp.minimum(i, n_a_tiles - 1), 0)),
            pl.BlockSpec((tm, d), lambda i: (jnp.clip(i - n_a_tiles, 0, n_b_tiles - 1), 0)),
            pl.BlockSpec((1, d), lambda i: (0, 0)),
            pl.BlockSpec((1, d), lambda i: (0, 0)),
            pl.BlockSpec((d, 4 * width), lambda i: (0, 0), pipeline_mode=pl.Buffered(1)),
        ],
        out_specs=[pl.BlockSpec((tm, d), lambda i: (i, 0)), row_out, row_out, row_out],
        out_shape=[jax.ShapeDtypeStruct((na + n_b_rows, d), f32)]
        + [jax.ShapeDtypeStruct((na + n_b_rows, width), f32)] * 3,
        compiler_params=pltpu.CompilerParams(dimension_semantics=("arbitrary",),
                                             vmem_limit_bytes=VMEM_LIMIT),
        name="inproj",
    )(xa, xb, ln_g, ln_b, w_in_b)


def _mixer_seq_kernel(glu_ref, bx_ref, gate_ref, sa_ref, sb_ref, sh_ref,
                      wa_ref, ba_ref, ng_ref, nb_ref, wb_ref, bb_ref, wg_ref, brg_ref, big_ref, lam_ref,
                      aout_ref, bout_ref, nsa_ref, nsb_ref, nsh_ref,
                      wina, winb, xsa, xsb, ca_s, cb_s, a_s, u_s, hcar):
    tau = pl.program_id(1)
    tt, width = glu_ref.shape
    ka, kb = wa_ref.shape[0], wb_ref.shape[0]
    ha, hb = sa_ref.shape[0], sb_ref.shape[0]

    @pl.when(tau == 0)
    def _():
        wina[0:ha, :] = sa_ref[...]
        winb[0:hb, :] = sb_ref[...]
        hcar[...] = jnp.broadcast_to(sh_ref[...], hcar.shape)

    wina[ha:ha + tt, :] = glu_ref[...]
    winb[hb:hb + tt, :] = bx_ref[...]
    for s in sorted({(ha - (ka - 1) + k) % SUBLANES for k in range(ka)} - {0}):
        xsa[s - 1] = wina[pl.ds(s, tt + ha - SUBLANES), :]
    for s in sorted({(hb - (kb - 1) + k) % SUBLANES for k in range(kb)} - {0}):
        xsb[s - 1] = winb[pl.ds(s, tt + hb - SUBLANES), :]

    def tap(win, xs, off, r0, lanes):
        q, s = divmod(off, SUBLANES)
        src = win if s == 0 else xs.at[s - 1]
        return src[pl.ds(r0 + SUBLANES * q, SUBLANES), lanes]

    rows = min(CONV_ROWS, tt)
    for g in range(width // LANES):
        lanes = pl.ds(g * LANES, LANES)
        bcast = lambda ref, k: jnp.broadcast_to(ref[k:k + 1, lanes], (SUBLANES, LANES))
        wa_k = [bcast(wa_ref, k) for k in range(ka)]
        wb_k = [bcast(wb_ref, k) for k in range(kb)]
        ba_g, bb_g = bcast(ba_ref, 0), bcast(bb_ref, 0)

        def conv_block(rb, carry):
            for j in range(rows // SUBLANES):
                r0 = pl.multiple_of(rb * rows, rows) + j * SUBLANES
                acc = ba_g
                for k in range(ka):
                    acc = acc + wa_k[k] * tap(wina, xsa, ha - (ka - 1) + k, r0, lanes)
                ca_s[pl.ds(r0, SUBLANES), lanes] = acc
                accb = bb_g
                for k in range(kb):
                    accb = accb + wb_k[k] * tap(winb, xsb, hb - (kb - 1) + k, r0, lanes)
                cb_s[pl.ds(r0, SUBLANES), lanes] = accb
            return carry

        lax.fori_loop(0, tt // rows, conv_block, 0)

    a_out = _group_norm_silu(ca_s[...], ng_ref[...], nb_ref[...], width // CONV_HEADS_A)
    aout_ref[...] = a_out.astype(aout_ref.dtype)

    a, u = _lru_gates(cb_s[...], wg_ref, brg_ref[...], big_ref[...], lam_ref[...])
    row8 = lax.broadcasted_iota(i32, (tt, width), 0) & (SUBLANES - 1)
    d = 1
    while d < SUBLANES:
        m = row8 >= d
        a_sh = jnp.where(m, pltpu.roll(a, d, axis=0), 1.0)
        u_sh = jnp.where(m, pltpu.roll(u, d, axis=0), 0.0)
        u = a * u_sh + u
        a = a * a_sh
        d *= 2
    a_s[...] = a
    u_s[...] = u

    def scan_block(j, h):
        r0 = pl.multiple_of(j * SUBLANES, SUBLANES)
        hblk = a_s[pl.ds(r0, SUBLANES), :] * h + u_s[pl.ds(r0, SUBLANES), :]
        u_s[pl.ds(r0, SUBLANES), :] = hblk
        return jnp.broadcast_to(hblk[SUBLANES - 1:SUBLANES, :], h.shape)

    h_last = lax.fori_loop(0, tt // SUBLANES, scan_block, hcar[...])
    hcar[...] = h_last
    bout_ref[...] = (u_s[...] * gate_ref[...]).astype(bout_ref.dtype)

    tail_a = wina[tt:tt + ha, :]
    tail_b = winb[tt:tt + hb, :]
    wina[0:ha, :] = tail_a
    winb[0:hb, :] = tail_b
    nsa_ref[...] = tail_a
    nsb_ref[...] = tail_b
    nsh_ref[...] = h_last[0:1, :]


def _mixer_seq(glu, bx, gate, row_off, sa_pad, sb_pad, sh, mix_w, n_seq, seq_len, tt):
    wa, ba, ng, nb, wb, bb, wg, brg, big, lam = mix_w
    width = glu.shape[1]
    ha, hb = sa_pad.shape[1], sb_pad.shape[1]
    n_t = seq_len // tt
    off = row_off // tt
    row_in = pl.BlockSpec((tt, width), lambda b, t: (off + b * n_t + t, 0))
    row_out = pl.BlockSpec((tt, width), lambda b, t: (b * n_t + t, 0))

    def const(shape):
        return pl.BlockSpec(shape, lambda b, t: (0,) * len(shape))

    def per_seq(r):
        return pl.BlockSpec((None, r, width), lambda b, t: (b, 0, 0))

    return pl.pallas_call(
        _mixer_seq_kernel,
        grid=(n_seq, n_t),
        in_specs=[row_in, row_in, row_in, per_seq(ha), per_seq(hb), per_seq(1),
                  const(wa.shape), const(ba.shape), const(ng.shape), const(nb.shape),
                  const(wb.shape), const(bb.shape), const(wg.shape), const(brg.shape),
                  const(big.shape), const(lam.shape)],
        out_specs=[row_out, row_out, per_seq(ha), per_seq(hb), per_seq(1)],
        out_shape=[jax.ShapeDtypeStruct((n_seq * seq_len, width), bf16),
                   jax.ShapeDtypeStruct((n_seq * seq_len, width), bf16),
                   jax.ShapeDtypeStruct((n_seq, ha, width), f32),
                   jax.ShapeDtypeStruct((n_seq, hb, width), f32),
                   jax.ShapeDtypeStruct((n_seq, 1, width), f32)],
        scratch_shapes=[pltpu.VMEM((ha + tt, width), f32),
                        pltpu.VMEM((hb + tt, width), f32),
                        pltpu.VMEM((SUBLANES - 1, tt + ha - SUBLANES, width), f32),
                        pltpu.VMEM((SUBLANES - 1, tt + hb - SUBLANES, width), f32),
                        pltpu.VMEM((tt, width), f32),
                        pltpu.VMEM((tt, width), f32),
                        pltpu.VMEM((tt, width), f32),
                        pltpu.VMEM((tt, width), f32),
                        pltpu.VMEM((SUBLANES, width), f32)],
        compiler_params=pltpu.CompilerParams(dimension_semantics=("arbitrary", "arbitrary"),
                                             vmem_limit_bytes=VMEM_LIMIT),
        name="mixer_seq",
    )(glu, bx, gate, sa_pad, sb_pad, sh, wa, ba, ng, nb, wb, bb, wg, brg, big, lam)


def _mixer_step_kernel(glu_ref, bx_ref, gate_ref, sa_ref, sb_ref, sh_ref,
                       wat_ref, wa2_ref, wbt_ref, wb2_ref,
                       ba_ref, ng_ref, nb_ref, bb_ref, wg_ref, brg_ref, big_ref, lam_ref,
                       aout_ref, bout_ref, nsa_ref, nsb_ref, nsh_ref,
                       hista, histb, hcar):
    t = pl.program_id(1)
    n_t = hista.shape[0]
    width = glu_ref.shape[1]

    def roll_buffer(s_ref, ns_ref, x_ref):
        shift = n_t - s_ref.shape[0]

        @pl.when(t == 0)
        def _():
            for r in range(max(-shift, 0)):
                ns_ref[r] = s_ref[r + n_t]

        if shift <= 0:
            ns_ref[t - shift] = x_ref[...]
        else:
            @pl.when(t >= shift)
            def _():
                ns_ref[t - shift] = x_ref[...]

    @pl.when(t == 0)
    def _():
        hista[...] = jnp.zeros_like(hista)
        histb[...] = jnp.zeros_like(histb)
        hcar[...] = sh_ref[...]

    roll_buffer(sa_ref, nsa_ref, glu_ref)
    roll_buffer(sb_ref, nsb_ref, bx_ref)
    hista[t] = glu_ref[...]
    histb[t] = bx_ref[...]

    ca = jnp.broadcast_to(ba_ref[...], glu_ref.shape)
    for r in range(sa_ref.shape[0]):
        ca = ca + wat_ref[r:r + 1, :] * sa_ref[r]
    cb = jnp.broadcast_to(bb_ref[...], bx_ref.shape)
    for r in range(sb_ref.shape[0]):
        cb = cb + wbt_ref[r:r + 1, :] * sb_ref[r]
    for j in range(n_t):
        ca = ca + wa2_ref[j:j + 1, :] * hista[j]
        cb = cb + wb2_ref[j:j + 1, :] * histb[j]
    a_out = _group_norm_silu(ca, ng_ref[...], nb_ref[...], width // CONV_HEADS_A)
    aout_ref[...] = a_out.astype(aout_ref.dtype)

    a, u = _lru_gates(cb, wg_ref, brg_ref[...], big_ref[...], lam_ref[...])
    h = a * hcar[...] + u
    hcar[...] = h
    nsh_ref[...] = h
    bout_ref[...] = (h * gate_ref[...]).astype(bout_ref.dtype)


def _mixer_step(glu, bx, gate, row_off, sa, sb, sh, step_w, mix_w, n_seq, n_t, bt):
    wat, wa2, wbt, wb2 = step_w
    wa, ba, ng, nb, wb, bb, wg, brg, big, lam = mix_w
    width = glu.shape[1]
    ka1, kb1 = sa.shape[1], sb.shape[1]
    off = row_off // bt
    nb_tiles = n_seq // bt
    row_in = pl.BlockSpec((bt, width), lambda j, t: (off + t * nb_tiles + j, 0))
    row_out = pl.BlockSpec((bt, width), lambda j, t: (t * nb_tiles + j, 0))

    def const(shape):
        return pl.BlockSpec(shape, lambda j, t: (0,) * len(shape))

    def per_t(arr):
        return pl.BlockSpec((None,) + arr.shape[1:], lambda j, t: (t, 0, 0))

    def state(r):
        return pl.BlockSpec((None, r, bt, width), lambda j, t: (0, 0, j, 0))

    state_h = pl.BlockSpec((None, bt, width), lambda j, t: (0, j, 0))
    return pl.pallas_call(
        _mixer_step_kernel,
        grid=(nb_tiles, n_t),
        in_specs=[row_in, row_in, row_in, state(ka1), state(kb1), state_h,
                  per_t(wat), per_t(wa2), per_t(wbt), per_t(wb2),
                  const(ba.shape), const(ng.shape), const(nb.shape), const(bb.shape),
                  const(wg.shape), const(brg.shape), const(big.shape), const(lam.shape)],
        out_specs=[row_out, row_out, state(ka1), state(kb1), state_h],
        out_shape=[jax.ShapeDtypeStruct((n_seq * n_t, width), bf16),
                   jax.ShapeDtypeStruct((n_seq * n_t, width), bf16),
                   jax.ShapeDtypeStruct((1, ka1, n_seq, width), f32),
                   jax.ShapeDtypeStruct((1, kb1, n_seq, width), f32),
                   jax.ShapeDtypeStruct((1, n_seq, width), f32)],
        scratch_shapes=[pltpu.VMEM((n_t, bt, width), f32),
                        pltpu.VMEM((n_t, bt, width), f32),
                        pltpu.VMEM((bt, width), f32)],
        compiler_params=pltpu.CompilerParams(dimension_semantics=("arbitrary", "arbitrary"),
                                             vmem_limit_bytes=VMEM_LIMIT),
        name="mixer_step",
    )(glu, bx, gate, sa, sb, sh, wat, wa2, wbt, wb2,
      ba, ng, nb, bb, wg, brg, big, lam)


def _outproj_kernel(ap_ref, bp_ref, as_ref, bs_ref, h_ref, wo_ref,
                    l1g_ref, l1b_ref, wrh_ref, wrl_ref, br_ref,
                    h1_ref, h1p_ref, z_ref, *, n_p_tiles, alpha):
    i = pl.program_id(0)
    is_p = i < n_p_tiles
    a = jnp.where(is_p, ap_ref[...], as_ref[...])
    b = jnp.where(is_p, bp_ref[...], bs_ref[...])
    width = a.shape[1]
    mix = (jnp.dot(a, wo_ref[0:width, :], preferred_element_type=f32)
           + jnp.dot(b, wo_ref[width:2 * width, :], preferred_element_type=f32))
    h1 = _layer_norm(alpha * h_ref[...] + mix, l1g_ref[...], l1b_ref[...])
    h1_ref[...] = h1

    xh = h1.astype(bf16)
    xh32 = xh.astype(f32)
    half = h1.shape[1] // 2
    lo = lax.shift_right_logical(lax.bitcast_convert_type(xh32[:, :half], jnp.uint32), jnp.uint32(16))
    hi = lax.bitcast_convert_type(xh32[:, half:], jnp.uint32) & jnp.uint32(0xFFFF0000)
    word = lo | hi
    n_sub = half // LANES
    for s in range(n_sub):
        h1p_ref[pl.ds(s, word.shape[0], stride=n_sub), :] = word[:, s * LANES:(s + 1) * LANES]

    xl = (h1 - xh32).astype(bf16)
    z_ref[...] = (jnp.dot(xh, wrh_ref[...], preferred_element_type=f32)
                  + jnp.dot(xl, wrh_ref[...], preferred_element_type=f32)
                  + jnp.dot(xh, wrl_ref[...], preferred_element_type=f32)) + br_ref[...]


def _route_kernel(z_ref, route_ref, routet_ref, cnt_ref, carry, *, sub):
    i = pl.program_id(0)
    z = z_ref[...]
    lane = lax.broadcasted_iota(i32, z.shape, 1).astype(f32)
    n_g, epg = float(N_GROUPS), float(EXPERTS_PER_GROUP)
    far = float(2 * ROUTE_LANES)

    gm = lane < n_g
    zg = jnp.where(gm, z, NEG_BIG)
    pg = jnp.where(gm, jnp.exp(zg - jnp.max(zg, axis=-1, keepdims=True)), 0.0)
    pg = pg / jnp.sum(pg, axis=-1, keepdims=True)
    g_top = jnp.max(pg, axis=-1, keepdims=True)
    g_idx = jnp.min(jnp.where(gm & (pg == g_top), lane, far), axis=-1, keepdims=True)

    lo = n_g + g_idx * epg
    em = (lane >= lo) & (lane < lo + epg)
    ze = jnp.where(em, z, NEG_BIG)
    pe = jnp.where(em, jnp.exp(ze - jnp.max(ze, axis=-1, keepdims=True)), 0.0)
    pe = pe / jnp.sum(pe, axis=-1, keepdims=True)
    v1 = jnp.max(jnp.where(em, pe, -1.0), axis=-1, keepdims=True)
    l1 = jnp.min(jnp.where(em & (pe == v1), lane, far), axis=-1, keepdims=True)
    pe2 = jnp.where(em & (lane != l1), pe, -1.0)
    v2 = jnp.max(pe2, axis=-1, keepdims=True)
    l2 = jnp.min(jnp.where(pe2 == v2, lane, far), axis=-1, keepdims=True)
    den = v1 + v2
    gate1 = v1 / den * g_top
    gate2 = v2 / den * g_top

    @pl.when(i == 0)
    def _():
        carry[...] = jnp.zeros_like(carry)

    o1 = lane == l1
    o2 = lane == l2
    onehot = jnp.where(o1 | o2, 1.0, 0.0)
    ri = lax.broadcasted_iota(i32, (sub, sub), 0)
    ci = lax.broadcasted_iota(i32, (sub, sub), 1)
    tri = jnp.where(ci < ri, 1.0, 0.0).astype(bf16)
    counts = carry[0:1, :]
    prefix = []
    for s in range(z.shape[0] // sub):
        oh = onehot[s * sub:(s + 1) * sub, :]
        prefix.append(jnp.dot(tri, oh.astype(bf16), preferred_element_type=f32) + counts)
        counts = counts + jnp.sum(oh, axis=0, keepdims=True)
    prefix = jnp.concatenate(prefix, axis=0)
    rank1 = jnp.sum(jnp.where(o1, prefix, 0.0), axis=-1, keepdims=True)
    rank2 = jnp.sum(jnp.where(o2, prefix, 0.0), axis=-1, keepdims=True)
    new_carry = jnp.broadcast_to(counts, carry.shape)
    carry[...] = new_carry
    cnt_ref[...] = new_carry

    route = jnp.where(lane == 0.0, l1 - n_g, 0.0)
    route = jnp.where(lane == 1.0, l2 - n_g, route)
    route = jnp.where(lane == 2.0, rank1, route)
    route = jnp.where(lane == 3.0, rank2, route)
    route = jnp.where(lane == 4.0, gate1, route)
    route = jnp.where(lane == 5.0, gate2, route)
    route_ref[...] = route
    routet_ref[...] = route.T[0:routet_ref.shape[0], :]


def _outproj(a_p, b_p, a_s, b_s, h, w_out_b, ln1_g, ln1_b,
             wr_hi, wr_lo, b_r, alpha, tm):
    n_p, width = a_p.shape
    n_s = a_s.shape[0]
    d = h.shape[1]
    n_p_tiles = n_p // tm
    n_s_tiles = n_s // tm
    n_tiles = n_p_tiles + n_s_tiles

    def p_map(i):
        return (jnp.minimum(i, n_p_tiles - 1), 0)

    def s_map(i):
        return (jnp.clip(i - n_p_tiles, 0, n_s_tiles - 1), 0)

    def const(shape, **kw):
        return pl.BlockSpec(shape, lambda i: (0,) * len(shape), **kw)

    return pl.pallas_call(
        functools.partial(_outproj_kernel, n_p_tiles=n_p_tiles, alpha=alpha),
        grid=(n_tiles,),
        in_specs=[pl.BlockSpec((tm, width), p_map), pl.BlockSpec((tm, width), p_map),
                  pl.BlockSpec((tm, width), s_map), pl.BlockSpec((tm, width), s_map),
                  pl.BlockSpec((tm, d), lambda i: (i, 0)),
                  const((2 * width, d), pipeline_mode=pl.Buffered(1)),
                  const((1, d)), const((1, d)),
                  const((d, ROUTE_LANES)), const((d, ROUTE_LANES)), const((1, ROUTE_LANES))],
        out_specs=[pl.BlockSpec((tm, d), lambda i: (i, 0)),
                   pl.BlockSpec((tm * (d // 2 // LANES), LANES), lambda i: (i, 0)),
                   pl.BlockSpec((tm, ROUTE_LANES), lambda i: (i, 0))],
        out_shape=[jax.ShapeDtypeStruct((n_p + n_s, d), f32),
                   jax.ShapeDtypeStruct(((n_p + n_s) * (d // 2 // LANES), LANES), jnp.uint32),
                   jax.ShapeDtypeStruct((n_p + n_s, ROUTE_LANES), f32)],
        compiler_params=pltpu.CompilerParams(dimension_semantics=("arbitrary",),
                                             vmem_limit_bytes=VMEM_LIMIT),
        name="outproj",
    )(a_p, b_p, a_s, b_s, h, w_out_b, ln1_g, ln1_b, wr_hi, wr_lo, b_r)


def _route_tile(n):
    return max(t for t in range(ROUTE_SUB, ROUTE_MAX_TILE + 1, ROUTE_SUB) if n % t == 0)


def _route(z):
    n = z.shape[0]
    tr = _route_tile(n)
    return pl.pallas_call(
        functools.partial(_route_kernel, sub=ROUTE_SUB),
        grid=(n // tr,),
        in_specs=[pl.BlockSpec((tr, ROUTE_LANES), lambda i: (i, 0))],
        out_specs=[pl.BlockSpec((tr, ROUTE_LANES), lambda i: (i, 0)),
                   pl.BlockSpec((SUBLANES, tr), lambda i: (0, i)),
                   pl.BlockSpec((SUBLANES, ROUTE_LANES), lambda i: (0, 0))],
        out_shape=[jax.ShapeDtypeStruct((n, ROUTE_LANES), f32),
                   jax.ShapeDtypeStruct((SUBLANES, n), f32),
                   jax.ShapeDtypeStruct((SUBLANES, ROUTE_LANES), f32)],
        scratch_shapes=[pltpu.VMEM((SUBLANES, ROUTE_LANES), f32)],
        compiler_params=pltpu.CompilerParams(dimension_semantics=("arbitrary",),
                                             vmem_limit_bytes=VMEM_LIMIT),
        name="route",
    )(z)


def _invert_kernel(dest_ref, flat_ref):
    def init(p, carry):
        flat_ref[p] = -1
        return carry

    def place(f, carry):
        flat_ref[dest_ref[f]] = f
        return carry

    lax.fori_loop(0, flat_ref.shape[0], init, 0, unroll=SCALAR_UNROLL)
    lax.fori_loop(0, dest_ref.shape[0], place, 0, unroll=SCALAR_UNROLL)


def _invert(dest, p_max):
    return pl.pallas_call(
        _invert_kernel,
        in_specs=[pl.BlockSpec(memory_space=pltpu.SMEM)],
        out_specs=pl.BlockSpec(memory_space=pltpu.SMEM),
        out_shape=jax.ShapeDtypeStruct((p_max,), i32),
        name="invert_dispatch",
    )(dest)


def _expert_kernel(src_ref, dst_ref, start_ref, nblk_ref,
                   h1_hbm, wg_ref, wu_ref, wd_ref, y2_hbm,
                   xbuf, ybuf, wgb, wub, wdb, gsem, osem, *, blk, n_rows):
    e = pl.program_id(0)
    n_e = pl.num_programs(0)
    nb = nblk_ref[e]
    g0 = start_ref[e]

    xs = xbuf.shape[1] // blk
    ys = ybuf.shape[1] // blk

    def gather(g, slot):
        for r in range(blk):
            tok = src_ref[g * blk + r]
            pltpu.make_async_copy(h1_hbm.at[pl.ds(pl.multiple_of(tok * xs, xs), xs), :],
                                  xbuf.at[slot, pl.ds(r * xs, xs), :], gsem.at[slot]
                                  ).start(priority=r % 2)

    def scatter(g, slot):
        for r in range(blk):
            row = dst_ref[(g + 1) * blk + r]
            pltpu.make_async_copy(ybuf.at[slot, pl.ds(r * ys, ys), :],
                                  y2_hbm.at[pl.ds(pl.multiple_of(row * ys, ys), ys), :],
                                  osem.at[slot]).start(priority=r % 2)

    def dump_copy(slot):
        dst = y2_hbm.at[pl.ds((n_rows + slot * blk) * ys, blk * ys), :]
        return pltpu.make_async_copy(ybuf.at[slot], dst, osem.at[slot])

    def wait_gather(slot):
        pltpu.make_async_copy(h1_hbm.at[pl.ds(0, blk * xs), :], xbuf.at[slot], gsem.at[slot]).wait()

    nx, ny = xbuf.shape[0], ybuf.shape[0]
    ahead = nx - 1

    @pl.when(e == 0)
    def _():
        ybuf[...] = jnp.zeros(ybuf.shape, ybuf.dtype)
        for s in range(ny - 1):
            dump_copy(s).start()
        for g in range(ahead):
            gather(g, g)

    @pl.when(nb > 0)
    def _():
        wgb[...] = wg_ref[...].astype(bf16)
        wub[...] = wu_ref[...].astype(bf16)
        wdb[...] = wd_ref[...].astype(bf16)

        def chunk(c, carry):
            g = g0 + c
            slot = g % nx
            yslot = g % ny
            wait_gather(slot)
            dump_copy(yslot).wait()
            gather(g + ahead, (g + ahead) % nx)
            scatter(g - 1, (g + ny - 1) % ny)
            xw = [xbuf[slot, pl.ds(s, blk, stride=xs), :] for s in range(xs)]
            x_lo = [lax.bitcast_convert_type(lax.shift_left(w, jnp.uint32(16)), f32) for w in xw]
            x_hi = [lax.bitcast_convert_type(w & jnp.uint32(0xFFFF0000), f32) for w in xw]
            x = jnp.concatenate(x_lo + x_hi, axis=1).astype(bf16)
            hg = jnp.dot(x, wgb[...], preferred_element_type=f32)
            hu = jnp.dot(x, wub[...], preferred_element_type=f32)
            hid = (hg * jax.nn.sigmoid(hg) * hu).astype(bf16)
            y = jnp.dot(hid, wdb[...], preferred_element_type=f32)
            for s in range(ys):
                ybuf[yslot, pl.ds(s, blk, stride=ys), :] = y[:, s * LANES:(s + 1) * LANES]
            return carry

        lax.fori_loop(0, nb, chunk, 0)

    @pl.when(e == n_e - 1)
    def _():
        g_end = g0 + nb
        scatter(g_end - 1, (g_end + ny - 1) % ny)
        for s in range(ny):
            dump_copy(s).wait()
        for j in range(ahead):
            wait_gather((g_end + j) % nx)


def _experts(h1p, src_rows, dst_rows, pad_start, nblk, w_gate, w_up, w_down, blk):
    n_e, d, d_e = w_gate.shape
    xs, ys = d // 2 // LANES, d // LANES
    n = h1p.shape[0] // xs
    n_rows = TOP_K * n
    grid_spec = pltpu.PrefetchScalarGridSpec(
        num_scalar_prefetch=4,
        grid=(n_e,),
        in_specs=[pl.BlockSpec(memory_space=pl.ANY),
                  pl.BlockSpec((None, d, d_e), lambda e, *_: (e, 0, 0)),
                  pl.BlockSpec((None, d, d_e), lambda e, *_: (e, 0, 0)),
                  pl.BlockSpec((None, d_e, d), lambda e, *_: (e, 0, 0))],
        out_specs=pl.BlockSpec(memory_space=pl.ANY),
        scratch_shapes=[pltpu.VMEM((GATHER_SLOTS, blk * xs, LANES), jnp.uint32),
                        pltpu.VMEM((SCATTER_SLOTS, blk * ys, LANES), f32),
                        pltpu.VMEM((d, d_e), bf16),
                        pltpu.VMEM((d, d_e), bf16),
                        pltpu.VMEM((d_e, d), bf16),
                        pltpu.SemaphoreType.DMA((GATHER_SLOTS,)),
                        pltpu.SemaphoreType.DMA((SCATTER_SLOTS,))],
    )
    return pl.pallas_call(
        functools.partial(_expert_kernel, blk=blk, n_rows=n_rows),
        grid_spec=grid_spec,
        out_shape=jax.ShapeDtypeStruct(((n_rows + SCATTER_SLOTS * blk) * ys, LANES), f32),
        compiler_params=pltpu.CompilerParams(dimension_semantics=("arbitrary",),
                                             vmem_limit_bytes=VMEM_LIMIT),
        name="experts",
    )(src_rows, dst_rows, pad_start, nblk, h1p, w_gate, w_up, w_down)


def _combine_kernel(h1_ref, route_ref, ya_ref, yb_ref, l2g_ref, l2b_ref, yp_ref, ysm_ref,
                    *, n_p_tiles, alpha):
    i = pl.program_id(0)
    route = route_ref[...]
    tm = route.shape[0]
    ys = ya_ref.shape[0] // tm

    def token_rows(ref):
        return jnp.concatenate([ref[pl.ds(s, tm, stride=ys), :] for s in range(ys)], axis=1)

    y = token_rows(ya_ref) * route[:, 4:5] + token_rows(yb_ref) * route[:, 5:6]
    out = _layer_norm(alpha * h1_ref[...] + y, l2g_ref[...], l2b_ref[...])

    @pl.when(i < n_p_tiles)
    def _():
        yp_ref[...] = out

    @pl.when(i >= n_p_tiles)
    def _():
        ysm_ref[...] = out


def _combine(h1, route, y2, ln2_g, ln2_b, n_p, alpha, tm):
    n, d = h1.shape
    n_s = n - n_p
    n_tiles = n // tm
    n_p_tiles = n_p // tm
    n_s_tiles = n_s // tm
    return pl.pallas_call(
        functools.partial(_combine_kernel, n_p_tiles=n_p_tiles, alpha=alpha),
        grid=(n_tiles,),
        in_specs=[pl.BlockSpec((tm, d), lambda i: (i, 0)),
                  pl.BlockSpec((tm, ROUTE_LANES), lambda i: (i, 0)),
                  pl.BlockSpec((tm * (d // LANES), LANES), lambda i: (i, 0)),
                  pl.BlockSpec((tm * (d // LANES), LANES), lambda i: (n_tiles + i, 0)),
                  pl.BlockSpec((1, d), lambda i: (0, 0)),
                  pl.BlockSpec((1, d), lambda i: (0, 0))],
        out_specs=[pl.BlockSpec((tm, d), lambda i: (jnp.minimum(i, n_p_tiles - 1), 0)),
                   pl.BlockSpec((tm, d), lambda i: (jnp.clip(i - n_p_tiles, 0, n_s_tiles - 1), 0))],
        out_shape=[jax.ShapeDtypeStruct((n_p, d), f32), jax.ShapeDtypeStruct((n_s, d), f32)],
        compiler_params=pltpu.CompilerParams(dimension_semantics=("arbitrary",),
                                             vmem_limit_bytes=VMEM_LIMIT),
        name="combine_ln2",
    )(h1, route, y2, y2, ln2_g, ln2_b)


def _shifted_taps(w, n_t, hist):
    k = w.shape[0]
    zero = jnp.zeros_like(w[0])
    wt = jnp.stack([jnp.stack([w[r - t] if 0 <= r - t < k else zero for r in range(hist)])
                    for t in range(n_t)])
    w2 = jnp.stack([jnp.stack([w[hist - t + j] if (j <= t and 0 <= hist - t + j < k) else zero
                               for j in range(n_t)]) for t in range(n_t)])
    return wt, w2


def kernel(x_prompt, x_sample, state_conv_a, state_conv_b, state_lru, meta_tokens, ln_in_g, ln_in_b, w_in, conv_a_w, conv_a_b, norm_a_g, norm_a_b, conv_b_w, conv_b_b, w_rg, b_rg, w_ig, b_ig, lru_lambda, w_out, ln1_g, ln1_b, w_router_group, b_router_group, w_router_expert, b_router_expert, w_gate, w_up, w_down, ln2_g, ln2_b):
    depth = w_in.shape[0]
    assert depth == 1, "single-layer trunk only"
    bp, tp, d = x_prompt.shape
    bs, ts, _ = x_sample.shape
    n_meta = meta_tokens.shape[0]
    width = conv_a_w.shape[2]
    ka, kb = conv_a_w.shape[1], conv_b_w.shape[1]
    n_e = w_gate.shape[1]
    alpha = (2.0 * depth) ** 0.25
    n_p, n_s = bp * tp, bs * ts
    n = n_p + n_s
    assert ts < ka and ts >= kb - 1
    assert n_p % TOKEN_TILE == 0 and n_s % TOKEN_TILE == 0 and tp % SEQ_TILE == 0
    assert bs % STEP_BATCH == 0 and n_p % STEP_BATCH == 0
    assert n_meta % BF16_ROWS == 0 and n_meta <= TOKEN_TILE and n % n_meta == 0
    assert N_GROUPS * (1 + EXPERTS_PER_GROUP) <= ROUTE_LANES and n_e == N_GROUPS * EXPERTS_PER_GROUP

    row = lambda v: v.reshape(1, -1).astype(f32)
    ln_g, ln_b = row(ln_in_g), row(ln_in_b)
    w_in_b = w_in[0].astype(bf16)
    w_out_b = w_out[0].astype(bf16)
    mix_w = (conv_a_w[0], row(conv_a_b[0]), row(norm_a_g[0]), row(norm_a_b[0]),
             conv_b_w[0], row(conv_b_b[0]),
             jnp.concatenate([w_rg[0], w_ig[0]], axis=-1).astype(bf16),
             row(b_rg[0]), row(b_ig[0]), row(lru_lambda[0]))
    ha, hb = _round_up(ka - 1, SUBLANES), _round_up(kb - 1, SUBLANES)

    x_p = x_prompt.reshape(n_p, d)
    x_s = jnp.concatenate([jnp.swapaxes(x_sample, 0, 1).reshape(n_s, d),
                           jnp.pad(meta_tokens, ((0, TOKEN_TILE - n_meta), (0, 0)))], axis=0)
    h, glu, bx, gate = _inproj(x_p, x_s, n_s + TOKEN_TILE, ln_g, ln_b, w_in_b, TOKEN_TILE)

    _, _, sa_m, sb_m, sh_m = _mixer_seq(
        glu, bx, gate, n, jnp.zeros((1, ha, width), f32), jnp.zeros((1, hb, width), f32),
        jnp.zeros((1, 1, width), f32), mix_w, 1, n_meta, n_meta)

    bcast = lambda s: jnp.broadcast_to(s, (bp,) + s.shape[1:])
    a_p, b_p, nsa_p, nsb_p, nsh_p = _mixer_seq(glu, bx, gate, 0, bcast(sa_m), bcast(sb_m), bcast(sh_m),
                                               mix_w, bp, tp, SEQ_TILE)

    row_major = lambda s: jnp.swapaxes(s, 1, 2)
    wat, wa2 = _shifted_taps(conv_a_w[0], ts, ka - 1)
    wbt, wb2 = _shifted_taps(conv_b_w[0], ts, kb - 1)
    a_s, b_s, nsa_s, nsb_s, nsh_s = _mixer_step(
        glu, bx, gate, n_p, row_major(state_conv_a), row_major(state_conv_b), state_lru,
        (wat, wa2, wbt, wb2), mix_w, bs, ts, STEP_BATCH)

    w_r = jnp.concatenate([w_router_group[0], w_router_expert[0]], axis=1)
    w_r = jnp.pad(w_r, ((0, 0), (0, ROUTE_LANES - w_r.shape[1])))
    wr_hi = w_r.astype(bf16)
    wr_lo = (w_r - wr_hi.astype(f32)).astype(bf16)
    b_r = jnp.concatenate([b_router_group[0], b_router_expert[0]])
    b_r = jnp.pad(b_r, (0, ROUTE_LANES - b_r.shape[0])).reshape(1, ROUTE_LANES)
    h1, h1p, logits = _outproj(a_p, b_p, a_s, b_s, h, w_out_b,
                               row(ln1_g[0]), row(ln1_b[0]), wr_hi, wr_lo, b_r, alpha, TOKEN_TILE)
    route, route_t, cnt = _route(logits)

    blk = EXPERT_ROWS
    e_idx = route_t[0:TOP_K].astype(i32)
    rank = route_t[TOP_K:2 * TOP_K].astype(i32)
    counts = cnt[0, N_GROUPS:N_GROUPS + n_e].astype(i32)
    nblk = (counts + blk - 1) // blk
    pad_end = jnp.cumsum(nblk * blk)
    pad_start = pad_end - nblk * blk
    start_of = jnp.sum(jnp.where(e_idx[None] == jnp.arange(n_e, dtype=i32)[:, None, None],
                                 pad_start[:, None, None], 0), axis=0)
    dest = (start_of + rank).reshape(-1)
    p_max = _round_up(TOP_K * n + n_e * (blk - 1), blk)
    flat = _invert(dest, p_max)
    pos = jnp.arange(p_max, dtype=i32)
    valid = flat >= 0
    src_rows = jnp.where(valid, flat % n, 0)
    dst_rows = jnp.where(valid, flat, TOP_K * n + pos % (SCATTER_SLOTS * blk))
    src_rows = jnp.concatenate([src_rows, jnp.zeros(((GATHER_SLOTS - 1) * blk,), i32)])
    dst_rows = jnp.concatenate([TOP_K * n + (SCATTER_SLOTS - 1) * blk + jnp.arange(blk, dtype=i32), dst_rows])
    y2 = _experts(h1p, src_rows, dst_rows, (pad_start // blk).astype(i32), nblk,
                  w_gate[0], w_up[0], w_down[0], blk)
    y_p, y_s = _combine(h1, route, y2, row(ln2_g[0]), row(ln2_b[0]), n_p, alpha, TOKEN_TILE)

    y_prompt = y_p.reshape(bp, tp, d)
    y_sample = jnp.swapaxes(y_s.reshape(ts, bs, d), 0, 1)
    return (y_prompt, y_sample,
            nsa_p[:, ha - (ka - 1):][None], nsb_p[:, hb - (kb - 1):][None], nsh_p.reshape(1, bp, width),
            row_major(nsa_s), row_major(nsb_s), nsh_s)
```

```python
import functools

import jax
import jax.numpy as jnp
from jax import lax
from jax.experimental import pallas as pl
from jax.experimental.pallas import tpu as pltpu

f32 = jnp.float32
bf16 = jnp.bfloat16
i32 = jnp.int32

CONV_HEADS_A = 8
LRU_HEADS = 8
LRU_C = 8.0
N_GROUPS = 8
EXPERTS_PER_GROUP = 8
TOP_K = 2
LN_EPS = 1e-5

SUBLANES = 8
LANES = 128
BF16_ROWS = 16
VMEM_LIMIT = 56 * 1024 * 1024

TOKEN_TILE = 256
SEQ_TILE = 256
CONV_ROWS = 64
STEP_BATCH = 64
EXPERT_ROWS = 128
ROUTE_LANES = 128
ROUTE_SUB = 128
ROUTE_MAX_TILE = 2304
GATHER_SLOTS = 4
SCATTER_SLOTS = 4
SCALAR_UNROLL = 8
NEG_BIG = -1e30


def _round_up(x, m):
    return (x + m - 1) // m * m


def _layer_norm(x, g, b):
    mu = jnp.mean(x, axis=-1, keepdims=True)
    xc = x - mu
    var = jnp.mean(xc * xc, axis=-1, keepdims=True)
    return xc * lax.rsqrt(var + LN_EPS) * g + b


def _group_norm_silu(x, g, b, group):
    parts = []
    for c in range(x.shape[1] // group):
        xg = x[:, c * group:(c + 1) * group]
        mu = jnp.mean(xg, axis=-1, keepdims=True)
        xc = xg - mu
        var = jnp.mean(xc * xc, axis=-1, keepdims=True)
        parts.append(xc * lax.rsqrt(var + LN_EPS))
    y = jnp.concatenate(parts, axis=-1) * g + b
    return y * jax.nn.sigmoid(y)


def _lru_gates(cb, wg_ref, brg, big, lam):
    heads, hd, _ = wg_ref.shape
    cbb = cb.astype(bf16)
    rs, iz = [], []
    for h in range(heads):
        z = jnp.dot(cbb[:, h * hd:(h + 1) * hd], wg_ref[h], preferred_element_type=f32)
        rs.append(z[:, :hd])
        iz.append(z[:, hd:])
    r = jax.nn.sigmoid(jnp.concatenate(rs, axis=-1) + brg)
    i = jax.nn.sigmoid(jnp.concatenate(iz, axis=-1) + big)
    log_a = -LRU_C * r * jax.nn.softplus(-lam)
    a = jnp.exp(log_a)
    u = jnp.sqrt(-jnp.tanh(log_a) * (a * a + 1.0)) * (i * cb)
    return a, u


def _inproj_kernel(xa_ref, xb_ref, g_ref, b_ref, w_ref, h_ref, glu_ref, bx_ref, gate_ref, *, n_a_tiles):
    i = pl.program_id(0)
    x = jnp.where(i < n_a_tiles, xa_ref[...], xb_ref[...])
    h = _layer_norm(x, g_ref[...], b_ref[...])
    h_ref[...] = h
    xn = h.astype(bf16)
    width = glu_ref.shape[1]
    ch = min(width, 512)
    for c in range(width // ch):
        lo = c * ch
        av = jnp.dot(xn, w_ref[:, lo:lo + ch], preferred_element_type=f32)
        ag = jnp.dot(xn, w_ref[:, width + lo:width + lo + ch], preferred_element_type=f32)
        glu_ref[:, lo:lo + ch] = av * jax.nn.sigmoid(ag)
        bx_ref[:, lo:lo + ch] = jnp.dot(xn, w_ref[:, 2 * width + lo:2 * width + lo + ch],
                                        preferred_element_type=f32)
        bg = jnp.dot(xn, w_ref[:, 3 * width + lo:3 * width + lo + ch], preferred_element_type=f32)
        gate_ref[:, lo:lo + ch] = jax.nn.gelu(bg)


def _inproj(xa, xb, n_b_rows, ln_g, ln_b, w_in_b, tm):
    na, d = xa.shape
    width = w_in_b.shape[1] // 4
    n_a_tiles = na // tm
    n_tiles = (na + n_b_rows) // tm
    n_b_tiles = max(n_tiles - n_a_tiles, 1)
    row_out = pl.BlockSpec((tm, width), lambda i: (i, 0))
    return pl.pallas_call(
        functools.partial(_inproj_kernel, n_a_tiles=n_a_tiles),
        grid=(n_tiles,),
        in_specs=[
            pl.BlockSpec((tm, d), lambda i: (jnp.minimum(i, n_a_tiles - 1), 0)),
            pl.BlockSpec((tm, d), lambda i: (jnp.clip(i - n_a_tiles, 0, n_b_tiles - 1), 0)),
            pl.BlockSpec((1, d), lambda i: (0, 0)),
            pl.BlockSpec((1, d), lambda i: (0, 0)),
            pl.BlockSpec((d, 4 * width), lambda i: (0, 0), pipeline_mode=pl.Buffered(1)),
        ],
        out_specs=[pl.BlockSpec((tm, d), lambda i: (i, 0)), row_out, row_out, row_out],
        out_shape=[jax.ShapeDtypeStruct((na + n_b_rows, d), f32)]
        + [jax.ShapeDtypeStruct((na + n_b_rows, width), f32)] * 3,
        compiler_params=pltpu.CompilerParams(dimension_semantics=("arbitrary",),
                                             vmem_limit_bytes=VMEM_LIMIT),
        name="inproj",
    )(xa, xb, ln_g, ln_b, w_in_b)


def _mixer_seq_kernel(glu_ref, bx_ref, gate_ref, sa_ref, sb_ref, sh_ref,
                      wa_ref, ba_ref, ng_ref, nb_ref, wb_ref, bb_ref, wg_ref, brg_ref, big_ref, lam_ref,
                      aout_ref, bout_ref, nsa_ref, nsb_ref, nsh_ref,
                      wina, winb, xsa, xsb, ca_s, cb_s, a_s, u_s, hcar):
    tau = pl.program_id(1)
    tt, width = glu_ref.shape
    ka, kb = wa_ref.shape[0], wb_ref.shape[0]
    ha, hb = sa_ref.shape[0], sb_ref.shape[0]

    @pl.when(tau == 0)
    def _():
        wina[0:ha, :] = sa_ref[...]
        winb[0:hb, :] = sb_ref[...]
        hcar[...] = jnp.broadcast_to(sh_ref[...], hcar.shape)

    wina[ha:ha + tt, :] = glu_ref[...]
    winb[hb:hb + tt, :] = bx_ref[...]
    for s in sorted({(ha - (ka - 1) + k) % SUBLANES for k in range(ka)} - {0}):
        xsa[s - 1] = wina[pl.ds(s, tt + ha - SUBLANES), :]
    for s in sorted({(hb - (kb - 1) + k) % SUBLANES for k in range(kb)} - {0}):
        xsb[s - 1] = winb[pl.ds(s, tt + hb - SUBLANES), :]

    def tap(win, xs, off, r0, lanes):
        q, s = divmod(off, SUBLANES)
        src = win if s == 0 else xs.at[s - 1]
        return src[pl.ds(r0 + SUBLANES * q, SUBLANES), lanes]

    rows = min(CONV_ROWS, tt)
    for g in range(width // LANES):
        lanes = pl.ds(g * LANES, LANES)
        bcast = lambda ref, k: jnp.broadcast_to(ref[k:k + 1, lanes], (SUBLANES, LANES))
        wa_k = [bcast(wa_ref, k) for k in range(ka)]
        wb_k = [bcast(wb_ref, k) for k in range(kb)]
        ba_g, bb_g = bcast(ba_ref, 0), bcast(bb_ref, 0)

        def conv_block(rb, carry):
            for j in range(rows // SUBLANES):
                r0 = pl.multiple_of(rb * rows, rows) + j * SUBLANES
                acc = ba_g
                for k in range(ka):
                    acc = acc + wa_k[k] * tap(wina, xsa, ha - (ka - 1) + k, r0, lanes)
                ca_s[pl.ds(r0, SUBLANES), lanes] = acc
                accb = bb_g
                for k in range(kb):
                    accb = accb + wb_k[k] * tap(winb, xsb, hb - (kb - 1) + k, r0, lanes)
                cb_s[pl.ds(r0, SUBLANES), lanes] = accb
            return carry

        lax.fori_loop(0, tt // rows, conv_block, 0)

    a_out = _group_norm_silu(ca_s[...], ng_ref[...], nb_ref[...], width // CONV_HEADS_A)
    aout_ref[...] = a_out.astype(aout_ref.dtype)

    a, u = _lru_gates(cb_s[...], wg_ref, brg_ref[...], big_ref[...], lam_ref[...])
    row8 = lax.broadcasted_iota(i32, (tt, width), 0) & (SUBLANES - 1)
    d = 1
    while d < SUBLANES:
        m = row8 >= d
        a_sh = jnp.where(m, pltpu.roll(a, d, axis=0), 1.0)
        u_sh = jnp.where(m, pltpu.roll(u, d, axis=0), 0.0)
        u = a * u_sh + u
        a = a * a_sh
        d *= 2
    a_s[...] = a
    u_s[...] = u

    def scan_block(j, h):
        r0 = pl.multiple_of(j * SUBLANES, SUBLANES)
        hblk = a_s[pl.ds(r0, SUBLANES), :] * h + u_s[pl.ds(r0, SUBLANES), :]
        u_s[pl.ds(r0, SUBLANES), :] = hblk
        return jnp.broadcast_to(hblk[SUBLANES - 1:SUBLANES, :], h.shape)

    h_last = lax.fori_loop(0, tt // SUBLANES, scan_block, hcar[...])
    hcar[...] = h_last
    bout_ref[...] = (u_s[...] * gate_ref[...]).astype(bout_ref.dtype)

    tail_a = wina[tt:tt + ha, :]
    tail_b = winb[tt:tt + hb, :]
    wina[0:ha, :] = tail_a
    winb[0:hb, :] = tail_b
    nsa_ref[...] = tail_a
    nsb_ref[...] = tail_b
    nsh_ref[...] = h_last[0:1, :]


def _mixer_seq(glu, bx, gate, row_off, sa_pad, sb_pad, sh, mix_w, n_seq, seq_len, tt):
    wa, ba, ng, nb, wb, bb, wg, brg, big, lam = mix_w
    width = glu.shape[1]
    ha, hb = sa_pad.shape[1], sb_pad.shape[1]
    n_t = seq_len // tt
    off = row_off // tt
    row_in = pl.BlockSpec((tt, width), lambda b, t: (off + b * n_t + t, 0))
    row_out = pl.BlockSpec((tt, width), lambda b, t: (b * n_t + t, 0))

    def const(shape):
        return pl.BlockSpec(shape, lambda b, t: (0,) * len(shape))

    def per_seq(r):
        return pl.BlockSpec((None, r, width), lambda b, t: (b, 0, 0))

    return pl.pallas_call(
        _mixer_seq_kernel,
        grid=(n_seq, n_t),
        in_specs=[row_in, row_in, row_in, per_seq(ha), per_seq(hb), per_seq(1),
                  const(wa.shape), const(ba.shape), const(ng.shape), const(nb.shape),
                  const(wb.shape), const(bb.shape), const(wg.shape), const(brg.shape),
                  const(big.shape), const(lam.shape)],
        out_specs=[row_out, row_out, per_seq(ha), per_seq(hb), per_seq(1)],
        out_shape=[jax.ShapeDtypeStruct((n_seq * seq_len, width), bf16),
                   jax.ShapeDtypeStruct((n_seq * seq_len, width), bf16),
                   jax.ShapeDtypeStruct((n_seq, ha, width), f32),
                   jax.ShapeDtypeStruct((n_seq, hb, width), f32),
                   jax.ShapeDtypeStruct((n_seq, 1, width), f32)],
        scratch_shapes=[pltpu.VMEM((ha + tt, width), f32),
                        pltpu.VMEM((hb + tt, width), f32),
                        pltpu.VMEM((SUBLANES - 1, tt + ha - SUBLANES, width), f32),
                        pltpu.VMEM((SUBLANES - 1, tt + hb - SUBLANES, width), f32),
                        pltpu.VMEM((tt, width), f32),
                        pltpu.VMEM((tt, width), f32),
                        pltpu.VMEM((tt, width), f32),
                        pltpu.VMEM((tt, width), f32),
                        pltpu.VMEM((SUBLANES, width), f32)],
        compiler_params=pltpu.CompilerParams(dimension_semantics=("arbitrary", "arbitrary"),
                                             vmem_limit_bytes=VMEM_LIMIT),
        name="mixer_seq",
    )(glu, bx, gate, sa_pad, sb_pad, sh, wa, ba, ng, nb, wb, bb, wg, brg, big, lam)


def _mixer_step_kernel(glu_ref, bx_ref, gate_ref, sa_ref, sb_ref, sh_ref,
                       wat_ref, wa2_ref, wbt_ref, wb2_ref,
                       ba_ref, ng_ref, nb_ref, bb_ref, wg_ref, brg_ref, big_ref, lam_ref,
                       aout_ref, bout_ref, nsa_ref, nsb_ref, nsh_ref,
                       hista, histb, hcar):
    t = pl.program_id(1)
    n_t = hista.shape[0]
    width = glu_ref.shape[1]

    def roll_buffer(s_ref, ns_ref, x_ref):
        shift = n_t - s_ref.shape[0]

        @pl.when(t == 0)
        def _():
            for r in range(max(-shift, 0)):
                ns_ref[r] = s_ref[r + n_t]

        if shift <= 0:
            ns_ref[t - shift] = x_ref[...]
        else:
            @pl.when(t >= shift)
            def _():
                ns_ref[t - shift] = x_ref[...]

    @pl.when(t == 0)
    def _():
        hista[...] = jnp.zeros_like(hista)
        histb[...] = jnp.zeros_like(histb)
        hcar[...] = sh_ref[...]

    roll_buffer(sa_ref, nsa_ref, glu_ref)
    roll_buffer(sb_ref, nsb_ref, bx_ref)
    hista[t] = glu_ref[...]
    histb[t] = bx_ref[...]

    ca = jnp.broadcast_to(ba_ref[...], glu_ref.shape)
    for r in range(sa_ref.shape[0]):
        ca = ca + wat_ref[r:r + 1, :] * sa_ref[r]
    cb = jnp.broadcast_to(bb_ref[...], bx_ref.shape)
    for r in range(sb_ref.shape[0]):
        cb = cb + wbt_ref[r:r + 1, :] * sb_ref[r]
    for j in range(n_t):
        ca = ca + wa2_ref[j:j + 1, :] * hista[j]
        cb = cb + wb2_ref[j:j + 1, :] * histb[j]
    a_out = _group_norm_silu(ca, ng_ref[...], nb_ref[...], width // CONV_HEADS_A)
    aout_ref[...] = a_out.astype(aout_ref.dtype)

    a, u = _lru_gates(cb, wg_ref, brg_ref[...], big_ref[...], lam_ref[...])
    h = a * hcar[...] + u
    hcar[...] = h
    nsh_ref[...] = h
    bout_ref[...] = (h * gate_ref[...]).astype(bout_ref.dtype)


def _mixer_step(glu, bx, gate, row_off, sa, sb, sh, step_w, mix_w, n_seq, n_t, bt):
    wat, wa2, wbt, wb2 = step_w
    wa, ba, ng, nb, wb, bb, wg, brg, big, lam = mix_w
    width = glu.shape[1]
    ka1, kb1 = sa.shape[1], sb.shape[1]
    off = row_off // bt
    nb_tiles = n_seq // bt
    row_in = pl.BlockSpec((bt, width), lambda j, t: (off + t * nb_tiles + j, 0))
    row_out = pl.BlockSpec((bt, width), lambda j, t: (t * nb_tiles + j, 0))

    def const(shape):
        return pl.BlockSpec(shape, lambda j, t: (0,) * len(shape))

    def per_t(arr):
        return pl.BlockSpec((None,) + arr.shape[1:], lambda j, t: (t, 0, 0))

    def state(r):
        return pl.BlockSpec((None, r, bt, width), lambda j, t: (0, 0, j, 0))

    state_h = pl.BlockSpec((None, bt, width), lambda j, t: (0, j, 0))
    return pl.pallas_call(
        _mixer_step_kernel,
        grid=(nb_tiles, n_t),
        in_specs=[row_in, row_in, row_in, state(ka1), state(kb1), state_h,
                  per_t(wat), per_t(wa2), per_t(wbt), per_t(wb2),
                  const(ba.shape), const(ng.shape), const(nb.shape), const(bb.shape),
                  const(wg.shape), const(brg.shape), const(big.shape), const(lam.shape)],
        out_specs=[row_out, row_out, state(ka1), state(kb1), state_h],
        out_shape=[jax.ShapeDtypeStruct((n_seq * n_t, width), bf16),
                   jax.ShapeDtypeStruct((n_seq * n_t, width), bf16),
                   jax.ShapeDtypeStruct((1, ka1, n_seq, width), f32),
                   jax.ShapeDtypeStruct((1, kb1, n_seq, width), f32),
                   jax.ShapeDtypeStruct((1, n_seq, width), f32)],
        scratch_shapes=[pltpu.VMEM((n_t, bt, width), f32),
                        pltpu.VMEM((n_t, bt, width), f32),
                        pltpu.VMEM((bt, width), f32)],
        compiler_params=pltpu.CompilerParams(dimension_semantics=("arbitrary", "arbitrary"),
                                             vmem_limit_bytes=VMEM_LIMIT),
        name="mixer_step",
    )(glu, bx, gate, sa, sb, sh, wat, wa2, wbt, wb2,
      ba, ng, nb, bb, wg, brg, big, lam)


def _outproj_kernel(ap_ref, bp_ref, as_ref, bs_ref, h_ref, wo_ref,
                    l1g_ref, l1b_ref, wrh_ref, wrl_ref, br_ref,
                    h1_ref, h1p_ref, z_ref, *, n_p_tiles, alpha):
    i = pl.program_id(0)
    is_p = i < n_p_tiles
    a = jnp.where(is_p, ap_ref[...], as_ref[...])
    b = jnp.where(is_p, bp_ref[...], bs_ref[...])
    width = a.shape[1]
    mix = (jnp.dot(a, wo_ref[0:width, :], preferred_element_type=f32)
           + jnp.dot(b, wo_ref[width:2 * width, :], preferred_element_type=f32))
    h1 = _layer_norm(alpha * h_ref[...] + mix, l1g_ref[...], l1b_ref[...])
    h1_ref[...] = h1

    xh = h1.astype(bf16)
    xh32 = xh.astype(f32)
    half = h1.shape[1] // 2
    lo = lax.shift_right_logical(lax.bitcast_convert_type(xh32[:, :half], jnp.uint32), jnp.uint32(16))
    hi = lax.bitcast_convert_type(xh32[:, half:], jnp.uint32) & jnp.uint32(0xFFFF0000)
    word = lo | hi
    n_sub = half // LANES
    for s in range(n_sub):
        h1p_ref[pl.ds(s, word.shape[0], stride=n_sub), :] = word[:, s * LANES:(s + 1) * LANES]

    xl = (h1 - xh32).astype(bf16)
    z_ref[...] = (jnp.dot(xh, wrh_ref[...], preferred_element_type=f32)
                  + jnp.dot(xl, wrh_ref[...], preferred_element_type=f32)
                  + jnp.dot(xh, wrl_ref[...], preferred_element_type=f32)) + br_ref[...]


def _route_kernel(z_ref, route_ref, routet_ref, cnt_ref, carry, *, sub):
    i = pl.program_id(0)
    z = z_ref[...]
    lane = lax.broadcasted_iota(i32, z.shape, 1).astype(f32)
    n_g, epg = float(N_GROUPS), float(EXPERTS_PER_GROUP)
    far = float(2 * ROUTE_LANES)

    gm = lane < n_g
    zg = jnp.where(gm, z, NEG_BIG)
    pg = jnp.where(gm, jnp.exp(zg - jnp.max(zg, axis=-1, keepdims=True)), 0.0)
    pg = pg / jnp.sum(pg, axis=-1, keepdims=True)
    g_top = jnp.max(pg, axis=-1, keepdims=True)
    g_idx = jnp.min(jnp.where(gm & (pg == g_top), lane, far), axis=-1, keepdims=True)

    lo = n_g + g_idx * epg
    em = (lane >= lo) & (lane < lo + epg)
    ze = jnp.where(em, z, NEG_BIG)
    pe = jnp.where(em, jnp.exp(ze - jnp.max(ze, axis=-1, keepdims=True)), 0.0)
    pe = pe / jnp.sum(pe, axis=-1, keepdims=True)
    v1 = jnp.max(jnp.where(em, pe, -1.0), axis=-1, keepdims=True)
    l1 = jnp.min(jnp.where(em & (pe == v1), lane, far), axis=-1, keepdims=True)
    pe2 = jnp.where(em & (lane != l1), pe, -1.0)
    v2 = jnp.max(pe2, axis=-1, keepdims=True)
    l2 = jnp.min(jnp.where(pe2 == v2, lane, far), axis=-1, keepdims=True)
    den = v1 + v2
    gate1 = v1 / den * g_top
    gate2 = v2 / den * g_top

    @pl.when(i == 0)
    def _():
        carry[...] = jnp.zeros_like(carry)

    o1 = lane == l1
    o2 = lane == l2
    onehot = jnp.where(o1 | o2, 1.0, 0.0)
    ri = lax.broadcasted_iota(i32, (sub, sub), 0)
    ci = lax.broadcasted_iota(i32, (sub, sub), 1)
    tri = jnp.where(ci < ri, 1.0, 0.0).astype(bf16)
    counts = carry[0:1, :]
    prefix = []
    for s in range(z.shape[0] // sub):
        oh = onehot[s * sub:(s + 1) * sub, :]
        prefix.append(jnp.dot(tri, oh.astype(bf16), preferred_element_type=f32) + counts)
        counts = counts + jnp.sum(oh, axis=0, keepdims=True)
    prefix = jnp.concatenate(prefix, axis=0)
    rank1 = jnp.sum(jnp.where(o1, prefix, 0.0), axis=-1, keepdims=True)
    rank2 = jnp.sum(jnp.where(o2, prefix, 0.0), axis=-1, keepdims=True)
    new_carry = jnp.broadcast_to(counts, carry.shape)
    carry[...] = new_carry
    cnt_ref[...] = new_carry

    route = jnp.where(lane == 0.0, l1 - n_g, 0.0)
    route = jnp.where(lane == 1.0, l2 - n_g, route)
    route = jnp.where(lane == 2.0, rank1, route)
    route = jnp.where(lane == 3.0, rank2, route)
    route = jnp.where(lane == 4.0, gate1, route)
    route = jnp.where(lane == 5.0, gate2, route)
    route_ref[...] = route
    routet_ref[...] = route.T[0:routet_ref.shape[0], :]


def _outproj(a_p, b_p, a_s, b_s, h, w_out_b, ln1_g, ln1_b,
             wr_hi, wr_lo, b_r, alpha, tm):
    n_p, width = a_p.shape
    n_s = a_s.shape[0]
    d = h.shape[1]
    n_p_tiles = n_p // tm
    n_s_tiles = n_s // tm
    n_tiles = n_p_tiles + n_s_tiles

    def p_map(i):
        return (jnp.minimum(i, n_p_tiles - 1), 0)

    def s_map(i):
        return (jnp.clip(i - n_p_tiles, 0, n_s_tiles - 1), 0)

    def const(shape, **kw):
        return pl.BlockSpec(shape, lambda i: (0,) * len(shape), **kw)

    return pl.pallas_call(
        functools.partial(_outproj_kernel, n_p_tiles=n_p_tiles, alpha=alpha),
        grid=(n_tiles,),
        in_specs=[pl.BlockSpec((tm, width), p_map), pl.BlockSpec((tm, width), p_map),
                  pl.BlockSpec((tm, width), s_map), pl.BlockSpec((tm, width), s_map),
                  pl.BlockSpec((tm, d), lambda i: (i, 0)),
                  const((2 * width, d), pipeline_mode=pl.Buffered(1)),
                  const((1, d)), const((1, d)),
                  const((d, ROUTE_LANES)), const((d, ROUTE_LANES)), const((1, ROUTE_LANES))],
        out_specs=[pl.BlockSpec((tm, d), lambda i: (i, 0)),
                   pl.BlockSpec((tm * (d // 2 // LANES), LANES), lambda i: (i, 0)),
                   pl.BlockSpec((tm, ROUTE_LANES), lambda i: (i, 0))],
        out_shape=[jax.ShapeDtypeStruct((n_p + n_s, d), f32),
                   jax.ShapeDtypeStruct(((n_p + n_s) * (d // 2 // LANES), LANES), jnp.uint32),
                   jax.ShapeDtypeStruct((n_p + n_s, ROUTE_LANES), f32)],
        compiler_params=pltpu.CompilerParams(dimension_semantics=("arbitrary",),
                                             vmem_limit_bytes=VMEM_LIMIT),
        name="outproj",
    )(a_p, b_p, a_s, b_s, h, w_out_b, ln1_g, ln1_b, wr_hi, wr_lo, b_r)


def _route_tile(n):
    return max(t for t in range(ROUTE_SUB, ROUTE_MAX_TILE + 1, ROUTE_SUB) if n % t == 0)


def _route(z):
    n = z.shape[0]
    tr = _route_tile(n)
    return pl.pallas_call(
        functools.partial(_route_kernel, sub=ROUTE_SUB),
        grid=(n // tr,),
        in_specs=[pl.BlockSpec((tr, ROUTE_LANES), lambda i: (i, 0))],
        out_specs=[pl.BlockSpec((tr, ROUTE_LANES), lambda i: (i, 0)),
                   pl.BlockSpec((SUBLANES, tr), lambda i: (0, i)),
                   pl.BlockSpec((SUBLANES, ROUTE_LANES), lambda i: (0, 0))],
        out_shape=[jax.ShapeDtypeStruct((n, ROUTE_LANES), f32),
                   jax.ShapeDtypeStruct((SUBLANES, n), f32),
                   jax.ShapeDtypeStruct((SUBLANES, ROUTE_LANES), f32)],
        scratch_shapes=[pltpu.VMEM((SUBLANES, ROUTE_LANES), f32)],
        compiler_params=pltpu.CompilerParams(dimension_semantics=("arbitrary",),
                                             vmem_limit_bytes=VMEM_LIMIT),
        name="route",
    )(z)


def _invert_kernel(dest_ref, flat_ref, fill):
    fill[...] = jnp.full(fill.shape, -1, fill.dtype)
    pltpu.sync_copy(fill, flat_ref)

    def place(f, carry):
        flat_ref[dest_ref[f]] = f
        return carry

    lax.fori_loop(0, dest_ref.shape[0], place, 0, unroll=SCALAR_UNROLL)


def _invert(dest, p_max):
    return pl.pallas_call(
        _invert_kernel,
        in_specs=[pl.BlockSpec(memory_space=pltpu.SMEM)],
        out_specs=pl.BlockSpec(memory_space=pltpu.SMEM),
        out_shape=jax.ShapeDtypeStruct((p_max,), i32),
        scratch_shapes=[pltpu.VMEM((p_max,), i32)],
        name="invert_dispatch",
    )(dest)


def _expert_kernel(src_ref, dst_ref, start_ref, nblk_ref,
                   h1_hbm, wg_ref, wu_ref, wd_ref, y2_hbm,
                   xbuf, ybuf, wgb, wub, wdb, gsem, osem, *, blk, n_rows):
    e = pl.program_id(0)
    n_e = pl.num_programs(0)
    nb = nblk_ref[e]
    g0 = start_ref[e]

    xs = xbuf.shape[1] // blk
    ys = ybuf.shape[1] // blk

    def gather(g, slot):
        for r in range(blk):
            tok = src_ref[g * blk + r]
            pltpu.make_async_copy(h1_hbm.at[pl.ds(pl.multiple_of(tok * xs, xs), xs), :],
                                  xbuf.at[slot, pl.ds(r * xs, xs), :], gsem.at[slot]
                                  ).start(priority=r % 2)

    def scatter(g, slot):
        for r in range(blk):
            row = dst_ref[(g + 1) * blk + r]
            pltpu.make_async_copy(ybuf.at[slot, pl.ds(r * ys, ys), :],
                                  y2_hbm.at[pl.ds(pl.multiple_of(row * ys, ys), ys), :],
                                  osem.at[slot]).start(priority=r % 2)

    def dump_copy(slot):
        dst = y2_hbm.at[pl.ds((n_rows + slot * blk) * ys, blk * ys), :]
        return pltpu.make_async_copy(ybuf.at[slot], dst, osem.at[slot])

    def wait_gather(slot):
        pltpu.make_async_copy(h1_hbm.at[pl.ds(0, blk * xs), :], xbuf.at[slot], gsem.at[slot]).wait()

    nx, ny = xbuf.shape[0], ybuf.shape[0]
    ahead = nx - 1

    @pl.when(e == 0)
    def _():
        ybuf[...] = jnp.zeros(ybuf.shape, ybuf.dtype)
        for s in range(ny - 1):
            dump_copy(s).start()
        for g in range(ahead):
            gather(g, g)

    @pl.when(nb > 0)
    def _():
        wgb[...] = wg_ref[...].astype(bf16)
        wub[...] = wu_ref[...].astype(bf16)
        wdb[...] = wd_ref[...].astype(bf16)

        def chunk(c, carry):
            g = g0 + c
            slot = g % nx
            yslot = g % ny
            wait_gather(slot)
            dump_copy(yslot).wait()
            gather(g + ahead, (g + ahead) % nx)
            scatter(g - 1, (g + ny - 1) % ny)
            xw = [xbuf[slot, pl.ds(s, blk, stride=xs), :] for s in range(xs)]
            x_lo = [lax.bitcast_convert_type(lax.shift_left(w, jnp.uint32(16)), f32) for w in xw]
            x_hi = [lax.bitcast_convert_type(w & jnp.uint32(0xFFFF0000), f32) for w in xw]
            x = jnp.concatenate(x_lo + x_hi, axis=1).astype(bf16)
            hg = jnp.dot(x, wgb[...], preferred_element_type=f32)
            hu = jnp.dot(x, wub[...], preferred_element_type=f32)
            hid = (hg * jax.nn.sigmoid(hg) * hu).astype(bf16)
            y = jnp.dot(hid, wdb[...], preferred_element_type=f32)
            for s in range(ys):
                ybuf[yslot, pl.ds(s, blk, stride=ys), :] = y[:, s * LANES:(s + 1) * LANES]
            return carry

        lax.fori_loop(0, nb, chunk, 0)

    @pl.when(e == n_e - 1)
    def _():
        g_end = g0 + nb
        scatter(g_end - 1, (g_end + ny - 1) % ny)
        for s in range(ny):
            dump_copy(s).wait()
        for j in range(ahead):
            wait_gather((g_end + j) % nx)


def _experts(h1p, src_rows, dst_rows, pad_start, nblk, w_gate, w_up, w_down, blk):
    n_e, d, d_e = w_gate.shape
    xs, ys = d // 2 // LANES, d // LANES
    n = h1p.shape[0] // xs
    n_rows = TOP_K * n
    grid_spec = pltpu.PrefetchScalarGridSpec(
        num_scalar_prefetch=4,
        grid=(n_e,),
        in_specs=[pl.BlockSpec(memory_space=pl.ANY),
                  pl.BlockSpec((None, d, d_e), lambda e, *_: (e, 0, 0)),
                  pl.BlockSpec((None, d, d_e), lambda e, *_: (e, 0, 0)),
                  pl.BlockSpec((None, d_e, d), lambda e, *_: (e, 0, 0))],
        out_specs=pl.BlockSpec(memory_space=pl.ANY),
        scratch_shapes=[pltpu.VMEM((GATHER_SLOTS, blk * xs, LANES), jnp.uint32),
                        pltpu.VMEM((SCATTER_SLOTS, blk * ys, LANES), f32),
                        pltpu.VMEM((d, d_e), bf16),
                        pltpu.VMEM((d, d_e), bf16),
                        pltpu.VMEM((d_e, d), bf16),
                        pltpu.SemaphoreType.DMA((GATHER_SLOTS,)),
                        pltpu.SemaphoreType.DMA((SCATTER_SLOTS,))],
    )
    return pl.pallas_call(
        functools.partial(_expert_kernel, blk=blk, n_rows=n_rows),
        grid_spec=grid_spec,
        out_shape=jax.ShapeDtypeStruct(((n_rows + SCATTER_SLOTS * blk) * ys, LANES), f32),
        compiler_params=pltpu.CompilerParams(dimension_semantics=("arbitrary",),
                                             vmem_limit_bytes=VMEM_LIMIT),
        name="experts",
    )(src_rows, dst_rows, pad_start, nblk, h1p, w_gate, w_up, w_down)


def _combine_kernel(h1_ref, route_ref, ya_ref, yb_ref, l2g_ref, l2b_ref, yp_ref, ysm_ref,
                    *, n_p_tiles, alpha):
    i = pl.program_id(0)
    route = route_ref[...]
    tm = route.shape[0]
    ys = ya_ref.shape[0] // tm

    def token_rows(ref):
        return jnp.concatenate([ref[pl.ds(s, tm, stride=ys), :] for s in range(ys)], axis=1)

    y = token_rows(ya_ref) * route[:, 4:5] + token_rows(yb_ref) * route[:, 5:6]
    out = _layer_norm(alpha * h1_ref[...] + y, l2g_ref[...], l2b_ref[...])

    @pl.when(i < n_p_tiles)
    def _():
        yp_ref[...] = out

    @pl.when(i >= n_p_tiles)
    def _():
        ysm_ref[...] = out


def _combine(h1, route, y2, ln2_g, ln2_b, n_p, alpha, tm):
    n, d = h1.shape
    n_s = n - n_p
    n_tiles = n // tm
    n_p_tiles = n_p // tm
    n_s_tiles = n_s // tm
    return pl.pallas_call(
        functools.partial(_combine_kernel, n_p_tiles=n_p_tiles, alpha=alpha),
        grid=(n_tiles,),
        in_specs=[pl.BlockSpec((tm, d), lambda i: (i, 0)),
                  pl.BlockSpec((tm, ROUTE_LANES), lambda i: (i, 0)),
                  pl.BlockSpec((tm * (d // LANES), LANES), lambda i: (i, 0)),
                  pl.BlockSpec((tm * (d // LANES), LANES), lambda i: (n_tiles + i, 0)),
                  pl.BlockSpec((1, d), lambda i: (0, 0)),
                  pl.BlockSpec((1, d), lambda i: (0, 0))],
        out_specs=[pl.BlockSpec((tm, d), lambda i: (jnp.minimum(i, n_p_tiles - 1), 0)),
                   pl.BlockSpec((tm, d), lambda i: (jnp.clip(i - n_p_tiles, 0, n_s_tiles - 1), 0))],
        out_shape=[jax.ShapeDtypeStruct((n_p, d), f32), jax.ShapeDtypeStruct((n_s, d), f32)],
        compiler_params=pltpu.CompilerParams(dimension_semantics=("arbitrary",),
                                             vmem_limit_bytes=VMEM_LIMIT),
        name="combine_ln2",
    )(h1, route, y2, y2, ln2_g, ln2_b)


def _shifted_taps(w, n_t, hist):
    k = w.shape[0]
    zero = jnp.zeros_like(w[0])
    wt = jnp.stack([jnp.stack([w[r - t] if 0 <= r - t < k else zero for r in range(hist)])
                    for t in range(n_t)])
    w2 = jnp.stack([jnp.stack([w[hist - t + j] if (j <= t and 0 <= hist - t + j < k) else zero
                               for j in range(n_t)]) for t in range(n_t)])
    return wt, w2


def kernel(x_prompt, x_sample, state_conv_a, state_conv_b, state_lru, meta_tokens, ln_in_g, ln_in_b, w_in, conv_a_w, conv_a_b, norm_a_g, norm_a_b, conv_b_w, conv_b_b, w_rg, b_rg, w_ig, b_ig, lru_lambda, w_out, ln1_g, ln1_b, w_router_group, b_router_group, w_router_expert, b_router_expert, w_gate, w_up, w_down, ln2_g, ln2_b):
    depth = w_in.shape[0]
    assert depth == 1, "single-layer trunk only"
    bp, tp, d = x_prompt.shape
    bs, ts, _ = x_sample.shape
    n_meta = meta_tokens.shape[0]
    width = conv_a_w.shape[2]
    ka, kb = conv_a_w.shape[1], conv_b_w.shape[1]
    n_e = w_gate.shape[1]
    alpha = (2.0 * depth) ** 0.25
    n_p, n_s = bp * tp, bs * ts
    n = n_p + n_s
    assert ts < ka and ts >= kb - 1
    assert n_p % TOKEN_TILE == 0 and n_s % TOKEN_TILE == 0 and tp % SEQ_TILE == 0
    assert bs % STEP_BATCH == 0 and n_p % STEP_BATCH == 0
    assert n_meta % BF16_ROWS == 0 and n_meta <= TOKEN_TILE and n % n_meta == 0
    assert N_GROUPS * (1 + EXPERTS_PER_GROUP) <= ROUTE_LANES and n_e == N_GROUPS * EXPERTS_PER_GROUP

    row = lambda v: v.reshape(1, -1).astype(f32)
    ln_g, ln_b = row(ln_in_g), row(ln_in_b)
    w_in_b = w_in[0].astype(bf16)
    w_out_b = w_out[0].astype(bf16)
    mix_w = (conv_a_w[0], row(conv_a_b[0]), row(norm_a_g[0]), row(norm_a_b[0]),
             conv_b_w[0], row(conv_b_b[0]),
             jnp.concatenate([w_rg[0], w_ig[0]], axis=-1).astype(bf16),
             row(b_rg[0]), row(b_ig[0]), row(lru_lambda[0]))
    ha, hb = _round_up(ka - 1, SUBLANES), _round_up(kb - 1, SUBLANES)

    x_p = x_prompt.reshape(n_p, d)
    x_s = jnp.concatenate([jnp.swapaxes(x_sample, 0, 1).reshape(n_s, d),
                           jnp.pad(meta_tokens, ((0, TOKEN_TILE - n_meta), (0, 0)))], axis=0)
    h, glu, bx, gate = _inproj(x_p, x_s, n_s + TOKEN_TILE, ln_g, ln_b, w_in_b, TOKEN_TILE)

    _, _, sa_m, sb_m, sh_m = _mixer_seq(
        glu, bx, gate, n, jnp.zeros((1, ha, width), f32), jnp.zeros((1, hb, width), f32),
        jnp.zeros((1, 1, width), f32), mix_w, 1, n_meta, n_meta)

    bcast = lambda s: jnp.broadcast_to(s, (bp,) + s.shape[1:])
    a_p, b_p, nsa_p, nsb_p, nsh_p = _mixer_seq(glu, bx, gate, 0, bcast(sa_m), bcast(sb_m), bcast(sh_m),
                                               mix_w, bp, tp, SEQ_TILE)

    row_major = lambda s: jnp.swapaxes(s, 1, 2)
    wat, wa2 = _shifted_taps(conv_a_w[0], ts, ka - 1)
    wbt, wb2 = _shifted_taps(conv_b_w[0], ts, kb - 1)
    a_s, b_s, nsa_s, nsb_s, nsh_s = _mixer_step(
        glu, bx, gate, n_p, row_major(state_conv_a), row_major(state_conv_b), state_lru,
        (wat, wa2, wbt, wb2), mix_w, bs, ts, STEP_BATCH)

    w_r = jnp.concatenate([w_router_group[0], w_router_expert[0]], axis=1)
    w_r = jnp.pad(w_r, ((0, 0), (0, ROUTE_LANES - w_r.shape[1])))
    wr_hi = w_r.astype(bf16)
    wr_lo = (w_r - wr_hi.astype(f32)).astype(bf16)
    b_r = jnp.concatenate([b_router_group[0], b_router_expert[0]])
    b_r = jnp.pad(b_r, (0, ROUTE_LANES - b_r.shape[0])).reshape(1, ROUTE_LANES)
    h1, h1p, logits = _outproj(a_p, b_p, a_s, b_s, h, w_out_b,
                               row(ln1_g[0]), row(ln1_b[0]), wr_hi, wr_lo, b_r, alpha, TOKEN_TILE)
    route, route_t, cnt = _route(logits)

    blk = EXPERT_ROWS
    e_idx = route_t[0:TOP_K].astype(i32)
    rank = route_t[TOP_K:2 * TOP_K].astype(i32)
    counts = cnt[0, N_GROUPS:N_GROUPS + n_e].astype(i32)
    nblk = (counts + blk - 1) // blk
    pad_end = jnp.cumsum(nblk * blk)
    pad_start = pad_end - nblk * blk
    start_of = jnp.sum(jnp.where(e_idx[None] == jnp.arange(n_e, dtype=i32)[:, None, None],
                                 pad_start[:, None, None], 0), axis=0)
    dest = (start_of + rank).reshape(-1)
    p_max = _round_up(TOP_K * n + n_e * (blk - 1), blk)
    flat = _invert(dest, p_max)
    pos = jnp.arange(p_max, dtype=i32)
    valid = flat >= 0
    src_rows = jnp.where(valid, flat % n, 0)
    dst_rows = jnp.where(valid, flat, TOP_K * n + pos % (SCATTER_SLOTS * blk))
    src_rows = jnp.concatenate([src_rows, jnp.zeros(((GATHER_SLOTS - 1) * blk,), i32)])
    dst_rows = jnp.concatenate([TOP_K * n + (SCATTER_SLOTS - 1) * blk + jnp.arange(blk, dtype=i32), dst_rows])
    y2 = _experts(h1p, src_rows, dst_rows, (pad_start // blk).astype(i32), nblk,
                  w_gate[0], w_up[0], w_down[0], blk)
    y_p, y_s = _combine(h1, route, y2, row(ln2_g[0]), row(ln2_b[0]), n_p, alpha, TOKEN_TILE)

    y_prompt = y_p.reshape(bp, tp, d)
    y_sample = jnp.swapaxes(y_s.reshape(ts, bs, d), 0, 1)
    return (y_prompt, y_sample,
            nsa_p[:, ha - (ka - 1):][None], nsb_p[:, hb - (kb - 1):][None], nsh_p.reshape(1, bp, width),
            row_major(nsa_s), row_major(nsb_s), nsh_s)
```

```python
import functools

import jax
import jax.numpy as jnp
from jax import lax
from jax.experimental import pallas as pl
from jax.experimental.pallas import tpu as pltpu

f32 = jnp.float32
bf16 = jnp.bfloat16
i32 = jnp.int32

CONV_HEADS_A = 8
LRU_HEADS = 8
LRU_C = 8.0
N_GROUPS = 8
EXPERTS_PER_GROUP = 8
TOP_K = 2
LN_EPS = 1e-5

SUBLANES = 8
LANES = 128
BF16_ROWS = 16
VMEM_LIMIT = 56 * 1024 * 1024

TOKEN_TILE = 256
SEQ_TILE = 256
CONV_ROWS = 64
STEP_BATCH = 64
EXPERT_ROWS = 96
ROUTE_LANES = 128
ROUTE_SUB = 128
ROUTE_MAX_TILE = 2304
GATHER_SLOTS = 4
SCATTER_SLOTS = 4
SCALAR_UNROLL = 8
NEG_BIG = -1e30


def _round_up(x, m):
    return (x + m - 1) // m * m


def _layer_norm(x, g, b):
    mu = jnp.mean(x, axis=-1, keepdims=True)
    xc = x - mu
    var = jnp.mean(xc * xc, axis=-1, keepdims=True)
    return xc * lax.rsqrt(var + LN_EPS) * g + b


def _group_norm_silu(x, g, b, group):
    parts = []
    for c in range(x.shape[1] // group):
        xg = x[:, c * group:(c + 1) * group]
        mu = jnp.mean(xg, axis=-1, keepdims=True)
        xc = xg - mu
        var = jnp.mean(xc * xc, axis=-1, keepdims=True)
        parts.append(xc * lax.rsqrt(var + LN_EPS))
    y = jnp.concatenate(parts, axis=-1) * g + b
    return y * jax.nn.sigmoid(y)


def _lru_gates(cb, wg_ref, brg, big, lam):
    heads, hd, _ = wg_ref.shape
    cbb = cb.astype(bf16)
    rs, iz = [], []
    for h in range(heads):
        z = jnp.dot(cbb[:, h * hd:(h + 1) * hd], wg_ref[h], preferred_element_type=f32)
        rs.append(z[:, :hd])
        iz.append(z[:, hd:])
    r = jax.nn.sigmoid(jnp.concatenate(rs, axis=-1) + brg)
    i = jax.nn.sigmoid(jnp.concatenate(iz, axis=-1) + big)
    log_a = -LRU_C * r * jax.nn.softplus(-lam)
    a = jnp.exp(log_a)
    u = jnp.sqrt(-jnp.tanh(log_a) * (a * a + 1.0)) * (i * cb)
    return a, u


def _inproj_kernel(xa_ref, xb_ref, g_ref, b_ref, w_ref, h_ref, glu_ref, bx_ref, gate_ref, *, n_a_tiles):
    i = pl.program_id(0)
    x = jnp.where(i < n_a_tiles, xa_ref[...], xb_ref[...])
    h = _layer_norm(x, g_ref[...], b_ref[...])
    h_ref[...] = h
    xn = h.astype(bf16)
    width = glu_ref.shape[1]
    ch = min(width, 512)
    for c in range(width // ch):
        lo = c * ch
        av = jnp.dot(xn, w_ref[:, lo:lo + ch], preferred_element_type=f32)
        ag = jnp.dot(xn, w_ref[:, width + lo:width + lo + ch], preferred_element_type=f32)
        glu_ref[:, lo:lo + ch] = av * jax.nn.sigmoid(ag)
        bx_ref[:, lo:lo + ch] = jnp.dot(xn, w_ref[:, 2 * width + lo:2 * width + lo + ch],
                                        preferred_element_type=f32)
        bg = jnp.dot(xn, w_ref[:, 3 * width + lo:3 * width + lo + ch], preferred_element_type=f32)
        gate_ref[:, lo:lo + ch] = jax.nn.gelu(bg)


def _inproj(xa, xb, n_b_rows, ln_g, ln_b, w_in_b, tm):
    na, d = xa.shape
    width = w_in_b.shape[1] // 4
    n_a_tiles = na // tm
    n_tiles = (na + n_b_rows) // tm
    n_b_tiles = max(n_tiles - n_a_tiles, 1)
    row_out = pl.BlockSpec((tm, width), lambda i: (i, 0))
    return pl.pallas_call(
        functools.partial(_inproj_kernel, n_a_tiles=n_a_tiles),
        grid=(n_tiles,),
        in_specs=[
            pl.BlockSpec((tm, d), lambda i: (jnp.minimum(i, n_a_tiles - 1), 0)),
            pl.BlockSpec((tm, d), lambda i: (jnp.clip(i - n_a_tiles, 0, n_b_tiles - 1), 0)),
            pl.BlockSpec((1, d), lambda i: (0, 0)),
            pl.BlockSpec((1, d), lambda i: (0, 0)),
            pl.BlockSpec((d, 4 * width), lambda i: (0, 0), pipeline_mode=pl.Buffered(1)),
        ],
        out_specs=[pl.BlockSpec((tm, d), lambda i: (i, 0)), row_out, row_out, row_out],
        out_shape=[jax.ShapeDtypeStruct((na + n_b_rows, d), f32)]
        + [jax.ShapeDtypeStruct((na + n_b_rows, width), f32)] * 3,
        compiler_params=pltpu.CompilerParams(dimension_semantics=("arbitrary",),
                                             vmem_limit_bytes=VMEM_LIMIT),
        name="inproj",
    )(xa, xb, ln_g, ln_b, w_in_b)


def _mixer_seq_kernel(glu_ref, bx_ref, gate_ref, sa_ref, sb_ref, sh_ref,
                      wa_ref, ba_ref, ng_ref, nb_ref, wb_ref, bb_ref, wg_ref, brg_ref, big_ref, lam_ref,
                      aout_ref, bout_ref, nsa_ref, nsb_ref, nsh_ref,
                      wina, winb, xsa, xsb, ca_s, cb_s, a_s, u_s, hcar):
    tau = pl.program_id(1)
    tt, width = glu_ref.shape
    ka, kb = wa_ref.shape[0], wb_ref.shape[0]
    ha, hb = sa_ref.shape[0], sb_ref.shape[0]

    @pl.when(tau == 0)
    def _():
        wina[0:ha, :] = sa_ref[...]
        winb[0:hb, :] = sb_ref[...]
        hcar[...] = jnp.broadcast_to(sh_ref[...], hcar.shape)

    wina[ha:ha + tt, :] = glu_ref[...]
    winb[hb:hb + tt, :] = bx_ref[...]
    for s in sorted({(ha - (ka - 1) + k) % SUBLANES for k in range(ka)} - {0}):
        xsa[s - 1] = wina[pl.ds(s, tt + ha - SUBLANES), :]
    for s in sorted({(hb - (kb - 1) + k) % SUBLANES for k in range(kb)} - {0}):
        xsb[s - 1] = winb[pl.ds(s, tt + hb - SUBLANES), :]

    def tap(win, xs, off, r0, lanes):
        q, s = divmod(off, SUBLANES)
        src = win if s == 0 else xs.at[s - 1]
        return src[pl.ds(r0 + SUBLANES * q, SUBLANES), lanes]

    rows = min(CONV_ROWS, tt)
    for g in range(width // LANES):
        lanes = pl.ds(g * LANES, LANES)
        bcast = lambda ref, k: jnp.broadcast_to(ref[k:k + 1, lanes], (SUBLANES, LANES))
        wa_k = [bcast(wa_ref, k) for k in range(ka)]
        wb_k = [bcast(wb_ref, k) for k in range(kb)]
        ba_g, bb_g = bcast(ba_ref, 0), bcast(bb_ref, 0)

        def conv_block(rb, carry):
            for j in range(rows // SUBLANES):
                r0 = pl.multiple_of(rb * rows, rows) + j * SUBLANES
                acc = ba_g
                for k in range(ka):
                    acc = acc + wa_k[k] * tap(wina, xsa, ha - (ka - 1) + k, r0, lanes)
                ca_s[pl.ds(r0, SUBLANES), lanes] = acc
                accb = bb_g
                for k in range(kb):
                    accb = accb + wb_k[k] * tap(winb, xsb, hb - (kb - 1) + k, r0, lanes)
                cb_s[pl.ds(r0, SUBLANES), lanes] = accb
            return carry

        lax.fori_loop(0, tt // rows, conv_block, 0)

    a_out = _group_norm_silu(ca_s[...], ng_ref[...], nb_ref[...], width // CONV_HEADS_A)
    aout_ref[...] = a_out.astype(aout_ref.dtype)

    a, u = _lru_gates(cb_s[...], wg_ref, brg_ref[...], big_ref[...], lam_ref[...])
    row8 = lax.broadcasted_iota(i32, (tt, width), 0) & (SUBLANES - 1)
    d = 1
    while d < SUBLANES:
        m = row8 >= d
        a_sh = jnp.where(m, pltpu.roll(a, d, axis=0), 1.0)
        u_sh = jnp.where(m, pltpu.roll(u, d, axis=0), 0.0)
        u = a * u_sh + u
        a = a * a_sh
        d *= 2
    a_s[...] = a
    u_s[...] = u

    def scan_block(j, h):
        r0 = pl.multiple_of(j * SUBLANES, SUBLANES)
        hblk = a_s[pl.ds(r0, SUBLANES), :] * h + u_s[pl.ds(r0, SUBLANES), :]
        u_s[pl.ds(r0, SUBLANES), :] = hblk
        return jnp.broadcast_to(hblk[SUBLANES - 1:SUBLANES, :], h.shape)

    h_last = lax.fori_loop(0, tt // SUBLANES, scan_block, hcar[...])
    hcar[...] = h_last
    bout_ref[...] = (u_s[...] * gate_ref[...]).astype(bout_ref.dtype)

    tail_a = wina[tt:tt + ha, :]
    tail_b = winb[tt:tt + hb, :]
    wina[0:ha, :] = tail_a
    winb[0:hb, :] = tail_b
    nsa_ref[...] = tail_a
    nsb_ref[...] = tail_b
    nsh_ref[...] = h_last[0:1, :]


def _mixer_seq(glu, bx, gate, row_off, sa_pad, sb_pad, sh, mix_w, n_seq, seq_len, tt):
    wa, ba, ng, nb, wb, bb, wg, brg, big, lam = mix_w
    width = glu.shape[1]
    ha, hb = sa_pad.shape[1], sb_pad.shape[1]
    n_t = seq_len // tt
    off = row_off // tt
    row_in = pl.BlockSpec((tt, width), lambda b, t: (off + b * n_t + t, 0))
    row_out = pl.BlockSpec((tt, width), lambda b, t: (b * n_t + t, 0))

    def const(shape):
        return pl.BlockSpec(shape, lambda b, t: (0,) * len(shape))

    def per_seq(r):
        return pl.BlockSpec((None, r, width), lambda b, t: (b, 0, 0))

    return pl.pallas_call(
        _mixer_seq_kernel,
        grid=(n_seq, n_t),
        in_specs=[row_in, row_in, row_in, per_seq(ha), per_seq(hb), per_seq(1),
                  const(wa.shape), const(ba.shape), const(ng.shape), const(nb.shape),
                  const(wb.shape), const(bb.shape), const(wg.shape), const(brg.shape),
                  const(big.shape), const(lam.shape)],
        out_specs=[row_out, row_out, per_seq(ha), per_seq(hb), per_seq(1)],
        out_shape=[jax.ShapeDtypeStruct((n_seq * seq_len, width), bf16),
                   jax.ShapeDtypeStruct((n_seq * seq_len, width), bf16),
                   jax.ShapeDtypeStruct((n_seq, ha, width), f32),
                   jax.ShapeDtypeStruct((n_seq, hb, width), f32),
                   jax.ShapeDtypeStruct((n_seq, 1, width), f32)],
        scratch_shapes=[pltpu.VMEM((ha + tt, width), f32),
                        pltpu.VMEM((hb + tt, width), f32),
                        pltpu.VMEM((SUBLANES - 1, tt + ha - SUBLANES, width), f32),
                        pltpu.VMEM((SUBLANES - 1, tt + hb - SUBLANES, width), f32),
                        pltpu.VMEM((tt, width), f32),
                        pltpu.VMEM((tt, width), f32),
                        pltpu.VMEM((tt, width), f32),
                        pltpu.VMEM((tt, width), f32),
                        pltpu.VMEM((SUBLANES, width), f32)],
        compiler_params=pltpu.CompilerParams(dimension_semantics=("arbitrary", "arbitrary"),
                                             vmem_limit_bytes=VMEM_LIMIT),
        name="mixer_seq",
    )(glu, bx, gate, sa_pad, sb_pad, sh, wa, ba, ng, nb, wb, bb, wg, brg, big, lam)


def _mixer_step_kernel(glu_ref, bx_ref, gate_ref, sa_ref, sb_ref, sh_ref,
                       wat_ref, wa2_ref, wbt_ref, wb2_ref,
                       ba_ref, ng_ref, nb_ref, bb_ref, wg_ref, brg_ref, big_ref, lam_ref,
                       aout_ref, bout_ref, nsa_ref, nsb_ref, nsh_ref,
                       hista, histb, hcar):
    t = pl.program_id(1)
    n_t = hista.shape[0]
    width = glu_ref.shape[1]

    def roll_buffer(s_ref, ns_ref, x_ref):
        shift = n_t - s_ref.shape[0]

        @pl.when(t == 0)
        def _():
            for r in range(max(-shift, 0)):
                ns_ref[r] = s_ref[r + n_t]

        if shift <= 0:
            ns_ref[t - shift] = x_ref[...]
        else:
            @pl.when(t >= shift)
            def _():
                ns_ref[t - shift] = x_ref[...]

    @pl.when(t == 0)
    def _():
        hista[...] = jnp.zeros_like(hista)
        histb[...] = jnp.zeros_like(histb)
        hcar[...] = sh_ref[...]

    roll_buffer(sa_ref, nsa_ref, glu_ref)
    roll_buffer(sb_ref, nsb_ref, bx_ref)
    hista[t] = glu_ref[...]
    histb[t] = bx_ref[...]

    ca = jnp.broadcast_to(ba_ref[...], glu_ref.shape)
    for r in range(sa_ref.shape[0]):
        ca = ca + wat_ref[r:r + 1, :] * sa_ref[r]
    cb = jnp.broadcast_to(bb_ref[...], bx_ref.shape)
    for r in range(sb_ref.shape[0]):
        cb = cb + wbt_ref[r:r + 1, :] * sb_ref[r]
    for j in range(n_t):
        ca = ca + wa2_ref[j:j + 1, :] * hista[j]
        cb = cb + wb2_ref[j:j + 1, :] * histb[j]
    a_out = _group_norm_silu(ca, ng_ref[...], nb_ref[...], width // CONV_HEADS_A)
    aout_ref[...] = a_out.astype(aout_ref.dtype)

    a, u = _lru_gates(cb, wg_ref, brg_ref[...], big_ref[...], lam_ref[...])
    h = a * hcar[...] + u
    hcar[...] = h
    nsh_ref[...] = h
    bout_ref[...] = (h * gate_ref[...]).astype(bout_ref.dtype)


def _mixer_step(glu, bx, gate, row_off, sa, sb, sh, step_w, mix_w, n_seq, n_t, bt):
    wat, wa2, wbt, wb2 = step_w
    wa, ba, ng, nb, wb, bb, wg, brg, big, lam = mix_w
    width = glu.shape[1]
    ka1, kb1 = sa.shape[1], sb.shape[1]
    off = row_off // bt
    nb_tiles = n_seq // bt
    row_in = pl.BlockSpec((bt, width), lambda j, t: (off + t * nb_tiles + j, 0))
    row_out = pl.BlockSpec((bt, width), lambda j, t: (t * nb_tiles + j, 0))

    def const(shape):
        return pl.BlockSpec(shape, lambda j, t: (0,) * len(shape))

    def per_t(arr):
        return pl.BlockSpec((None,) + arr.shape[1:], lambda j, t: (t, 0, 0))

    def state(r):
        return pl.BlockSpec((None, r, bt, width), lambda j, t: (0, 0, j, 0))

    state_h = pl.BlockSpec((None, bt, width), lambda j, t: (0, j, 0))
    return pl.pallas_call(
        _mixer_step_kernel,
        grid=(nb_tiles, n_t),
        in_specs=[row_in, row_in, row_in, state(ka1), state(kb1), state_h,
                  per_t(wat), per_t(wa2), per_t(wbt), per_t(wb2),
                  const(ba.shape), const(ng.shape), const(nb.shape), const(bb.shape),
                  const(wg.shape), const(brg.shape), const(big.shape), const(lam.shape)],
        out_specs=[row_out, row_out, state(ka1), state(kb1), state_h],
        out_shape=[jax.ShapeDtypeStruct((n_seq * n_t, width), bf16),
                   jax.ShapeDtypeStruct((n_seq * n_t, width), bf16),
                   jax.ShapeDtypeStruct((1, ka1, n_seq, width), f32),
                   jax.ShapeDtypeStruct((1, kb1, n_seq, width), f32),
                   jax.ShapeDtypeStruct((1, n_seq, width), f32)],
        scratch_shapes=[pltpu.VMEM((n_t, bt, width), f32),
                        pltpu.VMEM((n_t, bt, width), f32),
                        pltpu.VMEM((bt, width), f32)],
        compiler_params=pltpu.CompilerParams(dimension_semantics=("arbitrary", "arbitrary"),
                                             vmem_limit_bytes=VMEM_LIMIT),
        name="mixer_step",
    )(glu, bx, gate, sa, sb, sh, wat, wa2, wbt, wb2,
      ba, ng, nb, bb, wg, brg, big, lam)


def _outproj_kernel(ap_ref, bp_ref, as_ref, bs_ref, h_ref, wo_ref,
                    l1g_ref, l1b_ref, wrh_ref, wrl_ref, br_ref,
                    h1_ref, h1p_ref, z_ref, *, n_p_tiles, alpha):
    i = pl.program_id(0)
    is_p = i < n_p_tiles
    a = jnp.where(is_p, ap_ref[...], as_ref[...])
    b = jnp.where(is_p, bp_ref[...], bs_ref[...])
    width = a.shape[1]
    mix = (jnp.dot(a, wo_ref[0:width, :], preferred_element_type=f32)
           + jnp.dot(b, wo_ref[width:2 * width, :], preferred_element_type=f32))
    h1 = _layer_norm(alpha * h_ref[...] + mix, l1g_ref[...], l1b_ref[...])
    h1_ref[...] = h1

    xh = h1.astype(bf16)
    xh32 = xh.astype(f32)
    half = h1.shape[1] // 2
    lo = lax.shift_right_logical(lax.bitcast_convert_type(xh32[:, :half], jnp.uint32), jnp.uint32(16))
    hi = lax.bitcast_convert_type(xh32[:, half:], jnp.uint32) & jnp.uint32(0xFFFF0000)
    word = lo | hi
    n_sub = half // LANES
    for s in range(n_sub):
        h1p_ref[pl.ds(s, word.shape[0], stride=n_sub), :] = word[:, s * LANES:(s + 1) * LANES]

    xl = (h1 - xh32).astype(bf16)
    z_ref[...] = (jnp.dot(xh, wrh_ref[...], preferred_element_type=f32)
                  + jnp.dot(xl, wrh_ref[...], preferred_element_type=f32)
                  + jnp.dot(xh, wrl_ref[...], preferred_element_type=f32)) + br_ref[...]


def _route_kernel(z_ref, route_ref, routet_ref, cnt_ref, carry, *, sub):
    i = pl.program_id(0)
    z = z_ref[...]
    lane = lax.broadcasted_iota(i32, z.shape, 1).astype(f32)
    n_g, epg = float(N_GROUPS), float(EXPERTS_PER_GROUP)
    far = float(2 * ROUTE_LANES)

    gm = lane < n_g
    zg = jnp.where(gm, z, NEG_BIG)
    pg = jnp.where(gm, jnp.exp(zg - jnp.max(zg, axis=-1, keepdims=True)), 0.0)
    pg = pg / jnp.sum(pg, axis=-1, keepdims=True)
    g_top = jnp.max(pg, axis=-1, keepdims=True)
    g_idx = jnp.min(jnp.where(gm & (pg == g_top), lane, far), axis=-1, keepdims=True)

    lo = n_g + g_idx * epg
    em = (lane >= lo) & (lane < lo + epg)
    ze = jnp.where(em, z, NEG_BIG)
    pe = jnp.where(em, jnp.exp(ze - jnp.max(ze, axis=-1, keepdims=True)), 0.0)
    pe = pe / jnp.sum(pe, axis=-1, keepdims=True)
    v1 = jnp.max(jnp.where(em, pe, -1.0), axis=-1, keepdims=True)
    l1 = jnp.min(jnp.where(em & (pe == v1), lane, far), axis=-1, keepdims=True)
    pe2 = jnp.where(em & (lane != l1), pe, -1.0)
    v2 = jnp.max(pe2, axis=-1, keepdims=True)
    l2 = jnp.min(jnp.where(pe2 == v2, lane, far), axis=-1, keepdims=True)
    den = v1 + v2
    gate1 = v1 / den * g_top
    gate2 = v2 / den * g_top

    @pl.when(i == 0)
    def _():
        carry[...] = jnp.zeros_like(carry)

    o1 = lane == l1
    o2 = lane == l2
    onehot = jnp.where(o1 | o2, 1.0, 0.0)
    ri = lax.broadcasted_iota(i32, (sub, sub), 0)
    ci = lax.broadcasted_iota(i32, (sub, sub), 1)
    tri = jnp.where(ci < ri, 1.0, 0.0).astype(bf16)
    counts = carry[0:1, :]
    prefix = []
    for s in range(z.shape[0] // sub):
        oh = onehot[s * sub:(s + 1) * sub, :]
        prefix.append(jnp.dot(tri, oh.astype(bf16), preferred_element_type=f32) + counts)
        counts = counts + jnp.sum(oh, axis=0, keepdims=True)
    prefix = jnp.concatenate(prefix, axis=0)
    rank1 = jnp.sum(jnp.where(o1, prefix, 0.0), axis=-1, keepdims=True)
    rank2 = jnp.sum(jnp.where(o2, prefix, 0.0), axis=-1, keepdims=True)
    new_carry = jnp.broadcast_to(counts, carry.shape)
    carry[...] = new_carry
    cnt_ref[...] = new_carry

    route = jnp.where(lane == 0.0, l1 - n_g, 0.0)
    route = jnp.where(lane == 1.0, l2 - n_g, route)
    route = jnp.where(lane == 2.0, rank1, route)
    route = jnp.where(lane == 3.0, rank2, route)
    route = jnp.where(lane == 4.0, gate1, route)
    route = jnp.where(lane == 5.0, gate2, route)
    route_ref[...] = route
    routet_ref[...] = route.T[0:routet_ref.shape[0], :]


def _outproj(a_p, b_p, a_s, b_s, h, w_out_b, ln1_g, ln1_b,
             wr_hi, wr_lo, b_r, alpha, tm):
    n_p, width = a_p.shape
    n_s = a_s.shape[0]
    d = h.shape[1]
    n_p_tiles = n_p // tm
    n_s_tiles = n_s // tm
    n_tiles = n_p_tiles + n_s_tiles

    def p_map(i):
        return (jnp.minimum(i, n_p_tiles - 1), 0)

    def s_map(i):
        return (jnp.clip(i - n_p_tiles, 0, n_s_tiles - 1), 0)

    def const(shape, **kw):
        return pl.BlockSpec(shape, lambda i: (0,) * len(shape), **kw)

    return pl.pallas_call(
        functools.partial(_outproj_kernel, n_p_tiles=n_p_tiles, alpha=alpha),
        grid=(n_tiles,),
        in_specs=[pl.BlockSpec((tm, width), p_map), pl.BlockSpec((tm, width), p_map),
                  pl.BlockSpec((tm, width), s_map), pl.BlockSpec((tm, width), s_map),
                  pl.BlockSpec((tm, d), lambda i: (i, 0)),
                  const((2 * width, d), pipeline_mode=pl.Buffered(1)),
                  const((1, d)), const((1, d)),
                  const((d, ROUTE_LANES)), const((d, ROUTE_LANES)), const((1, ROUTE_LANES))],
        out_specs=[pl.BlockSpec((tm, d), lambda i: (i, 0)),
                   pl.BlockSpec((tm * (d // 2 // LANES), LANES), lambda i: (i, 0)),
                   pl.BlockSpec((tm, ROUTE_LANES), lambda i: (i, 0))],
        out_shape=[jax.ShapeDtypeStruct((n_p + n_s, d), f32),
                   jax.ShapeDtypeStruct(((n_p + n_s) * (d // 2 // LANES), LANES), jnp.uint32),
                   jax.ShapeDtypeStruct((n_p + n_s, ROUTE_LANES), f32)],
        compiler_params=pltpu.CompilerParams(dimension_semantics=("arbitrary",),
                                             vmem_limit_bytes=VMEM_LIMIT),
        name="outproj",
    )(a_p, b_p, a_s, b_s, h, w_out_b, ln1_g, ln1_b, wr_hi, wr_lo, b_r)


def _route_tile(n):
    return max(t for t in range(ROUTE_SUB, ROUTE_MAX_TILE + 1, ROUTE_SUB) if n % t == 0)


def _route(z):
    n = z.shape[0]
    tr = _route_tile(n)
    return pl.pallas_call(
        functools.partial(_route_kernel, sub=ROUTE_SUB),
        grid=(n // tr,),
        in_specs=[pl.BlockSpec((tr, ROUTE_LANES), lambda i: (i, 0))],
        out_specs=[pl.BlockSpec((tr, ROUTE_LANES), lambda i: (i, 0)),
                   pl.BlockSpec((SUBLANES, tr), lambda i: (0, i)),
                   pl.BlockSpec((SUBLANES, ROUTE_LANES), lambda i: (0, 0))],
        out_shape=[jax.ShapeDtypeStruct((n, ROUTE_LANES), f32),
                   jax.ShapeDtypeStruct((SUBLANES, n), f32),
                   jax.ShapeDtypeStruct((SUBLANES, ROUTE_LANES), f32)],
        scratch_shapes=[pltpu.VMEM((SUBLANES, ROUTE_LANES), f32)],
        compiler_params=pltpu.CompilerParams(dimension_semantics=("arbitrary",),
                                             vmem_limit_bytes=VMEM_LIMIT),
        name="route",
    )(z)


def _invert_kernel(dest_ref, flat_ref, fill):
    fill[...] = jnp.full(fill.shape, -1, fill.dtype)
    pltpu.sync_copy(fill, flat_ref)

    def place(f, carry):
        flat_ref[dest_ref[f]] = f
        return carry

    lax.fori_loop(0, dest_ref.shape[0], place, 0, unroll=SCALAR_UNROLL)


def _invert(dest, p_max):
    return pl.pallas_call(
        _invert_kernel,
        in_specs=[pl.BlockSpec(memory_space=pltpu.SMEM)],
        out_specs=pl.BlockSpec(memory_space=pltpu.SMEM),
        out_shape=jax.ShapeDtypeStruct((p_max,), i32),
        scratch_shapes=[pltpu.VMEM((p_max,), i32)],
        name="invert_dispatch",
    )(dest)


def _expert_kernel(src_ref, dst_ref, start_ref, nblk_ref,
                   h1_hbm, wg_ref, wu_ref, wd_ref, y2_hbm,
                   xbuf, ybuf, wgb, wub, wdb, gsem, osem, *, blk, n_rows):
    e = pl.program_id(0)
    n_e = pl.num_programs(0)
    nb = nblk_ref[e]
    g0 = start_ref[e]

    xs = xbuf.shape[1] // blk
    ys = ybuf.shape[1] // blk

    def gather(g, slot):
        for r in range(blk):
            tok = src_ref[g * blk + r]
            pltpu.make_async_copy(h1_hbm.at[pl.ds(pl.multiple_of(tok * xs, xs), xs), :],
                                  xbuf.at[slot, pl.ds(r * xs, xs), :], gsem.at[slot]
                                  ).start(priority=r % 2)

    def scatter(g, slot):
        for r in range(blk):
            row = dst_ref[(g + 1) * blk + r]
            pltpu.make_async_copy(ybuf.at[slot, pl.ds(r * ys, ys), :],
                                  y2_hbm.at[pl.ds(pl.multiple_of(row * ys, ys), ys), :],
                                  osem.at[slot]).start(priority=r % 2)

    def dump_copy(slot):
        dst = y2_hbm.at[pl.ds((n_rows + slot * blk) * ys, blk * ys), :]
        return pltpu.make_async_copy(ybuf.at[slot], dst, osem.at[slot])

    def wait_gather(slot):
        pltpu.make_async_copy(h1_hbm.at[pl.ds(0, blk * xs), :], xbuf.at[slot], gsem.at[slot]).wait()

    nx, ny = xbuf.shape[0], ybuf.shape[0]
    ahead = nx - 1

    @pl.when(e == 0)
    def _():
        ybuf[...] = jnp.zeros(ybuf.shape, ybuf.dtype)
        for s in range(ny - 1):
            dump_copy(s).start()
        for g in range(ahead):
            gather(g, g)

    @pl.when(nb > 0)
    def _():
        wgb[...] = wg_ref[...].astype(bf16)
        wub[...] = wu_ref[...].astype(bf16)
        wdb[...] = wd_ref[...].astype(bf16)

        def chunk(c, carry):
            g = g0 + c
            slot = g % nx
            yslot = g % ny
            wait_gather(slot)
            dump_copy(yslot).wait()
            gather(g + ahead, (g + ahead) % nx)
            scatter(g - 1, (g + ny - 1) % ny)
            xw = [xbuf[slot, pl.ds(s, blk, stride=xs), :] for s in range(xs)]
            x_lo = [lax.bitcast_convert_type(lax.shift_left(w, jnp.uint32(16)), f32) for w in xw]
            x_hi = [lax.bitcast_convert_type(w & jnp.uint32(0xFFFF0000), f32) for w in xw]
            x = jnp.concatenate(x_lo + x_hi, axis=1).astype(bf16)
            hg = jnp.dot(x, wgb[...], preferred_element_type=f32)
            hu = jnp.dot(x, wub[...], preferred_element_type=f32)
            hid = (hg * jax.nn.sigmoid(hg) * hu).astype(bf16)
            y = jnp.dot(hid, wdb[...], preferred_element_type=f32)
            for s in range(ys):
                ybuf[yslot, pl.ds(s, blk, stride=ys), :] = y[:, s * LANES:(s + 1) * LANES]
            return carry

        lax.fori_loop(0, nb, chunk, 0)

    @pl.when(e == n_e - 1)
    def _():
        g_end = g0 + nb
        scatter(g_end - 1, (g_end + ny - 1) % ny)
        for s in range(ny):
            dump_copy(s).wait()
        for j in range(ahead):
            wait_gather((g_end + j) % nx)


def _experts(h1p, src_rows, dst_rows, pad_start, nblk, w_gate, w_up, w_down, blk):
    n_e, d, d_e = w_gate.shape
    xs, ys = d // 2 // LANES, d // LANES
    n = h1p.shape[0] // xs
    n_rows = TOP_K * n
    grid_spec = pltpu.PrefetchScalarGridSpec(
        num_scalar_prefetch=4,
        grid=(n_e,),
        in_specs=[pl.BlockSpec(memory_space=pl.ANY),
                  pl.BlockSpec((None, d, d_e), lambda e, *_: (e, 0, 0)),
                  pl.BlockSpec((None, d, d_e), lambda e, *_: (e, 0, 0)),
                  pl.BlockSpec((None, d_e, d), lambda e, *_: (e, 0, 0))],
        out_specs=pl.BlockSpec(memory_space=pl.ANY),
        scratch_shapes=[pltpu.VMEM((GATHER_SLOTS, blk * xs, LANES), jnp.uint32),
                        pltpu.VMEM((SCATTER_SLOTS, blk * ys, LANES), f32),
                        pltpu.VMEM((d, d_e), bf16),
                        pltpu.VMEM((d, d_e), bf16),
                        pltpu.VMEM((d_e, d), bf16),
                        pltpu.SemaphoreType.DMA((GATHER_SLOTS,)),
                        pltpu.SemaphoreType.DMA((SCATTER_SLOTS,))],
    )
    return pl.pallas_call(
        functools.partial(_expert_kernel, blk=blk, n_rows=n_rows),
        grid_spec=grid_spec,
        out_shape=jax.ShapeDtypeStruct(((n_rows + SCATTER_SLOTS * blk) * ys, LANES), f32),
        compiler_params=pltpu.CompilerParams(dimension_semantics=("arbitrary",),
                                             vmem_limit_bytes=VMEM_LIMIT),
        name="experts",
    )(src_rows, dst_rows, pad_start, nblk, h1p, w_gate, w_up, w_down)


def _combine_kernel(h1_ref, route_ref, ya_ref, yb_ref, l2g_ref, l2b_ref, yp_ref, ysm_ref,
                    *, n_p_tiles, alpha):
    i = pl.program_id(0)
    route = route_ref[...]
    tm = route.shape[0]
    ys = ya_ref.shape[0] // tm

    def token_rows(ref):
        return jnp.concatenate([ref[pl.ds(s, tm, stride=ys), :] for s in range(ys)], axis=1)

    y = token_rows(ya_ref) * route[:, 4:5] + token_rows(yb_ref) * route[:, 5:6]
    out = _layer_norm(alpha * h1_ref[...] + y, l2g_ref[...], l2b_ref[...])

    @pl.when(i < n_p_tiles)
    def _():
        yp_ref[...] = out

    @pl.when(i >= n_p_tiles)
    def _():
        ysm_ref[...] = out


def _combine(h1, route, y2, ln2_g, ln2_b, n_p, alpha, tm):
    n, d = h1.shape
    n_s = n - n_p
    n_tiles = n // tm
    n_p_tiles = n_p // tm
    n_s_tiles = n_s // tm
    return pl.pallas_call(
        functools.partial(_combine_kernel, n_p_tiles=n_p_tiles, alpha=alpha),
        grid=(n_tiles,),
        in_specs=[pl.BlockSpec((tm, d), lambda i: (i, 0)),
                  pl.BlockSpec((tm, ROUTE_LANES), lambda i: (i, 0)),
                  pl.BlockSpec((tm * (d // LANES), LANES), lambda i: (i, 0)),
                  pl.BlockSpec((tm * (d // LANES), LANES), lambda i: (n_tiles + i, 0)),
                  pl.BlockSpec((1, d), lambda i: (0, 0)),
                  pl.BlockSpec((1, d), lambda i: (0, 0))],
        out_specs=[pl.BlockSpec((tm, d), lambda i: (jnp.minimum(i, n_p_tiles - 1), 0)),
                   pl.BlockSpec((tm, d), lambda i: (jnp.clip(i - n_p_tiles, 0, n_s_tiles - 1), 0))],
        out_shape=[jax.ShapeDtypeStruct((n_p, d), f32), jax.ShapeDtypeStruct((n_s, d), f32)],
        compiler_params=pltpu.CompilerParams(dimension_semantics=("arbitrary",),
                                             vmem_limit_bytes=VMEM_LIMIT),
        name="combine_ln2",
    )(h1, route, y2, y2, ln2_g, ln2_b)


def _shifted_taps(w, n_t, hist):
    k = w.shape[0]
    zero = jnp.zeros_like(w[0])
    wt = jnp.stack([jnp.stack([w[r - t] if 0 <= r - t < k else zero for r in range(hist)])
                    for t in range(n_t)])
    w2 = jnp.stack([jnp.stack([w[hist - t + j] if (j <= t and 0 <= hist - t + j < k) else zero
                               for j in range(n_t)]) for t in range(n_t)])
    return wt, w2


def kernel(x_prompt, x_sample, state_conv_a, state_conv_b, state_lru, meta_tokens, ln_in_g, ln_in_b, w_in, conv_a_w, conv_a_b, norm_a_g, norm_a_b, conv_b_w, conv_b_b, w_rg, b_rg, w_ig, b_ig, lru_lambda, w_out, ln1_g, ln1_b, w_router_group, b_router_group, w_router_expert, b_router_expert, w_gate, w_up, w_down, ln2_g, ln2_b):
    depth = w_in.shape[0]
    assert depth == 1, "single-layer trunk only"
    bp, tp, d = x_prompt.shape
    bs, ts, _ = x_sample.shape
    n_meta = meta_tokens.shape[0]
    width = conv_a_w.shape[2]
    ka, kb = conv_a_w.shape[1], conv_b_w.shape[1]
    n_e = w_gate.shape[1]
    alpha = (2.0 * depth) ** 0.25
    n_p, n_s = bp * tp, bs * ts
    n = n_p + n_s
    assert ts < ka and ts >= kb - 1
    assert n_p % TOKEN_TILE == 0 and n_s % TOKEN_TILE == 0 and tp % SEQ_TILE == 0
    assert bs % STEP_BATCH == 0 and n_p % STEP_BATCH == 0
    assert n_meta % BF16_ROWS == 0 and n_meta <= TOKEN_TILE and n % n_meta == 0
    assert N_GROUPS * (1 + EXPERTS_PER_GROUP) <= ROUTE_LANES and n_e == N_GROUPS * EXPERTS_PER_GROUP

    row = lambda v: v.reshape(1, -1).astype(f32)
    ln_g, ln_b = row(ln_in_g), row(ln_in_b)
    w_in_b = w_in[0].astype(bf16)
    w_out_b = w_out[0].astype(bf16)
    mix_w = (conv_a_w[0], row(conv_a_b[0]), row(norm_a_g[0]), row(norm_a_b[0]),
             conv_b_w[0], row(conv_b_b[0]),
             jnp.concatenate([w_rg[0], w_ig[0]], axis=-1).astype(bf16),
             row(b_rg[0]), row(b_ig[0]), row(lru_lambda[0]))
    ha, hb = _round_up(ka - 1, SUBLANES), _round_up(kb - 1, SUBLANES)

    x_p = x_prompt.reshape(n_p, d)
    x_s = jnp.concatenate([jnp.swapaxes(x_sample, 0, 1).reshape(n_s, d),
                           jnp.pad(meta_tokens, ((0, TOKEN_TILE - n_meta), (0, 0)))], axis=0)
    h, glu, bx, gate = _inproj(x_p, x_s, n_s + TOKEN_TILE, ln_g, ln_b, w_in_b, TOKEN_TILE)

    _, _, sa_m, sb_m, sh_m = _mixer_seq(
        glu, bx, gate, n, jnp.zeros((1, ha, width), f32), jnp.zeros((1, hb, width), f32),
        jnp.zeros((1, 1, width), f32), mix_w, 1, n_meta, n_meta)

    bcast = lambda s: jnp.broadcast_to(s, (bp,) + s.shape[1:])
    a_p, b_p, nsa_p, nsb_p, nsh_p = _mixer_seq(glu, bx, gate, 0, bcast(sa_m), bcast(sb_m), bcast(sh_m),
                                               mix_w, bp, tp, SEQ_TILE)

    row_major = lambda s: jnp.swapaxes(s, 1, 2)
    wat, wa2 = _shifted_taps(conv_a_w[0], ts, ka - 1)
    wbt, wb2 = _shifted_taps(conv_b_w[0], ts, kb - 1)
    a_s, b_s, nsa_s, nsb_s, nsh_s = _mixer_step(
        glu, bx, gate, n_p, row_major(state_conv_a), row_major(state_conv_b), state_lru,
        (wat, wa2, wbt, wb2), mix_w, bs, ts, STEP_BATCH)

    w_r = jnp.concatenate([w_router_group[0], w_router_expert[0]], axis=1)
    w_r = jnp.pad(w_r, ((0, 0), (0, ROUTE_LANES - w_r.shape[1])))
    wr_hi = w_r.astype(bf16)
    wr_lo = (w_r - wr_hi.astype(f32)).astype(bf16)
    b_r = jnp.concatenate([b_router_group[0], b_router_expert[0]])
    b_r = jnp.pad(b_r, (0, ROUTE_LANES - b_r.shape[0])).reshape(1, ROUTE_LANES)
    h1, h1p, logits = _outproj(a_p, b_p, a_s, b_s, h, w_out_b,
                               row(ln1_g[0]), row(ln1_b[0]), wr_hi, wr_lo, b_r, alpha, TOKEN_TILE)
    route, route_t, cnt = _route(logits)

    blk = EXPERT_ROWS
    e_idx = route_t[0:TOP_K].astype(i32)
    rank = route_t[TOP_K:2 * TOP_K].astype(i32)
    counts = cnt[0, N_GROUPS:N_GROUPS + n_e].astype(i32)
    nblk = (counts + blk - 1) // blk
    pad_end = jnp.cumsum(nblk * blk)
    pad_start = pad_end - nblk * blk
    start_of = jnp.sum(jnp.where(e_idx[None] == jnp.arange(n_e, dtype=i32)[:, None, None],
                                 pad_start[:, None, None], 0), axis=0)
    dest = (start_of + rank).reshape(-1)
    p_max = _round_up(TOP_K * n + n_e * (blk - 1), blk)
    flat = _invert(dest, p_max)
    pos = jnp.arange(p_max, dtype=i32)
    valid = flat >= 0
    src_rows = jnp.where(valid, flat % n, 0)
    dst_rows = jnp.where(valid, flat, TOP_K * n + pos % (SCATTER_SLOTS * blk))
    src_rows = jnp.concatenate([src_rows, jnp.zeros(((GATHER_SLOTS - 1) * blk,), i32)])
    dst_rows = jnp.concatenate([TOP_K * n + (SCATTER_SLOTS - 1) * blk + jnp.arange(blk, dtype=i32), dst_rows])
    y2 = _experts(h1p, src_rows, dst_rows, (pad_start // blk).astype(i32), nblk,
                  w_gate[0], w_up[0], w_down[0], blk)
    y_p, y_s = _combine(h1, route, y2, row(ln2_g[0]), row(ln2_b[0]), n_p, alpha, TOKEN_TILE)

    y_prompt = y_p.reshape(bp, tp, d)
    y_sample = jnp.swapaxes(y_s.reshape(ts, bs, d), 0, 1)
    return (y_prompt, y_sample,
            nsa_p[:, ha - (ka - 1):][None], nsb_p[:, hb - (kb - 1):][None], nsh_p.reshape(1, bp, width),
            row_major(nsa_s), row_major(nsb_s), nsh_s)
```

```python
import functools

import jax
import jax.numpy as jnp
from jax import lax
from jax.experimental import pallas as pl
from jax.experimental.pallas import tpu as pltpu

f32 = jnp.float32
bf16 = jnp.bfloat16
i32 = jnp.int32

CONV_HEADS_A = 8
LRU_HEADS = 8
LRU_C = 8.0
N_GROUPS = 8
EXPERTS_PER_GROUP = 8
TOP_K = 2
LN_EPS = 1e-5

SUBLANES = 8
LANES = 128
BF16_ROWS = 16
VMEM_LIMIT = 56 * 1024 * 1024

TOKEN_TILE = 256
SEQ_TILE = 256
CONV_ROWS = 128
STEP_BATCH = 64
EXPERT_ROWS = 128
ROUTE_LANES = 128
ROUTE_SUB = 128
ROUTE_MAX_TILE = 2304
GATHER_SLOTS = 4
SCATTER_SLOTS = 4
SCALAR_UNROLL = 8
NEG_BIG = -1e30


def _round_up(x, m):
    return (x + m - 1) // m * m


def _layer_norm(x, g, b):
    mu = jnp.mean(x, axis=-1, keepdims=True)
    xc = x - mu
    var = jnp.mean(xc * xc, axis=-1, keepdims=True)
    return xc * lax.rsqrt(var + LN_EPS) * g + b


def _group_norm_silu(x, g, b, group):
    parts = []
    for c in range(x.shape[1] // group):
        xg = x[:, c * group:(c + 1) * group]
        mu = jnp.mean(xg, axis=-1, keepdims=True)
        xc = xg - mu
        var = jnp.mean(xc * xc, axis=-1, keepdims=True)
        parts.append(xc * lax.rsqrt(var + LN_EPS))
    y = jnp.concatenate(parts, axis=-1) * g + b
    return y * jax.nn.sigmoid(y)


def _lru_gates(cb, wg_ref, brg, big, lam):
    heads, hd, _ = wg_ref.shape
    cbb = cb.astype(bf16)
    rs, iz = [], []
    for h in range(heads):
        z = jnp.dot(cbb[:, h * hd:(h + 1) * hd], wg_ref[h], preferred_element_type=f32)
        rs.append(z[:, :hd])
        iz.append(z[:, hd:])
    r = jax.nn.sigmoid(jnp.concatenate(rs, axis=-1) + brg)
    i = jax.nn.sigmoid(jnp.concatenate(iz, axis=-1) + big)
    log_a = -LRU_C * r * jax.nn.softplus(-lam)
    a = jnp.exp(log_a)
    u = jnp.sqrt(-jnp.tanh(log_a) * (a * a + 1.0)) * (i * cb)
    return a, u


def _inproj_kernel(xa_ref, xb_ref, g_ref, b_ref, w_ref, h_ref, glu_ref, bx_ref, gate_ref, *, n_a_tiles):
    i = pl.program_id(0)
    x = jnp.where(i < n_a_tiles, xa_ref[...], xb_ref[...])
    h = _layer_norm(x, g_ref[...], b_ref[...])
    h_ref[...] = h
    xn = h.astype(bf16)
    width = glu_ref.shape[1]
    ch = min(width, 512)
    for c in range(width // ch):
        lo = c * ch
        av = jnp.dot(xn, w_ref[:, lo:lo + ch], preferred_element_type=f32)
        ag = jnp.dot(xn, w_ref[:, width + lo:width + lo + ch], preferred_element_type=f32)
        glu_ref[:, lo:lo + ch] = av * jax.nn.sigmoid(ag)
        bx_ref[:, lo:lo + ch] = jnp.dot(xn, w_ref[:, 2 * width + lo:2 * width + lo + ch],
                                        preferred_element_type=f32)
        bg = jnp.dot(xn, w_ref[:, 3 * width + lo:3 * width + lo + ch], preferred_element_type=f32)
        gate_ref[:, lo:lo + ch] = jax.nn.gelu(bg)


def _inproj(xa, xb, n_b_rows, ln_g, ln_b, w_in_b, tm):
    na, d = xa.shape
    width = w_in_b.shape[1] // 4
    n_a_tiles = na // tm
    n_tiles = (na + n_b_rows) // tm
    n_b_tiles = max(n_tiles - n_a_tiles, 1)
    row_out = pl.BlockSpec((tm, width), lambda i: (i, 0))
    return pl.pallas_call(
        functools.partial(_inproj_kernel, n_a_tiles=n_a_tiles),
        grid=(n_tiles,),
        in_specs=[
            pl.BlockSpec((tm, d), lambda i: (jnp.minimum(i, n_a_tiles - 1), 0)),
            pl.BlockSpec((tm, d), lambda i: (jnp.clip(i - n_a_tiles, 0, n_b_tiles - 1), 0)),
            pl.BlockSpec((1, d), lambda i: (0, 0)),
            pl.BlockSpec((1, d), lambda i: (0, 0)),
            pl.BlockSpec((d, 4 * width), lambda i: (0, 0), pipeline_mode=pl.Buffered(1)),
        ],
        out_specs=[pl.BlockSpec((tm, d), lambda i: (i, 0)), row_out, row_out, row_out],
        out_shape=[jax.ShapeDtypeStruct((na + n_b_rows, d), f32)]
        + [jax.ShapeDtypeStruct((na + n_b_rows, width), f32)] * 3,
        compiler_params=pltpu.CompilerParams(dimension_semantics=("arbitrary",),
                                             vmem_limit_bytes=VMEM_LIMIT),
        name="inproj",
    )(xa, xb, ln_g, ln_b, w_in_b)


def _mixer_seq_kernel(glu_ref, bx_ref, gate_ref, sa_ref, sb_ref, sh_ref,
                      wa_ref, ba_ref, ng_ref, nb_ref, wb_ref, bb_ref, wg_ref, brg_ref, big_ref, lam_ref,
                      aout_ref, bout_ref, nsa_ref, nsb_ref, nsh_ref,
                      wina, winb, xsa, xsb, ca_s, cb_s, a_s, u_s, hcar):
    tau = pl.program_id(1)
    tt, width = glu_ref.shape
    ka, kb = wa_ref.shape[0], wb_ref.shape[0]
    ha, hb = sa_ref.shape[0], sb_ref.shape[0]

    @pl.when(tau == 0)
    def _():
        wina[0:ha, :] = sa_ref[...]
        winb[0:hb, :] = sb_ref[...]
        hcar[...] = jnp.broadcast_to(sh_ref[...], hcar.shape)

    wina[ha:ha + tt, :] = glu_ref[...]
    winb[hb:hb + tt, :] = bx_ref[...]
    for s in sorted({(ha - (ka - 1) + k) % SUBLANES for k in range(ka)} - {0}):
        xsa[s - 1] = wina[pl.ds(s, tt + ha - SUBLANES), :]
    for s in sorted({(hb - (kb - 1) + k) % SUBLANES for k in range(kb)} - {0}):
        xsb[s - 1] = winb[pl.ds(s, tt + hb - SUBLANES), :]

    def tap(win, xs, off, r0, lanes):
        q, s = divmod(off, SUBLANES)
        src = win if s == 0 else xs.at[s - 1]
        return src[pl.ds(r0 + SUBLANES * q, SUBLANES), lanes]

    rows = min(CONV_ROWS, tt)
    for g in range(width // LANES):
        lanes = pl.ds(g * LANES, LANES)
        bcast = lambda ref, k: jnp.broadcast_to(ref[k:k + 1, lanes], (SUBLANES, LANES))
        wa_k = [bcast(wa_ref, k) for k in range(ka)]
        wb_k = [bcast(wb_ref, k) for k in range(kb)]
        ba_g, bb_g = bcast(ba_ref, 0), bcast(bb_ref, 0)

        def conv_block(rb, carry):
            for j in range(rows // SUBLANES):
                r0 = pl.multiple_of(rb * rows, rows) + j * SUBLANES
                acc = ba_g
                for k in range(ka):
                    acc = acc + wa_k[k] * tap(wina, xsa, ha - (ka - 1) + k, r0, lanes)
                ca_s[pl.ds(r0, SUBLANES), lanes] = acc
                accb = bb_g
                for k in range(kb):
                    accb = accb + wb_k[k] * tap(winb, xsb, hb - (kb - 1) + k, r0, lanes)
                cb_s[pl.ds(r0, SUBLANES), lanes] = accb
            return carry

        lax.fori_loop(0, tt // rows, conv_block, 0)

    a_out = _group_norm_silu(ca_s[...], ng_ref[...], nb_ref[...], width // CONV_HEADS_A)
    aout_ref[...] = a_out.astype(aout_ref.dtype)

    a, u = _lru_gates(cb_s[...], wg_ref, brg_ref[...], big_ref[...], lam_ref[...])
    row8 = lax.broadcasted_iota(i32, (tt, width), 0) & (SUBLANES - 1)
    d = 1
    while d < SUBLANES:
        m = row8 >= d
        a_sh = jnp.where(m, pltpu.roll(a, d, axis=0), 1.0)
        u_sh = jnp.where(m, pltpu.roll(u, d, axis=0), 0.0)
        u = a * u_sh + u
        a = a * a_sh
        d *= 2
    a_s[...] = a
    u_s[...] = u

    def scan_block(j, h):
        r0 = pl.multiple_of(j * SUBLANES, SUBLANES)
        hblk = a_s[pl.ds(r0, SUBLANES), :] * h + u_s[pl.ds(r0, SUBLANES), :]
        u_s[pl.ds(r0, SUBLANES), :] = hblk
        return jnp.broadcast_to(hblk[SUBLANES - 1:SUBLANES, :], h.shape)

    h_last = lax.fori_loop(0, tt // SUBLANES, scan_block, hcar[...])
    hcar[...] = h_last
    bout_ref[...] = (u_s[...] * gate_ref[...]).astype(bout_ref.dtype)

    tail_a = wina[tt:tt + ha, :]
    tail_b = winb[tt:tt + hb, :]
    wina[0:ha, :] = tail_a
    winb[0:hb, :] = tail_b
    nsa_ref[...] = tail_a
    nsb_ref[...] = tail_b
    nsh_ref[...] = h_last[0:1, :]


def _mixer_seq(glu, bx, gate, row_off, sa_pad, sb_pad, sh, mix_w, n_seq, seq_len, tt):
    wa, ba, ng, nb, wb, bb, wg, brg, big, lam = mix_w
    width = glu.shape[1]
    ha, hb = sa_pad.shape[1], sb_pad.shape[1]
    n_t = seq_len // tt
    off = row_off // tt
    row_in = pl.BlockSpec((tt, width), lambda b, t: (off + b * n_t + t, 0))
    row_out = pl.BlockSpec((tt, width), lambda b, t: (b * n_t + t, 0))

    def const(shape):
        return pl.BlockSpec(shape, lambda b, t: (0,) * len(shape))

    def per_seq(r):
        return pl.BlockSpec((None, r, width), lambda b, t: (b, 0, 0))

    return pl.pallas_call(
        _mixer_seq_kernel,
        grid=(n_seq, n_t),
        in_specs=[row_in, row_in, row_in, per_seq(ha), per_seq(hb), per_seq(1),
                  const(wa.shape), const(ba.shape), const(ng.shape), const(nb.shape),
                  const(wb.shape), const(bb.shape), const(wg.shape), const(brg.shape),
                  const(big.shape), const(lam.shape)],
        out_specs=[row_out, row_out, per_seq(ha), per_seq(hb), per_seq(1)],
        out_shape=[jax.ShapeDtypeStruct((n_seq * seq_len, width), bf16),
                   jax.ShapeDtypeStruct((n_seq * seq_len, width), bf16),
                   jax.ShapeDtypeStruct((n_seq, ha, width), f32),
                   jax.ShapeDtypeStruct((n_seq, hb, width), f32),
                   jax.ShapeDtypeStruct((n_seq, 1, width), f32)],
        scratch_shapes=[pltpu.VMEM((ha + tt, width), f32),
                        pltpu.VMEM((hb + tt, width), f32),
                        pltpu.VMEM((SUBLANES - 1, tt + ha - SUBLANES, width), f32),
                        pltpu.VMEM((SUBLANES - 1, tt + hb - SUBLANES, width), f32),
                        pltpu.VMEM((tt, width), f32),
                        pltpu.VMEM((tt, width), f32),
                        pltpu.VMEM((tt, width), f32),
                        pltpu.VMEM((tt, width), f32),
                        pltpu.VMEM((SUBLANES, width), f32)],
        compiler_params=pltpu.CompilerParams(dimension_semantics=("arbitrary", "arbitrary"),
                                             vmem_limit_bytes=VMEM_LIMIT),
        name="mixer_seq",
    )(glu, bx, gate, sa_pad, sb_pad, sh, wa, ba, ng, nb, wb, bb, wg, brg, big, lam)


def _mixer_step_kernel(glu_ref, bx_ref, gate_ref, sa_ref, sb_ref, sh_ref,
                       wat_ref, wa2_ref, wbt_ref, wb2_ref,
                       ba_ref, ng_ref, nb_ref, bb_ref, wg_ref, brg_ref, big_ref, lam_ref,
                       aout_ref, bout_ref, nsa_ref, nsb_ref, nsh_ref,
                       hista, histb, hcar):
    t = pl.program_id(1)
    n_t = hista.shape[0]
    width = glu_ref.shape[1]

    def roll_buffer(s_ref, ns_ref, x_ref):
        shift = n_t - s_ref.shape[0]

        @pl.when(t == 0)
        def _():
            for r in range(max(-shift, 0)):
                ns_ref[r] = s_ref[r + n_t]

        if shift <= 0:
            ns_ref[t - shift] = x_ref[...]
        else:
            @pl.when(t >= shift)
            def _():
                ns_ref[t - shift] = x_ref[...]

    @pl.when(t == 0)
    def _():
        hista[...] = jnp.zeros_like(hista)
        histb[...] = jnp.zeros_like(histb)
        hcar[...] = sh_ref[...]

    roll_buffer(sa_ref, nsa_ref, glu_ref)
    roll_buffer(sb_ref, nsb_ref, bx_ref)
    hista[t] = glu_ref[...]
    histb[t] = bx_ref[...]

    ca = jnp.broadcast_to(ba_ref[...], glu_ref.shape)
    for r in range(sa_ref.shape[0]):
        ca = ca + wat_ref[r:r + 1, :] * sa_ref[r]
    cb = jnp.broadcast_to(bb_ref[...], bx_ref.shape)
    for r in range(sb_ref.shape[0]):
        cb = cb + wbt_ref[r:r + 1, :] * sb_ref[r]
    for j in range(n_t):
        ca = ca + wa2_ref[j:j + 1, :] * hista[j]
        cb = cb + wb2_ref[j:j + 1, :] * histb[j]
    a_out = _group_norm_silu(ca, ng_ref[...], nb_ref[...], width // CONV_HEADS_A)
    aout_ref[...] = a_out.astype(aout_ref.dtype)

    a, u = _lru_gates(cb, wg_ref, brg_ref[...], big_ref[...], lam_ref[...])
    h = a * hcar[...] + u
    hcar[...] = h
    nsh_ref[...] = h
    bout_ref[...] = (h * gate_ref[...]).astype(bout_ref.dtype)


def _mixer_step(glu, bx, gate, row_off, sa, sb, sh, step_w, mix_w, n_seq, n_t, bt):
    wat, wa2, wbt, wb2 = step_w
    wa, ba, ng, nb, wb, bb, wg, brg, big, lam = mix_w
    width = glu.shape[1]
    ka1, kb1 = sa.shape[1], sb.shape[1]
    off = row_off // bt
    nb_tiles = n_seq // bt
    row_in = pl.BlockSpec((bt, width), lambda j, t: (off + t * nb_tiles + j, 0))
    row_out = pl.BlockSpec((bt, width), lambda j, t: (t * nb_tiles + j, 0))

    def const(shape):
        return pl.BlockSpec(shape, lambda j, t: (0,) * len(shape))

    def per_t(arr):
        return pl.BlockSpec((None,) + arr.shape[1:], lambda j, t: (t, 0, 0))

    def state(r):
        return pl.BlockSpec((None, r, bt, width), lambda j, t: (0, 0, j, 0))

    state_h = pl.BlockSpec((None, bt, width), lambda j, t: (0, j, 0))
    return pl.pallas_call(
        _mixer_step_kernel,
        grid=(nb_tiles, n_t),
        in_specs=[row_in, row_in, row_in, state(ka1), state(kb1), state_h,
                  per_t(wat), per_t(wa2), per_t(wbt), per_t(wb2),
                  const(ba.shape), const(ng.shape), const(nb.shape), const(bb.shape),
                  const(wg.shape), const(brg.shape), const(big.shape), const(lam.shape)],
        out_specs=[row_out, row_out, state(ka1), state(kb1), state_h],
        out_shape=[jax.ShapeDtypeStruct((n_seq * n_t, width), bf16),
                   jax.ShapeDtypeStruct((n_seq * n_t, width), bf16),
                   jax.ShapeDtypeStruct((1, ka1, n_seq, width), f32),
                   jax.ShapeDtypeStruct((1, kb1, n_seq, width), f32),
                   jax.ShapeDtypeStruct((1, n_seq, width), f32)],
        scratch_shapes=[pltpu.VMEM((n_t, bt, width), f32),
                        pltpu.VMEM((n_t, bt, width), f32),
                        pltpu.VMEM((bt, width), f32)],
        compiler_params=pltpu.CompilerParams(dimension_semantics=("arbitrary", "arbitrary"),
                                             vmem_limit_bytes=VMEM_LIMIT),
        name="mixer_step",
    )(glu, bx, gate, sa, sb, sh, wat, wa2, wbt, wb2,
      ba, ng, nb, bb, wg, brg, big, lam)


def _outproj_kernel(ap_ref, bp_ref, as_ref, bs_ref, h_ref, wo_ref,
                    l1g_ref, l1b_ref, wrh_ref, wrl_ref, br_ref,
                    h1_ref, h1p_ref, z_ref, *, n_p_tiles, alpha):
    i = pl.program_id(0)
    is_p = i < n_p_tiles
    a = jnp.where(is_p, ap_ref[...], as_ref[...])
    b = jnp.where(is_p, bp_ref[...], bs_ref[...])
    width = a.shape[1]
    mix = (jnp.dot(a, wo_ref[0:width, :], preferred_element_type=f32)
           + jnp.dot(b, wo_ref[width:2 * width, :], preferred_element_type=f32))
    h1 = _layer_norm(alpha * h_ref[...] + mix, l1g_ref[...], l1b_ref[...])
    h1_ref[...] = h1

    xh = h1.astype(bf16)
    xh32 = xh.astype(f32)
    half = h1.shape[1] // 2
    lo = lax.shift_right_logical(lax.bitcast_convert_type(xh32[:, :half], jnp.uint32), jnp.uint32(16))
    hi = lax.bitcast_convert_type(xh32[:, half:], jnp.uint32) & jnp.uint32(0xFFFF0000)
    word = lo | hi
    n_sub = half // LANES
    for s in range(n_sub):
        h1p_ref[pl.ds(s, word.shape[0], stride=n_sub), :] = word[:, s * LANES:(s + 1) * LANES]

    xl = (h1 - xh32).astype(bf16)
    z_ref[...] = (jnp.dot(xh, wrh_ref[...], preferred_element_type=f32)
                  + jnp.dot(xl, wrh_ref[...], preferred_element_type=f32)
                  + jnp.dot(xh, wrl_ref[...], preferred_element_type=f32)) + br_ref[...]


def _route_kernel(z_ref, route_ref, routet_ref, cnt_ref, carry, *, sub):
    i = pl.program_id(0)
    z = z_ref[...]
    lane = lax.broadcasted_iota(i32, z.shape, 1).astype(f32)
    n_g, epg = float(N_GROUPS), float(EXPERTS_PER_GROUP)
    far = float(2 * ROUTE_LANES)

    gm = lane < n_g
    zg = jnp.where(gm, z, NEG_BIG)
    pg = jnp.where(gm, jnp.exp(zg - jnp.max(zg, axis=-1, keepdims=True)), 0.0)
    pg = pg / jnp.sum(pg, axis=-1, keepdims=True)
    g_top = jnp.max(pg, axis=-1, keepdims=True)
    g_idx = jnp.min(jnp.where(gm & (pg == g_top), lane, far), axis=-1, keepdims=True)

    lo = n_g + g_idx * epg
    em = (lane >= lo) & (lane < lo + epg)
    ze = jnp.where(em, z, NEG_BIG)
    pe = jnp.where(em, jnp.exp(ze - jnp.max(ze, axis=-1, keepdims=True)), 0.0)
    pe = pe / jnp.sum(pe, axis=-1, keepdims=True)
    v1 = jnp.max(jnp.where(em, pe, -1.0), axis=-1, keepdims=True)
    l1 = jnp.min(jnp.where(em & (pe == v1), lane, far), axis=-1, keepdims=True)
    pe2 = jnp.where(em & (lane != l1), pe, -1.0)
    v2 = jnp.max(pe2, axis=-1, keepdims=True)
    l2 = jnp.min(jnp.where(pe2 == v2, lane, far), axis=-1, keepdims=True)
    den = v1 + v2
    gate1 = v1 / den * g_top
    gate2 = v2 / den * g_top

    @pl.when(i == 0)
    def _():
        carry[...] = jnp.zeros_like(carry)

    o1 = lane == l1
    o2 = lane == l2
    onehot = jnp.where(o1 | o2, 1.0, 0.0)
    ri = lax.broadcasted_iota(i32, (sub, sub), 0)
    ci = lax.broadcasted_iota(i32, (sub, sub), 1)
    tri = jnp.where(ci < ri, 1.0, 0.0).astype(bf16)
    counts = carry[0:1, :]
    prefix = []
    for s in range(z.shape[0] // sub):
        oh = onehot[s * sub:(s + 1) * sub, :]
        prefix.append(jnp.dot(tri, oh.astype(bf16), preferred_element_type=f32) + counts)
        counts = counts + jnp.sum(oh, axis=0, keepdims=True)
    prefix = jnp.concatenate(prefix, axis=0)
    rank1 = jnp.sum(jnp.where(o1, prefix, 0.0), axis=-1, keepdims=True)
    rank2 = jnp.sum(jnp.where(o2, prefix, 0.0), axis=-1, keepdims=True)
    new_carry = jnp.broadcast_to(counts, carry.shape)
    carry[...] = new_carry
    cnt_ref[...] = new_carry

    route = jnp.where(lane == 0.0, l1 - n_g, 0.0)
    route = jnp.where(lane == 1.0, l2 - n_g, route)
    route = jnp.where(lane == 2.0, rank1, route)
    route = jnp.where(lane == 3.0, rank2, route)
    route = jnp.where(lane == 4.0, gate1, route)
    route = jnp.where(lane == 5.0, gate2, route)
    route_ref[...] = route
    routet_ref[...] = route.T[0:routet_ref.shape[0], :]


def _outproj(a_p, b_p, a_s, b_s, h, w_out_b, ln1_g, ln1_b,
             wr_hi, wr_lo, b_r, alpha, tm):
    n_p, width = a_p.shape
    n_s = a_s.shape[0]
    d = h.shape[1]
    n_p_tiles = n_p // tm
    n_s_tiles = n_s // tm
    n_tiles = n_p_tiles + n_s_tiles

    def p_map(i):
        return (jnp.minimum(i, n_p_tiles - 1), 0)

    def s_map(i):
        return (jnp.clip(i - n_p_tiles, 0, n_s_tiles - 1), 0)

    def const(shape, **kw):
        return pl.BlockSpec(shape, lambda i: (0,) * len(shape), **kw)

    return pl.pallas_call(
        functools.partial(_outproj_kernel, n_p_tiles=n_p_tiles, alpha=alpha),
        grid=(n_tiles,),
        in_specs=[pl.BlockSpec((tm, width), p_map), pl.BlockSpec((tm, width), p_map),
                  pl.BlockSpec((tm, width), s_map), pl.BlockSpec((tm, width), s_map),
                  pl.BlockSpec((tm, d), lambda i: (i, 0)),
                  const((2 * width, d), pipeline_mode=pl.Buffered(1)),
                  const((1, d)), const((1, d)),
                  const((d, ROUTE_LANES)), const((d, ROUTE_LANES)), const((1, ROUTE_LANES))],
        out_specs=[pl.BlockSpec((tm, d), lambda i: (i, 0)),
                   pl.BlockSpec((tm * (d // 2 // LANES), LANES), lambda i: (i, 0)),
                   pl.BlockSpec((tm, ROUTE_LANES), lambda i: (i, 0))],
        out_shape=[jax.ShapeDtypeStruct((n_p + n_s, d), f32),
                   jax.ShapeDtypeStruct(((n_p + n_s) * (d // 2 // LANES), LANES), jnp.uint32),
                   jax.ShapeDtypeStruct((n_p + n_s, ROUTE_LANES), f32)],
        compiler_params=pltpu.CompilerParams(dimension_semantics=("arbitrary",),
                                             vmem_limit_bytes=VMEM_LIMIT),
        name="outproj",
    )(a_p, b_p, a_s, b_s, h, w_out_b, ln1_g, ln1_b, wr_hi, wr_lo, b_r)


def _route_tile(n):
    return max(t for t in range(ROUTE_SUB, ROUTE_MAX_TILE + 1, ROUTE_SUB) if n % t == 0)


def _route(z):
    n = z.shape[0]
    tr = _route_tile(n)
    return pl.pallas_call(
        functools.partial(_route_kernel, sub=ROUTE_SUB),
        grid=(n // tr,),
        in_specs=[pl.BlockSpec((tr, ROUTE_LANES), lambda i: (i, 0))],
        out_specs=[pl.BlockSpec((tr, ROUTE_LANES), lambda i: (i, 0)),
                   pl.BlockSpec((SUBLANES, tr), lambda i: (0, i)),
                   pl.BlockSpec((SUBLANES, ROUTE_LANES), lambda i: (0, 0))],
        out_shape=[jax.ShapeDtypeStruct((n, ROUTE_LANES), f32),
                   jax.ShapeDtypeStruct((SUBLANES, n), f32),
                   jax.ShapeDtypeStruct((SUBLANES, ROUTE_LANES), f32)],
        scratch_shapes=[pltpu.VMEM((SUBLANES, ROUTE_LANES), f32)],
        compiler_params=pltpu.CompilerParams(dimension_semantics=("arbitrary",),
                                             vmem_limit_bytes=VMEM_LIMIT),
        name="route",
    )(z)


def _invert_kernel(dest_ref, flat_ref, fill):
    fill[...] = jnp.full(fill.shape, -1, fill.dtype)
    pltpu.sync_copy(fill, flat_ref)

    def place(f, carry):
        flat_ref[dest_ref[f]] = f
        return carry

    lax.fori_loop(0, dest_ref.shape[0], place, 0, unroll=SCALAR_UNROLL)


def _invert(dest, p_max):
    return pl.pallas_call(
        _invert_kernel,
        in_specs=[pl.BlockSpec(memory_space=pltpu.SMEM)],
        out_specs=pl.BlockSpec(memory_space=pltpu.SMEM),
        out_shape=jax.ShapeDtypeStruct((p_max,), i32),
        scratch_shapes=[pltpu.VMEM((p_max,), i32)],
        name="invert_dispatch",
    )(dest)


def _expert_kernel(src_ref, dst_ref, start_ref, nblk_ref,
                   h1_hbm, wg_ref, wu_ref, wd_ref, y2_hbm,
                   xbuf, ybuf, wgb, wub, wdb, gsem, osem, *, blk, n_rows):
    e = pl.program_id(0)
    n_e = pl.num_programs(0)
    nb = nblk_ref[e]
    g0 = start_ref[e]

    xs = xbuf.shape[1] // blk
    ys = ybuf.shape[1] // blk

    def gather(g, slot):
        for r in range(blk):
            tok = src_ref[g * blk + r]
            pltpu.make_async_copy(h1_hbm.at[pl.ds(pl.multiple_of(tok * xs, xs), xs), :],
                                  xbuf.at[slot, pl.ds(r * xs, xs), :], gsem.at[slot]
                                  ).start(priority=r % 2)

    def scatter(g, slot):
        for r in range(blk):
            row = dst_ref[(g + 1) * blk + r]
            pltpu.make_async_copy(ybuf.at[slot, pl.ds(r * ys, ys), :],
                                  y2_hbm.at[pl.ds(pl.multiple_of(row * ys, ys), ys), :],
                                  osem.at[slot]).start(priority=r % 2)

    def dump_copy(slot):
        dst = y2_hbm.at[pl.ds((n_rows + slot * blk) * ys, blk * ys), :]
        return pltpu.make_async_copy(ybuf.at[slot], dst, osem.at[slot])

    def wait_gather(slot):
        pltpu.make_async_copy(h1_hbm.at[pl.ds(0, blk * xs), :], xbuf.at[slot], gsem.at[slot]).wait()

    nx, ny = xbuf.shape[0], ybuf.shape[0]
    ahead = nx - 1

    @pl.when(e == 0)
    def _():
        ybuf[...] = jnp.zeros(ybuf.shape, ybuf.dtype)
        for s in range(ny - 1):
            dump_copy(s).start()
        for g in range(ahead):
            gather(g, g)

    @pl.when(nb > 0)
    def _():
        wgb[...] = wg_ref[...].astype(bf16)
        wub[...] = wu_ref[...].astype(bf16)
        wdb[...] = wd_ref[...].astype(bf16)

        def chunk(c, carry):
            g = g0 + c
            slot = g % nx
            yslot = g % ny
            wait_gather(slot)
            dump_copy(yslot).wait()
            gather(g + ahead, (g + ahead) % nx)
            scatter(g - 1, (g + ny - 1) % ny)
            xw = [xbuf[slot, pl.ds(s, blk, stride=xs), :] for s in range(xs)]
            x_lo = [lax.bitcast_convert_type(lax.shift_left(w, jnp.uint32(16)), f32) for w in xw]
            x_hi = [lax.bitcast_convert_type(w & jnp.uint32(0xFFFF0000), f32) for w in xw]
            x = jnp.concatenate(x_lo + x_hi, axis=1).astype(bf16)
            hg = jnp.dot(x, wgb[...], preferred_element_type=f32)
            hu = jnp.dot(x, wub[...], preferred_element_type=f32)
            hid = (hg * jax.nn.sigmoid(hg) * hu).astype(bf16)
            y = jnp.dot(hid, wdb[...], preferred_element_type=f32)
            for s in range(ys):
                ybuf[yslot, pl.ds(s, blk, stride=ys), :] = y[:, s * LANES:(s + 1) * LANES]
            return carry

        lax.fori_loop(0, nb, chunk, 0)

    @pl.when(e == n_e - 1)
    def _():
        g_end = g0 + nb
        scatter(g_end - 1, (g_end + ny - 1) % ny)
        for s in range(ny):
            dump_copy(s).wait()
        for j in range(ahead):
            wait_gather((g_end + j) % nx)


def _experts(h1p, src_rows, dst_rows, pad_start, nblk, w_gate, w_up, w_down, blk):
    n_e, d, d_e = w_gate.shape
    xs, ys = d // 2 // LANES, d // LANES
    n = h1p.shape[0] // xs
    n_rows = TOP_K * n
    grid_spec = pltpu.PrefetchScalarGridSpec(
        num_scalar_prefetch=4,
        grid=(n_e,),
        in_specs=[pl.BlockSpec(memory_space=pl.ANY),
                  pl.BlockSpec((None, d, d_e), lambda e, *_: (e, 0, 0)),
                  pl.BlockSpec((None, d, d_e), lambda e, *_: (e, 0, 0)),
                  pl.BlockSpec((None, d_e, d), lambda e, *_: (e, 0, 0))],
        out_specs=pl.BlockSpec(memory_space=pl.ANY),
        scratch_shapes=[pltpu.VMEM((GATHER_SLOTS, blk * xs, LANES), jnp.uint32),
                        pltpu.VMEM((SCATTER_SLOTS, blk * ys, LANES), f32),
                        pltpu.VMEM((d, d_e), bf16),
                        pltpu.VMEM((d, d_e), bf16),
                        pltpu.VMEM((d_e, d), bf16),
                        pltpu.SemaphoreType.DMA((GATHER_SLOTS,)),
                        pltpu.SemaphoreType.DMA((SCATTER_SLOTS,))],
    )
    return pl.pallas_call(
        functools.partial(_expert_kernel, blk=blk, n_rows=n_rows),
        grid_spec=grid_spec,
        out_shape=jax.ShapeDtypeStruct(((n_rows + SCATTER_SLOTS * blk) * ys, LANES), f32),
        compiler_params=pltpu.CompilerParams(dimension_semantics=("arbitrary",),
                                             vmem_limit_bytes=VMEM_LIMIT),
        name="experts",
    )(src_rows, dst_rows, pad_start, nblk, h1p, w_gate, w_up, w_down)


def _combine_kernel(h1_ref, route_ref, ya_ref, yb_ref, l2g_ref, l2b_ref, yp_ref, ysm_ref,
                    *, n_p_tiles, alpha):
    i = pl.program_id(0)
    route = route_ref[...]
    tm = route.shape[0]
    ys = ya_ref.shape[0] // tm

    def token_rows(ref):
        return jnp.concatenate([ref[pl.ds(s, tm, stride=ys), :] for s in range(ys)], axis=1)

    y = token_rows(ya_ref) * route[:, 4:5] + token_rows(yb_ref) * route[:, 5:6]
    out = _layer_norm(alpha * h1_ref[...] + y, l2g_ref[...], l2b_ref[...])

    @pl.when(i < n_p_tiles)
    def _():
        yp_ref[...] = out

    @pl.when(i >= n_p_tiles)
    def _():
        ysm_ref[...] = out


def _combine(h1, route, y2, ln2_g, ln2_b, n_p, alpha, tm):
    n, d = h1.shape
    n_s = n - n_p
    n_tiles = n // tm
    n_p_tiles = n_p // tm
    n_s_tiles = n_s // tm
    return pl.pallas_call(
        functools.partial(_combine_kernel, n_p_tiles=n_p_tiles, alpha=alpha),
        grid=(n_tiles,),
        in_specs=[pl.BlockSpec((tm, d), lambda i: (i, 0)),
                  pl.BlockSpec((tm, ROUTE_LANES), lambda i: (i, 0)),
                  pl.BlockSpec((tm * (d // LANES), LANES), lambda i: (i, 0)),
                  pl.BlockSpec((tm * (d // LANES), LANES), lambda i: (n_tiles + i, 0)),
                  pl.BlockSpec((1, d), lambda i: (0, 0)),
                  pl.BlockSpec((1, d), lambda i: (0, 0))],
        out_specs=[pl.BlockSpec((tm, d), lambda i: (jnp.minimum(i, n_p_tiles - 1), 0)),
                   pl.BlockSpec((tm, d), lambda i: (jnp.clip(i - n_p_tiles, 0, n_s_tiles - 1), 0))],
        out_shape=[jax.ShapeDtypeStruct((n_p, d), f32), jax.ShapeDtypeStruct((n_s, d), f32)],
        compiler_params=pltpu.CompilerParams(dimension_semantics=("arbitrary",),
                                             vmem_limit_bytes=VMEM_LIMIT),
        name="combine_ln2",
    )(h1, route, y2, y2, ln2_g, ln2_b)


def _shifted_taps(w, n_t, hist):
    k = w.shape[0]
    zero = jnp.zeros_like(w[0])
    wt = jnp.stack([jnp.stack([w[r - t] if 0 <= r - t < k else zero for r in range(hist)])
                    for t in range(n_t)])
    w2 = jnp.stack([jnp.stack([w[hist - t + j] if (j <= t and 0 <= hist - t + j < k) else zero
                               for j in range(n_t)]) for t in range(n_t)])
    return wt, w2


def kernel(x_prompt, x_sample, state_conv_a, state_conv_b, state_lru, meta_tokens, ln_in_g, ln_in_b, w_in, conv_a_w, conv_a_b, norm_a_g, norm_a_b, conv_b_w, conv_b_b, w_rg, b_rg, w_ig, b_ig, lru_lambda, w_out, ln1_g, ln1_b, w_router_group, b_router_group, w_router_expert, b_router_expert, w_gate, w_up, w_down, ln2_g, ln2_b):
    depth = w_in.shape[0]
    assert depth == 1, "single-layer trunk only"
    bp, tp, d = x_prompt.shape
    bs, ts, _ = x_sample.shape
    n_meta = meta_tokens.shape[0]
    width = conv_a_w.shape[2]
    ka, kb = conv_a_w.shape[1], conv_b_w.shape[1]
    n_e = w_gate.shape[1]
    alpha = (2.0 * depth) ** 0.25
    n_p, n_s = bp * tp, bs * ts
    n = n_p + n_s
    assert ts < ka and ts >= kb - 1
    assert n_p % TOKEN_TILE == 0 and n_s % TOKEN_TILE == 0 and tp % SEQ_TILE == 0
    assert bs % STEP_BATCH == 0 and n_p % STEP_BATCH == 0
    assert n_meta % BF16_ROWS == 0 and n_meta <= TOKEN_TILE and n % n_meta == 0
    assert N_GROUPS * (1 + EXPERTS_PER_GROUP) <= ROUTE_LANES and n_e == N_GROUPS * EXPERTS_PER_GROUP

    row = lambda v: v.reshape(1, -1).astype(f32)
    ln_g, ln_b = row(ln_in_g), row(ln_in_b)
    w_in_b = w_in[0].astype(bf16)
    w_out_b = w_out[0].astype(bf16)
    mix_w = (conv_a_w[0], row(conv_a_b[0]), row(norm_a_g[0]), row(norm_a_b[0]),
             conv_b_w[0], row(conv_b_b[0]),
             jnp.concatenate([w_rg[0], w_ig[0]], axis=-1).astype(bf16),
             row(b_rg[0]), row(b_ig[0]), row(lru_lambda[0]))
    ha, hb = _round_up(ka - 1, SUBLANES), _round_up(kb - 1, SUBLANES)

    x_p = x_prompt.reshape(n_p, d)
    x_s = jnp.concatenate([jnp.swapaxes(x_sample, 0, 1).reshape(n_s, d),
                           jnp.pad(meta_tokens, ((0, TOKEN_TILE - n_meta), (0, 0)))], axis=0)
    h, glu, bx, gate = _inproj(x_p, x_s, n_s + TOKEN_TILE, ln_g, ln_b, w_in_b, TOKEN_TILE)

    _, _, sa_m, sb_m, sh_m = _mixer_seq(
        glu, bx, gate, n, jnp.zeros((1, ha, width), f32), jnp.zeros((1, hb, width), f32),
        jnp.zeros((1, 1, width), f32), mix_w, 1, n_meta, n_meta)

    bcast = lambda s: jnp.broadcast_to(s, (bp,) + s.shape[1:])
    a_p, b_p, nsa_p, nsb_p, nsh_p = _mixer_seq(glu, bx, gate, 0, bcast(sa_m), bcast(sb_m), bcast(sh_m),
                                               mix_w, bp, tp, SEQ_TILE)

    row_major = lambda s: jnp.swapaxes(s, 1, 2)
    wat, wa2 = _shifted_taps(conv_a_w[0], ts, ka - 1)
    wbt, wb2 = _shifted_taps(conv_b_w[0], ts, kb - 1)
    a_s, b_s, nsa_s, nsb_s, nsh_s = _mixer_step(
        glu, bx, gate, n_p, row_major(state_conv_a), row_major(state_conv_b), state_lru,
        (wat, wa2, wbt, wb2), mix_w, bs, ts, STEP_BATCH)

    w_r = jnp.concatenate([w_router_group[0], w_router_expert[0]], axis=1)
    w_r = jnp.pad(w_r, ((0, 0), (0, ROUTE_LANES - w_r.shape[1])))
    wr_hi = w_r.astype(bf16)
    wr_lo = (w_r - wr_hi.astype(f32)).astype(bf16)
    b_r = jnp.concatenate([b_router_group[0], b_router_expert[0]])
    b_r = jnp.pad(b_r, (0, ROUTE_LANES - b_r.shape[0])).reshape(1, ROUTE_LANES)
    h1, h1p, logits = _outproj(a_p, b_p, a_s, b_s, h, w_out_b,
                               row(ln1_g[0]), row(ln1_b[0]), wr_hi, wr_lo, b_r, alpha, TOKEN_TILE)
    route, route_t, cnt = _route(logits)

    blk = EXPERT_ROWS
    e_idx = route_t[0:TOP_K].astype(i32)
    rank = route_t[TOP_K:2 * TOP_K].astype(i32)
    counts = cnt[0, N_GROUPS:N_GROUPS + n_e].astype(i32)
    nblk = (counts + blk - 1) // blk
    pad_end = jnp.cumsum(nblk * blk)
    pad_start = pad_end - nblk * blk
    start_of = jnp.sum(jnp.where(e_idx[None] == jnp.arange(n_e, dtype=i32)[:, None, None],
                                 pad_start[:, None, None], 0), axis=0)
    dest = (start_of + rank).reshape(-1)
    p_max = _round_up(TOP_K * n + n_e * (blk - 1), blk)
    flat = _invert(dest, p_max)
    pos = jnp.arange(p_max, dtype=i32)
    valid = flat >= 0
    src_rows = jnp.where(valid, flat % n, 0)
    dst_rows = jnp.where(valid, flat, TOP_K * n + pos % (SCATTER_SLOTS * blk))
    src_rows = jnp.concatenate([src_rows, jnp.zeros(((GATHER_SLOTS - 1) * blk,), i32)])
    dst_rows = jnp.concatenate([TOP_K * n + (SCATTER_SLOTS - 1) * blk + jnp.arange(blk, dtype=i32), dst_rows])
    y2 = _experts(h1p, src_rows, dst_rows, (pad_start // blk).astype(i32), nblk,
                  w_gate[0], w_up[0], w_down[0], blk)
    y_p, y_s = _combine(h1, route, y2, row(ln2_g[0]), row(ln2_b[0]), n_p, alpha, TOKEN_TILE)

    y_prompt = y_p.reshape(bp, tp, d)
    y_sample = jnp.swapaxes(y_s.reshape(ts, bs, d), 0, 1)
    return (y_prompt, y_sample,
            nsa_p[:, ha - (ka - 1):][None], nsb_p[:, hb - (kb - 1):][None], nsh_p.reshape(1, bp, width),
            row_major(nsa_s), row_major(nsb_s), nsh_s)
```
